```python
import jax, jax.numpy as jnp
from jax import lax
import numpy as np

D_MODEL = 1024
BATCH = 2
SEQ = 8192
DEPTH = 1

SB_HEADS = 8
SB_HEAD_DIM = 64
SB_BLOCK = 128
SB_W = SB_HEADS * SB_HEAD_DIM
GDN_HEADS = 8
GDN_DK = 64
GDN_DV = 64
GDN_CONV = 4
GDN_CHUNK = 64
GDN_QK_W = GDN_HEADS * GDN_DK
GDN_V_W = GDN_HEADS * GDN_DV
GDN_CONV_W = 2 * GDN_QK_W + GDN_V_W
IN_WIDTHS = (SB_W, SB_W, SB_W, GDN_CONV_W, GDN_HEADS, GDN_HEADS, GDN_V_W, 2 * D_MODEL)
IN_W = sum(IN_WIDTHS)
PEER_HEADS = 8
PEER_NKEYS = 128
PEER_N_EXPERTS = PEER_NKEYS * PEER_NKEYS
PEER_QDIM = 256
PEER_HALF = PEER_QDIM // 2
PEER_TOPK = 16
PEER_BLOCK = 128
EPS = 1e-6

kernel_name = "hybrid_sb_gdn_peer_block"


def rmsnorm(x, w):
    x32 = x.astype(jnp.float32)
    y = x32 * lax.rsqrt(jnp.mean(x32 * x32, axis=-1, keepdims=True) + EPS) * w.astype(jnp.float32)
    return y.astype(x.dtype)


def l2norm(x):
    return x * lax.rsqrt(jnp.sum(x * x, axis=-1, keepdims=True) + EPS)


def modulate(h, shift, scale):
    return h * (1.0 + scale[:, None, :]) + shift[:, None, :]


def stick_breaking_attention(q, k, v):
    B, S, H, d = q.shape
    nb = S // SB_BLOCK
    qh = q.astype(jnp.float32).transpose(0, 2, 1, 3) * (d ** -0.5)
    kh = k.astype(jnp.float32).transpose(0, 2, 1, 3)
    vh = v.astype(jnp.float32).transpose(0, 2, 1, 3)
    qb = qh.reshape(B, H, nb, SB_BLOCK, d).transpose(2, 0, 1, 3, 4)
    key_pos = jnp.arange(S)

    def block(args):
        q_blk, start = args
        z = jnp.einsum('bhqd,bhkd->bhqk', q_blk, kh)
        qpos = start + jnp.arange(SB_BLOCK)
        causal = key_pos[None, :] < qpos[:, None]
        log_keep = jnp.where(causal, jax.nn.log_sigmoid(-z), 0.0)
        log_stick = lax.cumsum(log_keep, axis=3, reverse=True) - log_keep
        attn = jnp.where(causal, jnp.exp(jax.nn.log_sigmoid(z) + log_stick), 0.0)
        return jnp.einsum('bhqk,bhkd->bhqd', attn, vh)

    starts = jnp.arange(nb) * SB_BLOCK
    out = lax.map(block, (qb, starts))
    out = out.transpose(1, 0, 3, 2, 4).reshape(B, S, H * d)
    return out.astype(q.dtype)


def chunk_gated_delta_rule(q, k, v, g, beta):
    B, H, L, dk = q.shape
    dv = v.shape[-1]
    C = GDN_CHUNK
    N = L // C
    q = q.reshape(B, H, N, C, dk)
    k = k.reshape(B, H, N, C, dk)
    v = v.reshape(B, H, N, C, dv)
    g = g.reshape(B, H, N, C)
    beta = beta.reshape(B, H, N, C)
    G = jnp.cumsum(g, axis=-1)
    idx = jnp.arange(C)
    incl = idx[:, None] >= idx[None, :]
    strict = idx[:, None] > idx[None, :]
    decay = jnp.exp(jnp.where(incl, G[..., :, None] - G[..., None, :], -jnp.inf))
    kk = jnp.einsum('bhncd,bhnsd->bhncs', k, k)
    A = jnp.where(strict, beta[..., :, None] * kk * decay, 0.0)
    M = jnp.eye(C, dtype=jnp.float32) + A
    rhs = jnp.concatenate([v * beta[..., None], k * (beta * jnp.exp(G))[..., None]], axis=-1)
    sol = lax.linalg.triangular_solve(M, rhs, left_side=True, lower=True, unit_diagonal=True)
    u, w = sol[..., :dv], sol[..., dv:]
    qk = jnp.einsum('bhncd,bhnsd->bhncs', q, k) * decay
    q_dec = q * jnp.exp(G)[..., None]
    k_dec = k * jnp.exp(G[..., -1:] - G)[..., None]
    chunk_decay = jnp.exp(G[..., -1])

    def step(S, inp):
        u_c, w_c, qk_c, qd_c, kd_c, cd_c = inp
        v_new = u_c - jnp.einsum('bhcd,bhde->bhce', w_c, S)
        o = jnp.einsum('bhcd,bhde->bhce', qd_c, S) + jnp.einsum('bhcs,bhse->bhce', qk_c, v_new)
        S = S * cd_c[..., None, None] + jnp.einsum('bhcd,bhce->bhde', kd_c, v_new)
        return S, o

    xs = tuple(jnp.moveaxis(t, 2, 0) for t in (u, w, qk, q_dec, k_dec, chunk_decay))
    S0 = jnp.zeros((B, H, dk, dv), jnp.float32)
    _, o = lax.scan(step, S0, xs)
    return jnp.moveaxis(o, 0, 2).reshape(B, H, L, dv)


def gated_deltanet(qkv, a, b, z, conv_w, A_log, dt_bias, o_norm_w):
    B, S, _ = qkv.shape
    dtype = qkv.dtype
    conv = lax.conv_general_dilated(
        qkv.astype(jnp.float32), conv_w.astype(jnp.float32)[:, None, :],
        window_strides=(1,), padding=[(GDN_CONV - 1, 0)],
        dimension_numbers=('NWC', 'WIO', 'NWC'), feature_group_count=GDN_CONV_W)
    conv = jax.nn.silu(conv)
    q = conv[..., :GDN_QK_W].reshape(B, S, GDN_HEADS, GDN_DK)
    k = conv[..., GDN_QK_W:2 * GDN_QK_W].reshape(B, S, GDN_HEADS, GDN_DK)
    v = conv[..., 2 * GDN_QK_W:].reshape(B, S, GDN_HEADS, GDN_DV)
    q = l2norm(q) * (GDN_DK ** -0.5)
    k = l2norm(k)
    beta = jax.nn.sigmoid(b.astype(jnp.float32))
    g = -jnp.exp(A_log.astype(jnp.float32)) * jax.nn.softplus(a.astype(jnp.float32) + dt_bias.astype(jnp.float32))
    o = chunk_gated_delta_rule(q.transpose(0, 2, 1, 3), k.transpose(0, 2, 1, 3), v.transpose(0, 2, 1, 3),
                               g.transpose(0, 2, 1), beta.transpose(0, 2, 1))
    o = o.transpose(0, 2, 1, 3)
    o = rmsnorm(o, o_norm_w) * jax.nn.silu(z.astype(jnp.float32).reshape(B, S, GDN_HEADS, GDN_DV))
    return o.reshape(B, S, GDN_V_W).astype(dtype)


def peer_ffn(h, w_q, sub_keys, u_tab, v_tab):
    B, S, D = h.shape
    T = B * S
    hf = h.reshape(T, D)
    q = (hf @ w_q).astype(jnp.float32).reshape(T, PEER_HEADS, 2, PEER_HALF)
    s = jnp.einsum('thpd,hpkd->thpk', q, sub_keys.astype(jnp.float32))
    s1, i1 = lax.top_k(s[:, :, 0], PEER_TOPK)
    s2, i2 = lax.top_k(s[:, :, 1], PEER_TOPK)
    cand = (s1[..., :, None] + s2[..., None, :]).reshape(T, PEER_HEADS, PEER_TOPK * PEER_TOPK)
    cidx = (i1[..., :, None] * PEER_NKEYS + i2[..., None, :]).reshape(T, PEER_HEADS, PEER_TOPK * PEER_TOPK)
    top, pos = lax.top_k(cand, PEER_TOPK)
    eidx = jnp.take_along_axis(cidx, pos, axis=-1).reshape(T, PEER_HEADS * PEER_TOPK)
    gate = jax.nn.softmax(top, axis=-1).reshape(T, PEER_HEADS * PEER_TOPK)
    nb = T // PEER_BLOCK

    def block(args):
        h_b, idx_b, g_b = args
        u_sel = u_tab[idx_b]
        act = jax.nn.gelu(jnp.einsum('tkd,td->tk', u_sel, h_b).astype(jnp.float32), approximate=False)
        coef = (g_b * act).astype(h_b.dtype)
        return jnp.einsum('tk,tkd->td', coef, v_tab[idx_b])

    out = lax.map(block, (hf.reshape(nb, PEER_BLOCK, D),
                          eidx.reshape(nb, PEER_BLOCK, PEER_HEADS * PEER_TOPK),
                          gate.reshape(nb, PEER_BLOCK, PEER_HEADS * PEER_TOPK)))
    return out.reshape(B, S, D)


def setup_inputs(seed: int = 0) -> dict:
    key = jax.random.key(seed)
    ks = jax.random.split(key, 24)
    f32 = jnp.float32
    D = D_MODEL
    nrm = lambda k, shape, s: jax.random.normal(k, shape, f32) * s
    dt = jnp.exp(jax.random.uniform(ks[10], (DEPTH, GDN_HEADS), f32, np.log(1e-3), np.log(1e-1)))
    return {
        "x": nrm(ks[0], (BATCH, SEQ, D), 1.0),
        "c": nrm(ks[1], (BATCH, D), 1.0),
        "w_ada": nrm(ks[2], (DEPTH, D, 6 * D), 0.5 * D ** -0.5),
        "b_ada": nrm(ks[3], (DEPTH, 6 * D), 0.02),
        "norm1_w": 1.0 + nrm(ks[4], (DEPTH, D), 0.02),
        "w_in": nrm(ks[5], (DEPTH, D, IN_W), D ** -0.5),
        "sb_q_norm_w": 1.0 + nrm(ks[6], (DEPTH, SB_HEAD_DIM), 0.02),
        "sb_k_norm_w": 1.0 + nrm(ks[7], (DEPTH, SB_HEAD_DIM), 0.02),
        "gdn_conv_w": nrm(ks[8], (DEPTH, GDN_CONV, GDN_CONV_W), 0.5),
        "gdn_A_log": jnp.log(jax.random.uniform(ks[9], (DEPTH, GDN_HEADS), f32, 1.0, 16.0)),
        "gdn_dt_bias": dt + jnp.log(-jnp.expm1(-dt)),
        "gdn_o_norm_w": 1.0 + nrm(ks[11], (DEPTH, GDN_DV), 0.02),
        "w_proj_sb": nrm(ks[12], (DEPTH, SB_W, D), SB_W ** -0.5),
        "w_proj_gdn": nrm(ks[13], (DEPTH, GDN_V_W, D), GDN_V_W ** -0.5),
        "w_o": nrm(ks[14], (DEPTH, D, D), D ** -0.5),
        "norm2_w": 1.0 + nrm(ks[15], (DEPTH, D), 0.02),
        "peer_w_q": nrm(ks[16], (DEPTH, D, PEER_HEADS * PEER_QDIM), D ** -0.5),
        "peer_sub_keys": nrm(ks[17], (DEPTH, PEER_HEADS, 2, PEER_NKEYS, PEER_HALF), PEER_HALF ** -0.5),
        "peer_u": nrm(ks[18], (DEPTH, PEER_N_EXPERTS, D), D ** -0.5),
        "peer_v": nrm(ks[19], (DEPTH, PEER_N_EXPERTS, D), PEER_HEADS ** -0.5),
    }


def reference(x, c, w_ada, b_ada, norm1_w, w_in, sb_q_norm_w, sb_k_norm_w, gdn_conv_w, gdn_A_log,
              gdn_dt_bias, gdn_o_norm_w, w_proj_sb, w_proj_gdn, w_o, norm2_w, peer_w_q, peer_sub_keys,
              peer_u, peer_v):
    B, S, D = x.shape
    split_at = np.cumsum(IN_WIDTHS)[:-1].tolist()
    for l in range(DEPTH):
        mod = (jax.nn.silu(c) @ w_ada[l] + b_ada[l]).reshape(B, 6, D)
        shift1, scale1, gate1, shift2, scale2, gate2 = [mod[:, i] for i in range(6)]

        h = modulate(rmsnorm(x, norm1_w[l]), shift1, scale1)
        proj = h @ w_in[l]
        sb_q, sb_k, sb_v, gdn_qkv, gdn_a, gdn_b, gdn_z, gate_logits = jnp.split(proj, split_at, axis=-1)
        sb_q = rmsnorm(sb_q.reshape(B, S, SB_HEADS, SB_HEAD_DIM), sb_q_norm_w[l])
        sb_k = rmsnorm(sb_k.reshape(B, S, SB_HEADS, SB_HEAD_DIM), sb_k_norm_w[l])
        sb_v = sb_v.reshape(B, S, SB_HEADS, SB_HEAD_DIM)
        y_sb = stick_breaking_attention(sb_q, sb_k, sb_v)
        y_gdn = gated_deltanet(gdn_qkv, gdn_a, gdn_b, gdn_z, gdn_conv_w[l], gdn_A_log[l],
                               gdn_dt_bias[l], gdn_o_norm_w[l])
        gates = jax.nn.sigmoid(gate_logits.astype(jnp.float32)).astype(x.dtype).reshape(B, S, 2, D)
        merged = gates[:, :, 0] * (y_sb @ w_proj_sb[l]) + gates[:, :, 1] * (y_gdn @ w_proj_gdn[l])
        x = x + gate1[:, None, :] * (merged @ w_o[l])

        h2 = modulate(rmsnorm(x, norm2_w[l]), shift2, scale2)
        x = x + gate2[:, None, :] * peer_ffn(h2, peer_w_q[l], peer_sub_keys[l], peer_u[l], peer_v[l])
    return x
```

```python
import functools

import jax
import jax.numpy as jnp
from jax import lax
from jax.experimental import pallas as pl
from jax.experimental.pallas import tpu as pltpu

F32 = jnp.float32
BF16 = jnp.bfloat16
HI = lax.Precision.HIGHEST
EPS = 1e-6

LANES = 128
HEAD_DIM = 64
GDN_CHUNK = 64
GDN_CONV = 4
PEER_TOPK = 16
NOT_RANKED = 99.0
VMEM_LIMIT = 56 * 1024 * 1024

NT_DIMS = (((1,), (1,)), ((), ()))
TN_DIMS = (((0,), (0,)), ((), ()))


def _cparams(*sem):
    return pltpu.CompilerParams(dimension_semantics=sem, vmem_limit_bytes=VMEM_LIMIT)


def _sigmoid(x):
    return 1.0 / (1.0 + jnp.exp(-x))


def _silu(x):
    return x * _sigmoid(x)


def _softplus(x):
    return jnp.maximum(x, 0.0) + jnp.log(1.0 + jnp.exp(-jnp.abs(x)))


def _block_diag_ones(n, group, value=1.0):
    r = jnp.arange(n) // group
    return jnp.where(r[:, None] == r[None, :], value, 0.0).astype(F32)


def _ada_kernel(c_ref, w_ref, b_ref, o_ref):
    c = c_ref[...]
    o_ref[...] = jnp.dot(_silu(c), w_ref[...], preferred_element_type=F32, precision=HI) + b_ref[...]


def _ada_call(c_pad, w_ada, b_ada):
    rows, d = c_pad.shape
    n = w_ada.shape[1]
    tn = 512
    return pl.pallas_call(
        _ada_kernel,
        grid=(n // tn,),
        in_specs=[pl.BlockSpec((rows, d), lambda j: (0, 0)),
                  pl.BlockSpec((d, tn), lambda j: (0, j)),
                  pl.BlockSpec((1, tn), lambda j: (0, j))],
        out_specs=pl.BlockSpec((rows, tn), lambda j: (0, j)),
        out_shape=jax.ShapeDtypeStruct((rows, n), F32),
        compiler_params=_cparams("arbitrary"),
        name="ada_mod",
    )(c_pad, w_ada, b_ada.reshape(1, n))


def _inproj_kernel(x_ref, mod_ref, n1_ref, w_ref, qkw_ref, bd_ref, o_ref, h_scr, *, n_qk_tiles):
    j = pl.program_id(1)

    @pl.when(j == 0)
    def _():
        x = x_ref[...]
        ms = jnp.mean(x * x, axis=-1, keepdims=True)
        y = x * lax.rsqrt(ms + EPS) * n1_ref[...]
        shift = mod_ref[0, 0:1, :]
        scale = mod_ref[0, 1:2, :]
        h_scr[...] = (y * (1.0 + scale) + shift).astype(BF16)

    acc = jnp.dot(h_scr[...], w_ref[...], preferred_element_type=F32)

    @pl.when(j < n_qk_tiles)
    def _():
        ms = jnp.dot(acc * acc, bd_ref[...], preferred_element_type=F32, precision=HI)
        o_ref[...] = acc * lax.rsqrt(ms + EPS) * qkw_ref[0]

    @pl.when(j >= n_qk_tiles)
    def _():
        o_ref[...] = acc


def _inproj_call(x2d, mod, n1w, w_packed, qkw, seq_len, tm=512, tn=256):
    t, d = x2d.shape
    n = w_packed.shape[1]
    n_qk_tiles = qkw.shape[0]
    bd = _block_diag_ones(tn, HEAD_DIM, 1.0 / HEAD_DIM)
    tiles_per_seq = seq_len // tm
    return pl.pallas_call(
        functools.partial(_inproj_kernel, n_qk_tiles=n_qk_tiles),
        grid=(t // tm, n // tn),
        in_specs=[pl.BlockSpec((tm, d), lambda i, j: (i, 0)),
                  pl.BlockSpec((1, 6, d), lambda i, j: (i // tiles_per_seq, 0, 0)),
                  pl.BlockSpec((1, d), lambda i, j: (0, 0)),
                  pl.BlockSpec((d, tn), lambda i, j: (0, j)),
                  pl.BlockSpec((1, 1, tn), lambda i, j: (jnp.minimum(j, n_qk_tiles - 1), 0, 0)),
                  pl.BlockSpec((tn, tn), lambda i, j: (0, 0))],
        out_specs=pl.BlockSpec((tm, tn), lambda i, j: (i, j)),
        out_shape=jax.ShapeDtypeStruct((t, n), F32),
        scratch_shapes=[pltpu.VMEM((tm, d), BF16)],
        compiler_params=_cparams("arbitrary", "arbitrary"),
        name="in_proj",
    )(x2d, mod, n1w, w_packed, qkw, bd)


def _sb_kernel(q_ref, k_ref, v_ref, tri_ref, o_ref, acc_ref, r_ref, *, tq, tk):
    i = pl.program_id(2)
    lane = lax.broadcasted_iota(jnp.int32, (1, LANES), 1)
    first_half = lane < HEAD_DIM
    q = q_ref[...]
    q_heads = (jnp.where(first_half, q, 0.0).astype(BF16), jnp.where(first_half, 0.0, q).astype(BF16))
    acc_ref[...] = jnp.zeros_like(acc_ref)
    r_ref[...] = jnp.zeros_like(r_ref)
    tri = tri_ref[...]
    row_pos = i * tq + lax.broadcasted_iota(jnp.int32, (tq, tk), 0)
    col_off = lax.broadcasted_iota(jnp.int32, (tq, tk), 1)
    blocks_per_q = tq // tk

    def step(j, masked):
        ks = pl.multiple_of(j * tk, tk)
        kb = k_ref[pl.ds(ks, tk), :].astype(BF16)
        vb = v_ref[pl.ds(ks, tk), :].astype(BF16)
        if masked:
            causal = (ks + col_off) < row_pos
        for hd in range(2):
            z = lax.dot_general(q_heads[hd], kb, NT_DIMS, preferred_element_type=F32)
            sp = _softplus(z)
            if masked:
                sp = jnp.where(causal, sp, 0.0)
            cst = jnp.dot(sp.astype(BF16), tri, preferred_element_type=F32)
            later_in_block = cst[:, :tk]
            block_total = cst[:, tk:]
            r = r_ref[hd]
            a = jnp.exp(z - sp - later_in_block - r)
            if masked:
                a = jnp.where(causal, a, 0.0)
            acc_ref[hd] += jnp.dot(a.astype(BF16), vb, preferred_element_type=F32)
            r_ref[hd] = r + block_total

    last = (i + 1) * blocks_per_q - 1
    for d in range(blocks_per_q):
        step(last - d, True)

    def body(it, carry):
        step(i * blocks_per_q - 1 - it, False)
        return carry

    lax.fori_loop(0, i * blocks_per_q, body, 0)
    o_ref[...] = jnp.where(first_half, acc_ref[0], acc_ref[1])


def _sb_call(proj, batch, seq_len, n_pairs, tq=256, tk=128):
    t = proj.shape[0]
    nq = seq_len // tq
    kk = jnp.arange(tk)
    tri = jnp.concatenate([(kk[:, None] > kk[None, :]).astype(BF16), jnp.ones((tk, tk), BF16)], axis=1)
    return pl.pallas_call(
        functools.partial(_sb_kernel, tq=tq, tk=tk),
        grid=(batch, n_pairs, nq),
        in_specs=[pl.BlockSpec((tq, LANES), lambda b, p, i: (b * nq + i, p)),
                  pl.BlockSpec((seq_len, LANES), lambda b, p, i: (b, n_pairs + p)),
                  pl.BlockSpec((seq_len, LANES), lambda b, p, i: (b, 2 * n_pairs + p)),
                  pl.BlockSpec((tk, 2 * tk), lambda b, p, i: (0, 0))],
        out_specs=pl.BlockSpec((tq, LANES), lambda b, p, i: (b * nq + i, p)),
        out_shape=jax.ShapeDtypeStruct((t, n_pairs * LANES), F32),
        scratch_shapes=[pltpu.VMEM((2, tq, LANES), F32), pltpu.VMEM((2, tq, tk), F32)],
        compiler_params=_cparams("arbitrary", "arbitrary", "arbitrary"),
        name="sb_attention",
    )(proj, proj, proj, tri)


def _gdn_prep_kernel(q_ref, k_ref, v_ref, qt_ref, kt_ref, vt_ref, ab_ref, cw_ref, alog_ref, dtb_ref,
                     ea_ref, eb_ref, lc_ref, jc_ref, bd_ref,
                     qn_o, qd_o, kn_o, kd_o, kb_o, kbg_o, vb_o, cd_o, gx_o, scr, *, tm, tiles_per_seq):
    i = pl.program_id(0)
    keep_tail = (i % tiles_per_seq != 0).astype(F32)
    w = cw_ref[...]
    width = q_ref.shape[1]

    def conv_silu(cur_ref, tail_ref, col0):
        scr[0:8, :] = tail_ref[...] * keep_tail
        scr[8:, :] = cur_ref[...]
        y = jnp.zeros((tm, width), F32)
        for tap in range(GDN_CONV):
            off = 8 - (GDN_CONV - 1) + tap
            y = y + scr[off:off + tm, :] * w[tap:tap + 1, col0:col0 + width]
        return _silu(y)

    bd = bd_ref[...]
    cq = conv_silu(q_ref, qt_ref, 0)
    qn = cq * lax.rsqrt(jnp.dot(cq * cq, bd, preferred_element_type=F32, precision=HI) + EPS) * (HEAD_DIM ** -0.5)
    ck = conv_silu(k_ref, kt_ref, width)
    kn = ck * lax.rsqrt(jnp.dot(ck * ck, bd, preferred_element_type=F32, precision=HI) + EPS)
    cv = conv_silu(v_ref, vt_ref, 2 * width)

    ab = ab_ref[...]
    g = -jnp.exp(alog_ref[...]) * _softplus(ab + dtb_ref[...])
    beta = _sigmoid(ab)
    g_cum = jnp.dot(lc_ref[...], g, preferred_element_type=F32, precision=HI)
    g_tot = jnp.dot(jc_ref[...], g, preferred_element_type=F32, precision=HI)
    ea = ea_ref[...]
    gx = jnp.dot(g_cum, ea, preferred_element_type=F32, precision=HI)
    glx = jnp.dot(g_tot, ea, preferred_element_type=F32, precision=HI)
    bx = jnp.dot(beta, eb_ref[...], preferred_element_type=F32, precision=HI)

    e_g = jnp.exp(gx)
    kb = kn * bx
    qn_o[...] = qn
    qd_o[...] = qn * e_g
    kn_o[...] = kn
    kd_o[...] = kn * jnp.exp(glx - gx)
    kb_o[...] = kb
    kbg_o[...] = kb * e_g
    vb_o[...] = cv * bx
    cd_o[...] = jnp.exp(glx)
    gx_o[...] = gx


def _gdn_prep_call(proj, conv_w, a_log, dt_bias, seq_len, n_heads, col_qkv, col_ab, tm=512):
    t = proj.shape[0]
    width = n_heads * HEAD_DIM
    abw = 2 * LANES
    tiles_per_seq = seq_len // tm
    sub = tm // 8
    alog_pad = jnp.zeros((1, abw), F32).at[0, :n_heads].set(a_log)
    dtb_pad = jnp.zeros((1, abw), F32).at[0, :n_heads].set(dt_bias)
    head_of_lane = jnp.arange(width) // HEAD_DIM
    rows = jnp.arange(abw)
    ea = (rows[:, None] == head_of_lane[None, :]).astype(F32)
    eb = (rows[:, None] == head_of_lane[None, :] + n_heads).astype(F32)
    tok = jnp.arange(tm)
    same_chunk = (tok[:, None] // GDN_CHUNK) == (tok[None, :] // GDN_CHUNK)
    lc = (same_chunk & (tok[:, None] >= tok[None, :])).astype(F32)
    jc = same_chunk.astype(F32)
    bd = _block_diag_ones(width, HEAD_DIM)

    cur = lambda c: pl.BlockSpec((tm, width), lambda i: (i, c))
    tail = lambda c: pl.BlockSpec((8, width), lambda i: (jnp.maximum(i * sub - 1, 0), c))
    full = lambda a: pl.BlockSpec(a.shape, lambda i: (0,) * a.ndim)
    cq, ck, cv = col_qkv
    out_spec = pl.BlockSpec((tm, width), lambda i: (i, 0))
    out_shape = jax.ShapeDtypeStruct((t, width), F32)
    return pl.pallas_call(
        functools.partial(_gdn_prep_kernel, tm=tm, tiles_per_seq=tiles_per_seq),
        grid=(t // tm,),
        in_specs=[cur(cq), cur(ck), cur(cv), tail(cq), tail(ck), tail(cv),
                  pl.BlockSpec((tm, abw), lambda i: (i, col_ab)),
                  full(conv_w), full(alog_pad), full(dtb_pad), full(ea), full(eb), full(lc), full(jc), full(bd)],
        out_specs=[out_spec] * 9,
        out_shape=[out_shape] * 9,
        scratch_shapes=[pltpu.VMEM((tm + 8, width), F32)],
        compiler_params=_cparams("arbitrary"),
        name="gdn_prep",
    )(proj, proj, proj, proj, proj, proj, proj, conv_w, alog_pad, dtb_pad, ea, eb, lc, jc, bd)


def _gdn_core_kernel(qn_ref, qd_ref, kn_ref, kd_ref, kb_ref, kbg_ref, vb_ref, cd_ref, gx_ref,
                     o_ref, s_scr, *, ts, n_heads):
    c_len = GDN_CHUNK

    @pl.when(pl.program_id(1) == 0)
    def _():
        s_scr[...] = jnp.zeros_like(s_scr)

    row = lax.broadcasted_iota(jnp.int32, (c_len, c_len), 0)
    col = lax.broadcasted_iota(jnp.int32, (c_len, c_len), 1)
    strict = row > col
    incl = row >= col
    eye = row == col
    ones = jnp.ones((c_len, c_len), F32)
    n_doublings = (c_len - 1).bit_length() - 1

    def mm(a, b):
        return jnp.dot(a.astype(BF16), b.astype(BF16), preferred_element_type=F32)

    def chunk(c, carry):
        r0 = pl.multiple_of(c * c_len, c_len)
        for pair in range(n_heads // 2):
            lanes = slice(pair * LANES, (pair + 1) * LANES)
            tiles = [ref[pl.ds(r0, c_len), lanes] for ref in
                     (qn_ref, qd_ref, kn_ref, kd_ref, kb_ref, kbg_ref, vb_ref, cd_ref, gx_ref)]
            outs = []
            for half in range(2):
                h = 2 * pair + half
                hl = slice(half * HEAD_DIM, (half + 1) * HEAD_DIM)
                qn, qd, kn, kd, kb, kbg, vb, cd, gx = [x[:, hl] for x in tiles]
                g_row = jnp.dot(ones, jnp.where(eye, gx, 0.0), preferred_element_type=F32, precision=HI)
                decay = jnp.where(incl, jnp.exp(gx - g_row), 0.0)
                kn_b = kn.astype(BF16)
                a_mat = jnp.where(strict, lax.dot_general(kb.astype(BF16), kn_b, NT_DIMS,
                                                          preferred_element_type=F32) * decay, 0.0)
                x_pow = -a_mat
                t_inv = jnp.where(eye, 1.0, x_pow)
                for _ in range(n_doublings):
                    x_pow = mm(x_pow, x_pow)
                    t_inv = t_inv + mm(t_inv, x_pow)
                u = mm(t_inv, vb)
                w = mm(t_inv, kbg)
                qk = lax.dot_general(qn.astype(BF16), kn_b, NT_DIMS, preferred_element_type=F32) * decay
                s = s_scr[h]
                v_new = u - mm(w, s)
                outs.append(mm(qd, s) + mm(qk, v_new))
                s_scr[h] = s * cd + lax.dot_general(kd.astype(BF16), v_new.astype(BF16), TN_DIMS,
                                                    preferred_element_type=F32)
            o_ref[pl.ds(r0, c_len), lanes] = jnp.concatenate(outs, axis=1)
        return carry

    lax.fori_loop(0, ts // c_len, chunk, 0)


def _gdn_core_call(prep, batch, seq_len, n_heads, ts=512):
    t, width = prep[0].shape
    n_seq_tiles = seq_len // ts
    spec = pl.BlockSpec((ts, width), lambda b, s: (b * n_seq_tiles + s, 0))
    return pl.pallas_call(
        functools.partial(_gdn_core_kernel, ts=ts, n_heads=n_heads),
        grid=(batch, n_seq_tiles),
        in_specs=[spec] * 9,
        out_specs=spec,
        out_shape=jax.ShapeDtypeStruct((t, width), F32),
        scratch_shapes=[pltpu.VMEM((n_heads, HEAD_DIM, HEAD_DIM), F32)],
        compiler_params=_cparams("arbitrary", "arbitrary"),
        name="gdn_core",
    )(*prep)


def _merge_kernel(x_ref, ysb_ref, og_ref, z_ref, g0_ref, g1_ref, mod_ref, onw_ref, n2_ref, bd_ref,
                  wsb_ref, wgdn_ref, wo_ref, wq_ref, x1_o, h2_o, pq_o):
    og = og_ref[...]
    ms = jnp.dot(og * og, bd_ref[...], preferred_element_type=F32, precision=HI)
    ygdn = og * lax.rsqrt(ms + EPS) * onw_ref[...] * _silu(z_ref[...])
    m = (_sigmoid(g0_ref[...]) * jnp.dot(ysb_ref[...].astype(BF16), wsb_ref[...], preferred_element_type=F32)
         + _sigmoid(g1_ref[...]) * jnp.dot(ygdn.astype(BF16), wgdn_ref[...], preferred_element_type=F32))
    gate1 = mod_ref[0, 2:3, :]
    shift2 = mod_ref[0, 3:4, :]
    scale2 = mod_ref[0, 4:5, :]
    x1 = x_ref[...] + gate1 * jnp.dot(m.astype(BF16), wo_ref[...], preferred_element_type=F32)
    x1_o[...] = x1
    ms2 = jnp.mean(x1 * x1, axis=-1, keepdims=True)
    h2 = x1 * lax.rsqrt(ms2 + EPS) * n2_ref[...] * (1.0 + scale2) + shift2
    h2b = h2.astype(BF16)
    h2_o[...] = h2b
    pq_o[...] = jnp.dot(h2b, wq_ref[...], preferred_element_type=F32)


def _merge_call(x2d, ysb, ogdn, proj, mod, onw, n2w, wsb, wgdn, wo, wq, seq_len, col_g0, col_g1, col_z, tm=256):
    t, d = x2d.shape
    width = ysb.shape[1]
    nq = wq.shape[1]
    bd = _block_diag_ones(width, HEAD_DIM, 1.0 / HEAD_DIM)
    tiles_per_seq = seq_len // tm
    full = lambda a: pl.BlockSpec(a.shape, lambda i: (0,) * a.ndim)
    return pl.pallas_call(
        _merge_kernel,
        grid=(t // tm,),
        in_specs=[pl.BlockSpec((tm, d), lambda i: (i, 0)),
                  pl.BlockSpec((tm, width), lambda i: (i, 0)),
                  pl.BlockSpec((tm, width), lambda i: (i, 0)),
                  pl.BlockSpec((tm, width), lambda i: (i, col_z)),
                  pl.BlockSpec((tm, d), lambda i: (i, col_g0)),
                  pl.BlockSpec((tm, d), lambda i: (i, col_g1)),
                  pl.BlockSpec((1, 6, d), lambda i: (i // tiles_per_seq, 0, 0)),
                  full(onw), full(n2w), full(bd), full(wsb), full(wgdn), full(wo), full(wq)],
        out_specs=[pl.BlockSpec((tm, d), lambda i: (i, 0)),
                   pl.BlockSpec((tm, d), lambda i: (i, 0)),
                   pl.BlockSpec((tm, nq), lambda i: (i, 0))],
        out_shape=[jax.ShapeDtypeStruct((t, d), F32),
                   jax.ShapeDtypeStruct((t, d), BF16),
                   jax.ShapeDtypeStruct((t, nq), F32)],
        compiler_params=_cparams("arbitrary"),
        name="merge_proj",
    )(x2d, ysb, ogdn, proj, proj, proj, mod, onw, n2w, bd, wsb, wgdn, wo, wq)


def _extract_topk(s, k):
    n = s.shape[0]
    iota = lax.broadcasted_iota(jnp.int32, s.shape, 0).astype(F32)
    rank = jnp.full(s.shape, NOT_RANKED, F32)
    vals, idxs = [], []
    for r in range(k):
        m = jnp.max(s, axis=0, keepdims=True)
        idx = jnp.min(jnp.where(s == m, iota, float(n)), axis=0, keepdims=True)
        hit = iota == idx
        rank = jnp.where(hit, float(r), rank)
        s = jnp.where(hit, -jnp.inf, s)
        vals.append(m)
        idxs.append(idx)
    return vals, idxs, rank


def _route_kernel(pq_ref, keys_ref, rank2_o, e2_o, cnt1_o, w1_o):
    k = PEER_TOPK
    half = keys_ref.shape[3]
    iota_k = lax.broadcasted_iota(jnp.int32, (k, pq_ref.shape[0]), 0).astype(F32)
    scores = [lax.dot_general(keys_ref[0, part], pq_ref[:, part * half:(part + 1) * half], NT_DIMS,
                              preferred_element_type=F32, precision=HI) for part in range(2)]
    v1, _, rank1 = _extract_topk(scores[0], k)
    v2, _, rank2 = _extract_topk(scores[1], k)
    top1 = jnp.concatenate(v1, axis=0)
    top2 = jnp.concatenate(v2, axis=0)
    cand = jnp.concatenate([v1[a] + top2 for a in range(k)], axis=0)
    _, pos, _ = _extract_topk(cand, k)
    count = jnp.zeros_like(top1)
    for r in range(k):
        count = count + (iota_k == jnp.floor(pos[r] * (1.0 / k))).astype(F32)
    e1 = jnp.exp(top1 - v1[0])
    e2 = jnp.exp(top2 - v2[0])
    z = jnp.zeros_like(v1[0])
    for a in range(k):
        z = z + e1[a:a + 1] * jnp.sum(jnp.where(iota_k < count[a:a + 1], e2, 0.0), axis=0, keepdims=True)
    inv_z = 1.0 / z
    cnt1 = jnp.zeros_like(rank1)
    for a in range(k):
        cnt1 = jnp.where(rank1 == float(a), count[a:a + 1], cnt1)
    rank2_o[0] = rank2
    e2_o[0] = jnp.exp(scores[1] - v2[0])
    cnt1_o[0] = cnt1
    w1_o[0] = jnp.where(rank1 < float(k), jnp.exp(scores[0] - v1[0]) * inv_z, 0.0)


def _route_call(pq, sub_keys, tt=256):
    t = pq.shape[0]
    n_heads, _, n_keys, half = sub_keys.shape
    out_spec = pl.BlockSpec((1, n_keys, tt), lambda i, h: (h, 0, i))
    out_shape = jax.ShapeDtypeStruct((n_heads, n_keys, t), F32)
    return pl.pallas_call(
        _route_kernel,
        grid=(t // tt, n_heads),
        in_specs=[pl.BlockSpec((tt, 2 * half), lambda i, h: (i, h)),
                  pl.BlockSpec((1, 2, n_keys, half), lambda i, h: (h, 0, 0, 0))],
        out_specs=[out_spec] * 4,
        out_shape=[out_shape] * 4,
        compiler_params=_cparams("arbitrary", "arbitrary"),
        name="peer_route",
    )(pq, sub_keys)


def _peer_kernel(h2_ref, u_ref, vt_ref, rank2_ref, e2_ref, cnt1_ref, w1_ref, x1_ref, mod_ref,
                 o_ref, act_scr, coef_scr, acc_scr, *, n_heads, n_keys, ec):
    c = pl.program_id(1)

    @pl.when(c == 0)
    def _():
        acc_scr[...] = jnp.zeros_like(acc_scr)

    act_scr[...] = lax.dot_general(u_ref[...], h2_ref[...], NT_DIMS, preferred_element_type=F32)
    subs = ec // n_keys

    def sub_block(sub, carry):
        i1 = c * subs + sub
        gate = None
        for h in range(n_heads):
            cnt = cnt1_ref[h, pl.ds(i1, 1), :]
            w1 = w1_ref[h, pl.ds(i1, 1), :]
            term = jnp.where(rank2_ref[h] < cnt, e2_ref[h], 0.0) * w1
            gate = term if gate is None else gate + term
        rows = pl.ds(pl.multiple_of(sub * n_keys, n_keys), n_keys)
        a = act_scr[rows, :]
        gelu = 0.5 * a * (1.0 + lax.erf(a * (2.0 ** -0.5)))
        coef_scr[rows, :] = (gate * gelu).astype(BF16)
        return carry

    lax.fori_loop(0, subs, sub_block, 0)
    acc_scr[...] += jnp.dot(vt_ref[...], coef_scr[...], preferred_element_type=F32)

    @pl.when(c == pl.num_programs(1) - 1)
    def _():
        gate2 = mod_ref[0, 5:6, :]
        o_ref[...] = x1_ref[...] + gate2 * acc_scr[...].T


def _peer_call(h2, u_b, vt_b, rank2, e2, cnt1, w1, x1, mod, seq_len, tt=512, ec=1024):
    t, d = h2.shape
    n_exp = u_b.shape[0]
    n_heads, n_keys, _ = rank2.shape
    tiles_per_seq = seq_len // tt
    route_spec = pl.BlockSpec((n_heads, n_keys, tt), lambda i, c: (0, 0, i))
    return pl.pallas_call(
        functools.partial(_peer_kernel, n_heads=n_heads, n_keys=n_keys, ec=ec),
        grid=(t // tt, n_exp // ec),
        in_specs=[pl.BlockSpec((tt, d), lambda i, c: (i, 0)),
                  pl.BlockSpec((ec, d), lambda i, c: (c, 0)),
                  pl.BlockSpec((d, ec), lambda i, c: (0, c)),
                  route_spec, route_spec, route_spec, route_spec,
                  pl.BlockSpec((tt, d), lambda i, c: (i, 0)),
                  pl.BlockSpec((1, 6, d), lambda i, c: (i // tiles_per_seq, 0, 0))],
        out_specs=pl.BlockSpec((tt, d), lambda i, c: (i, 0)),
        out_shape=jax.ShapeDtypeStruct((t, d), F32),
        scratch_shapes=[pltpu.VMEM((ec, tt), F32), pltpu.VMEM((ec, tt), BF16), pltpu.VMEM((d, tt), F32)],
        compiler_params=_cparams("arbitrary", "arbitrary"),
        name="peer_experts",
    )(h2, u_b, vt_b, rank2, e2, cnt1, w1, x1, mod)


def _pack_in_proj(w_in, sb_w, gdn_qk_w, gdn_v_w, n_gdn_heads, d_model):
    o_sbq, o_sbk, o_sbv = 0, sb_w, 2 * sb_w
    o_gdn = 3 * sb_w
    conv_w = 2 * gdn_qk_w + gdn_v_w
    o_a = o_gdn + conv_w
    o_b = o_a + n_gdn_heads
    o_z = o_b + n_gdn_heads
    o_gate = o_z + gdn_v_w
    pad = 2 * LANES - 2 * n_gdn_heads
    packed = jnp.concatenate([
        w_in[:, o_sbq:o_a],
        w_in[:, o_gate:o_gate + 2 * d_model],
        w_in[:, o_z:o_z + gdn_v_w],
        w_in[:, o_a:o_z],
        jnp.zeros((w_in.shape[0], pad), w_in.dtype)], axis=1).astype(BF16)
    return packed


def _block(x2d, c, w_ada, b_ada, norm1_w, w_in, sb_q_norm_w, sb_k_norm_w, gdn_conv_w, gdn_A_log,
           gdn_dt_bias, gdn_o_norm_w, w_proj_sb, w_proj_gdn, w_o, norm2_w, peer_w_q, peer_sub_keys,
           peer_u, peer_v, batch, seq_len):
    t, d = x2d.shape
    sb_w = w_proj_sb.shape[0]
    gdn_v_w = w_proj_gdn.shape[0]
    gdn_qk_w = (gdn_conv_w.shape[1] - gdn_v_w) // 2
    n_sb_heads = sb_w // HEAD_DIM
    n_gdn_heads = gdn_v_w // HEAD_DIM
    assert gdn_qk_w == gdn_v_w == sb_w and d % (4 * LANES) == 0

    c_pad = jnp.zeros((8, d), F32).at[:batch].set(c)
    mod = _ada_call(c_pad, w_ada, b_ada)[:batch].reshape(batch, 6, d)

    w_packed = _pack_in_proj(w_in, sb_w, gdn_qk_w, gdn_v_w, n_gdn_heads, d)
    tn = 2 * LANES
    heads_per_tile = tn // HEAD_DIM
    q_tiles = sb_w // tn
    qkw = jnp.concatenate([jnp.tile(sb_q_norm_w * (HEAD_DIM ** -0.5), (q_tiles, heads_per_tile)),
                           jnp.tile(sb_k_norm_w, (q_tiles, heads_per_tile))], axis=0).reshape(2 * q_tiles, 1, tn)
    proj = _inproj_call(x2d, mod, norm1_w.reshape(1, d), w_packed, qkw, seq_len, tn=tn)

    ysb = _sb_call(proj, batch, seq_len, n_sb_heads // 2)

    col_gdn = 3 * sb_w // gdn_v_w
    col_gate = (3 * sb_w + 3 * gdn_v_w) // d
    col_z = (3 * sb_w + 3 * gdn_v_w + 2 * d) // gdn_v_w
    col_ab = (3 * sb_w + 4 * gdn_v_w + 2 * d) // (2 * LANES)
    prep = _gdn_prep_call(proj, gdn_conv_w, gdn_A_log, gdn_dt_bias, seq_len, n_gdn_heads,
                          (col_gdn, col_gdn + 1, col_gdn + 2), col_ab)
    ogdn = _gdn_core_call(prep, batch, seq_len, n_gdn_heads)

    x1, h2, pq = _merge_call(
        x2d, ysb, ogdn, proj, mod, jnp.tile(gdn_o_norm_w, n_gdn_heads).reshape(1, gdn_v_w), norm2_w.reshape(1, d),
        w_proj_sb.astype(BF16), w_proj_gdn.astype(BF16), w_o.astype(BF16), peer_w_q.astype(BF16),
        seq_len, col_gate, col_gate + 1, col_z)

    rank2, e2, cnt1, w1 = _route_call(pq, peer_sub_keys)
    return _peer_call(h2, peer_u.astype(BF16), peer_v.T.astype(BF16), rank2, e2, cnt1, w1, x1, mod, seq_len)


def kernel(x, c, w_ada, b_ada, norm1_w, w_in, sb_q_norm_w, sb_k_norm_w, gdn_conv_w, gdn_A_log, gdn_dt_bias,
           gdn_o_norm_w, w_proj_sb, w_proj_gdn, w_o, norm2_w, peer_w_q, peer_sub_keys, peer_u, peer_v):
    batch, seq_len, d = x.shape
    x2d = x.reshape(batch * seq_len, d)
    for l in range(w_ada.shape[0]):
        x2d = _block(x2d, c, w_ada[l], b_ada[l], norm1_w[l], w_in[l], sb_q_norm_w[l], sb_k_norm_w[l],
                     gdn_conv_w[l], gdn_A_log[l], gdn_dt_bias[l], gdn_o_norm_w[l], w_proj_sb[l], w_proj_gdn[l],
                     w_o[l], norm2_w[l], peer_w_q[l], peer_sub_keys[l], peer_u[l], peer_v[l], batch, seq_len)
    return x2d.reshape(batch, seq_len, d)
```

```python
import functools

import jax
import jax.numpy as jnp
from jax import lax
from jax.experimental import pallas as pl
from jax.experimental.pallas import tpu as pltpu

F32 = jnp.float32
BF16 = jnp.bfloat16
HI = lax.Precision.HIGHEST
EPS = 1e-6
LOG2E = 1.4426950408889634

LANES = 128
HEAD_DIM = 64
GDN_BLOCK = 256
GDN_CONV = 4
PEER_TOPK = 16
NOT_RANKED = 99.0
VMEM_LIMIT = 56 * 1024 * 1024

NT_DIMS = (((1,), (1,)), ((), ()))
TN_DIMS = (((0,), (0,)), ((), ()))


def _cparams(*sem):
    return pltpu.CompilerParams(dimension_semantics=sem, vmem_limit_bytes=VMEM_LIMIT)


def _sigmoid(x):
    return 1.0 / (1.0 + jnp.exp(-x))


def _silu(x):
    return x * _sigmoid(x)


def _softplus(x):
    return jnp.maximum(x, 0.0) + jnp.log(1.0 + jnp.exp(-jnp.abs(x)))


def _block_diag_ones(n, group, value=1.0):
    r = jnp.arange(n) // group
    return jnp.where(r[:, None] == r[None, :], value, 0.0).astype(F32)


def _ada_kernel(c_ref, w_ref, b_ref, o_ref):
    c = c_ref[...]
    o_ref[...] = jnp.dot(_silu(c), w_ref[...], preferred_element_type=F32, precision=HI) + b_ref[...]


def _ada_call(c_pad, w_ada, b_ada):
    rows, d = c_pad.shape
    n = w_ada.shape[1]
    tn = 512
    return pl.pallas_call(
        _ada_kernel,
        grid=(n // tn,),
        in_specs=[pl.BlockSpec((rows, d), lambda j: (0, 0)),
                  pl.BlockSpec((d, tn), lambda j: (0, j)),
                  pl.BlockSpec((1, tn), lambda j: (0, j))],
        out_specs=pl.BlockSpec((rows, tn), lambda j: (0, j)),
        out_shape=jax.ShapeDtypeStruct((rows, n), F32),
        compiler_params=_cparams("arbitrary"),
        name="ada_mod",
    )(c_pad, w_ada, b_ada.reshape(1, n))


def _inproj_kernel(x_ref, mod_ref, n1_ref, w_ref, qkw_ref, bd_ref, o_ref, h_scr, *, n_qk_tiles):
    j = pl.program_id(1)

    @pl.when(j == 0)
    def _():
        x = x_ref[...]
        ms = jnp.mean(x * x, axis=-1, keepdims=True)
        y = x * lax.rsqrt(ms + EPS) * n1_ref[...]
        shift = mod_ref[0, 0:1, :]
        scale = mod_ref[0, 1:2, :]
        h_scr[...] = (y * (1.0 + scale) + shift).astype(BF16)

    acc = jnp.dot(h_scr[...], w_ref[...], preferred_element_type=F32)

    @pl.when(j < n_qk_tiles)
    def _():
        ms = jnp.dot(acc * acc, bd_ref[...], preferred_element_type=F32, precision=HI)
        o_ref[...] = acc * lax.rsqrt(ms + EPS) * qkw_ref[0]

    @pl.when(j >= n_qk_tiles)
    def _():
        o_ref[...] = acc


def _inproj_call(x2d, mod, n1w, w_packed, qkw, seq_len, tm=512, tn=256):
    t, d = x2d.shape
    n = w_packed.shape[1]
    n_qk_tiles = qkw.shape[0]
    bd = _block_diag_ones(tn, HEAD_DIM, 1.0 / HEAD_DIM)
    tiles_per_seq = seq_len // tm
    return pl.pallas_call(
        functools.partial(_inproj_kernel, n_qk_tiles=n_qk_tiles),
        grid=(t // tm, n // tn),
        in_specs=[pl.BlockSpec((tm, d), lambda i, j: (i, 0)),
                  pl.BlockSpec((1, 6, d), lambda i, j: (i // tiles_per_seq, 0, 0)),
                  pl.BlockSpec((1, d), lambda i, j: (0, 0)),
                  pl.BlockSpec((d, tn), lambda i, j: (0, j)),
                  pl.BlockSpec((1, 1, tn), lambda i, j: (jnp.minimum(j, n_qk_tiles - 1), 0, 0)),
                  pl.BlockSpec((tn, tn), lambda i, j: (0, 0))],
        out_specs=pl.BlockSpec((tm, tn), lambda i, j: (i, j)),
        out_shape=jax.ShapeDtypeStruct((t, n), F32),
        scratch_shapes=[pltpu.VMEM((tm, d), BF16)],
        compiler_params=_cparams("arbitrary", "arbitrary"),
        name="in_proj",
    )(x2d, mod, n1w, w_packed, qkw, bd)


def _sb_kernel(q_ref, k_ref, v_ref, wlo_ref, whi_ref, o_ref, acc_ref, r_ref, *, tq):
    i = pl.program_id(2)
    half = whi_ref.shape[0]
    lane = lax.broadcasted_iota(jnp.int32, (1, LANES), 1)
    first_half = lane < HEAD_DIM
    q = q_ref[...]
    q_heads = (jnp.where(first_half, q, 0.0).astype(BF16), jnp.where(first_half, 0.0, q).astype(BF16))
    acc_ref[...] = jnp.zeros_like(acc_ref)
    r_ref[...] = jnp.zeros_like(r_ref)

    def step(j, masked):
        ks = pl.multiple_of(j * tq, tq)
        kb = k_ref[pl.ds(ks, tq), :].astype(BF16)
        vb = v_ref[pl.ds(ks, tq), :].astype(BF16)
        if masked:
            causal = (lax.broadcasted_iota(jnp.int32, (tq, tq), 1) < lax.broadcasted_iota(jnp.int32, (tq, tq), 0))
        for hd in range(2):
            y = lax.dot_general(q_heads[hd], kb, NT_DIMS, preferred_element_type=F32)
            neg_abs = lax.bitcast_convert_type(lax.bitcast_convert_type(y, jnp.uint32) | jnp.uint32(0x80000000), F32)
            sp = jnp.maximum(y, 0.0) + jnp.log(1.0 + jnp.exp2(neg_abs)) * LOG2E
            if masked:
                sp = jnp.where(causal, sp, 0.0)
            spb = sp.astype(BF16)
            later_lo = jnp.dot(spb, wlo_ref[...], preferred_element_type=F32)
            later_hi = jnp.dot(spb[:, half:], whi_ref[...], preferred_element_type=F32)
            a = jnp.exp2(y - sp - jnp.concatenate([later_lo, later_hi], axis=1))
            if masked:
                a = jnp.where(causal, a, 0.0)
            pv = jnp.dot(a.astype(BF16), vb, preferred_element_type=F32)
            r = r_ref[hd]
            acc_ref[hd] += jnp.exp2(-r) * pv
            r_ref[hd] = r + (later_lo[:, 0:1] + sp[:, 0:1])

    step(i, True)

    def body(it, carry):
        step(i - 1 - it, False)
        return carry

    lax.fori_loop(0, i, body, 0)
    o_ref[...] = jnp.where(first_half, acc_ref[0], acc_ref[1])


def _sb_call(proj, batch, seq_len, n_pairs, tq=512):
    t = proj.shape[0]
    nq = seq_len // tq
    half = tq // 2
    kk = jnp.arange(tq)
    later = (kk[:, None] > kk[None, :]).astype(BF16)
    w_lo = later[:, :half]
    w_hi = later[half:, half:]
    return pl.pallas_call(
        functools.partial(_sb_kernel, tq=tq),
        grid=(batch, n_pairs, nq),
        in_specs=[pl.BlockSpec((tq, LANES), lambda b, p, i: (b * nq + i, p)),
                  pl.BlockSpec((seq_len, LANES), lambda b, p, i: (b, n_pairs + p)),
                  pl.BlockSpec((seq_len, LANES), lambda b, p, i: (b, 2 * n_pairs + p)),
                  pl.BlockSpec((tq, half), lambda b, p, i: (0, 0)),
                  pl.BlockSpec((half, half), lambda b, p, i: (0, 0))],
        out_specs=pl.BlockSpec((tq, LANES), lambda b, p, i: (b * nq + i, p)),
        out_shape=jax.ShapeDtypeStruct((t, n_pairs * LANES), F32),
        scratch_shapes=[pltpu.VMEM((2, tq, LANES), F32), pltpu.VMEM((2, tq, 1), F32)],
        compiler_params=_cparams("arbitrary", "arbitrary", "arbitrary"),
        name="sb_attention",
    )(proj, proj, proj, w_lo, w_hi)


def _gdn_prep_kernel(q_ref, k_ref, v_ref, qt_ref, kt_ref, vt_ref, ab_ref, cw_ref, alog_ref, dtb_ref,
                     ea_ref, eb_ref, lc_ref, jc_ref, bd_ref,
                     qn_o, qd_o, kn_o, kd_o, kb_o, kbg_o, vb_o, gcol_o, grow_o, scr, *, tm, tiles_per_seq):
    i = pl.program_id(0)
    keep_tail = (i % tiles_per_seq != 0).astype(F32)
    w = cw_ref[...]
    width = q_ref.shape[1]

    def conv_silu(cur_ref, tail_ref, col0):
        scr[0:8, :] = tail_ref[...] * keep_tail
        scr[8:, :] = cur_ref[...]
        y = jnp.zeros((tm, width), F32)
        for tap in range(GDN_CONV):
            off = 8 - (GDN_CONV - 1) + tap
            y = y + scr[off:off + tm, :] * w[tap:tap + 1, col0:col0 + width]
        return _silu(y)

    bd = bd_ref[...]
    cq = conv_silu(q_ref, qt_ref, 0)
    qn = cq * lax.rsqrt(jnp.dot(cq * cq, bd, preferred_element_type=F32, precision=HI) + EPS) * (HEAD_DIM ** -0.5)
    ck = conv_silu(k_ref, kt_ref, width)
    kn = ck * lax.rsqrt(jnp.dot(ck * ck, bd, preferred_element_type=F32, precision=HI) + EPS)
    cv = conv_silu(v_ref, vt_ref, 2 * width)

    ab = ab_ref[...]
    g = -jnp.exp(alog_ref[...]) * _softplus(ab + dtb_ref[...])
    beta = _sigmoid(ab)
    g_cum = jnp.dot(lc_ref[...], g, preferred_element_type=F32, precision=HI)
    g_tot = jnp.dot(jc_ref[...], g, preferred_element_type=F32, precision=HI)
    ea = ea_ref[...]
    gx = jnp.dot(g_cum, ea, preferred_element_type=F32, precision=HI)
    glx = jnp.dot(g_tot, ea, preferred_element_type=F32, precision=HI)
    bx = jnp.dot(beta, eb_ref[...], preferred_element_type=F32, precision=HI)

    e_g = jnp.exp(gx)
    kb = kn * bx
    qn_o[...] = qn
    qd_o[...] = qn * e_g
    kn_o[...] = kn
    kd_o[...] = kn * jnp.exp(glx - gx)
    kb_o[...] = kb
    kbg_o[...] = kb * e_g
    vb_o[...] = cv * bx
    g_heads = g_cum[:, :LANES]
    gcol_o[...] = g_heads
    grow_o[...] = g_heads.T[0:8, :]


def _gdn_prep_call(proj, conv_w, a_log, dt_bias, seq_len, n_heads, col_qkv, col_ab, chunk, tm=512):
    t = proj.shape[0]
    assert n_heads <= 8 and tm % chunk == 0
    width = n_heads * HEAD_DIM
    abw = 2 * LANES
    tiles_per_seq = seq_len // tm
    sub = tm // 8
    alog_pad = jnp.zeros((1, abw), F32).at[0, :n_heads].set(a_log)
    dtb_pad = jnp.zeros((1, abw), F32).at[0, :n_heads].set(dt_bias)
    head_of_lane = jnp.arange(width) // HEAD_DIM
    rows = jnp.arange(abw)
    ea = (rows[:, None] == head_of_lane[None, :]).astype(F32)
    eb = (rows[:, None] == head_of_lane[None, :] + n_heads).astype(F32)
    tok = jnp.arange(tm)
    same_chunk = (tok[:, None] // chunk) == (tok[None, :] // chunk)
    lc = (same_chunk & (tok[:, None] >= tok[None, :])).astype(F32)
    jc = same_chunk.astype(F32)
    bd = _block_diag_ones(width, HEAD_DIM)

    cur = lambda c: pl.BlockSpec((tm, width), lambda i: (i, c))
    tail = lambda c: pl.BlockSpec((8, width), lambda i: (jnp.maximum(i * sub - 1, 0), c))
    full = lambda a: pl.BlockSpec(a.shape, lambda i: (0,) * a.ndim)
    cq, ck, cv = col_qkv
    out_spec = pl.BlockSpec((tm, width), lambda i: (i, 0))
    out_shape = jax.ShapeDtypeStruct((t, width), F32)
    return pl.pallas_call(
        functools.partial(_gdn_prep_kernel, tm=tm, tiles_per_seq=tiles_per_seq),
        grid=(t // tm,),
        in_specs=[cur(cq), cur(ck), cur(cv), tail(cq), tail(ck), tail(cv),
                  pl.BlockSpec((tm, abw), lambda i: (i, col_ab)),
                  full(conv_w), full(alog_pad), full(dtb_pad), full(ea), full(eb), full(lc), full(jc), full(bd)],
        out_specs=[out_spec] * 7 + [pl.BlockSpec((tm, LANES), lambda i: (i, 0)), pl.BlockSpec((8, tm), lambda i: (0, i))],
        out_shape=[out_shape] * 7 + [jax.ShapeDtypeStruct((t, LANES), F32), jax.ShapeDtypeStruct((8, t), F32)],
        scratch_shapes=[pltpu.VMEM((tm + 8, width), F32)],
        compiler_params=_cparams("arbitrary"),
        name="gdn_prep",
    )(proj, proj, proj, proj, proj, proj, proj, conv_w, alog_pad, dtb_pad, ea, eb, lc, jc, bd)


def _gdn_core_kernel(qn_ref, qd_ref, kn_ref, kd_ref, kb_ref, kbg_ref, vb_ref, gcol_ref, grow_ref,
                     o_ref, s_scr, *, ts, n_heads, c_len):
    @pl.when(pl.program_id(1) == 0)
    def _():
        s_scr[...] = jnp.zeros_like(s_scr)

    row = lax.broadcasted_iota(jnp.int32, (c_len, c_len), 0)
    col = lax.broadcasted_iota(jnp.int32, (c_len, c_len), 1)
    strict = row > col
    incl = row >= col
    eye_state = (lax.broadcasted_iota(jnp.int32, (HEAD_DIM, HEAD_DIM), 0)
                 == lax.broadcasted_iota(jnp.int32, (HEAD_DIM, HEAD_DIM), 1))
    n_levels = (c_len - 1).bit_length()
    n_chunks = ts // c_len
    problems = [(c, h) for c in range(n_chunks) for h in range(n_heads)]

    def bf(x):
        return x.astype(BF16)

    def mm(a, b):
        return jnp.dot(a, b, preferred_element_type=F32)

    def head_tile(ref, c, h):
        pair, half = divmod(h, 2)
        tile = ref[c * c_len:(c + 1) * c_len, pair * LANES:(pair + 1) * LANES]
        return tile[:, half * HEAD_DIM:(half + 1) * HEAD_DIM]

    sibling = [((row >> k) ^ (col >> k)) == 1 for k in range(n_levels)]

    xs, ms, rs, qks = [], [], [], []
    for c, h in problems:
        g_col = gcol_ref[c * c_len:(c + 1) * c_len, h:h + 1]
        g_row = grow_ref[h:h + 1, c * c_len:(c + 1) * c_len]
        decay = jnp.where(incl, jnp.exp(g_col - g_row), 0.0)
        kn_b = bf(head_tile(kn_ref, c, h))
        kk = lax.dot_general(bf(head_tile(kb_ref, c, h)), kn_b, NT_DIMS, preferred_element_type=F32)
        x = jnp.where(strict, -(kk * decay), 0.0)
        xs.append(x)
        ms.append(jnp.where(row == col, 1.0, jnp.where(sibling[0], x, 0.0)))
        qks.append(bf(lax.dot_general(bf(head_tile(qn_ref, c, h)), kn_b, NT_DIMS, preferred_element_type=F32) * decay))
        rs.append(bf(jnp.concatenate([head_tile(vb_ref, c, h), head_tile(kbg_ref, c, h)], axis=1)))

    for k in range(1, n_levels):
        for idx in range(len(problems)):
            m_b = bf(ms[idx])
            left = mm(m_b, bf(jnp.where(sibling[k], xs[idx], 0.0)))
            ms[idx] = ms[idx] + mm(bf(left), m_b)

    q_eff, o_zero, p_mat, b_mat = {}, {}, {}, {}
    for idx, (c, h) in enumerate(problems):
        sol = bf(mm(bf(ms[idx]), rs[idx]))
        m1 = mm(qks[idx], sol)
        q_eff[c, h] = bf(head_tile(qd_ref, c, h) - m1[:, HEAD_DIM:])
        o_zero[c, h] = m1[:, :HEAD_DIM]
        m2 = lax.dot_general(bf(head_tile(kd_ref, c, h)), sol, TN_DIMS, preferred_element_type=F32)
        chunk_decay = jnp.exp(gcol_ref[(c + 1) * c_len - 1:(c + 1) * c_len, h:h + 1])
        p_mat[c, h] = bf(jnp.where(eye_state, chunk_decay, 0.0) - m2[:, HEAD_DIM:])
        b_mat[c, h] = m2[:, :HEAD_DIM]

    states = [s_scr[h] for h in range(n_heads)]
    for c in range(n_chunks):
        outs = []
        for h in range(n_heads):
            s_b = bf(states[h])
            outs.append(mm(q_eff[c, h], s_b) + o_zero[c, h])
            states[h] = mm(p_mat[c, h], s_b) + b_mat[c, h]
        o_ref[c * c_len:(c + 1) * c_len, :] = jnp.concatenate(outs, axis=1)
    for h in range(n_heads):
        s_scr[h] = states[h]


def _gdn_core_call(prep, batch, seq_len, n_heads, c_len, ts=512):
    t, width = prep[0].shape
    n_seq_tiles = seq_len // ts
    spec = pl.BlockSpec((ts, width), lambda b, s: (b * n_seq_tiles + s, 0))
    return pl.pallas_call(
        functools.partial(_gdn_core_kernel, ts=ts, n_heads=n_heads, c_len=c_len),
        grid=(batch, n_seq_tiles),
        in_specs=[spec] * 7 + [pl.BlockSpec((ts, LANES), lambda b, s: (b * n_seq_tiles + s, 0)),
                               pl.BlockSpec((8, ts), lambda b, s: (0, b * n_seq_tiles + s))],
        out_specs=spec,
        out_shape=jax.ShapeDtypeStruct((t, width), F32),
        scratch_shapes=[pltpu.VMEM((n_heads, HEAD_DIM, HEAD_DIM), F32)],
        compiler_params=_cparams("arbitrary", "arbitrary"),
        name="gdn_core",
    )(*prep)


def _merge_kernel(x_ref, ysb_ref, og_ref, z_ref, g0_ref, g1_ref, mod_ref, onw_ref, n2_ref, bd_ref,
                  wsb_ref, wgdn_ref, wo_ref, wq_ref, x1_o, h2_o, pq_o):
    og = og_ref[...]
    ms = jnp.dot(og * og, bd_ref[...], preferred_element_type=F32, precision=HI)
    ygdn = og * lax.rsqrt(ms + EPS) * onw_ref[...] * _silu(z_ref[...])
    m = (_sigmoid(g0_ref[...]) * jnp.dot(ysb_ref[...].astype(BF16), wsb_ref[...], preferred_element_type=F32)
         + _sigmoid(g1_ref[...]) * jnp.dot(ygdn.astype(BF16), wgdn_ref[...], preferred_element_type=F32))
    gate1 = mod_ref[0, 2:3, :]
    shift2 = mod_ref[0, 3:4, :]
    scale2 = mod_ref[0, 4:5, :]
    x1 = x_ref[...] + gate1 * jnp.dot(m.astype(BF16), wo_ref[...], preferred_element_type=F32)
    x1_o[...] = x1
    ms2 = jnp.mean(x1 * x1, axis=-1, keepdims=True)
    h2 = x1 * lax.rsqrt(ms2 + EPS) * n2_ref[...] * (1.0 + scale2) + shift2
    h2b = h2.astype(BF16)
    h2_o[...] = h2b
    pq_o[...] = jnp.dot(h2b, wq_ref[...], preferred_element_type=F32)


def _merge_call(x2d, ysb, ogdn, proj, mod, onw, n2w, wsb, wgdn, wo, wq, seq_len, col_g0, col_g1, col_z, tm=256):
    t, d = x2d.shape
    width = ysb.shape[1]
    nq = wq.shape[1]
    bd = _block_diag_ones(width, HEAD_DIM, 1.0 / HEAD_DIM)
    tiles_per_seq = seq_len // tm
    full = lambda a: pl.BlockSpec(a.shape, lambda i: (0,) * a.ndim)
    return pl.pallas_call(
        _merge_kernel,
        grid=(t // tm,),
        in_specs=[pl.BlockSpec((tm, d), lambda i: (i, 0)),
                  pl.BlockSpec((tm, width), lambda i: (i, 0)),
                  pl.BlockSpec((tm, width), lambda i: (i, 0)),
                  pl.BlockSpec((tm, width), lambda i: (i, col_z)),
                  pl.BlockSpec((tm, d), lambda i: (i, col_g0)),
                  pl.BlockSpec((tm, d), lambda i: (i, col_g1)),
                  pl.BlockSpec((1, 6, d), lambda i: (i // tiles_per_seq, 0, 0)),
                  full(onw), full(n2w), full(bd), full(wsb), full(wgdn), full(wo), full(wq)],
        out_specs=[pl.BlockSpec((tm, d), lambda i: (i, 0)),
                   pl.BlockSpec((tm, d), lambda i: (i, 0)),
                   pl.BlockSpec((tm, nq), lambda i: (i, 0))],
        out_shape=[jax.ShapeDtypeStruct((t, d), F32),
                   jax.ShapeDtypeStruct((t, d), BF16),
                   jax.ShapeDtypeStruct((t, nq), F32)],
        compiler_params=_cparams("arbitrary"),
        name="merge_proj",
    )(x2d, ysb, ogdn, proj, proj, proj, mod, onw, n2w, bd, wsb, wgdn, wo, wq)


def _extract_topk(s, k):
    n = s.shape[0]
    iota = lax.broadcasted_iota(jnp.int32, s.shape, 0).astype(F32)
    rank = jnp.full(s.shape, NOT_RANKED, F32)
    vals, idxs = [], []
    for r in range(k):
        m = jnp.max(s, axis=0, keepdims=True)
        idx = jnp.min(jnp.where(s == m, iota, float(n)), axis=0, keepdims=True)
        hit = iota == idx
        rank = jnp.where(hit, float(r), rank)
        s = jnp.where(hit, -jnp.inf, s)
        vals.append(m)
        idxs.append(idx)
    return vals, idxs, rank


def _candidate_tables(k):
    pairs = [(a, b) for a in range(k) for b in range(k) if (a + 1) * (b + 1) <= k]
    n_pad = -(-len(pairs) // 8) * 8
    sel_a = jnp.zeros((n_pad, k), F32).at[jnp.arange(len(pairs)), jnp.array([a for a, _ in pairs])].set(1.0)
    sel_b = jnp.zeros((n_pad, k), F32).at[jnp.arange(len(pairs)), jnp.array([b for _, b in pairs])].set(1.0)
    return sel_a, sel_b, len(pairs)


def _route_kernel(pq_ref, keys_ref, sela_ref, selb_ref, rank2_o, e2_o, cnt1_o, w1_o, *, n_cand):
    k = PEER_TOPK
    half = keys_ref.shape[3]
    tt = pq_ref.shape[0]
    iota_k = lax.broadcasted_iota(jnp.int32, (k, tt), 0).astype(F32)
    scores = [lax.dot_general(keys_ref[0, part], pq_ref[:, part * half:(part + 1) * half], NT_DIMS,
                              preferred_element_type=F32, precision=HI) for part in range(2)]
    v1, _, rank1 = _extract_topk(scores[0], k)
    v2, _, rank2 = _extract_topk(scores[1], k)
    top1 = jnp.concatenate(v1, axis=0)
    top2 = jnp.concatenate(v2, axis=0)
    sel_a = sela_ref[...]
    cand = (jnp.dot(sel_a, top1, preferred_element_type=F32, precision=HI)
            + jnp.dot(selb_ref[...], top2, preferred_element_type=F32, precision=HI))
    cand_row = lax.broadcasted_iota(jnp.int32, cand.shape, 0)
    _, _, cand_rank = _extract_topk(jnp.where(cand_row < n_cand, cand, -jnp.inf), k)
    chosen = (cand_rank < float(k)).astype(BF16)
    count = lax.dot_general(sel_a.astype(BF16), chosen, TN_DIMS, preferred_element_type=F32)
    e1 = jnp.exp(top1 - v1[0])
    e2 = jnp.exp(top2 - v2[0])
    z = jnp.zeros_like(v1[0])
    for a in range(k):
        z = z + e1[a:a + 1] * jnp.sum(jnp.where(iota_k < count[a:a + 1], e2, 0.0), axis=0, keepdims=True)
    inv_z = 1.0 / z
    cnt1 = jnp.zeros_like(rank1)
    for a in range(k):
        cnt1 = jnp.where(rank1 == float(a), count[a:a + 1], cnt1)
    rank2_o[0] = rank2.astype(BF16)
    e2_o[0] = jnp.exp(scores[1] - v2[0]).astype(BF16)
    cnt1_o[0] = cnt1
    w1_o[0] = jnp.where(rank1 < float(k), jnp.exp(scores[0] - v1[0]) * inv_z, 0.0)


def _route_call(pq, sub_keys, tt=256):
    t = pq.shape[0]
    n_heads, _, n_keys, half = sub_keys.shape
    sel_a, sel_b, n_cand = _candidate_tables(PEER_TOPK)
    out_spec = pl.BlockSpec((1, n_keys, tt), lambda i, h: (h, 0, i))
    shape = lambda dt: jax.ShapeDtypeStruct((n_heads, n_keys, t), dt)
    return pl.pallas_call(
        functools.partial(_route_kernel, n_cand=n_cand),
        grid=(t // tt, n_heads),
        in_specs=[pl.BlockSpec((tt, 2 * half), lambda i, h: (i, h)),
                  pl.BlockSpec((1, 2, n_keys, half), lambda i, h: (h, 0, 0, 0)),
                  pl.BlockSpec(sel_a.shape, lambda i, h: (0, 0)),
                  pl.BlockSpec(sel_b.shape, lambda i, h: (0, 0))],
        out_specs=[out_spec] * 4,
        out_shape=[shape(BF16), shape(BF16), shape(F32), shape(F32)],
        compiler_params=_cparams("arbitrary", "arbitrary"),
        name="peer_route",
    )(pq, sub_keys, sel_a, sel_b)


def _peer_kernel(h2_ref, u_ref, vt_ref, rank2_ref, e2_ref, cnt1_ref, w1_ref, x1_ref, mod_ref,
                 o_ref, acc_scr, *, n_heads, n_keys, ec):
    c = pl.program_id(1)

    @pl.when(c == 0)
    def _():
        acc_scr[...] = jnp.zeros_like(acc_scr)

    act = lax.dot_general(u_ref[...], h2_ref[...], NT_DIMS, preferred_element_type=F32)
    subs = ec // n_keys
    zero = jnp.zeros((), BF16)
    coefs = []
    for sub in range(subs):
        i1 = c * subs + sub
        gate = None
        for h in range(n_heads):
            cnt = cnt1_ref[h, pl.ds(i1, 1), :].astype(BF16)
            w1 = w1_ref[h, pl.ds(i1, 1), :].astype(BF16)
            term = jnp.where(rank2_ref[h] < cnt, e2_ref[h], zero) * w1
            gate = term if gate is None else gate + term
        a = act[sub * n_keys:(sub + 1) * n_keys, :]
        gelu = 0.5 * a * (1.0 + lax.erf(a * (2.0 ** -0.5)))
        coefs.append(gate * gelu.astype(BF16))
    acc_scr[...] += jnp.dot(vt_ref[...], jnp.concatenate(coefs, axis=0), preferred_element_type=F32)

    @pl.when(c == pl.num_programs(1) - 1)
    def _():
        gate2 = mod_ref[0, 5:6, :]
        o_ref[...] = x1_ref[...] + gate2 * acc_scr[...].T


def _peer_call(h2, u_b, vt_b, rank2, e2, cnt1, w1, x1, mod, seq_len, tt=512, ec=1024):
    t, d = h2.shape
    n_exp = u_b.shape[0]
    n_heads, n_keys, _ = rank2.shape
    tiles_per_seq = seq_len // tt
    route_spec = pl.BlockSpec((n_heads, n_keys, tt), lambda i, c: (0, 0, i))
    return pl.pallas_call(
        functools.partial(_peer_kernel, n_heads=n_heads, n_keys=n_keys, ec=ec),
        grid=(t // tt, n_exp // ec),
        in_specs=[pl.BlockSpec((tt, d), lambda i, c: (i, 0)),
                  pl.BlockSpec((ec, d), lambda i, c: (c, 0)),
                  pl.BlockSpec((d, ec), lambda i, c: (0, c)),
                  route_spec, route_spec, route_spec, route_spec,
                  pl.BlockSpec((tt, d), lambda i, c: (i, 0)),
                  pl.BlockSpec((1, 6, d), lambda i, c: (i // tiles_per_seq, 0, 0))],
        out_specs=pl.BlockSpec((tt, d), lambda i, c: (i, 0)),
        out_shape=jax.ShapeDtypeStruct((t, d), F32),
        scratch_shapes=[pltpu.VMEM((d, tt), F32)],
        compiler_params=_cparams("arbitrary", "arbitrary"),
        name="peer_experts",
    )(h2, u_b, vt_b, rank2, e2, cnt1, w1, x1, mod)


def _pack_in_proj(w_in, sb_w, gdn_qk_w, gdn_v_w, n_gdn_heads, d_model):
    o_sbq, o_sbk, o_sbv = 0, sb_w, 2 * sb_w
    o_gdn = 3 * sb_w
    conv_w = 2 * gdn_qk_w + gdn_v_w
    o_a = o_gdn + conv_w
    o_b = o_a + n_gdn_heads
    o_z = o_b + n_gdn_heads
    o_gate = o_z + gdn_v_w
    pad = 2 * LANES - 2 * n_gdn_heads
    packed = jnp.concatenate([
        w_in[:, o_sbq:o_a],
        w_in[:, o_gate:o_gate + 2 * d_model],
        w_in[:, o_z:o_z + gdn_v_w],
        w_in[:, o_a:o_z],
        jnp.zeros((w_in.shape[0], pad), w_in.dtype)], axis=1).astype(BF16)
    return packed


def _block(x2d, c, w_ada, b_ada, norm1_w, w_in, sb_q_norm_w, sb_k_norm_w, gdn_conv_w, gdn_A_log,
           gdn_dt_bias, gdn_o_norm_w, w_proj_sb, w_proj_gdn, w_o, norm2_w, peer_w_q, peer_sub_keys,
           peer_u, peer_v, batch, seq_len):
    t, d = x2d.shape
    sb_w = w_proj_sb.shape[0]
    gdn_v_w = w_proj_gdn.shape[0]
    gdn_qk_w = (gdn_conv_w.shape[1] - gdn_v_w) // 2
    n_sb_heads = sb_w // HEAD_DIM
    n_gdn_heads = gdn_v_w // HEAD_DIM
    assert gdn_qk_w == gdn_v_w == sb_w and d % (4 * LANES) == 0

    c_pad = jnp.zeros((8, d), F32).at[:batch].set(c)
    mod = _ada_call(c_pad, w_ada, b_ada)[:batch].reshape(batch, 6, d)

    w_packed = _pack_in_proj(w_in, sb_w, gdn_qk_w, gdn_v_w, n_gdn_heads, d)
    tn = 2 * LANES
    heads_per_tile = tn // HEAD_DIM
    q_tiles = sb_w // tn
    qkw = jnp.concatenate([jnp.tile(sb_q_norm_w * (HEAD_DIM ** -0.5 * LOG2E), (q_tiles, heads_per_tile)),
                           jnp.tile(sb_k_norm_w, (q_tiles, heads_per_tile))], axis=0).reshape(2 * q_tiles, 1, tn)
    proj = _inproj_call(x2d, mod, norm1_w.reshape(1, d), w_packed, qkw, seq_len, tn=tn)

    ysb = _sb_call(proj, batch, seq_len, n_sb_heads // 2)

    col_gdn = 3 * sb_w // gdn_v_w
    col_gate = (3 * sb_w + 3 * gdn_v_w) // d
    col_z = (3 * sb_w + 3 * gdn_v_w + 2 * d) // gdn_v_w
    col_ab = (3 * sb_w + 4 * gdn_v_w + 2 * d) // (2 * LANES)
    prep = _gdn_prep_call(proj, gdn_conv_w, gdn_A_log, gdn_dt_bias, seq_len, n_gdn_heads,
                          (col_gdn, col_gdn + 1, col_gdn + 2), col_ab, GDN_BLOCK)
    ogdn = _gdn_core_call(prep, batch, seq_len, n_gdn_heads, GDN_BLOCK)

    x1, h2, pq = _merge_call(
        x2d, ysb, ogdn, proj, mod, jnp.tile(gdn_o_norm_w, n_gdn_heads).reshape(1, gdn_v_w), norm2_w.reshape(1, d),
        w_proj_sb.astype(BF16), w_proj_gdn.astype(BF16), w_o.astype(BF16), peer_w_q.astype(BF16),
        seq_len, col_gate, col_gate + 1, col_z)

    rank2, e2, cnt1, w1 = _route_call(pq, peer_sub_keys)
    return _peer_call(h2, peer_u.astype(BF16), peer_v.T.astype(BF16), rank2, e2, cnt1, w1, x1, mod, seq_len)


def kernel(x, c, w_ada, b_ada, norm1_w, w_in, sb_q_norm_w, sb_k_norm_w, gdn_conv_w, gdn_A_log, gdn_dt_bias,
           gdn_o_norm_w, w_proj_sb, w_proj_gdn, w_o, norm2_w, peer_w_q, peer_sub_keys, peer_u, peer_v):
    batch, seq_len, d = x.shape
    x2d = x.reshape(batch * seq_len, d)
    for l in range(w_ada.shape[0]):
        x2d = _block(x2d, c, w_ada[l], b_ada[l], norm1_w[l], w_in[l], sb_q_norm_w[l], sb_k_norm_w[l],
                     gdn_conv_w[l], gdn_A_log[l], gdn_dt_bias[l], gdn_o_norm_w[l], w_proj_sb[l], w_proj_gdn[l],
                     w_o[l], norm2_w[l], peer_w_q[l], peer_sub_keys[l], peer_u[l], peer_v[l], batch, seq_len)
    return x2d.reshape(batch, seq_len, d)
```

```python
import functools

import jax
import jax.numpy as jnp
from jax import lax
from jax.experimental import pallas as pl
from jax.experimental.pallas import tpu as pltpu

F32 = jnp.float32
BF16 = jnp.bfloat16
HI = lax.Precision.HIGHEST
EPS = 1e-6
LOG2E = 1.4426950408889634

LANES = 128
HEAD_DIM = 64
GDN_BLOCK = 256
GDN_CONV = 4
PEER_TOPK = 16
NOT_RANKED = 99.0
VMEM_LIMIT = 56 * 1024 * 1024

NT_DIMS = (((1,), (1,)), ((), ()))
TN_DIMS = (((0,), (0,)), ((), ()))


def _cparams(*sem, flags=None):
    return pltpu.CompilerParams(dimension_semantics=sem, vmem_limit_bytes=VMEM_LIMIT, flags=flags)


def _sigmoid(x):
    return 1.0 / (1.0 + jnp.exp(-x))


def _silu(x):
    return x * _sigmoid(x)


def _softplus(x):
    return jnp.maximum(x, 0.0) + jnp.log(1.0 + jnp.exp(-jnp.abs(x)))


def _block_diag_ones(n, group, value=1.0, dtype=F32):
    r = jnp.arange(n) // group
    return jnp.where(r[:, None] == r[None, :], value, 0.0).astype(dtype)


def _group_sum_sq(x, bd):
    slab = bd.shape[0]
    outs = []
    for c0 in range(0, x.shape[1], slab):
        sq = x[:, c0:c0 + slab] * x[:, c0:c0 + slab]
        hi = sq.astype(BF16)
        lo = (sq - hi.astype(F32)).astype(BF16)
        outs.append(jnp.dot(hi, bd, preferred_element_type=F32) + jnp.dot(lo, bd, preferred_element_type=F32))
    return outs[0] if len(outs) == 1 else jnp.concatenate(outs, axis=1)


def _ada_kernel(c_ref, w_ref, b_ref, o_ref):
    c = c_ref[...]
    o_ref[...] = jnp.dot(_silu(c), w_ref[...], preferred_element_type=F32, precision=HI) + b_ref[...]


def _ada_call(c_pad, w_ada, b_ada):
    rows, d = c_pad.shape
    n = w_ada.shape[1]
    tn = 512
    return pl.pallas_call(
        _ada_kernel,
        grid=(n // tn,),
        in_specs=[pl.BlockSpec((rows, d), lambda j: (0, 0)),
                  pl.BlockSpec((d, tn), lambda j: (0, j)),
                  pl.BlockSpec((1, tn), lambda j: (0, j))],
        out_specs=pl.BlockSpec((rows, tn), lambda j: (0, j)),
        out_shape=jax.ShapeDtypeStruct((rows, n), F32),
        compiler_params=_cparams("arbitrary"),
        name="ada_mod",
    )(c_pad, w_ada, b_ada.reshape(1, n))


def _inproj_kernel(x_ref, mod_ref, n1_ref, w_ref, qkw_ref, bd_ref, o_ref, h_scr, *, n_qk_tiles):
    j = pl.program_id(1)

    @pl.when(j == 0)
    def _():
        x = x_ref[...]
        ms = jnp.mean(x * x, axis=-1, keepdims=True)
        y = x * lax.rsqrt(ms + EPS) * n1_ref[...]
        shift = mod_ref[0, 0:1, :]
        scale = mod_ref[0, 1:2, :]
        h_scr[...] = (y * (1.0 + scale) + shift).astype(BF16)

    acc = jnp.dot(h_scr[...], w_ref[...], preferred_element_type=F32)

    @pl.when(j < n_qk_tiles)
    def _():
        o_ref[...] = acc * lax.rsqrt(_group_sum_sq(acc, bd_ref[...]) + EPS) * qkw_ref[0]

    @pl.when(j >= n_qk_tiles)
    def _():
        o_ref[...] = acc


def _inproj_call(x2d, mod, n1w, w_packed, qkw, seq_len, tm=1024, tn=512):
    t, d = x2d.shape
    n = w_packed.shape[1]
    n_qk_tiles = qkw.shape[0]
    bd = _block_diag_ones(2 * LANES, HEAD_DIM, 1.0 / HEAD_DIM, BF16)
    tiles_per_seq = seq_len // tm
    return pl.pallas_call(
        functools.partial(_inproj_kernel, n_qk_tiles=n_qk_tiles),
        grid=(t // tm, n // tn),
        in_specs=[pl.BlockSpec((tm, d), lambda i, j: (i, 0)),
                  pl.BlockSpec((1, 6, d), lambda i, j: (i // tiles_per_seq, 0, 0)),
                  pl.BlockSpec((1, d), lambda i, j: (0, 0)),
                  pl.BlockSpec((d, tn), lambda i, j: (0, j)),
                  pl.BlockSpec((1, 1, tn), lambda i, j: (jnp.minimum(j, n_qk_tiles - 1), 0, 0)),
                  pl.BlockSpec(bd.shape, lambda i, j: (0, 0))],
        out_specs=pl.BlockSpec((tm, tn), lambda i, j: (i, j)),
        out_shape=jax.ShapeDtypeStruct((t, n), F32),
        scratch_shapes=[pltpu.VMEM((tm, d), BF16)],
        compiler_params=_cparams("arbitrary", "arbitrary"),
        name="in_proj",
    )(x2d, mod, n1w, w_packed, qkw, bd)


def _sb_kernel(q_ref, k_ref, v_ref, wlo_ref, whi_ref, o_ref, acc_ref, r_ref, *, tq):
    i = pl.program_id(2)
    half = whi_ref.shape[0]
    lane = lax.broadcasted_iota(jnp.int32, (1, LANES), 1)
    first_half = lane < HEAD_DIM
    q = q_ref[...]
    q_heads = (jnp.where(first_half, q, 0.0).astype(BF16), jnp.where(first_half, 0.0, q).astype(BF16))
    acc_ref[...] = jnp.zeros_like(acc_ref)
    r_ref[...] = jnp.zeros_like(r_ref)

    def step(j, masked):
        ks = pl.multiple_of(j * tq, tq)
        kb = k_ref[pl.ds(ks, tq), :].astype(BF16)
        vb = v_ref[pl.ds(ks, tq), :].astype(BF16)
        if masked:
            causal = (lax.broadcasted_iota(jnp.int32, (tq, tq), 1) < lax.broadcasted_iota(jnp.int32, (tq, tq), 0))
        for hd in range(2):
            y = lax.dot_general(q_heads[hd], kb, NT_DIMS, preferred_element_type=F32)
            neg_abs = lax.bitcast_convert_type(lax.bitcast_convert_type(y, jnp.uint32) | jnp.uint32(0x80000000), F32)
            sp = jnp.maximum(y, 0.0) + jnp.log(1.0 + jnp.exp2(neg_abs)) * LOG2E
            if masked:
                sp = jnp.where(causal, sp, 0.0)
            spb = sp.astype(BF16)
            later_lo = jnp.dot(spb, wlo_ref[...], preferred_element_type=F32)
            later_hi = jnp.dot(spb[:, half:], whi_ref[...], preferred_element_type=F32)
            a = jnp.exp2(y - sp - jnp.concatenate([later_lo, later_hi], axis=1))
            if masked:
                a = jnp.where(causal, a, 0.0)
            pv = jnp.dot(a.astype(BF16), vb, preferred_element_type=F32)
            r = r_ref[hd]
            acc_ref[hd] += jnp.exp2(-r) * pv
            r_ref[hd] = r + (later_lo[:, 0:1] + sp[:, 0:1])

    step(i, True)

    def body(it, carry):
        j = i - 1 - 2 * it
        step(j, False)
        step(j - 1, False)
        return carry

    lax.fori_loop(0, i // 2, body, 0)

    @pl.when(i % 2 == 1)
    def _():
        step(0, False)

    o_ref[...] = jnp.where(first_half, acc_ref[0], acc_ref[1])


def _sb_call(proj, batch, seq_len, n_pairs, tq=512):
    t = proj.shape[0]
    nq = seq_len // tq
    half = tq // 2
    kk = jnp.arange(tq)
    later = (kk[:, None] > kk[None, :]).astype(BF16)
    w_lo = later[:, :half]
    w_hi = later[half:, half:]
    return pl.pallas_call(
        functools.partial(_sb_kernel, tq=tq),
        grid=(batch, n_pairs, nq),
        in_specs=[pl.BlockSpec((tq, LANES), lambda b, p, i: (b * nq + i, p)),
                  pl.BlockSpec((seq_len, LANES), lambda b, p, i: (b, n_pairs + p)),
                  pl.BlockSpec((seq_len, LANES), lambda b, p, i: (b, 2 * n_pairs + p)),
                  pl.BlockSpec((tq, half), lambda b, p, i: (0, 0)),
                  pl.BlockSpec((half, half), lambda b, p, i: (0, 0))],
        out_specs=pl.BlockSpec((tq, LANES), lambda b, p, i: (b * nq + i, p)),
        out_shape=jax.ShapeDtypeStruct((t, n_pairs * LANES), F32),
        scratch_shapes=[pltpu.VMEM((2, tq, LANES), F32), pltpu.VMEM((2, tq, 1), F32)],
        compiler_params=_cparams("arbitrary", "arbitrary", "arbitrary"),
        name="sb_attention",
    )(proj, proj, proj, w_lo, w_hi)


def _gdn_prep_kernel(q_ref, k_ref, v_ref, qt_ref, kt_ref, vt_ref, ab_ref, cw_ref, alog_ref, dtb_ref,
                     ea_ref, eb_ref, lc_ref, jc_ref, bd_ref,
                     qn_o, qd_o, kn_o, kd_o, kb_o, kbg_o, vb_o, gcol_o, grow_o, scr, *, tm, tiles_per_seq):
    i = pl.program_id(0)
    keep_tail = (i % tiles_per_seq != 0).astype(F32)
    w = cw_ref[...]
    width = q_ref.shape[1]

    def conv_silu(cur_ref, tail_ref, col0):
        scr[0:8, :] = tail_ref[...] * keep_tail
        scr[8:, :] = cur_ref[...]
        y = jnp.zeros((tm, width), F32)
        for tap in range(GDN_CONV):
            off = 8 - (GDN_CONV - 1) + tap
            y = y + scr[off:off + tm, :] * w[tap:tap + 1, col0:col0 + width]
        return _silu(y)

    bd = bd_ref[...]
    cq = conv_silu(q_ref, qt_ref, 0)
    qn = cq * lax.rsqrt(_group_sum_sq(cq, bd) + EPS) * (HEAD_DIM ** -0.5)
    ck = conv_silu(k_ref, kt_ref, width)
    kn = ck * lax.rsqrt(_group_sum_sq(ck, bd) + EPS)
    cv = conv_silu(v_ref, vt_ref, 2 * width)

    ab = ab_ref[...]
    g = -jnp.exp(alog_ref[...]) * _softplus(ab + dtb_ref[...])
    beta = _sigmoid(ab)
    g_cum = jnp.dot(lc_ref[...], g, preferred_element_type=F32, precision=HI)
    g_tot = jnp.dot(jc_ref[...], g, preferred_element_type=F32, precision=HI)
    ea = ea_ref[...]
    gx = jnp.dot(g_cum, ea, preferred_element_type=F32, precision=HI)
    glx = jnp.dot(g_tot, ea, preferred_element_type=F32, precision=HI)
    bx = jnp.dot(beta, eb_ref[...], preferred_element_type=F32, precision=HI)

    e_g = jnp.exp(gx)
    kb = kn * bx
    qn_o[...] = qn
    qd_o[...] = qn * e_g
    kn_o[...] = kn
    kd_o[...] = kn * jnp.exp(glx - gx)
    kb_o[...] = kb
    kbg_o[...] = kb * e_g
    vb_o[...] = cv * bx
    g_heads = g_cum[:, :LANES]
    gcol_o[...] = g_heads
    grow_o[...] = g_heads.T[0:8, :]


def _gdn_prep_call(proj, conv_w, a_log, dt_bias, seq_len, n_heads, col_qkv, col_ab, chunk, tm=512):
    t = proj.shape[0]
    assert n_heads <= 8 and tm % chunk == 0
    width = n_heads * HEAD_DIM
    abw = 2 * LANES
    tiles_per_seq = seq_len // tm
    sub = tm // 8
    alog_pad = jnp.zeros((1, abw), F32).at[0, :n_heads].set(a_log)
    dtb_pad = jnp.zeros((1, abw), F32).at[0, :n_heads].set(dt_bias)
    head_of_lane = jnp.arange(width) // HEAD_DIM
    rows = jnp.arange(abw)
    ea = (rows[:, None] == head_of_lane[None, :]).astype(F32)
    eb = (rows[:, None] == head_of_lane[None, :] + n_heads).astype(F32)
    tok = jnp.arange(tm)
    same_chunk = (tok[:, None] // chunk) == (tok[None, :] // chunk)
    lc = (same_chunk & (tok[:, None] >= tok[None, :])).astype(F32)
    jc = same_chunk.astype(F32)
    bd = _block_diag_ones(2 * LANES, HEAD_DIM, 1.0, BF16)

    cur = lambda c: pl.BlockSpec((tm, width), lambda i: (i, c))
    tail = lambda c: pl.BlockSpec((8, width), lambda i: (jnp.maximum(i * sub - 1, 0), c))
    full = lambda a: pl.BlockSpec(a.shape, lambda i: (0,) * a.ndim)
    cq, ck, cv = col_qkv
    out_spec = pl.BlockSpec((tm, width), lambda i: (i, 0))
    out_shape = jax.ShapeDtypeStruct((t, width), F32)
    return pl.pallas_call(
        functools.partial(_gdn_prep_kernel, tm=tm, tiles_per_seq=tiles_per_seq),
        grid=(t // tm,),
        in_specs=[cur(cq), cur(ck), cur(cv), tail(cq), tail(ck), tail(cv),
                  pl.BlockSpec((tm, abw), lambda i: (i, col_ab)),
                  full(conv_w), full(alog_pad), full(dtb_pad), full(ea), full(eb), full(lc), full(jc), full(bd)],
        out_specs=[out_spec] * 7 + [pl.BlockSpec((tm, LANES), lambda i: (i, 0)), pl.BlockSpec((8, tm), lambda i: (0, i))],
        out_shape=[out_shape] * 7 + [jax.ShapeDtypeStruct((t, LANES), F32), jax.ShapeDtypeStruct((8, t), F32)],
        scratch_shapes=[pltpu.VMEM((tm + 8, width), F32)],
        compiler_params=_cparams("arbitrary"),
        name="gdn_prep",
    )(proj, proj, proj, proj, proj, proj, proj, conv_w, alog_pad, dtb_pad, ea, eb, lc, jc, bd)


def _gdn_core_kernel(qn_ref, qd_ref, kn_ref, kd_ref, kb_ref, kbg_ref, vb_ref, gcol_ref, grow_ref,
                     o_ref, s_scr, *, ts, n_heads, c_len):
    @pl.when(pl.program_id(1) == 0)
    def _():
        s_scr[...] = jnp.zeros_like(s_scr)

    row = lax.broadcasted_iota(jnp.int32, (c_len, c_len), 0)
    col = lax.broadcasted_iota(jnp.int32, (c_len, c_len), 1)
    strict = row > col
    incl = row >= col
    eye_state = (lax.broadcasted_iota(jnp.int32, (HEAD_DIM, HEAD_DIM), 0)
                 == lax.broadcasted_iota(jnp.int32, (HEAD_DIM, HEAD_DIM), 1))
    n_levels = (c_len - 1).bit_length()
    n_chunks = ts // c_len
    problems = [(c, h) for c in range(n_chunks) for h in range(n_heads)]

    def bf(x):
        return x.astype(BF16)

    def mm(a, b):
        return jnp.dot(a, b, preferred_element_type=F32)

    def head_tile(ref, c, h):
        pair, half = divmod(h, 2)
        tile = ref[c * c_len:(c + 1) * c_len, pair * LANES:(pair + 1) * LANES]
        return tile[:, half * HEAD_DIM:(half + 1) * HEAD_DIM]

    sibling = [((row >> k) ^ (col >> k)) == 1 for k in range(n_levels)]

    xs, ms, rs, qks = [], [], [], []
    for c, h in problems:
        g_col = gcol_ref[c * c_len:(c + 1) * c_len, h:h + 1]
        g_row = grow_ref[h:h + 1, c * c_len:(c + 1) * c_len]
        decay = jnp.where(incl, jnp.exp(g_col - g_row), 0.0)
        kn_b = bf(head_tile(kn_ref, c, h))
        kk = lax.dot_general(bf(head_tile(kb_ref, c, h)), kn_b, NT_DIMS, preferred_element_type=F32)
        x = jnp.where(strict, -(kk * decay), 0.0)
        xs.append(x)
        ms.append(jnp.where(row == col, 1.0, jnp.where(sibling[0], x, 0.0)))
        qks.append(bf(lax.dot_general(bf(head_tile(qn_ref, c, h)), kn_b, NT_DIMS, preferred_element_type=F32) * decay))
        rs.append(bf(jnp.concatenate([head_tile(vb_ref, c, h), head_tile(kbg_ref, c, h)], axis=1)))

    for k in range(1, n_levels):
        for idx in range(len(problems)):
            m_b = bf(ms[idx])
            left = mm(m_b, bf(jnp.where(sibling[k], xs[idx], 0.0)))
            ms[idx] = ms[idx] + mm(bf(left), m_b)

    q_eff, o_zero, p_mat, b_mat = {}, {}, {}, {}
    for idx, (c, h) in enumerate(problems):
        sol = bf(mm(bf(ms[idx]), rs[idx]))
        m1 = mm(qks[idx], sol)
        q_eff[c, h] = bf(head_tile(qd_ref, c, h) - m1[:, HEAD_DIM:])
        o_zero[c, h] = m1[:, :HEAD_DIM]
        m2 = lax.dot_general(bf(head_tile(kd_ref, c, h)), sol, TN_DIMS, preferred_element_type=F32)
        chunk_decay = jnp.exp(gcol_ref[(c + 1) * c_len - 1:(c + 1) * c_len, h:h + 1])
        p_mat[c, h] = bf(jnp.where(eye_state, chunk_decay, 0.0) - m2[:, HEAD_DIM:])
        b_mat[c, h] = m2[:, :HEAD_DIM]

    states = [s_scr[h] for h in range(n_heads)]
    for c in range(n_chunks):
        outs = []
        for h in range(n_heads):
            s_b = bf(states[h])
            outs.append(mm(q_eff[c, h], s_b) + o_zero[c, h])
            states[h] = mm(p_mat[c, h], s_b) + b_mat[c, h]
        o_ref[c * c_len:(c + 1) * c_len, :] = jnp.concatenate(outs, axis=1)
    for h in range(n_heads):
        s_scr[h] = states[h]


def _gdn_core_call(prep, batch, seq_len, n_heads, c_len, ts=512):
    t, width = prep[0].shape
    n_seq_tiles = seq_len // ts
    spec = pl.BlockSpec((ts, width), lambda b, s: (b * n_seq_tiles + s, 0))
    return pl.pallas_call(
        functools.partial(_gdn_core_kernel, ts=ts, n_heads=n_heads, c_len=c_len),
        grid=(batch, n_seq_tiles),
        in_specs=[spec] * 7 + [pl.BlockSpec((ts, LANES), lambda b, s: (b * n_seq_tiles + s, 0)),
                               pl.BlockSpec((8, ts), lambda b, s: (0, b * n_seq_tiles + s))],
        out_specs=spec,
        out_shape=jax.ShapeDtypeStruct((t, width), F32),
        scratch_shapes=[pltpu.VMEM((n_heads, HEAD_DIM, HEAD_DIM), F32)],
        compiler_params=_cparams("arbitrary", "arbitrary"),
        name="gdn_core",
    )(*prep)


def _merge_kernel(x_ref, ysb_ref, og_ref, z_ref, g0_ref, g1_ref, mod_ref, onw_ref, n2_ref, bd_ref,
                  wsb_ref, wgdn_ref, wo_ref, wq_ref, x1_o, h2_o, pq_o):
    og = og_ref[...]
    ygdn = og * lax.rsqrt(_group_sum_sq(og, bd_ref[...]) + EPS) * onw_ref[...] * _silu(z_ref[...])
    m = (_sigmoid(g0_ref[...]) * jnp.dot(ysb_ref[...].astype(BF16), wsb_ref[...], preferred_element_type=F32)
         + _sigmoid(g1_ref[...]) * jnp.dot(ygdn.astype(BF16), wgdn_ref[...], preferred_element_type=F32))
    gate1 = mod_ref[0, 2:3, :]
    shift2 = mod_ref[0, 3:4, :]
    scale2 = mod_ref[0, 4:5, :]
    x1 = x_ref[...] + gate1 * jnp.dot(m.astype(BF16), wo_ref[...], preferred_element_type=F32)
    x1_o[...] = x1
    ms2 = jnp.mean(x1 * x1, axis=-1, keepdims=True)
    h2 = x1 * lax.rsqrt(ms2 + EPS) * n2_ref[...] * (1.0 + scale2) + shift2
    h2b = h2.astype(BF16)
    h2_o[...] = h2b
    pq_o[...] = jnp.dot(h2b, wq_ref[...], preferred_element_type=F32)


def _merge_call(x2d, ysb, ogdn, proj, mod, onw, n2w, wsb, wgdn, wo, wq, seq_len, col_g0, col_g1, col_z, tm=256):
    t, d = x2d.shape
    width = ysb.shape[1]
    nq = wq.shape[1]
    bd = _block_diag_ones(2 * LANES, HEAD_DIM, 1.0 / HEAD_DIM, BF16)
    tiles_per_seq = seq_len // tm
    full = lambda a: pl.BlockSpec(a.shape, lambda i: (0,) * a.ndim)
    return pl.pallas_call(
        _merge_kernel,
        grid=(t // tm,),
        in_specs=[pl.BlockSpec((tm, d), lambda i: (i, 0)),
                  pl.BlockSpec((tm, width), lambda i: (i, 0)),
                  pl.BlockSpec((tm, width), lambda i: (i, 0)),
                  pl.BlockSpec((tm, width), lambda i: (i, col_z)),
                  pl.BlockSpec((tm, d), lambda i: (i, col_g0)),
                  pl.BlockSpec((tm, d), lambda i: (i, col_g1)),
                  pl.BlockSpec((1, 6, d), lambda i: (i // tiles_per_seq, 0, 0)),
                  full(onw), full(n2w), full(bd), full(wsb), full(wgdn), full(wo), full(wq)],
        out_specs=[pl.BlockSpec((tm, d), lambda i: (i, 0)),
                   pl.BlockSpec((tm, d), lambda i: (i, 0)),
                   pl.BlockSpec((tm, nq), lambda i: (i, 0))],
        out_shape=[jax.ShapeDtypeStruct((t, d), F32),
                   jax.ShapeDtypeStruct((t, d), BF16),
                   jax.ShapeDtypeStruct((t, nq), F32)],
        compiler_params=_cparams("arbitrary"),
        name="merge_proj",
    )(x2d, ysb, ogdn, proj, proj, proj, mod, onw, n2w, bd, wsb, wgdn, wo, wq)


def _extract_topk(s, k):
    n = s.shape[0]
    iota = lax.broadcasted_iota(jnp.int32, s.shape, 0).astype(F32)
    rank = jnp.full(s.shape, NOT_RANKED, F32)
    vals, idxs = [], []
    for r in range(k):
        m = jnp.max(s, axis=0, keepdims=True)
        idx = jnp.min(jnp.where(s == m, iota, float(n)), axis=0, keepdims=True)
        hit = iota == idx
        rank = jnp.where(hit, float(r), rank)
        s = jnp.where(hit, -jnp.inf, s)
        vals.append(m)
        idxs.append(idx)
    return vals, idxs, rank


def _candidate_tables(k):
    pairs = [(a, b) for a in range(k) for b in range(k) if (a + 1) * (b + 1) <= k]
    n_pad = -(-len(pairs) // 8) * 8
    sel_a = jnp.zeros((n_pad, k), F32).at[jnp.arange(len(pairs)), jnp.array([a for a, _ in pairs])].set(1.0)
    sel_b = jnp.zeros((n_pad, k), F32).at[jnp.arange(len(pairs)), jnp.array([b for _, b in pairs])].set(1.0)
    return sel_a, sel_b, len(pairs)


def _route_kernel(pq_ref, keys_ref, sela_ref, selb_ref, rank2_o, e2_o, cnt1_o, w1_o, *, n_cand):
    k = PEER_TOPK
    hp, _, _, half = keys_ref.shape
    tt = pq_ref.shape[0]
    w = hp * tt
    iota_k = lax.broadcasted_iota(jnp.int32, (k, w), 0).astype(F32)
    s_all = jnp.concatenate(
        [lax.dot_general(keys_ref[hh, part], pq_ref[:, (2 * hh + part) * half:(2 * hh + part + 1) * half], NT_DIMS,
                         preferred_element_type=F32, precision=HI) for part in range(2) for hh in range(hp)], axis=1)
    vals, _, rank = _extract_topk(s_all, k)
    top = jnp.concatenate(vals, axis=0)
    top1, top2 = top[:, :w], top[:, w:]
    v1, v2 = [v[:, :w] for v in vals], [v[:, w:] for v in vals]
    rank1, rank2 = rank[:, :w], rank[:, w:]
    scores = (s_all[:, :w], s_all[:, w:])
    sel_a = sela_ref[...]
    cand = (jnp.dot(sel_a, top1, preferred_element_type=F32, precision=HI)
            + jnp.dot(selb_ref[...], top2, preferred_element_type=F32, precision=HI))
    cand_row = lax.broadcasted_iota(jnp.int32, cand.shape, 0)
    _, _, cand_rank = _extract_topk(jnp.where(cand_row < n_cand, cand, -jnp.inf), k)
    chosen = (cand_rank < float(k)).astype(BF16)
    count = lax.dot_general(sel_a.astype(BF16), chosen, TN_DIMS, preferred_element_type=F32)
    e1 = jnp.exp(top1 - v1[0])
    e2 = jnp.exp(top2 - v2[0])
    z = jnp.zeros_like(v1[0])
    for a in range(k):
        z = z + e1[a:a + 1] * jnp.sum(jnp.where(iota_k < count[a:a + 1], e2, 0.0), axis=0, keepdims=True)
    inv_z = 1.0 / z
    cnt1 = jnp.zeros_like(rank1)
    for a in range(k):
        cnt1 = jnp.where(rank1 == float(a), count[a:a + 1], cnt1)
    e2_all = jnp.exp(scores[1] - v2[0]).astype(BF16)
    w1_all = jnp.where(rank1 < float(k), jnp.exp(scores[0] - v1[0]) * inv_z, 0.0)
    for hh in range(hp):
        lanes = slice(hh * tt, (hh + 1) * tt)
        rank2_o[hh] = rank2[:, lanes].astype(BF16)
        e2_o[hh] = e2_all[:, lanes]
        cnt1_o[hh] = cnt1[:, lanes]
        w1_o[hh] = w1_all[:, lanes]


def _route_call(pq, sub_keys, tt=256, hp=4):
    t = pq.shape[0]
    n_heads, _, n_keys, half = sub_keys.shape
    sel_a, sel_b, n_cand = _candidate_tables(PEER_TOPK)
    out_spec = pl.BlockSpec((hp, n_keys, tt), lambda i, h: (h, 0, i))
    shape = lambda dt: jax.ShapeDtypeStruct((n_heads, n_keys, t), dt)
    return pl.pallas_call(
        functools.partial(_route_kernel, n_cand=n_cand),
        grid=(t // tt, n_heads // hp),
        in_specs=[pl.BlockSpec((tt, hp * 2 * half), lambda i, h: (i, h)),
                  pl.BlockSpec((hp, 2, n_keys, half), lambda i, h: (h, 0, 0, 0)),
                  pl.BlockSpec(sel_a.shape, lambda i, h: (0, 0)),
                  pl.BlockSpec(sel_b.shape, lambda i, h: (0, 0))],
        out_specs=[out_spec] * 4,
        out_shape=[shape(BF16), shape(BF16), shape(F32), shape(F32)],
        compiler_params=_cparams("arbitrary", "arbitrary"),
        name="peer_route",
    )(pq, sub_keys, sel_a, sel_b)


def _peer_kernel(h2_ref, u_ref, vt_ref, rank2_ref, e2_ref, cnt1_ref, w1_ref, x1_ref, mod_ref,
                 o_ref, acc_scr, *, n_heads, n_keys, ec):
    c = pl.program_id(1)

    @pl.when(c == 0)
    def _():
        acc_scr[...] = jnp.zeros_like(acc_scr)

    act = lax.dot_general(u_ref[...], h2_ref[...], NT_DIMS, preferred_element_type=F32)
    zero = jnp.zeros((), BF16)
    coefs = []
    for sub in range(ec // n_keys):
        gate = None
        for h in range(n_heads):
            cnt = cnt1_ref[h, sub:sub + 1, :].astype(BF16)
            w1 = w1_ref[h, sub:sub + 1, :].astype(BF16)
            term = jnp.where(rank2_ref[h] < cnt, e2_ref[h], zero) * w1
            gate = term if gate is None else gate + term
        a = act[sub * n_keys:(sub + 1) * n_keys, :]
        gelu = 0.5 * a * (1.0 + lax.erf(a * (2.0 ** -0.5)))
        coefs.append(gate * gelu.astype(BF16))
    acc_scr[...] += jnp.dot(vt_ref[...], jnp.concatenate(coefs, axis=0), preferred_element_type=F32)

    @pl.when(c == pl.num_programs(1) - 1)
    def _():
        gate2 = mod_ref[0, 5:6, :]
        o_ref[...] = x1_ref[...] + gate2 * acc_scr[...].T


def _peer_call(h2, u_b, vt_b, rank2, e2, cnt1, w1, x1, mod, seq_len, tt=512, ec=1024):
    t, d = h2.shape
    n_exp = u_b.shape[0]
    n_heads, n_keys, _ = rank2.shape
    tiles_per_seq = seq_len // tt
    route_spec = pl.BlockSpec((n_heads, n_keys, tt), lambda i, c: (0, 0, i))
    row_spec = pl.BlockSpec((n_heads, ec // n_keys, tt), lambda i, c: (0, c, i))
    return pl.pallas_call(
        functools.partial(_peer_kernel, n_heads=n_heads, n_keys=n_keys, ec=ec),
        grid=(t // tt, n_exp // ec),
        in_specs=[pl.BlockSpec((tt, d), lambda i, c: (i, 0)),
                  pl.BlockSpec((ec, d), lambda i, c: (c, 0)),
                  pl.BlockSpec((d, ec), lambda i, c: (0, c)),
                  route_spec, route_spec, row_spec, row_spec,
                  pl.BlockSpec((tt, d), lambda i, c: (i, 0)),
                  pl.BlockSpec((1, 6, d), lambda i, c: (i // tiles_per_seq, 0, 0))],
        out_specs=pl.BlockSpec((tt, d), lambda i, c: (i, 0)),
        out_shape=jax.ShapeDtypeStruct((t, d), F32),
        scratch_shapes=[pltpu.VMEM((d, tt), F32)],
        compiler_params=_cparams("arbitrary", "arbitrary"),
        name="peer_experts",
    )(h2, u_b, vt_b, rank2, e2, cnt1, w1, x1, mod)


def _pack_in_proj(w_in, sb_w, gdn_qk_w, gdn_v_w, n_gdn_heads, d_model, tn):
    o_sbq, o_sbk, o_sbv = 0, sb_w, 2 * sb_w
    o_gdn = 3 * sb_w
    conv_w = 2 * gdn_qk_w + gdn_v_w
    o_a = o_gdn + conv_w
    o_b = o_a + n_gdn_heads
    o_z = o_b + n_gdn_heads
    o_gate = o_z + gdn_v_w
    pad = (-w_in.shape[1]) % tn
    packed = jnp.concatenate([
        w_in[:, o_sbq:o_a],
        w_in[:, o_gate:o_gate + 2 * d_model],
        w_in[:, o_z:o_z + gdn_v_w],
        w_in[:, o_a:o_z],
        jnp.zeros((w_in.shape[0], pad), w_in.dtype)], axis=1).astype(BF16)
    return packed


def _block(x2d, c, w_ada, b_ada, norm1_w, w_in, sb_q_norm_w, sb_k_norm_w, gdn_conv_w, gdn_A_log,
           gdn_dt_bias, gdn_o_norm_w, w_proj_sb, w_proj_gdn, w_o, norm2_w, peer_w_q, peer_sub_keys,
           peer_u, peer_v, batch, seq_len):
    t, d = x2d.shape
    sb_w = w_proj_sb.shape[0]
    gdn_v_w = w_proj_gdn.shape[0]
    gdn_qk_w = (gdn_conv_w.shape[1] - gdn_v_w) // 2
    n_sb_heads = sb_w // HEAD_DIM
    n_gdn_heads = gdn_v_w // HEAD_DIM
    assert gdn_qk_w == gdn_v_w == sb_w and d % (4 * LANES) == 0

    c_pad = jnp.zeros((8, d), F32).at[:batch].set(c)
    mod = _ada_call(c_pad, w_ada, b_ada)[:batch].reshape(batch, 6, d)

    tn = sb_w
    w_packed = _pack_in_proj(w_in, sb_w, gdn_qk_w, gdn_v_w, n_gdn_heads, d, tn)
    heads_per_tile = tn // HEAD_DIM
    q_tiles = sb_w // tn
    qkw = jnp.concatenate([jnp.tile(sb_q_norm_w * (HEAD_DIM ** -0.5 * LOG2E), (q_tiles, heads_per_tile)),
                           jnp.tile(sb_k_norm_w, (q_tiles, heads_per_tile))], axis=0).reshape(2 * q_tiles, 1, tn)
    proj = _inproj_call(x2d, mod, norm1_w.reshape(1, d), w_packed, qkw, seq_len, tn=tn)

    ysb = _sb_call(proj, batch, seq_len, n_sb_heads // 2)

    col_gdn = 3 * sb_w // gdn_v_w
    col_gate = (3 * sb_w + 3 * gdn_v_w) // d
    col_z = (3 * sb_w + 3 * gdn_v_w + 2 * d) // gdn_v_w
    col_ab = (3 * sb_w + 4 * gdn_v_w + 2 * d) // (2 * LANES)
    prep = _gdn_prep_call(proj, gdn_conv_w, gdn_A_log, gdn_dt_bias, seq_len, n_gdn_heads,
                          (col_gdn, col_gdn + 1, col_gdn + 2), col_ab, GDN_BLOCK)
    ogdn = _gdn_core_call(prep, batch, seq_len, n_gdn_heads, GDN_BLOCK)

    x1, h2, pq = _merge_call(
        x2d, ysb, ogdn, proj, mod, jnp.tile(gdn_o_norm_w, n_gdn_heads).reshape(1, gdn_v_w), norm2_w.reshape(1, d),
        w_proj_sb.astype(BF16), w_proj_gdn.astype(BF16), w_o.astype(BF16), peer_w_q.astype(BF16),
        seq_len, col_gate, col_gate + 1, col_z)

    rank2, e2, cnt1, w1 = _route_call(pq, peer_sub_keys)
    return _peer_call(h2, peer_u.astype(BF16), peer_v.T.astype(BF16), rank2, e2, cnt1, w1, x1, mod, seq_len)


def kernel(x, c, w_ada, b_ada, norm1_w, w_in, sb_q_norm_w, sb_k_norm_w, gdn_conv_w, gdn_A_log, gdn_dt_bias,
           gdn_o_norm_w, w_proj_sb, w_proj_gdn, w_o, norm2_w, peer_w_q, peer_sub_keys, peer_u, peer_v):
    batch, seq_len, d = x.shape
    x2d = x.reshape(batch * seq_len, d)
    for l in range(w_ada.shape[0]):
        x2d = _block(x2d, c, w_ada[l], b_ada[l], norm1_w[l], w_in[l], sb_q_norm_w[l], sb_k_norm_w[l],
                     gdn_conv_w[l], gdn_A_log[l], gdn_dt_bias[l], gdn_o_norm_w[l], w_proj_sb[l], w_proj_gdn[l],
                     w_o[l], norm2_w[l], peer_w_q[l], peer_sub_keys[l], peer_u[l], peer_v[l], batch, seq_len)
    return x2d.reshape(batch, seq_len, d)
```

```python
import functools

import jax
import jax.numpy as jnp
from jax import lax
from jax.experimental import pallas as pl
from jax.experimental.pallas import tpu as pltpu

F32 = jnp.float32
BF16 = jnp.bfloat16
HI = lax.Precision.HIGHEST
EPS = 1e-6
LOG2E = 1.4426950408889634

LANES = 128
HEAD_DIM = 64
GDN_BLOCK = 256
GDN_CONV = 4
PEER_TOPK = 16
NOT_RANKED = 99.0
VMEM_LIMIT = 56 * 1024 * 1024

NT_DIMS = (((1,), (1,)), ((), ()))
TN_DIMS = (((0,), (0,)), ((), ()))


def _cparams(*sem, flags=None):
    return pltpu.CompilerParams(dimension_semantics=sem, vmem_limit_bytes=VMEM_LIMIT, flags=flags)


def _sigmoid(x):
    return 1.0 / (1.0 + jnp.exp(-x))


def _silu(x):
    return x * _sigmoid(x)


def _softplus(x):
    return jnp.maximum(x, 0.0) + jnp.log(1.0 + jnp.exp(-jnp.abs(x)))


def _block_diag_ones(n, group, value=1.0, dtype=F32):
    r = jnp.arange(n) // group
    return jnp.where(r[:, None] == r[None, :], value, 0.0).astype(dtype)


def _group_sum_sq(x, bd):
    slab = bd.shape[0]
    outs = []
    for c0 in range(0, x.shape[1], slab):
        sq = x[:, c0:c0 + slab] * x[:, c0:c0 + slab]
        hi = sq.astype(BF16)
        lo = (sq - hi.astype(F32)).astype(BF16)
        outs.append(jnp.dot(hi, bd, preferred_element_type=F32) + jnp.dot(lo, bd, preferred_element_type=F32))
    return outs[0] if len(outs) == 1 else jnp.concatenate(outs, axis=1)


def _ada_kernel(c_ref, w_ref, b_ref, o_ref):
    c = c_ref[...]
    o_ref[...] = jnp.dot(_silu(c), w_ref[...], preferred_element_type=F32, precision=HI) + b_ref[...]


def _ada_call(c_pad, w_ada, b_ada):
    rows, d = c_pad.shape
    n = w_ada.shape[1]
    tn = 512
    return pl.pallas_call(
        _ada_kernel,
        grid=(n // tn,),
        in_specs=[pl.BlockSpec((rows, d), lambda j: (0, 0)),
                  pl.BlockSpec((d, tn), lambda j: (0, j)),
                  pl.BlockSpec((1, tn), lambda j: (0, j))],
        out_specs=pl.BlockSpec((rows, tn), lambda j: (0, j)),
        out_shape=jax.ShapeDtypeStruct((rows, n), F32),
        compiler_params=_cparams("arbitrary"),
        name="ada_mod",
    )(c_pad, w_ada, b_ada.reshape(1, n))


def _inproj_kernel(x_ref, mod_ref, n1_ref, w_ref, qkw_ref, bd_ref, o_ref, h_scr, *, n_qk_tiles):
    j = pl.program_id(1)

    @pl.when(j == 0)
    def _():
        x = x_ref[...]
        ms = jnp.mean(x * x, axis=-1, keepdims=True)
        y = x * lax.rsqrt(ms + EPS) * n1_ref[...]
        shift = mod_ref[0, 0:1, :]
        scale = mod_ref[0, 1:2, :]
        h_scr[...] = (y * (1.0 + scale) + shift).astype(BF16)

    acc = jnp.dot(h_scr[...], w_ref[...], preferred_element_type=F32)

    @pl.when(j < n_qk_tiles)
    def _():
        o_ref[...] = acc * lax.rsqrt(_group_sum_sq(acc, bd_ref[...]) + EPS) * qkw_ref[0]

    @pl.when(j >= n_qk_tiles)
    def _():
        o_ref[...] = acc


def _inproj_call(x2d, mod, n1w, w_packed, qkw, seq_len, tm=1024, tn=512):
    t, d = x2d.shape
    n = w_packed.shape[1]
    n_qk_tiles = qkw.shape[0]
    bd = _block_diag_ones(2 * LANES, HEAD_DIM, 1.0 / HEAD_DIM, BF16)
    tiles_per_seq = seq_len // tm
    return pl.pallas_call(
        functools.partial(_inproj_kernel, n_qk_tiles=n_qk_tiles),
        grid=(t // tm, n // tn),
        in_specs=[pl.BlockSpec((tm, d), lambda i, j: (i, 0)),
                  pl.BlockSpec((1, 6, d), lambda i, j: (i // tiles_per_seq, 0, 0)),
                  pl.BlockSpec((1, d), lambda i, j: (0, 0)),
                  pl.BlockSpec((d, tn), lambda i, j: (0, j)),
                  pl.BlockSpec((1, 1, tn), lambda i, j: (jnp.minimum(j, n_qk_tiles - 1), 0, 0)),
                  pl.BlockSpec(bd.shape, lambda i, j: (0, 0))],
        out_specs=pl.BlockSpec((tm, tn), lambda i, j: (i, j)),
        out_shape=jax.ShapeDtypeStruct((t, n), F32),
        scratch_shapes=[pltpu.VMEM((tm, d), BF16)],
        compiler_params=_cparams("arbitrary", "arbitrary"),
        name="in_proj",
    )(x2d, mod, n1w, w_packed, qkw, bd)


def _sb_kernel(q_ref, k_ref, v_ref, wlo_ref, whi_ref, o_ref, acc_ref, r_ref, *, tq):
    i = pl.program_id(2)
    half = whi_ref.shape[0]
    lane = lax.broadcasted_iota(jnp.int32, (1, LANES), 1)
    first_half = lane < HEAD_DIM
    q = q_ref[...]
    q_heads = (jnp.where(first_half, q, 0.0).astype(BF16), jnp.where(first_half, 0.0, q).astype(BF16))
    acc_ref[...] = jnp.zeros_like(acc_ref)
    r_ref[...] = jnp.zeros_like(r_ref)

    def step(j, masked):
        ks = pl.multiple_of(j * tq, tq)
        kb = k_ref[pl.ds(ks, tq), :].astype(BF16)
        vb = v_ref[pl.ds(ks, tq), :].astype(BF16)
        if masked:
            causal = (lax.broadcasted_iota(jnp.int32, (tq, tq), 1) < lax.broadcasted_iota(jnp.int32, (tq, tq), 0))
        for hd in range(2):
            y = lax.dot_general(q_heads[hd], kb, NT_DIMS, preferred_element_type=F32)
            neg_abs = lax.bitcast_convert_type(lax.bitcast_convert_type(y, jnp.uint32) | jnp.uint32(0x80000000), F32)
            sp = jnp.maximum(y, 0.0) + jnp.log(1.0 + jnp.exp2(neg_abs)) * LOG2E
            if masked:
                sp = jnp.where(causal, sp, 0.0)
            spb = sp.astype(BF16)
            later_lo = jnp.dot(spb, wlo_ref[...], preferred_element_type=F32)
            later_hi = jnp.dot(spb[:, half:], whi_ref[...], preferred_element_type=F32)
            a = jnp.exp2(y - sp - jnp.concatenate([later_lo, later_hi], axis=1))
            if masked:
                a = jnp.where(causal, a, 0.0)
            pv = jnp.dot(a.astype(BF16), vb, preferred_element_type=F32)
            r = r_ref[hd]
            acc_ref[hd] += jnp.exp2(-r) * pv
            r_ref[hd] = r + (later_lo[:, 0:1] + sp[:, 0:1])

    step(i, True)

    def body(it, carry):
        j = i - 1 - 2 * it
        step(j, False)
        step(j - 1, False)
        return carry

    lax.fori_loop(0, i // 2, body, 0)

    @pl.when(i % 2 == 1)
    def _():
        step(0, False)

    o_ref[...] = jnp.where(first_half, acc_ref[0], acc_ref[1])


def _sb_call(proj, batch, seq_len, n_pairs, tq=512):
    t = proj.shape[0]
    nq = seq_len // tq
    half = tq // 2
    kk = jnp.arange(tq)
    later = (kk[:, None] > kk[None, :]).astype(BF16)
    w_lo = later[:, :half]
    w_hi = later[half:, half:]
    return pl.pallas_call(
        functools.partial(_sb_kernel, tq=tq),
        grid=(batch, n_pairs, nq),
        in_specs=[pl.BlockSpec((tq, LANES), lambda b, p, i: (b * nq + i, p)),
                  pl.BlockSpec((seq_len, LANES), lambda b, p, i: (b, n_pairs + p)),
                  pl.BlockSpec((seq_len, LANES), lambda b, p, i: (b, 2 * n_pairs + p)),
                  pl.BlockSpec((tq, half), lambda b, p, i: (0, 0)),
                  pl.BlockSpec((half, half), lambda b, p, i: (0, 0))],
        out_specs=pl.BlockSpec((tq, LANES), lambda b, p, i: (b * nq + i, p)),
        out_shape=jax.ShapeDtypeStruct((t, n_pairs * LANES), F32),
        scratch_shapes=[pltpu.VMEM((2, tq, LANES), F32), pltpu.VMEM((2, tq, 1), F32)],
        compiler_params=_cparams("arbitrary", "arbitrary", "arbitrary"),
        name="sb_attention",
    )(proj, proj, proj, w_lo, w_hi)


def _gdn_prep_kernel(q_ref, k_ref, v_ref, qt_ref, kt_ref, vt_ref, ab_ref, cw_ref, alog_ref, dtb_ref,
                     ea_ref, eb_ref, lc_ref, jc_ref, bd_ref,
                     qn_o, qd_o, kn_o, kd_o, kb_o, kbg_o, vb_o, gcol_o, grow_o, scr, *, tm, tiles_per_seq):
    i = pl.program_id(0)
    keep_tail = (i % tiles_per_seq != 0).astype(F32)
    w = cw_ref[...]
    width = q_ref.shape[1]

    def conv_silu(cur_ref, tail_ref, col0):
        scr[0:8, :] = tail_ref[...] * keep_tail
        scr[8:, :] = cur_ref[...]
        y = jnp.zeros((tm, width), F32)
        for tap in range(GDN_CONV):
            off = 8 - (GDN_CONV - 1) + tap
            y = y + scr[off:off + tm, :] * w[tap:tap + 1, col0:col0 + width]
        return _silu(y)

    bd = bd_ref[...]
    cq = conv_silu(q_ref, qt_ref, 0)
    qn = cq * lax.rsqrt(_group_sum_sq(cq, bd) + EPS) * (HEAD_DIM ** -0.5)
    ck = conv_silu(k_ref, kt_ref, width)
    kn = ck * lax.rsqrt(_group_sum_sq(ck, bd) + EPS)
    cv = conv_silu(v_ref, vt_ref, 2 * width)

    ab = ab_ref[...]
    g = -jnp.exp(alog_ref[...]) * _softplus(ab + dtb_ref[...])
    beta = _sigmoid(ab)
    g_cum = jnp.dot(lc_ref[...], g, preferred_element_type=F32, precision=HI)
    g_tot = jnp.dot(jc_ref[...], g, preferred_element_type=F32, precision=HI)
    ea = ea_ref[...]
    gx = jnp.dot(g_cum, ea, preferred_element_type=F32, precision=HI)
    glx = jnp.dot(g_tot, ea, preferred_element_type=F32, precision=HI)
    bx = jnp.dot(beta, eb_ref[...], preferred_element_type=F32, precision=HI)

    e_g = jnp.exp(gx)
    kb = kn * bx
    qn_o[...] = qn
    qd_o[...] = qn * e_g
    kn_o[...] = kn
    kd_o[...] = kn * jnp.exp(glx - gx)
    kb_o[...] = kb
    kbg_o[...] = kb * e_g
    vb_o[...] = cv * bx
    g_heads = g_cum[:, :LANES]
    gcol_o[...] = g_heads
    grow_o[...] = g_heads.T[0:8, :]


def _gdn_prep_call(proj, conv_w, a_log, dt_bias, seq_len, n_heads, col_qkv, col_ab, chunk, tm=512):
    t = proj.shape[0]
    assert n_heads <= 8 and tm % chunk == 0
    width = n_heads * HEAD_DIM
    abw = 2 * LANES
    tiles_per_seq = seq_len // tm
    sub = tm // 8
    alog_pad = jnp.zeros((1, abw), F32).at[0, :n_heads].set(a_log)
    dtb_pad = jnp.zeros((1, abw), F32).at[0, :n_heads].set(dt_bias)
    head_of_lane = jnp.arange(width) // HEAD_DIM
    rows = jnp.arange(abw)
    ea = (rows[:, None] == head_of_lane[None, :]).astype(F32)
    eb = (rows[:, None] == head_of_lane[None, :] + n_heads).astype(F32)
    tok = jnp.arange(tm)
    same_chunk = (tok[:, None] // chunk) == (tok[None, :] // chunk)
    lc = (same_chunk & (tok[:, None] >= tok[None, :])).astype(F32)
    jc = same_chunk.astype(F32)
    bd = _block_diag_ones(2 * LANES, HEAD_DIM, 1.0, BF16)

    cur = lambda c: pl.BlockSpec((tm, width), lambda i: (i, c))
    tail = lambda c: pl.BlockSpec((8, width), lambda i: (jnp.maximum(i * sub - 1, 0), c))
    full = lambda a: pl.BlockSpec(a.shape, lambda i: (0,) * a.ndim)
    cq, ck, cv = col_qkv
    out_spec = pl.BlockSpec((tm, width), lambda i: (i, 0))
    out_shape = jax.ShapeDtypeStruct((t, width), F32)
    return pl.pallas_call(
        functools.partial(_gdn_prep_kernel, tm=tm, tiles_per_seq=tiles_per_seq),
        grid=(t // tm,),
        in_specs=[cur(cq), cur(ck), cur(cv), tail(cq), tail(ck), tail(cv),
                  pl.BlockSpec((tm, abw), lambda i: (i, col_ab)),
                  full(conv_w), full(alog_pad), full(dtb_pad), full(ea), full(eb), full(lc), full(jc), full(bd)],
        out_specs=[out_spec] * 7 + [pl.BlockSpec((tm, LANES), lambda i: (i, 0)), pl.BlockSpec((8, tm), lambda i: (0, i))],
        out_shape=[out_shape] * 7 + [jax.ShapeDtypeStruct((t, LANES), F32), jax.ShapeDtypeStruct((8, t), F32)],
        scratch_shapes=[pltpu.VMEM((tm + 8, width), F32)],
        compiler_params=_cparams("arbitrary"),
        name="gdn_prep",
    )(proj, proj, proj, proj, proj, proj, proj, conv_w, alog_pad, dtb_pad, ea, eb, lc, jc, bd)


def _gdn_core_kernel(qn_ref, qd_ref, kn_ref, kd_ref, kb_ref, kbg_ref, vb_ref, gcol_ref, grow_ref,
                     o_ref, s_scr, *, ts, n_heads, c_len):
    @pl.when(pl.program_id(1) == 0)
    def _():
        s_scr[...] = jnp.zeros_like(s_scr)

    row = lax.broadcasted_iota(jnp.int32, (c_len, c_len), 0)
    col = lax.broadcasted_iota(jnp.int32, (c_len, c_len), 1)
    strict = row > col
    incl = row >= col
    eye_state = (lax.broadcasted_iota(jnp.int32, (HEAD_DIM, HEAD_DIM), 0)
                 == lax.broadcasted_iota(jnp.int32, (HEAD_DIM, HEAD_DIM), 1))
    n_levels = (c_len - 1).bit_length()
    n_chunks = ts // c_len
    problems = [(c, h) for c in range(n_chunks) for h in range(n_heads)]

    def bf(x):
        return x.astype(BF16)

    def mm(a, b):
        return jnp.dot(a, b, preferred_element_type=F32)

    def head_tile(ref, c, h):
        pair, half = divmod(h, 2)
        tile = ref[c * c_len:(c + 1) * c_len, pair * LANES:(pair + 1) * LANES]
        return tile[:, half * HEAD_DIM:(half + 1) * HEAD_DIM]

    sibling = [((row >> k) ^ (col >> k)) == 1 for k in range(n_levels)]

    xs, ms, rs, qks = [], [], [], []
    for c, h in problems:
        g_col = gcol_ref[c * c_len:(c + 1) * c_len, h:h + 1]
        g_row = grow_ref[h:h + 1, c * c_len:(c + 1) * c_len]
        decay = jnp.where(incl, jnp.exp(g_col - g_row), 0.0)
        kn_b = bf(head_tile(kn_ref, c, h))
        kk = lax.dot_general(bf(head_tile(kb_ref, c, h)), kn_b, NT_DIMS, preferred_element_type=F32)
        x = jnp.where(strict, -(kk * decay), 0.0)
        xs.append(x)
        ms.append(jnp.where(row == col, 1.0, jnp.where(sibling[0], x, 0.0)))
        qks.append(bf(lax.dot_general(bf(head_tile(qn_ref, c, h)), kn_b, NT_DIMS, preferred_element_type=F32) * decay))
        rs.append(bf(jnp.concatenate([head_tile(vb_ref, c, h), head_tile(kbg_ref, c, h)], axis=1)))

    for k in range(1, n_levels):
        for idx in range(len(problems)):
            m_b = bf(ms[idx])
            left = mm(m_b, bf(jnp.where(sibling[k], xs[idx], 0.0)))
            ms[idx] = ms[idx] + mm(bf(left), m_b)

    q_eff, o_zero, p_mat, b_mat = {}, {}, {}, {}
    for idx, (c, h) in enumerate(problems):
        sol = bf(mm(bf(ms[idx]), rs[idx]))
        m1 = mm(qks[idx], sol)
        q_eff[c, h] = bf(head_tile(qd_ref, c, h) - m1[:, HEAD_DIM:])
        o_zero[c, h] = m1[:, :HEAD_DIM]
        m2 = lax.dot_general(bf(head_tile(kd_ref, c, h)), sol, TN_DIMS, preferred_element_type=F32)
        chunk_decay = jnp.exp(gcol_ref[(c + 1) * c_len - 1:(c + 1) * c_len, h:h + 1])
        p_mat[c, h] = bf(jnp.where(eye_state, chunk_decay, 0.0) - m2[:, HEAD_DIM:])
        b_mat[c, h] = m2[:, :HEAD_DIM]

    states = [s_scr[h] for h in range(n_heads)]
    for c in range(n_chunks):
        outs = []
        for h in range(n_heads):
            s_b = bf(states[h])
            outs.append(mm(q_eff[c, h], s_b) + o_zero[c, h])
            states[h] = mm(p_mat[c, h], s_b) + b_mat[c, h]
        o_ref[c * c_len:(c + 1) * c_len, :] = jnp.concatenate(outs, axis=1)
    for h in range(n_heads):
        s_scr[h] = states[h]


def _gdn_core_call(prep, batch, seq_len, n_heads, c_len, ts=512):
    t, width = prep[0].shape
    n_seq_tiles = seq_len // ts
    spec = pl.BlockSpec((ts, width), lambda b, s: (b * n_seq_tiles + s, 0))
    return pl.pallas_call(
        functools.partial(_gdn_core_kernel, ts=ts, n_heads=n_heads, c_len=c_len),
        grid=(batch, n_seq_tiles),
        in_specs=[spec] * 7 + [pl.BlockSpec((ts, LANES), lambda b, s: (b * n_seq_tiles + s, 0)),
                               pl.BlockSpec((8, ts), lambda b, s: (0, b * n_seq_tiles + s))],
        out_specs=spec,
        out_shape=jax.ShapeDtypeStruct((t, width), F32),
        scratch_shapes=[pltpu.VMEM((n_heads, HEAD_DIM, HEAD_DIM), F32)],
        compiler_params=_cparams("arbitrary", "arbitrary"),
        name="gdn_core",
    )(*prep)


def _merge_kernel(x_ref, ysb_ref, og_ref, z_ref, g0_ref, g1_ref, mod_ref, onw_ref, n2_ref, bd_ref,
                  wsb_ref, wgdn_ref, wo_ref, wq_ref, x1_o, h2t_o, pq_o):
    og = og_ref[...]
    ygdn = og * lax.rsqrt(_group_sum_sq(og, bd_ref[...]) + EPS) * onw_ref[...] * _silu(z_ref[...])
    m = (_sigmoid(g0_ref[...]) * jnp.dot(ysb_ref[...].astype(BF16), wsb_ref[...], preferred_element_type=F32)
         + _sigmoid(g1_ref[...]) * jnp.dot(ygdn.astype(BF16), wgdn_ref[...], preferred_element_type=F32))
    gate1 = mod_ref[0, 2:3, :]
    shift2 = mod_ref[0, 3:4, :]
    scale2 = mod_ref[0, 4:5, :]
    x1 = x_ref[...] + gate1 * jnp.dot(m.astype(BF16), wo_ref[...], preferred_element_type=F32)
    x1_o[...] = x1
    ms2 = jnp.mean(x1 * x1, axis=-1, keepdims=True)
    h2 = x1 * lax.rsqrt(ms2 + EPS) * n2_ref[...] * (1.0 + scale2) + shift2
    h2t_o[...] = h2.T.astype(BF16)
    pq_o[...] = jnp.dot(h2.astype(BF16), wq_ref[...], preferred_element_type=F32)


def _merge_call(x2d, ysb, ogdn, proj, mod, onw, n2w, wsb, wgdn, wo, wq, seq_len, col_g0, col_g1, col_z, tm=256):
    t, d = x2d.shape
    width = ysb.shape[1]
    nq = wq.shape[1]
    bd = _block_diag_ones(2 * LANES, HEAD_DIM, 1.0 / HEAD_DIM, BF16)
    tiles_per_seq = seq_len // tm
    full = lambda a: pl.BlockSpec(a.shape, lambda i: (0,) * a.ndim)
    return pl.pallas_call(
        _merge_kernel,
        grid=(t // tm,),
        in_specs=[pl.BlockSpec((tm, d), lambda i: (i, 0)),
                  pl.BlockSpec((tm, width), lambda i: (i, 0)),
                  pl.BlockSpec((tm, width), lambda i: (i, 0)),
                  pl.BlockSpec((tm, width), lambda i: (i, col_z)),
                  pl.BlockSpec((tm, d), lambda i: (i, col_g0)),
                  pl.BlockSpec((tm, d), lambda i: (i, col_g1)),
                  pl.BlockSpec((1, 6, d), lambda i: (i // tiles_per_seq, 0, 0)),
                  full(onw), full(n2w), full(bd), full(wsb), full(wgdn), full(wo), full(wq)],
        out_specs=[pl.BlockSpec((tm, d), lambda i: (i, 0)),
                   pl.BlockSpec((d, tm), lambda i: (0, i)),
                   pl.BlockSpec((tm, nq), lambda i: (i, 0))],
        out_shape=[jax.ShapeDtypeStruct((t, d), F32),
                   jax.ShapeDtypeStruct((d, t), BF16),
                   jax.ShapeDtypeStruct((t, nq), F32)],
        compiler_params=_cparams("arbitrary"),
        name="merge_proj",
    )(x2d, ysb, ogdn, proj, proj, proj, mod, onw, n2w, bd, wsb, wgdn, wo, wq)


def _extract_topk(s, k):
    n = s.shape[0]
    iota = lax.broadcasted_iota(jnp.int32, s.shape, 0).astype(F32)
    rank = jnp.full(s.shape, NOT_RANKED, F32)
    vals, idxs = [], []
    for r in range(k):
        m = jnp.max(s, axis=0, keepdims=True)
        idx = jnp.min(jnp.where(s == m, iota, float(n)), axis=0, keepdims=True)
        hit = iota == idx
        rank = jnp.where(hit, float(r), rank)
        s = jnp.where(hit, -jnp.inf, s)
        vals.append(m)
        idxs.append(idx)
    return vals, idxs, rank


def _candidate_tables(k):
    pairs = [(a, b) for a in range(k) for b in range(k) if (a + 1) * (b + 1) <= k]
    n_pad = -(-len(pairs) // 8) * 8
    sel_a = jnp.zeros((n_pad, k), F32).at[jnp.arange(len(pairs)), jnp.array([a for a, _ in pairs])].set(1.0)
    sel_b = jnp.zeros((n_pad, k), F32).at[jnp.arange(len(pairs)), jnp.array([b for _, b in pairs])].set(1.0)
    return sel_a, sel_b, len(pairs)


def _bf16_pair_words(x):
    bits = lax.bitcast_convert_type(x.astype(BF16).astype(F32), jnp.uint32)
    return bits | (bits >> 16)


def _route_kernel(pq_ref, keys_ref, sela_ref, selb_ref, rank2_o, e2_o, cnt1_o, w1_o, *, n_cand):
    k = PEER_TOPK
    hp, _, _, half = keys_ref.shape
    tt = pq_ref.shape[0]
    w = hp * tt
    iota_k = lax.broadcasted_iota(jnp.int32, (k, w), 0).astype(F32)
    s_all = jnp.concatenate(
        [lax.dot_general(keys_ref[hh, part], pq_ref[:, (2 * hh + part) * half:(2 * hh + part + 1) * half], NT_DIMS,
                         preferred_element_type=F32, precision=HI) for part in range(2) for hh in range(hp)], axis=1)
    vals, _, rank = _extract_topk(s_all, k)
    top = jnp.concatenate(vals, axis=0)
    top1, top2 = top[:, :w], top[:, w:]
    v1, v2 = [v[:, :w] for v in vals], [v[:, w:] for v in vals]
    rank1, rank2 = rank[:, :w], rank[:, w:]
    scores = (s_all[:, :w], s_all[:, w:])
    sel_a = sela_ref[...]
    cand = (jnp.dot(sel_a, top1, preferred_element_type=F32, precision=HI)
            + jnp.dot(selb_ref[...], top2, preferred_element_type=F32, precision=HI))
    cand_row = lax.broadcasted_iota(jnp.int32, cand.shape, 0)
    _, _, cand_rank = _extract_topk(jnp.where(cand_row < n_cand, cand, -jnp.inf), k)
    chosen = (cand_rank < float(k)).astype(BF16)
    count = lax.dot_general(sel_a.astype(BF16), chosen, TN_DIMS, preferred_element_type=F32)
    e1 = jnp.exp(top1 - v1[0])
    e2 = jnp.exp(top2 - v2[0])
    z = jnp.zeros_like(v1[0])
    for a in range(k):
        z = z + e1[a:a + 1] * jnp.sum(jnp.where(iota_k < count[a:a + 1], e2, 0.0), axis=0, keepdims=True)
    inv_z = 1.0 / z
    cnt1 = jnp.zeros_like(rank1)
    for a in range(k):
        cnt1 = jnp.where(rank1 == float(a), count[a:a + 1], cnt1)
    e2_all = jnp.exp(scores[1] - v2[0]).astype(BF16)
    w1_all = jnp.where(rank1 < float(k), jnp.exp(scores[0] - v1[0]) * inv_z, 0.0)
    cnt1_words = _bf16_pair_words(cnt1)
    w1_words = _bf16_pair_words(w1_all)
    for hh in range(hp):
        lanes = slice(hh * tt, (hh + 1) * tt)
        rank2_o[hh] = rank2[:, lanes].astype(BF16)
        e2_o[hh] = e2_all[:, lanes]
        cnt1_o[hh] = cnt1_words[:, lanes]
        w1_o[hh] = w1_words[:, lanes]


def _route_call(pq, sub_keys, tt=256, hp=4):
    t = pq.shape[0]
    n_heads, _, n_keys, half = sub_keys.shape
    sel_a, sel_b, n_cand = _candidate_tables(PEER_TOPK)
    out_spec = pl.BlockSpec((hp, n_keys, tt), lambda i, h: (h, 0, i))
    shape = lambda dt: jax.ShapeDtypeStruct((n_heads, n_keys, t), dt)
    return pl.pallas_call(
        functools.partial(_route_kernel, n_cand=n_cand),
        grid=(t // tt, n_heads // hp),
        in_specs=[pl.BlockSpec((tt, hp * 2 * half), lambda i, h: (i, h)),
                  pl.BlockSpec((hp, 2, n_keys, half), lambda i, h: (h, 0, 0, 0)),
                  pl.BlockSpec(sel_a.shape, lambda i, h: (0, 0)),
                  pl.BlockSpec(sel_b.shape, lambda i, h: (0, 0))],
        out_specs=[out_spec] * 4,
        out_shape=[shape(BF16), shape(BF16), shape(jnp.uint32), shape(jnp.uint32)],
        compiler_params=_cparams("arbitrary", "arbitrary"),
        name="peer_route",
    )(pq, sub_keys, sel_a, sel_b)


def _peer_kernel(h2t_ref, u_ref, vt_ref, rank2_ref, e2_ref, cnt1_ref, w1_ref, x1_ref, mod_ref,
                 o_ref, acc_scr, *, n_heads, n_keys, ec, group):
    c = pl.program_id(1)

    @pl.when(c == 0)
    def _():
        acc_scr[...] = jnp.zeros_like(acc_scr)

    h2t = h2t_ref[...]
    tt = h2t.shape[1]
    zero = jnp.zeros((), BF16)
    n_groups = ec // group
    subs_per_group = group // n_keys

    def act_of(grp):
        return jnp.dot(u_ref[grp * group:(grp + 1) * group, :], h2t, preferred_element_type=F32)

    def coef_of(grp, act):
        coefs = []
        for s_loc in range(subs_per_group):
            sub = grp * subs_per_group + s_loc
            gate = None
            for h in range(n_heads):
                cnt = pltpu.bitcast(jnp.broadcast_to(cnt1_ref[h, sub:sub + 1, :], (8, tt)), BF16)
                w1 = pltpu.bitcast(jnp.broadcast_to(w1_ref[h, sub:sub + 1, :], (8, tt)), BF16)
                term = jnp.where(rank2_ref[h] < cnt[None], e2_ref[h], zero) * w1[None]
                gate = term if gate is None else gate + term
            a = act[s_loc * n_keys:(s_loc + 1) * n_keys, :].astype(BF16)
            gelu = (0.5 * a) * (1.0 + lax.erf(a * (2.0 ** -0.5)))
            coefs.append(gate.reshape(n_keys, tt) * gelu)
        return jnp.concatenate(coefs, axis=0)

    total = acc_scr[...]
    act = act_of(0)
    for grp in range(n_groups):
        act_next = act_of(grp + 1) if grp + 1 < n_groups else None
        coef = coef_of(grp, act)
        total = total + jnp.dot(vt_ref[:, grp * group:(grp + 1) * group], coef, preferred_element_type=F32)
        act = act_next
    acc_scr[...] = total

    @pl.when(c == pl.num_programs(1) - 1)
    def _():
        gate2 = mod_ref[0, 5:6, :]
        o_ref[...] = x1_ref[...] + gate2 * acc_scr[...].T


def _peer_call(h2t, u_b, vt_b, rank2, e2, cnt1, w1, x1, mod, seq_len, tt=512, ec=2048, group=1024):
    d, t = h2t.shape
    n_exp = u_b.shape[0]
    n_heads, n_keys, _ = rank2.shape
    tiles_per_seq = seq_len // tt
    tile_rows = 16
    rank2 = rank2.reshape(n_heads, n_keys // tile_rows, tile_rows, t)
    e2 = e2.reshape(n_heads, n_keys // tile_rows, tile_rows, t)
    route_spec = pl.BlockSpec((n_heads, n_keys // tile_rows, tile_rows, tt), lambda i, c: (0, 0, 0, i))
    row_spec = pl.BlockSpec((n_heads, ec // n_keys, tt), lambda i, c: (0, c, i))
    return pl.pallas_call(
        functools.partial(_peer_kernel, n_heads=n_heads, n_keys=n_keys, ec=ec, group=group),
        grid=(t // tt, n_exp // ec),
        in_specs=[pl.BlockSpec((d, tt), lambda i, c: (0, i)),
                  pl.BlockSpec((ec, d), lambda i, c: (c, 0)),
                  pl.BlockSpec((d, ec), lambda i, c: (0, c)),
                  route_spec, route_spec, row_spec, row_spec,
                  pl.BlockSpec((tt, d), lambda i, c: (i, 0)),
                  pl.BlockSpec((1, 6, d), lambda i, c: (i // tiles_per_seq, 0, 0))],
        out_specs=pl.BlockSpec((tt, d), lambda i, c: (i, 0)),
        out_shape=jax.ShapeDtypeStruct((t, d), F32),
        scratch_shapes=[pltpu.VMEM((d, tt), F32)],
        compiler_params=_cparams("arbitrary", "arbitrary"),
        name="peer_experts",
    )(h2t, u_b, vt_b, rank2, e2, cnt1, w1, x1, mod)


def _pack_in_proj(w_in, sb_w, gdn_qk_w, gdn_v_w, n_gdn_heads, d_model, tn):
    o_sbq, o_sbk, o_sbv = 0, sb_w, 2 * sb_w
    o_gdn = 3 * sb_w
    conv_w = 2 * gdn_qk_w + gdn_v_w
    o_a = o_gdn + conv_w
    o_b = o_a + n_gdn_heads
    o_z = o_b + n_gdn_heads
    o_gate = o_z + gdn_v_w
    pad = (-w_in.shape[1]) % tn
    packed = jnp.concatenate([
        w_in[:, o_sbq:o_a],
        w_in[:, o_gate:o_gate + 2 * d_model],
        w_in[:, o_z:o_z + gdn_v_w],
        w_in[:, o_a:o_z],
        jnp.zeros((w_in.shape[0], pad), w_in.dtype)], axis=1).astype(BF16)
    return packed


def _block(x2d, c, w_ada, b_ada, norm1_w, w_in, sb_q_norm_w, sb_k_norm_w, gdn_conv_w, gdn_A_log,
           gdn_dt_bias, gdn_o_norm_w, w_proj_sb, w_proj_gdn, w_o, norm2_w, peer_w_q, peer_sub_keys,
           peer_u, peer_v, batch, seq_len):
    t, d = x2d.shape
    sb_w = w_proj_sb.shape[0]
    gdn_v_w = w_proj_gdn.shape[0]
    gdn_qk_w = (gdn_conv_w.shape[1] - gdn_v_w) // 2
    n_sb_heads = sb_w // HEAD_DIM
    n_gdn_heads = gdn_v_w // HEAD_DIM
    assert gdn_qk_w == gdn_v_w == sb_w and d % (4 * LANES) == 0

    c_pad = jnp.zeros((8, d), F32).at[:batch].set(c)
    mod = _ada_call(c_pad, w_ada, b_ada)[:batch].reshape(batch, 6, d)

    tn = sb_w
    w_packed = _pack_in_proj(w_in, sb_w, gdn_qk_w, gdn_v_w, n_gdn_heads, d, tn)
    heads_per_tile = tn // HEAD_DIM
    q_tiles = sb_w // tn
    qkw = jnp.concatenate([jnp.tile(sb_q_norm_w * (HEAD_DIM ** -0.5 * LOG2E), (q_tiles, heads_per_tile)),
                           jnp.tile(sb_k_norm_w, (q_tiles, heads_per_tile))], axis=0).reshape(2 * q_tiles, 1, tn)
    proj = _inproj_call(x2d, mod, norm1_w.reshape(1, d), w_packed, qkw, seq_len, tn=tn)

    ysb = _sb_call(proj, batch, seq_len, n_sb_heads // 2)

    col_gdn = 3 * sb_w // gdn_v_w
    col_gate = (3 * sb_w + 3 * gdn_v_w) // d
    col_z = (3 * sb_w + 3 * gdn_v_w + 2 * d) // gdn_v_w
    col_ab = (3 * sb_w + 4 * gdn_v_w + 2 * d) // (2 * LANES)
    prep = _gdn_prep_call(proj, gdn_conv_w, gdn_A_log, gdn_dt_bias, seq_len, n_gdn_heads,
                          (col_gdn, col_gdn + 1, col_gdn + 2), col_ab, GDN_BLOCK)
    ogdn = _gdn_core_call(prep, batch, seq_len, n_gdn_heads, GDN_BLOCK)

    x1, h2, pq = _merge_call(
        x2d, ysb, ogdn, proj, mod, jnp.tile(gdn_o_norm_w, n_gdn_heads).reshape(1, gdn_v_w), norm2_w.reshape(1, d),
        w_proj_sb.astype(BF16), w_proj_gdn.astype(BF16), w_o.astype(BF16), peer_w_q.astype(BF16),
        seq_len, col_gate, col_gate + 1, col_z)

    rank2, e2, cnt1, w1 = _route_call(pq, peer_sub_keys)
    return _peer_call(h2, peer_u.astype(BF16), peer_v.T.astype(BF16), rank2, e2, cnt1, w1, x1, mod, seq_len)


def kernel(x, c, w_ada, b_ada, norm1_w, w_in, sb_q_norm_w, sb_k_norm_w, gdn_conv_w, gdn_A_log, gdn_dt_bias,
           gdn_o_norm_w, w_proj_sb, w_proj_gdn, w_o, norm2_w, peer_w_q, peer_sub_keys, peer_u, peer_v):
    batch, seq_len, d = x.shape
    x2d = x.reshape(batch * seq_len, d)
    for l in range(w_ada.shape[0]):
        x2d = _block(x2d, c, w_ada[l], b_ada[l], norm1_w[l], w_in[l], sb_q_norm_w[l], sb_k_norm_w[l],
                     gdn_conv_w[l], gdn_A_log[l], gdn_dt_bias[l], gdn_o_norm_w[l], w_proj_sb[l], w_proj_gdn[l],
                     w_o[l], norm2_w[l], peer_w_q[l], peer_sub_keys[l], peer_u[l], peer_v[l], batch, seq_len)
    return x2d.reshape(batch, seq_len, d)
```

```python
import functools

import jax
import jax.numpy as jnp
from jax import lax
from jax.experimental import pallas as pl
from jax.experimental.pallas import tpu as pltpu

F32 = jnp.float32
BF16 = jnp.bfloat16
HI = lax.Precision.HIGHEST
EPS = 1e-6
LOG2E = 1.4426950408889634

LANES = 128
HEAD_DIM = 64
GDN_BLOCK = 256
GDN_CONV = 4
PEER_TOPK = 16
NOT_RANKED = 99.0
VMEM_LIMIT = 56 * 1024 * 1024

NT_DIMS = (((1,), (1,)), ((), ()))
TN_DIMS = (((0,), (0,)), ((), ()))


def _cparams(*sem, flags=None):
    return pltpu.CompilerParams(dimension_semantics=sem, vmem_limit_bytes=VMEM_LIMIT, flags=flags)


def _sigmoid(x):
    return 1.0 / (1.0 + jnp.exp(-x))


def _silu(x):
    return x * _sigmoid(x)


def _softplus(x):
    return jnp.maximum(x, 0.0) + jnp.log(1.0 + jnp.exp(-jnp.abs(x)))


def _block_diag_ones(n, group, value=1.0, dtype=F32):
    r = jnp.arange(n) // group
    return jnp.where(r[:, None] == r[None, :], value, 0.0).astype(dtype)


def _group_sum_sq(x, bd):
    slab = bd.shape[0]
    outs = []
    for c0 in range(0, x.shape[1], slab):
        sq = x[:, c0:c0 + slab] * x[:, c0:c0 + slab]
        hi = sq.astype(BF16)
        lo = (sq - hi.astype(F32)).astype(BF16)
        outs.append(jnp.dot(hi, bd, preferred_element_type=F32) + jnp.dot(lo, bd, preferred_element_type=F32))
    return outs[0] if len(outs) == 1 else jnp.concatenate(outs, axis=1)


def _ada_kernel(c_ref, w_ref, b_ref, o_ref):
    c = c_ref[...]
    o_ref[...] = jnp.dot(_silu(c), w_ref[...], preferred_element_type=F32, precision=HI) + b_ref[...]


def _ada_call(c_pad, w_ada, b_ada):
    rows, d = c_pad.shape
    n = w_ada.shape[1]
    tn = 512
    return pl.pallas_call(
        _ada_kernel,
        grid=(n // tn,),
        in_specs=[pl.BlockSpec((rows, d), lambda j: (0, 0)),
                  pl.BlockSpec((d, tn), lambda j: (0, j)),
                  pl.BlockSpec((1, tn), lambda j: (0, j))],
        out_specs=pl.BlockSpec((rows, tn), lambda j: (0, j)),
        out_shape=jax.ShapeDtypeStruct((rows, n), F32),
        compiler_params=_cparams("arbitrary"),
        name="ada_mod",
    )(c_pad, w_ada, b_ada.reshape(1, n))


def _inproj_kernel(x_ref, mod_ref, n1_ref, w_ref, qkw_ref, bd_ref, o_ref, h_scr, *, n_qk_tiles):
    j = pl.program_id(1)

    @pl.when(j == 0)
    def _():
        x = x_ref[...]
        ms = jnp.mean(x * x, axis=-1, keepdims=True)
        y = x * lax.rsqrt(ms + EPS) * n1_ref[...]
        shift = mod_ref[0, 0:1, :]
        scale = mod_ref[0, 1:2, :]
        h_scr[...] = (y * (1.0 + scale) + shift).astype(BF16)

    acc = jnp.dot(h_scr[...], w_ref[...], preferred_element_type=F32)

    @pl.when(j < n_qk_tiles)
    def _():
        o_ref[...] = acc * lax.rsqrt(_group_sum_sq(acc, bd_ref[...]) + EPS) * qkw_ref[0]

    @pl.when(j >= n_qk_tiles)
    def _():
        o_ref[...] = acc


def _inproj_call(x2d, mod, n1w, w_packed, qkw, seq_len, tm=1024, tn=512):
    t, d = x2d.shape
    n = w_packed.shape[1]
    n_qk_tiles = qkw.shape[0]
    bd = _block_diag_ones(2 * LANES, HEAD_DIM, 1.0 / HEAD_DIM, BF16)
    tiles_per_seq = seq_len // tm
    return pl.pallas_call(
        functools.partial(_inproj_kernel, n_qk_tiles=n_qk_tiles),
        grid=(t // tm, n // tn),
        in_specs=[pl.BlockSpec((tm, d), lambda i, j: (i, 0)),
                  pl.BlockSpec((1, 6, d), lambda i, j: (i // tiles_per_seq, 0, 0)),
                  pl.BlockSpec((1, d), lambda i, j: (0, 0)),
                  pl.BlockSpec((d, tn), lambda i, j: (0, j)),
                  pl.BlockSpec((1, 1, tn), lambda i, j: (jnp.minimum(j, n_qk_tiles - 1), 0, 0)),
                  pl.BlockSpec(bd.shape, lambda i, j: (0, 0))],
        out_specs=pl.BlockSpec((tm, tn), lambda i, j: (i, j)),
        out_shape=jax.ShapeDtypeStruct((t, n), F32),
        scratch_shapes=[pltpu.VMEM((tm, d), BF16)],
        compiler_params=_cparams("arbitrary", "arbitrary"),
        name="in_proj",
    )(x2d, mod, n1w, w_packed, qkw, bd)


def _sb_kernel(q_ref, k_ref, v_ref, wlo_ref, whi_ref, o_ref, acc_ref, r_ref, *, tq):
    i = pl.program_id(2)
    half = whi_ref.shape[0]
    lane = lax.broadcasted_iota(jnp.int32, (1, LANES), 1)
    first_half = lane < HEAD_DIM
    q = q_ref[...]
    q_heads = (jnp.where(first_half, q, 0.0).astype(BF16), jnp.where(first_half, 0.0, q).astype(BF16))
    acc_ref[...] = jnp.zeros_like(acc_ref)
    r_ref[...] = jnp.zeros_like(r_ref)

    def run(blocks, masked):
        items = [(j, hd) for j in blocks for hd in range(2)]
        st = [dict() for _ in items]
        if masked:
            causal = (lax.broadcasted_iota(jnp.int32, (tq, tq), 1) < lax.broadcasted_iota(jnp.int32, (tq, tq), 0))

        def rows_of(n):
            return pl.ds(pl.multiple_of(items[n][0] * tq, tq), tq)

        def logits(n):
            kb = k_ref[rows_of(n), :].astype(BF16)
            st[n]["y"] = lax.dot_general(q_heads[items[n][1]], kb, NT_DIMS, preferred_element_type=F32)

        def softplus(n):
            y = st[n]["y"]
            neg_abs = lax.bitcast_convert_type(lax.bitcast_convert_type(y, jnp.uint32) | jnp.uint32(0x80000000), F32)
            sp = jnp.maximum(y, 0.0) + jnp.log(1.0 + jnp.exp2(neg_abs)) * LOG2E
            if masked:
                sp = jnp.where(causal, sp, 0.0)
            st[n]["sp"] = sp

        def later_sums(n):
            spb = st[n]["sp"].astype(BF16)
            later_lo = jnp.dot(spb, wlo_ref[...], preferred_element_type=F32)
            later_hi = jnp.dot(spb[:, half:], whi_ref[...], preferred_element_type=F32)
            st[n]["later"] = jnp.concatenate([later_lo, later_hi], axis=1)
            st[n]["total"] = later_lo[:, 0:1] + st[n]["sp"][:, 0:1]

        def weights(n):
            a = jnp.exp2(st[n]["y"] - st[n]["sp"] - st[n]["later"])
            if masked:
                a = jnp.where(causal, a, 0.0)
            st[n]["a"] = a.astype(BF16)

        def accumulate(n):
            hd = items[n][1]
            pv = jnp.dot(st[n]["a"], v_ref[rows_of(n), :].astype(BF16), preferred_element_type=F32)
            r = r_ref[hd]
            acc_ref[hd] += jnp.exp2(-r) * pv
            r_ref[hd] = r + st[n]["total"]

        stages = (logits, softplus, later_sums, weights, accumulate)
        for wave in range(len(items) + len(stages) - 1):
            for n in range(len(items)):
                if 0 <= wave - n < len(stages):
                    stages[wave - n](n)

    run([i], True)

    def body(it, carry):
        j = i - 1 - 2 * it
        run([j, j - 1], False)
        return carry

    lax.fori_loop(0, i // 2, body, 0)

    @pl.when(i % 2 == 1)
    def _():
        run([0], False)

    o_ref[...] = jnp.where(first_half, acc_ref[0], acc_ref[1])


def _sb_call(proj, batch, seq_len, n_pairs, tq=512):
    t = proj.shape[0]
    nq = seq_len // tq
    half = tq // 2
    kk = jnp.arange(tq)
    later = (kk[:, None] > kk[None, :]).astype(BF16)
    w_lo = later[:, :half]
    w_hi = later[half:, half:]
    return pl.pallas_call(
        functools.partial(_sb_kernel, tq=tq),
        grid=(batch, n_pairs, nq),
        in_specs=[pl.BlockSpec((tq, LANES), lambda b, p, i: (b * nq + i, p)),
                  pl.BlockSpec((seq_len, LANES), lambda b, p, i: (b, n_pairs + p)),
                  pl.BlockSpec((seq_len, LANES), lambda b, p, i: (b, 2 * n_pairs + p)),
                  pl.BlockSpec((tq, half), lambda b, p, i: (0, 0)),
                  pl.BlockSpec((half, half), lambda b, p, i: (0, 0))],
        out_specs=pl.BlockSpec((tq, LANES), lambda b, p, i: (b * nq + i, p)),
        out_shape=jax.ShapeDtypeStruct((t, n_pairs * LANES), F32),
        scratch_shapes=[pltpu.VMEM((2, tq, LANES), F32), pltpu.VMEM((2, tq, 1), F32)],
        compiler_params=_cparams("arbitrary", "arbitrary", "arbitrary"),
        name="sb_attention",
    )(proj, proj, proj, w_lo, w_hi)


def _gdn_prep_kernel(q_ref, k_ref, v_ref, qt_ref, kt_ref, vt_ref, ab_ref, cw_ref, alog_ref, dtb_ref,
                     ea_ref, eb_ref, lc_ref, jc_ref, bd_ref,
                     qn_o, qd_o, kn_o, kd_o, kb_o, kbg_o, vb_o, gcol_o, grow_o, scr, *, tm, tiles_per_seq):
    i = pl.program_id(0)
    keep_tail = (i % tiles_per_seq != 0).astype(F32)
    w = cw_ref[...]
    width = q_ref.shape[1]

    def conv_silu(cur_ref, tail_ref, col0):
        scr[0:8, :] = tail_ref[...] * keep_tail
        scr[8:, :] = cur_ref[...]
        y = jnp.zeros((tm, width), F32)
        for tap in range(GDN_CONV):
            off = 8 - (GDN_CONV - 1) + tap
            y = y + scr[off:off + tm, :] * w[tap:tap + 1, col0:col0 + width]
        return _silu(y)

    bd = bd_ref[...]
    cq = conv_silu(q_ref, qt_ref, 0)
    qn = cq * lax.rsqrt(_group_sum_sq(cq, bd) + EPS) * (HEAD_DIM ** -0.5)
    ck = conv_silu(k_ref, kt_ref, width)
    kn = ck * lax.rsqrt(_group_sum_sq(ck, bd) + EPS)
    cv = conv_silu(v_ref, vt_ref, 2 * width)

    ab = ab_ref[...]
    g = -jnp.exp(alog_ref[...]) * _softplus(ab + dtb_ref[...])
    beta = _sigmoid(ab)
    g_cum = jnp.dot(lc_ref[...], g, preferred_element_type=F32, precision=HI)
    g_tot = jnp.dot(jc_ref[...], g, preferred_element_type=F32, precision=HI)
    ea = ea_ref[...]
    gx = jnp.dot(g_cum, ea, preferred_element_type=F32, precision=HI)
    glx = jnp.dot(g_tot, ea, preferred_element_type=F32, precision=HI)
    bx = jnp.dot(beta, eb_ref[...], preferred_element_type=F32, precision=HI)

    e_g = jnp.exp(gx)
    kb = kn * bx
    qn_o[...] = qn
    qd_o[...] = qn * e_g
    kn_o[...] = kn
    kd_o[...] = kn * jnp.exp(glx - gx)
    kb_o[...] = kb
    kbg_o[...] = kb * e_g
    vb_o[...] = cv * bx
    g_heads = g_cum[:, :LANES]
    gcol_o[...] = g_heads
    grow_o[...] = g_heads.T[0:8, :]


def _gdn_prep_call(proj, conv_w, a_log, dt_bias, seq_len, n_heads, col_qkv, col_ab, chunk, tm=512):
    t = proj.shape[0]
    assert n_heads <= 8 and tm % chunk == 0
    width = n_heads * HEAD_DIM
    abw = 2 * LANES
    tiles_per_seq = seq_len // tm
    sub = tm // 8
    alog_pad = jnp.zeros((1, abw), F32).at[0, :n_heads].set(a_log)
    dtb_pad = jnp.zeros((1, abw), F32).at[0, :n_heads].set(dt_bias)
    head_of_lane = jnp.arange(width) // HEAD_DIM
    rows = jnp.arange(abw)
    ea = (rows[:, None] == head_of_lane[None, :]).astype(F32)
    eb = (rows[:, None] == head_of_lane[None, :] + n_heads).astype(F32)
    tok = jnp.arange(tm)
    same_chunk = (tok[:, None] // chunk) == (tok[None, :] // chunk)
    lc = (same_chunk & (tok[:, None] >= tok[None, :])).astype(F32)
    jc = same_chunk.astype(F32)
    bd = _block_diag_ones(2 * LANES, HEAD_DIM, 1.0, BF16)

    cur = lambda c: pl.BlockSpec((tm, width), lambda i: (i, c))
    tail = lambda c: pl.BlockSpec((8, width), lambda i: (jnp.maximum(i * sub - 1, 0), c))
    full = lambda a: pl.BlockSpec(a.shape, lambda i: (0,) * a.ndim)
    cq, ck, cv = col_qkv
    out_spec = pl.BlockSpec((tm, width), lambda i: (i, 0))
    out_shape = jax.ShapeDtypeStruct((t, width), F32)
    return pl.pallas_call(
        functools.partial(_gdn_prep_kernel, tm=tm, tiles_per_seq=tiles_per_seq),
        grid=(t // tm,),
        in_specs=[cur(cq), cur(ck), cur(cv), tail(cq), tail(ck), tail(cv),
                  pl.BlockSpec((tm, abw), lambda i: (i, col_ab)),
                  full(conv_w), full(alog_pad), full(dtb_pad), full(ea), full(eb), full(lc), full(jc), full(bd)],
        out_specs=[out_spec] * 7 + [pl.BlockSpec((tm, LANES), lambda i: (i, 0)), pl.BlockSpec((8, tm), lambda i: (0, i))],
        out_shape=[out_shape] * 7 + [jax.ShapeDtypeStruct((t, LANES), F32), jax.ShapeDtypeStruct((8, t), F32)],
        scratch_shapes=[pltpu.VMEM((tm + 8, width), F32)],
        compiler_params=_cparams("arbitrary"),
        name="gdn_prep",
    )(proj, proj, proj, proj, proj, proj, proj, conv_w, alog_pad, dtb_pad, ea, eb, lc, jc, bd)


def _gdn_core_kernel(qn_ref, qd_ref, kn_ref, kd_ref, kb_ref, kbg_ref, vb_ref, gcol_ref, grow_ref,
                     o_ref, s_scr, *, ts, n_heads, c_len):
    @pl.when(pl.program_id(1) == 0)
    def _():
        s_scr[...] = jnp.zeros_like(s_scr)

    row = lax.broadcasted_iota(jnp.int32, (c_len, c_len), 0)
    col = lax.broadcasted_iota(jnp.int32, (c_len, c_len), 1)
    strict = row > col
    incl = row >= col
    eye_state = (lax.broadcasted_iota(jnp.int32, (HEAD_DIM, HEAD_DIM), 0)
                 == lax.broadcasted_iota(jnp.int32, (HEAD_DIM, HEAD_DIM), 1))
    n_levels = (c_len - 1).bit_length()
    n_chunks = ts // c_len
    problems = [(c, h) for c in range(n_chunks) for h in range(n_heads)]

    def bf(x):
        return x.astype(BF16)

    def mm(a, b):
        return jnp.dot(a, b, preferred_element_type=F32)

    def head_tile(ref, c, h):
        pair, half = divmod(h, 2)
        tile = ref[c * c_len:(c + 1) * c_len, pair * LANES:(pair + 1) * LANES]
        return tile[:, half * HEAD_DIM:(half + 1) * HEAD_DIM]

    sibling = [((row >> k) ^ (col >> k)) == 1 for k in range(n_levels)]

    xs, ms, rs, qks = [], [], [], []
    for c, h in problems:
        g_col = gcol_ref[c * c_len:(c + 1) * c_len, h:h + 1]
        g_row = grow_ref[h:h + 1, c * c_len:(c + 1) * c_len]
        decay = jnp.where(incl, jnp.exp(g_col - g_row), 0.0)
        kn_b = bf(head_tile(kn_ref, c, h))
        kk = lax.dot_general(bf(head_tile(kb_ref, c, h)), kn_b, NT_DIMS, preferred_element_type=F32)
        x = jnp.where(strict, -(kk * decay), 0.0)
        xs.append(x)
        ms.append(jnp.where(row == col, 1.0, jnp.where(sibling[0], x, 0.0)))
        qks.append(bf(lax.dot_general(bf(head_tile(qn_ref, c, h)), kn_b, NT_DIMS, preferred_element_type=F32) * decay))
        rs.append(bf(jnp.concatenate([head_tile(vb_ref, c, h), head_tile(kbg_ref, c, h)], axis=1)))

    for k in range(1, n_levels):
        for idx in range(len(problems)):
            m_b = bf(ms[idx])
            left = mm(m_b, bf(jnp.where(sibling[k], xs[idx], 0.0)))
            ms[idx] = ms[idx] + mm(bf(left), m_b)

    q_eff, o_zero, p_mat, b_mat = {}, {}, {}, {}
    for idx, (c, h) in enumerate(problems):
        sol = bf(mm(bf(ms[idx]), rs[idx]))
        m1 = mm(qks[idx], sol)
        q_eff[c, h] = bf(head_tile(qd_ref, c, h) - m1[:, HEAD_DIM:])
        o_zero[c, h] = m1[:, :HEAD_DIM]
        m2 = lax.dot_general(bf(head_tile(kd_ref, c, h)), sol, TN_DIMS, preferred_element_type=F32)
        chunk_decay = jnp.exp(gcol_ref[(c + 1) * c_len - 1:(c + 1) * c_len, h:h + 1])
        p_mat[c, h] = bf(jnp.where(eye_state, chunk_decay, 0.0) - m2[:, HEAD_DIM:])
        b_mat[c, h] = m2[:, :HEAD_DIM]

    states = [s_scr[h] for h in range(n_heads)]
    for c in range(n_chunks):
        outs = []
        for h in range(n_heads):
            s_b = bf(states[h])
            outs.append(mm(q_eff[c, h], s_b) + o_zero[c, h])
            states[h] = mm(p_mat[c, h], s_b) + b_mat[c, h]
        o_ref[c * c_len:(c + 1) * c_len, :] = jnp.concatenate(outs, axis=1)
    for h in range(n_heads):
        s_scr[h] = states[h]


def _gdn_core_call(prep, batch, seq_len, n_heads, c_len, ts=512):
    t, width = prep[0].shape
    n_seq_tiles = seq_len // ts
    spec = pl.BlockSpec((ts, width), lambda b, s: (b * n_seq_tiles + s, 0))
    return pl.pallas_call(
        functools.partial(_gdn_core_kernel, ts=ts, n_heads=n_heads, c_len=c_len),
        grid=(batch, n_seq_tiles),
        in_specs=[spec] * 7 + [pl.BlockSpec((ts, LANES), lambda b, s: (b * n_seq_tiles + s, 0)),
                               pl.BlockSpec((8, ts), lambda b, s: (0, b * n_seq_tiles + s))],
        out_specs=spec,
        out_shape=jax.ShapeDtypeStruct((t, width), F32),
        scratch_shapes=[pltpu.VMEM((n_heads, HEAD_DIM, HEAD_DIM), F32)],
        compiler_params=_cparams("arbitrary", "arbitrary"),
        name="gdn_core",
    )(*prep)


def _merge_kernel(x_ref, ysb_ref, og_ref, z_ref, g0_ref, g1_ref, mod_ref, onw_ref, n2_ref, bd_ref,
                  wsb_ref, wgdn_ref, wo_ref, wq_ref, x1_o, h2t_o, pq_o):
    og = og_ref[...]
    ygdn = og * lax.rsqrt(_group_sum_sq(og, bd_ref[...]) + EPS) * onw_ref[...] * _silu(z_ref[...])
    m = (_sigmoid(g0_ref[...]) * jnp.dot(ysb_ref[...].astype(BF16), wsb_ref[...], preferred_element_type=F32)
         + _sigmoid(g1_ref[...]) * jnp.dot(ygdn.astype(BF16), wgdn_ref[...], preferred_element_type=F32))
    gate1 = mod_ref[0, 2:3, :]
    shift2 = mod_ref[0, 3:4, :]
    scale2 = mod_ref[0, 4:5, :]
    x1 = x_ref[...] + gate1 * jnp.dot(m.astype(BF16), wo_ref[...], preferred_element_type=F32)
    x1_o[...] = x1
    ms2 = jnp.mean(x1 * x1, axis=-1, keepdims=True)
    h2 = x1 * lax.rsqrt(ms2 + EPS) * n2_ref[...] * (1.0 + scale2) + shift2
    h2t_o[...] = h2.T.astype(BF16)
    pq_o[...] = jnp.dot(h2.astype(BF16), wq_ref[...], preferred_element_type=F32)


def _merge_call(x2d, ysb, ogdn, proj, mod, onw, n2w, wsb, wgdn, wo, wq, seq_len, col_g0, col_g1, col_z, tm=256):
    t, d = x2d.shape
    width = ysb.shape[1]
    nq = wq.shape[1]
    bd = _block_diag_ones(2 * LANES, HEAD_DIM, 1.0 / HEAD_DIM, BF16)
    tiles_per_seq = seq_len // tm
    full = lambda a: pl.BlockSpec(a.shape, lambda i: (0,) * a.ndim)
    return pl.pallas_call(
        _merge_kernel,
        grid=(t // tm,),
        in_specs=[pl.BlockSpec((tm, d), lambda i: (i, 0)),
                  pl.BlockSpec((tm, width), lambda i: (i, 0)),
                  pl.BlockSpec((tm, width), lambda i: (i, 0)),
                  pl.BlockSpec((tm, width), lambda i: (i, col_z)),
                  pl.BlockSpec((tm, d), lambda i: (i, col_g0)),
                  pl.BlockSpec((tm, d), lambda i: (i, col_g1)),
                  pl.BlockSpec((1, 6, d), lambda i: (i // tiles_per_seq, 0, 0)),
                  full(onw), full(n2w), full(bd), full(wsb), full(wgdn), full(wo), full(wq)],
        out_specs=[pl.BlockSpec((tm, d), lambda i: (i, 0)),
                   pl.BlockSpec((d, tm), lambda i: (0, i)),
                   pl.BlockSpec((tm, nq), lambda i: (i, 0))],
        out_shape=[jax.ShapeDtypeStruct((t, d), F32),
                   jax.ShapeDtypeStruct((d, t), BF16),
                   jax.ShapeDtypeStruct((t, nq), F32)],
        compiler_params=_cparams("arbitrary"),
        name="merge_proj",
    )(x2d, ysb, ogdn, proj, proj, proj, mod, onw, n2w, bd, wsb, wgdn, wo, wq)


def _extract_topk(s, k, break_ties):
    n = s.shape[0]
    if not break_ties:
        int_min = -2 ** 31
        bits = lax.bitcast_convert_type(s, jnp.int32)
        bits = jnp.where(bits == jnp.int32(int_min), 0, bits)
        key = bits ^ ((bits >> 31) & jnp.int32(0x7FFFFFFF))
        vals = []
        for r in range(k):
            m = jnp.max(key, axis=0, keepdims=True)
            key = jnp.where(key == m, jnp.int32(int_min + r), key)
            vals.append(lax.bitcast_convert_type(m ^ ((m >> 31) & jnp.int32(0x7FFFFFFF)), F32))
        taken = key < jnp.int32(int_min + k)
        rank = jnp.where(taken, (key - jnp.int32(int_min)).astype(F32), NOT_RANKED)
        return vals, rank
    iota = lax.broadcasted_iota(jnp.int32, s.shape, 0).astype(F32)
    rank = jnp.full(s.shape, NOT_RANKED, F32)
    vals = []
    for r in range(k):
        m = jnp.max(s, axis=0, keepdims=True)
        hit = iota == jnp.min(jnp.where(s == m, iota, float(n)), axis=0, keepdims=True)
        rank = jnp.where(hit, float(r), rank)
        s = jnp.where(hit, -jnp.inf, s)
        vals.append(m)
    return vals, rank


def _candidate_tables(k):
    pairs = [(a, b) for a in range(k) for b in range(k) if (a + 1) * (b + 1) <= k]
    n_pad = -(-len(pairs) // 8) * 8
    sel_a = jnp.zeros((n_pad, k), F32).at[jnp.arange(len(pairs)), jnp.array([a for a, _ in pairs])].set(1.0)
    sel_b = jnp.zeros((n_pad, k), F32).at[jnp.arange(len(pairs)), jnp.array([b for _, b in pairs])].set(1.0)
    return sel_a, sel_b, len(pairs)


def _bf16_pair_words(x):
    bits = lax.bitcast_convert_type(x.astype(BF16).astype(F32), jnp.uint32)
    return bits | (bits >> 16)


def _route_kernel(pq_ref, keys_ref, sela_ref, selb_ref, rank2_o, e2_o, cnt1_o, w1_o, *, n_cand):
    k = PEER_TOPK
    hp, _, _, half = keys_ref.shape
    tt = pq_ref.shape[0]
    w = hp * tt
    iota_k = lax.broadcasted_iota(jnp.int32, (k, w), 0).astype(F32)
    s_all = jnp.concatenate(
        [lax.dot_general(keys_ref[hh, part], pq_ref[:, (2 * hh + part) * half:(2 * hh + part + 1) * half], NT_DIMS,
                         preferred_element_type=F32, precision=HI) for part in range(2) for hh in range(hp)], axis=1)
    scores = (s_all[:, :w], s_all[:, w:])
    sel_a = sela_ref[...]

    def route(break_ties):
        vals, rank = _extract_topk(s_all, k, break_ties)
        top = jnp.concatenate(vals, axis=0)
        top1, top2 = top[:, :w], top[:, w:]
        v1, v2 = [v[:, :w] for v in vals], [v[:, w:] for v in vals]
        rank1, rank2 = rank[:, :w], rank[:, w:]
        cand = (jnp.dot(sel_a, top1, preferred_element_type=F32, precision=HI)
                + jnp.dot(selb_ref[...], top2, preferred_element_type=F32, precision=HI))
        cand_row = lax.broadcasted_iota(jnp.int32, cand.shape, 0)
        _, cand_rank = _extract_topk(jnp.where(cand_row < n_cand, cand, -jnp.inf), k, break_ties)
        chosen = (cand_rank < float(k)).astype(BF16)
        count = lax.dot_general(sel_a.astype(BF16), chosen, TN_DIMS, preferred_element_type=F32)
        e1 = jnp.exp(top1 - v1[0])
        e2 = jnp.exp(top2 - v2[0])
        z = jnp.zeros_like(v1[0])
        for a in range(k):
            z = z + e1[a:a + 1] * jnp.sum(jnp.where(iota_k < count[a:a + 1], e2, 0.0), axis=0, keepdims=True)
        inv_z = 1.0 / z
        cnt1 = jnp.zeros_like(rank1)
        for a in range(k):
            cnt1 = jnp.where(rank1 == float(a), count[a:a + 1], cnt1)
        e2_all = jnp.exp(scores[1] - v2[0]).astype(BF16)
        w1_all = jnp.where(rank1 < float(k), jnp.exp(scores[0] - v1[0]) * inv_z, 0.0)
        cnt1_words = _bf16_pair_words(cnt1)
        w1_words = _bf16_pair_words(w1_all)
        for hh in range(hp):
            lanes = slice(hh * tt, (hh + 1) * tt)
            rank2_o[hh] = rank2[:, lanes].astype(BF16)
            e2_o[hh] = e2_all[:, lanes]
            cnt1_o[hh] = cnt1_words[:, lanes]
            w1_o[hh] = w1_words[:, lanes]
        n_ranked = jnp.sum((rank < float(k)).astype(F32), axis=0, keepdims=True)
        n_chosen = jnp.sum((cand_rank < float(k)).astype(F32), axis=0, keepdims=True)
        return jnp.maximum(jnp.max(jnp.abs(n_ranked - k)), jnp.max(jnp.abs(n_chosen - k)))

    excess = route(False)

    @pl.when(excess > 0.0)
    def _():
        route(True)


def _route_call(pq, sub_keys, tt=256, hp=4):
    t = pq.shape[0]
    n_heads, _, n_keys, half = sub_keys.shape
    sel_a, sel_b, n_cand = _candidate_tables(PEER_TOPK)
    out_spec = pl.BlockSpec((hp, n_keys, tt), lambda i, h: (h, 0, i))
    shape = lambda dt: jax.ShapeDtypeStruct((n_heads, n_keys, t), dt)
    return pl.pallas_call(
        functools.partial(_route_kernel, n_cand=n_cand),
        grid=(t // tt, n_heads // hp),
        in_specs=[pl.BlockSpec((tt, hp * 2 * half), lambda i, h: (i, h)),
                  pl.BlockSpec((hp, 2, n_keys, half), lambda i, h: (h, 0, 0, 0)),
                  pl.BlockSpec(sel_a.shape, lambda i, h: (0, 0)),
                  pl.BlockSpec(sel_b.shape, lambda i, h: (0, 0))],
        out_specs=[out_spec] * 4,
        out_shape=[shape(BF16), shape(BF16), shape(jnp.uint32), shape(jnp.uint32)],
        compiler_params=_cparams("arbitrary", "arbitrary"),
        name="peer_route",
    )(pq, sub_keys, sel_a, sel_b)


def _peer_kernel(h2t_ref, u_ref, vt_ref, rank2_ref, e2_ref, cnt1_ref, w1_ref, x1_ref, mod_ref,
                 o_ref, acc_scr, *, n_heads, n_keys, ec, group):
    c = pl.program_id(1)

    @pl.when(c == 0)
    def _():
        acc_scr[...] = jnp.zeros_like(acc_scr)

    h2t = h2t_ref[...]
    tt = h2t.shape[1]
    zero = jnp.zeros((), BF16)
    n_groups = ec // group
    subs_per_group = group // n_keys

    def act_of(grp):
        return jnp.dot(u_ref[grp * group:(grp + 1) * group, :], h2t, preferred_element_type=F32)

    def coef_of(grp, act):
        coefs = []
        for s_loc in range(subs_per_group):
            sub = grp * subs_per_group + s_loc
            gate = None
            for h in range(n_heads):
                cnt = pltpu.bitcast(jnp.broadcast_to(cnt1_ref[h, sub:sub + 1, :], (8, tt)), BF16)
                w1 = pltpu.bitcast(jnp.broadcast_to(w1_ref[h, sub:sub + 1, :], (8, tt)), BF16)
                term = jnp.where(rank2_ref[h] < cnt[None], e2_ref[h], zero) * w1[None]
                gate = term if gate is None else gate + term
            a = act[s_loc * n_keys:(s_loc + 1) * n_keys, :].astype(BF16)
            gelu = (0.5 * a) * (1.0 + lax.erf(a * (2.0 ** -0.5)))
            coefs.append(gate.reshape(n_keys, tt) * gelu)
        return jnp.concatenate(coefs, axis=0)

    total = acc_scr[...]
    act = act_of(0)
    for grp in range(n_groups):
        act_next = act_of(grp + 1) if grp + 1 < n_groups else None
        coef = coef_of(grp, act)
        total = total + jnp.dot(vt_ref[:, grp * group:(grp + 1) * group], coef, preferred_element_type=F32)
        act = act_next
    acc_scr[...] = total

    @pl.when(c == pl.num_programs(1) - 1)
    def _():
        gate2 = mod_ref[0, 5:6, :]
        o_ref[...] = x1_ref[...] + gate2 * acc_scr[...].T


def _peer_call(h2t, u_b, vt_b, rank2, e2, cnt1, w1, x1, mod, seq_len, tt=512, ec=2048, group=1024):
    d, t = h2t.shape
    n_exp = u_b.shape[0]
    n_heads, n_keys, _ = rank2.shape
    tiles_per_seq = seq_len // tt
    tile_rows = 16
    rank2 = rank2.reshape(n_heads, n_keys // tile_rows, tile_rows, t)
    e2 = e2.reshape(n_heads, n_keys // tile_rows, tile_rows, t)
    route_spec = pl.BlockSpec((n_heads, n_keys // tile_rows, tile_rows, tt), lambda i, c: (0, 0, 0, i))
    row_spec = pl.BlockSpec((n_heads, ec // n_keys, tt), lambda i, c: (0, c, i))
    return pl.pallas_call(
        functools.partial(_peer_kernel, n_heads=n_heads, n_keys=n_keys, ec=ec, group=group),
        grid=(t // tt, n_exp // ec),
        in_specs=[pl.BlockSpec((d, tt), lambda i, c: (0, i)),
                  pl.BlockSpec((ec, d), lambda i, c: (c, 0)),
                  pl.BlockSpec((d, ec), lambda i, c: (0, c)),
                  route_spec, route_spec, row_spec, row_spec,
                  pl.BlockSpec((tt, d), lambda i, c: (i, 0)),
                  pl.BlockSpec((1, 6, d), lambda i, c: (i // tiles_per_seq, 0, 0))],
        out_specs=pl.BlockSpec((tt, d), lambda i, c: (i, 0)),
        out_shape=jax.ShapeDtypeStruct((t, d), F32),
        scratch_shapes=[pltpu.VMEM((d, tt), F32)],
        compiler_params=_cparams("arbitrary", "arbitrary"),
        name="peer_experts",
    )(h2t, u_b, vt_b, rank2, e2, cnt1, w1, x1, mod)


def _pack_in_proj(w_in, sb_w, gdn_qk_w, gdn_v_w, n_gdn_heads, d_model, tn):
    o_sbq, o_sbk, o_sbv = 0, sb_w, 2 * sb_w
    o_gdn = 3 * sb_w
    conv_w = 2 * gdn_qk_w + gdn_v_w
    o_a = o_gdn + conv_w
    o_b = o_a + n_gdn_heads
    o_z = o_b + n_gdn_heads
    o_gate = o_z + gdn_v_w
    pad = (-w_in.shape[1]) % tn
    packed = jnp.concatenate([
        w_in[:, o_sbq:o_a],
        w_in[:, o_gate:o_gate + 2 * d_model],
        w_in[:, o_z:o_z + gdn_v_w],
        w_in[:, o_a:o_z],
        jnp.zeros((w_in.shape[0], pad), w_in.dtype)], axis=1).astype(BF16)
    return packed


def _block(x2d, c, w_ada, b_ada, norm1_w, w_in, sb_q_norm_w, sb_k_norm_w, gdn_conv_w, gdn_A_log,
           gdn_dt_bias, gdn_o_norm_w, w_proj_sb, w_proj_gdn, w_o, norm2_w, peer_w_q, peer_sub_keys,
           peer_u, peer_v, batch, seq_len):
    t, d = x2d.shape
    sb_w = w_proj_sb.shape[0]
    gdn_v_w = w_proj_gdn.shape[0]
    gdn_qk_w = (gdn_conv_w.shape[1] - gdn_v_w) // 2
    n_sb_heads = sb_w // HEAD_DIM
    n_gdn_heads = gdn_v_w // HEAD_DIM
    assert gdn_qk_w == gdn_v_w == sb_w and d % (4 * LANES) == 0

    c_pad = jnp.zeros((8, d), F32).at[:batch].set(c)
    mod = _ada_call(c_pad, w_ada, b_ada)[:batch].reshape(batch, 6, d)

    tn = sb_w
    w_packed = _pack_in_proj(w_in, sb_w, gdn_qk_w, gdn_v_w, n_gdn_heads, d, tn)
    heads_per_tile = tn // HEAD_DIM
    q_tiles = sb_w // tn
    qkw = jnp.concatenate([jnp.tile(sb_q_norm_w * (HEAD_DIM ** -0.5 * LOG2E), (q_tiles, heads_per_tile)),
                           jnp.tile(sb_k_norm_w, (q_tiles, heads_per_tile))], axis=0).reshape(2 * q_tiles, 1, tn)
    proj = _inproj_call(x2d, mod, norm1_w.reshape(1, d), w_packed, qkw, seq_len, tn=tn)

    ysb = _sb_call(proj, batch, seq_len, n_sb_heads // 2)

    col_gdn = 3 * sb_w // gdn_v_w
    col_gate = (3 * sb_w + 3 * gdn_v_w) // d
    col_z = (3 * sb_w + 3 * gdn_v_w + 2 * d) // gdn_v_w
    col_ab = (3 * sb_w + 4 * gdn_v_w + 2 * d) // (2 * LANES)
    prep = _gdn_prep_call(proj, gdn_conv_w, gdn_A_log, gdn_dt_bias, seq_len, n_gdn_heads,
                          (col_gdn, col_gdn + 1, col_gdn + 2), col_ab, GDN_BLOCK)
    ogdn = _gdn_core_call(prep, batch, seq_len, n_gdn_heads, GDN_BLOCK)

    x1, h2, pq = _merge_call(
        x2d, ysb, ogdn, proj, mod, jnp.tile(gdn_o_norm_w, n_gdn_heads).reshape(1, gdn_v_w), norm2_w.reshape(1, d),
        w_proj_sb.astype(BF16), w_proj_gdn.astype(BF16), w_o.astype(BF16), peer_w_q.astype(BF16),
        seq_len, col_gate, col_gate + 1, col_z)

    rank2, e2, cnt1, w1 = _route_call(pq, peer_sub_keys)
    return _peer_call(h2, peer_u.astype(BF16), peer_v.T.astype(BF16), rank2, e2, cnt1, w1, x1, mod, seq_len)


def kernel(x, c, w_ada, b_ada, norm1_w, w_in, sb_q_norm_w, sb_k_norm_w, gdn_conv_w, gdn_A_log, gdn_dt_bias,
           gdn_o_norm_w, w_proj_sb, w_proj_gdn, w_o, norm2_w, peer_w_q, peer_sub_keys, peer_u, peer_v):
    batch, seq_len, d = x.shape
    x2d = x.reshape(batch * seq_len, d)
    for l in range(w_ada.shape[0]):
        x2d = _block(x2d, c, w_ada[l], b_ada[l], norm1_w[l], w_in[l], sb_q_norm_w[l], sb_k_norm_w[l],
                     gdn_conv_w[l], gdn_A_log[l], gdn_dt_bias[l], gdn_o_norm_w[l], w_proj_sb[l], w_proj_gdn[l],
                     w_o[l], norm2_w[l], peer_w_q[l], peer_sub_keys[l], peer_u[l], peer_v[l], batch, seq_len)
    return x2d.reshape(batch, seq_len, d)
```

```python
import functools

import jax
import jax.numpy as jnp
from jax import lax
from jax.experimental import pallas as pl
from jax.experimental.pallas import tpu as pltpu

F32 = jnp.float32
BF16 = jnp.bfloat16
HI = lax.Precision.HIGHEST
EPS = 1e-6
LOG2E = 1.4426950408889634

LANES = 128
HEAD_DIM = 64
GDN_BLOCK = 256
GDN_CONV = 4
PEER_TOPK = 16
NOT_RANKED = 99.0
VMEM_LIMIT = 56 * 1024 * 1024

NT_DIMS = (((1,), (1,)), ((), ()))
TN_DIMS = (((0,), (0,)), ((), ()))


def _cparams(*sem, flags=None):
    return pltpu.CompilerParams(dimension_semantics=sem, vmem_limit_bytes=VMEM_LIMIT, flags=flags)


def _sigmoid(x):
    return 1.0 / (1.0 + jnp.exp(-x))


def _silu(x):
    return x * _sigmoid(x)


def _softplus(x):
    return jnp.maximum(x, 0.0) + jnp.log(1.0 + jnp.exp(-jnp.abs(x)))


def _block_diag_ones(n, group, value=1.0, dtype=F32):
    r = jnp.arange(n) // group
    return jnp.where(r[:, None] == r[None, :], value, 0.0).astype(dtype)


def _group_sum_sq(x, bd):
    slab = bd.shape[0]
    outs = []
    for c0 in range(0, x.shape[1], slab):
        sq = x[:, c0:c0 + slab] * x[:, c0:c0 + slab]
        hi = sq.astype(BF16)
        lo = (sq - hi.astype(F32)).astype(BF16)
        outs.append(jnp.dot(hi, bd, preferred_element_type=F32) + jnp.dot(lo, bd, preferred_element_type=F32))
    return outs[0] if len(outs) == 1 else jnp.concatenate(outs, axis=1)


def _ada_kernel(c_ref, w_ref, b_ref, o_ref):
    c = c_ref[...]
    o_ref[...] = jnp.dot(_silu(c), w_ref[...], preferred_element_type=F32, precision=HI) + b_ref[...]


def _ada_call(c_pad, w_ada, b_ada):
    rows, d = c_pad.shape
    n = w_ada.shape[1]
    tn = 512
    return pl.pallas_call(
        _ada_kernel,
        grid=(n // tn,),
        in_specs=[pl.BlockSpec((rows, d), lambda j: (0, 0)),
                  pl.BlockSpec((d, tn), lambda j: (0, j)),
                  pl.BlockSpec((1, tn), lambda j: (0, j))],
        out_specs=pl.BlockSpec((rows, tn), lambda j: (0, j)),
        out_shape=jax.ShapeDtypeStruct((rows, n), F32),
        compiler_params=_cparams("arbitrary"),
        name="ada_mod",
    )(c_pad, w_ada, b_ada.reshape(1, n))


def _inproj_kernel(x_ref, mod_ref, n1_ref, w_ref, qkw_ref, bd_ref, o_ref, h_scr, *, n_qk_tiles):
    j = pl.program_id(1)

    @pl.when(j == 0)
    def _():
        x = x_ref[...]
        ms = jnp.mean(x * x, axis=-1, keepdims=True)
        y = x * lax.rsqrt(ms + EPS) * n1_ref[...]
        shift = mod_ref[0, 0:1, :]
        scale = mod_ref[0, 1:2, :]
        h_scr[...] = (y * (1.0 + scale) + shift).astype(BF16)

    acc = jnp.dot(h_scr[...], w_ref[...], preferred_element_type=F32)

    @pl.when(j < n_qk_tiles)
    def _():
        o_ref[...] = acc * lax.rsqrt(_group_sum_sq(acc, bd_ref[...]) + EPS) * qkw_ref[0]

    @pl.when(j >= n_qk_tiles)
    def _():
        o_ref[...] = acc


def _inproj_call(x2d, mod, n1w, w_packed, qkw, seq_len, tm=1024, tn=512):
    t, d = x2d.shape
    n = w_packed.shape[1]
    n_qk_tiles = qkw.shape[0]
    bd = _block_diag_ones(2 * LANES, HEAD_DIM, 1.0 / HEAD_DIM, BF16)
    tiles_per_seq = seq_len // tm
    return pl.pallas_call(
        functools.partial(_inproj_kernel, n_qk_tiles=n_qk_tiles),
        grid=(t // tm, n // tn),
        in_specs=[pl.BlockSpec((tm, d), lambda i, j: (i, 0)),
                  pl.BlockSpec((1, 6, d), lambda i, j: (i // tiles_per_seq, 0, 0)),
                  pl.BlockSpec((1, d), lambda i, j: (0, 0)),
                  pl.BlockSpec((d, tn), lambda i, j: (0, j)),
                  pl.BlockSpec((1, 1, tn), lambda i, j: (jnp.minimum(j, n_qk_tiles - 1), 0, 0)),
                  pl.BlockSpec(bd.shape, lambda i, j: (0, 0))],
        out_specs=pl.BlockSpec((tm, tn), lambda i, j: (i, j)),
        out_shape=jax.ShapeDtypeStruct((t, n), F32),
        scratch_shapes=[pltpu.VMEM((tm, d), BF16)],
        compiler_params=_cparams("arbitrary", "arbitrary"),
        name="in_proj",
    )(x2d, mod, n1w, w_packed, qkw, bd)


def _sb_kernel(q_ref, k_ref, v_ref, wlo_ref, whi_ref, o_ref, acc_ref, r_ref, *, tq):
    i = pl.program_id(2)
    half = whi_ref.shape[0]
    lane = lax.broadcasted_iota(jnp.int32, (1, LANES), 1)
    first_half = lane < HEAD_DIM
    q = q_ref[...]
    q_heads = (jnp.where(first_half, q, 0.0).astype(BF16), jnp.where(first_half, 0.0, q).astype(BF16))
    acc_ref[...] = jnp.zeros_like(acc_ref)
    r_ref[...] = jnp.zeros_like(r_ref)

    def run(blocks, masked):
        items = [(j, hd) for j in blocks for hd in range(2)]
        st = [dict() for _ in items]
        if masked:
            causal = (lax.broadcasted_iota(jnp.int32, (tq, tq), 1) < lax.broadcasted_iota(jnp.int32, (tq, tq), 0))

        def rows_of(n):
            return pl.ds(pl.multiple_of(items[n][0] * tq, tq), tq)

        def logits(n):
            kb = k_ref[rows_of(n), :].astype(BF16)
            st[n]["y"] = lax.dot_general(q_heads[items[n][1]], kb, NT_DIMS, preferred_element_type=F32)

        def softplus(n):
            y = st[n]["y"]
            neg_abs = lax.bitcast_convert_type(lax.bitcast_convert_type(y, jnp.uint32) | jnp.uint32(0x80000000), F32)
            sp = jnp.maximum(y, 0.0) + jnp.log(1.0 + jnp.exp2(neg_abs)) * LOG2E
            if masked:
                sp = jnp.where(causal, sp, 0.0)
            st[n]["sp"] = sp

        def later_sums(n):
            spb = st[n]["sp"].astype(BF16)
            later_lo = jnp.dot(spb, wlo_ref[...], preferred_element_type=F32)
            later_hi = jnp.dot(spb[:, half:], whi_ref[...], preferred_element_type=F32)
            st[n]["later"] = jnp.concatenate([later_lo, later_hi], axis=1)
            st[n]["total"] = later_lo[:, 0:1] + st[n]["sp"][:, 0:1]

        def weights(n):
            a = jnp.exp2(st[n]["y"] - st[n]["sp"] - st[n]["later"])
            if masked:
                a = jnp.where(causal, a, 0.0)
            st[n]["a"] = a.astype(BF16)

        def accumulate(n):
            hd = items[n][1]
            pv = jnp.dot(st[n]["a"], v_ref[rows_of(n), :].astype(BF16), preferred_element_type=F32)
            r = r_ref[hd]
            acc_ref[hd] += jnp.exp2(-r) * pv
            r_ref[hd] = r + st[n]["total"]

        stages = (logits, softplus, later_sums, weights, accumulate)
        for wave in range(len(items) + len(stages) - 1):
            for n in range(len(items)):
                if 0 <= wave - n < len(stages):
                    stages[wave - n](n)

    run([i], True)

    def body(it, carry):
        j = i - 1 - 2 * it
        run([j, j - 1], False)
        return carry

    lax.fori_loop(0, i // 2, body, 0)

    @pl.when(i % 2 == 1)
    def _():
        run([0], False)

    o_ref[...] = jnp.where(first_half, acc_ref[0], acc_ref[1])


def _sb_call(proj, batch, seq_len, n_pairs, tq=512):
    t = proj.shape[0]
    nq = seq_len // tq
    half = tq // 2
    kk = jnp.arange(tq)
    later = (kk[:, None] > kk[None, :]).astype(BF16)
    w_lo = later[:, :half]
    w_hi = later[half:, half:]
    return pl.pallas_call(
        functools.partial(_sb_kernel, tq=tq),
        grid=(batch, n_pairs, nq),
        in_specs=[pl.BlockSpec((tq, LANES), lambda b, p, i: (b * nq + i, p)),
                  pl.BlockSpec((seq_len, LANES), lambda b, p, i: (b, n_pairs + p)),
                  pl.BlockSpec((seq_len, LANES), lambda b, p, i: (b, 2 * n_pairs + p)),
                  pl.BlockSpec((tq, half), lambda b, p, i: (0, 0)),
                  pl.BlockSpec((half, half), lambda b, p, i: (0, 0))],
        out_specs=pl.BlockSpec((tq, LANES), lambda b, p, i: (b * nq + i, p)),
        out_shape=jax.ShapeDtypeStruct((t, n_pairs * LANES), F32),
        scratch_shapes=[pltpu.VMEM((2, tq, LANES), F32), pltpu.VMEM((2, tq, 1), F32)],
        compiler_params=_cparams("arbitrary", "arbitrary", "arbitrary"),
        name="sb_attention",
    )(proj, proj, proj, w_lo, w_hi)


def _gdn_prep_kernel(q_ref, k_ref, v_ref, qt_ref, kt_ref, vt_ref, ab_ref, cw_ref, alog_ref, dtb_ref,
                     ea_ref, eb_ref, lc_ref, jc_ref, bd_ref,
                     qn_o, qd_o, kn_o, kd_o, kb_o, kbg_o, vb_o, gcol_o, grow_o, scr, *, tm, tiles_per_seq):
    i = pl.program_id(0)
    keep_tail = (i % tiles_per_seq != 0).astype(F32)
    w = cw_ref[...]
    width = q_ref.shape[1]

    def conv_silu(cur_ref, tail_ref, col0):
        scr[0:8, :] = tail_ref[...] * keep_tail
        scr[8:, :] = cur_ref[...]
        y = jnp.zeros((tm, width), F32)
        for tap in range(GDN_CONV):
            off = 8 - (GDN_CONV - 1) + tap
            y = y + scr[off:off + tm, :] * w[tap:tap + 1, col0:col0 + width]
        return _silu(y)

    bd = bd_ref[...]
    cq = conv_silu(q_ref, qt_ref, 0)
    qn = cq * lax.rsqrt(_group_sum_sq(cq, bd) + EPS) * (HEAD_DIM ** -0.5)
    ck = conv_silu(k_ref, kt_ref, width)
    kn = ck * lax.rsqrt(_group_sum_sq(ck, bd) + EPS)
    cv = conv_silu(v_ref, vt_ref, 2 * width)

    ab = ab_ref[...]
    g = -jnp.exp(alog_ref[...]) * _softplus(ab + dtb_ref[...])
    beta = _sigmoid(ab)
    g_cum = jnp.dot(lc_ref[...], g, preferred_element_type=F32, precision=HI)
    g_tot = jnp.dot(jc_ref[...], g, preferred_element_type=F32, precision=HI)
    ea = ea_ref[...]
    gx = jnp.dot(g_cum, ea, preferred_element_type=F32, precision=HI)
    glx = jnp.dot(g_tot, ea, preferred_element_type=F32, precision=HI)
    bx = jnp.dot(beta, eb_ref[...], preferred_element_type=F32, precision=HI)

    e_g = jnp.exp(gx)
    kb = kn * bx
    qn_o[...] = qn
    qd_o[...] = qn * e_g
    kn_o[...] = kn
    kd_o[...] = kn * jnp.exp(glx - gx)
    kb_o[...] = kb
    kbg_o[...] = kb * e_g
    vb_o[...] = cv * bx
    g_heads = g_cum[:, :LANES]
    gcol_o[...] = g_heads
    grow_o[...] = g_heads.T[0:8, :]


def _gdn_prep_call(proj, conv_w, a_log, dt_bias, seq_len, n_heads, col_qkv, col_ab, chunk, tm=512):
    t = proj.shape[0]
    assert n_heads <= 8 and tm % chunk == 0
    width = n_heads * HEAD_DIM
    abw = 2 * LANES
    tiles_per_seq = seq_len // tm
    sub = tm // 8
    alog_pad = jnp.zeros((1, abw), F32).at[0, :n_heads].set(a_log)
    dtb_pad = jnp.zeros((1, abw), F32).at[0, :n_heads].set(dt_bias)
    head_of_lane = jnp.arange(width) // HEAD_DIM
    rows = jnp.arange(abw)
    ea = (rows[:, None] == head_of_lane[None, :]).astype(F32)
    eb = (rows[:, None] == head_of_lane[None, :] + n_heads).astype(F32)
    tok = jnp.arange(tm)
    same_chunk = (tok[:, None] // chunk) == (tok[None, :] // chunk)
    lc = (same_chunk & (tok[:, None] >= tok[None, :])).astype(F32)
    jc = same_chunk.astype(F32)
    bd = _block_diag_ones(2 * LANES, HEAD_DIM, 1.0, BF16)

    cur = lambda c: pl.BlockSpec((tm, width), lambda i: (i, c))
    tail = lambda c: pl.BlockSpec((8, width), lambda i: (jnp.maximum(i * sub - 1, 0), c))
    full = lambda a: pl.BlockSpec(a.shape, lambda i: (0,) * a.ndim)
    cq, ck, cv = col_qkv
    out_spec = pl.BlockSpec((tm, width), lambda i: (i, 0))
    out_shape = jax.ShapeDtypeStruct((t, width), F32)
    return pl.pallas_call(
        functools.partial(_gdn_prep_kernel, tm=tm, tiles_per_seq=tiles_per_seq),
        grid=(t // tm,),
        in_specs=[cur(cq), cur(ck), cur(cv), tail(cq), tail(ck), tail(cv),
                  pl.BlockSpec((tm, abw), lambda i: (i, col_ab)),
                  full(conv_w), full(alog_pad), full(dtb_pad), full(ea), full(eb), full(lc), full(jc), full(bd)],
        out_specs=[out_spec] * 7 + [pl.BlockSpec((tm, LANES), lambda i: (i, 0)), pl.BlockSpec((8, tm), lambda i: (0, i))],
        out_shape=[out_shape] * 7 + [jax.ShapeDtypeStruct((t, LANES), F32), jax.ShapeDtypeStruct((8, t), F32)],
        scratch_shapes=[pltpu.VMEM((tm + 8, width), F32)],
        compiler_params=_cparams("arbitrary"),
        name="gdn_prep",
    )(proj, proj, proj, proj, proj, proj, proj, conv_w, alog_pad, dtb_pad, ea, eb, lc, jc, bd)


def _gdn_core_kernel(qn_ref, qd_ref, kn_ref, kd_ref, kb_ref, kbg_ref, vb_ref, gcol_ref, grow_ref,
                     o_ref, s_scr, *, ts, n_heads, c_len):
    @pl.when(pl.program_id(1) == 0)
    def _():
        s_scr[...] = jnp.zeros_like(s_scr)

    row = lax.broadcasted_iota(jnp.int32, (c_len, c_len), 0)
    col = lax.broadcasted_iota(jnp.int32, (c_len, c_len), 1)
    strict = row > col
    incl = row >= col
    eye_state = (lax.broadcasted_iota(jnp.int32, (HEAD_DIM, HEAD_DIM), 0)
                 == lax.broadcasted_iota(jnp.int32, (HEAD_DIM, HEAD_DIM), 1))
    n_levels = (c_len - 1).bit_length()
    n_chunks = ts // c_len
    problems = [(c, h) for c in range(n_chunks) for h in range(n_heads)]

    def bf(x):
        return x.astype(BF16)

    def mm(a, b):
        return jnp.dot(a, b, preferred_element_type=F32)

    def head_tile(ref, c, h):
        pair, half = divmod(h, 2)
        tile = ref[c * c_len:(c + 1) * c_len, pair * LANES:(pair + 1) * LANES]
        return tile[:, half * HEAD_DIM:(half + 1) * HEAD_DIM]

    sibling = [((row >> k) ^ (col >> k)) == 1 for k in range(n_levels)]

    xs, ms, rs, qks = [], [], [], []
    for c, h in problems:
        g_col = gcol_ref[c * c_len:(c + 1) * c_len, h:h + 1]
        g_row = grow_ref[h:h + 1, c * c_len:(c + 1) * c_len]
        decay = jnp.where(incl, jnp.exp(g_col - g_row), 0.0)
        kn_b = bf(head_tile(kn_ref, c, h))
        kk = lax.dot_general(bf(head_tile(kb_ref, c, h)), kn_b, NT_DIMS, preferred_element_type=F32)
        x = jnp.where(strict, -(kk * decay), 0.0)
        xs.append(x)
        ms.append(jnp.where(row == col, 1.0, jnp.where(sibling[0], x, 0.0)))
        qks.append(bf(lax.dot_general(bf(head_tile(qn_ref, c, h)), kn_b, NT_DIMS, preferred_element_type=F32) * decay))
        rs.append(bf(jnp.concatenate([head_tile(vb_ref, c, h), head_tile(kbg_ref, c, h)], axis=1)))

    for k in range(1, n_levels):
        for idx in range(len(problems)):
            m_b = bf(ms[idx])
            left = mm(m_b, bf(jnp.where(sibling[k], xs[idx], 0.0)))
            ms[idx] = ms[idx] + mm(bf(left), m_b)

    q_eff, o_zero, p_mat, b_mat = {}, {}, {}, {}
    for idx, (c, h) in enumerate(problems):
        sol = bf(mm(bf(ms[idx]), rs[idx]))
        m1 = mm(qks[idx], sol)
        q_eff[c, h] = bf(head_tile(qd_ref, c, h) - m1[:, HEAD_DIM:])
        o_zero[c, h] = m1[:, :HEAD_DIM]
        m2 = lax.dot_general(bf(head_tile(kd_ref, c, h)), sol, TN_DIMS, preferred_element_type=F32)
        chunk_decay = jnp.exp(gcol_ref[(c + 1) * c_len - 1:(c + 1) * c_len, h:h + 1])
        p_mat[c, h] = bf(jnp.where(eye_state, chunk_decay, 0.0) - m2[:, HEAD_DIM:])
        b_mat[c, h] = m2[:, :HEAD_DIM]

    states = [s_scr[h] for h in range(n_heads)]
    for c in range(n_chunks):
        outs = []
        for h in range(n_heads):
            s_b = bf(states[h])
            outs.append(mm(q_eff[c, h], s_b) + o_zero[c, h])
            states[h] = mm(p_mat[c, h], s_b) + b_mat[c, h]
        o_ref[c * c_len:(c + 1) * c_len, :] = jnp.concatenate(outs, axis=1)
    for h in range(n_heads):
        s_scr[h] = states[h]


def _gdn_core_call(prep, batch, seq_len, n_heads, c_len, ts=512):
    t, width = prep[0].shape
    n_seq_tiles = seq_len // ts
    spec = pl.BlockSpec((ts, width), lambda b, s: (b * n_seq_tiles + s, 0))
    return pl.pallas_call(
        functools.partial(_gdn_core_kernel, ts=ts, n_heads=n_heads, c_len=c_len),
        grid=(batch, n_seq_tiles),
        in_specs=[spec] * 7 + [pl.BlockSpec((ts, LANES), lambda b, s: (b * n_seq_tiles + s, 0)),
                               pl.BlockSpec((8, ts), lambda b, s: (0, b * n_seq_tiles + s))],
        out_specs=spec,
        out_shape=jax.ShapeDtypeStruct((t, width), F32),
        scratch_shapes=[pltpu.VMEM((n_heads, HEAD_DIM, HEAD_DIM), F32)],
        compiler_params=_cparams("arbitrary", "arbitrary"),
        name="gdn_core",
    )(*prep)


def _merge_kernel(x_ref, ysb_ref, og_ref, z_ref, g0_ref, g1_ref, mod_ref, onw_ref, n2_ref, bd_ref,
                  wsb_ref, wgdn_ref, wo_ref, wq_ref, x1_o, h2t_o, pq_o):
    og = og_ref[...]
    ygdn = og * lax.rsqrt(_group_sum_sq(og, bd_ref[...]) + EPS) * onw_ref[...] * _silu(z_ref[...])
    m = (_sigmoid(g0_ref[...]) * jnp.dot(ysb_ref[...].astype(BF16), wsb_ref[...], preferred_element_type=F32)
         + _sigmoid(g1_ref[...]) * jnp.dot(ygdn.astype(BF16), wgdn_ref[...], preferred_element_type=F32))
    gate1 = mod_ref[0, 2:3, :]
    shift2 = mod_ref[0, 3:4, :]
    scale2 = mod_ref[0, 4:5, :]
    x1 = x_ref[...] + gate1 * jnp.dot(m.astype(BF16), wo_ref[...], preferred_element_type=F32)
    x1_o[...] = x1
    ms2 = jnp.mean(x1 * x1, axis=-1, keepdims=True)
    h2 = x1 * lax.rsqrt(ms2 + EPS) * n2_ref[...] * (1.0 + scale2) + shift2
    h2t_o[...] = h2.T.astype(BF16)
    pq_o[...] = jnp.dot(h2.astype(BF16), wq_ref[...], preferred_element_type=F32)


def _merge_call(x2d, ysb, ogdn, proj, mod, onw, n2w, wsb, wgdn, wo, wq, seq_len, col_g0, col_g1, col_z, tm=256):
    t, d = x2d.shape
    width = ysb.shape[1]
    nq = wq.shape[1]
    bd = _block_diag_ones(2 * LANES, HEAD_DIM, 1.0 / HEAD_DIM, BF16)
    tiles_per_seq = seq_len // tm
    full = lambda a: pl.BlockSpec(a.shape, lambda i: (0,) * a.ndim)
    return pl.pallas_call(
        _merge_kernel,
        grid=(t // tm,),
        in_specs=[pl.BlockSpec((tm, d), lambda i: (i, 0)),
                  pl.BlockSpec((tm, width), lambda i: (i, 0)),
                  pl.BlockSpec((tm, width), lambda i: (i, 0)),
                  pl.BlockSpec((tm, width), lambda i: (i, col_z)),
                  pl.BlockSpec((tm, d), lambda i: (i, col_g0)),
                  pl.BlockSpec((tm, d), lambda i: (i, col_g1)),
                  pl.BlockSpec((1, 6, d), lambda i: (i // tiles_per_seq, 0, 0)),
                  full(onw), full(n2w), full(bd), full(wsb), full(wgdn), full(wo), full(wq)],
        out_specs=[pl.BlockSpec((tm, d), lambda i: (i, 0)),
                   pl.BlockSpec((d, tm), lambda i: (0, i)),
                   pl.BlockSpec((tm, nq), lambda i: (i, 0))],
        out_shape=[jax.ShapeDtypeStruct((t, d), F32),
                   jax.ShapeDtypeStruct((d, t), BF16),
                   jax.ShapeDtypeStruct((t, nq), F32)],
        compiler_params=_cparams("arbitrary"),
        name="merge_proj",
    )(x2d, ysb, ogdn, proj, proj, proj, mod, onw, n2w, bd, wsb, wgdn, wo, wq)


def _extract_topk(s, k, break_ties):
    n = s.shape[0]
    if not break_ties:
        int_min = -2 ** 31
        bits = lax.bitcast_convert_type(s, jnp.int32)
        bits = jnp.where(bits == jnp.int32(int_min), 0, bits)
        key = bits ^ ((bits >> 31) & jnp.int32(0x7FFFFFFF))
        vals = []
        for r in range(k):
            m = jnp.max(key, axis=0, keepdims=True)
            key = jnp.where(key == m, jnp.int32(int_min + r), key)
            vals.append(lax.bitcast_convert_type(m ^ ((m >> 31) & jnp.int32(0x7FFFFFFF)), F32))
        taken = key < jnp.int32(int_min + k)
        rank = jnp.where(taken, (key - jnp.int32(int_min)).astype(F32), NOT_RANKED)
        return vals, rank
    iota = lax.broadcasted_iota(jnp.int32, s.shape, 0).astype(F32)
    rank = jnp.full(s.shape, NOT_RANKED, F32)
    vals = []
    for r in range(k):
        m = jnp.max(s, axis=0, keepdims=True)
        hit = iota == jnp.min(jnp.where(s == m, iota, float(n)), axis=0, keepdims=True)
        rank = jnp.where(hit, float(r), rank)
        s = jnp.where(hit, -jnp.inf, s)
        vals.append(m)
    return vals, rank


def _candidate_tables(k):
    pairs = [(a, b) for a in range(k) for b in range(k) if (a + 1) * (b + 1) <= k]
    n_pad = -(-len(pairs) // 8) * 8
    sel_a = jnp.zeros((n_pad, k), F32).at[jnp.arange(len(pairs)), jnp.array([a for a, _ in pairs])].set(1.0)
    sel_b = jnp.zeros((n_pad, k), F32).at[jnp.arange(len(pairs)), jnp.array([b for _, b in pairs])].set(1.0)
    return sel_a, sel_b, len(pairs)


def _bf16_pair_words(x):
    bits = lax.bitcast_convert_type(x.astype(BF16).astype(F32), jnp.uint32)
    return bits | (bits >> 16)


def _route_kernel(pq_ref, keys_ref, sela_ref, selb_ref, rank2_o, e2_o, cnt1_o, w1_o, *, n_cand):
    k = PEER_TOPK
    hp, _, _, half = keys_ref.shape
    tt = pq_ref.shape[0]
    w = hp * tt
    iota_k = lax.broadcasted_iota(jnp.int32, (k, w), 0).astype(F32)
    s_all = jnp.concatenate(
        [lax.dot_general(keys_ref[hh, part], pq_ref[:, (2 * hh + part) * half:(2 * hh + part + 1) * half], NT_DIMS,
                         preferred_element_type=F32, precision=HI) for part in range(2) for hh in range(hp)], axis=1)
    scores = (s_all[:, :w], s_all[:, w:])
    sel_a = sela_ref[...]

    def route(break_ties):
        vals, rank = _extract_topk(s_all, k, break_ties)
        top = jnp.concatenate(vals, axis=0)
        top1, top2 = top[:, :w], top[:, w:]
        v1, v2 = [v[:, :w] for v in vals], [v[:, w:] for v in vals]
        rank1, rank2 = rank[:, :w], rank[:, w:]
        cand = (jnp.dot(sel_a, top1, preferred_element_type=F32, precision=HI)
                + jnp.dot(selb_ref[...], top2, preferred_element_type=F32, precision=HI))
        cand_row = lax.broadcasted_iota(jnp.int32, cand.shape, 0)
        _, cand_rank = _extract_topk(jnp.where(cand_row < n_cand, cand, -jnp.inf), k, break_ties)
        chosen = (cand_rank < float(k)).astype(BF16)
        count = lax.dot_general(sel_a.astype(BF16), chosen, TN_DIMS, preferred_element_type=F32)
        e1 = jnp.exp(top1 - v1[0])
        e2 = jnp.exp(top2 - v2[0])
        z = jnp.zeros_like(v1[0])
        for a in range(k):
            z = z + e1[a:a + 1] * jnp.sum(jnp.where(iota_k < count[a:a + 1], e2, 0.0), axis=0, keepdims=True)
        inv_z = 1.0 / z
        cnt1 = jnp.zeros_like(rank1)
        for a in range(k):
            cnt1 = jnp.where(rank1 == float(a), count[a:a + 1], cnt1)
        e2_all = jnp.exp(scores[1] - v2[0]).astype(BF16)
        w1_all = jnp.where(rank1 < float(k), jnp.exp(scores[0] - v1[0]) * inv_z, 0.0)
        cnt1_words = _bf16_pair_words(cnt1)
        w1_words = _bf16_pair_words(w1_all)
        for hh in range(hp):
            lanes = slice(hh * tt, (hh + 1) * tt)
            rank2_o[hh] = rank2[:, lanes].astype(BF16)
            e2_o[hh] = e2_all[:, lanes]
            cnt1_o[hh] = cnt1_words[:, lanes]
            w1_o[hh] = w1_words[:, lanes]
        n_ranked = jnp.sum((rank < float(k)).astype(F32), axis=0, keepdims=True)
        n_chosen = jnp.sum((cand_rank < float(k)).astype(F32), axis=0, keepdims=True)
        return jnp.maximum(jnp.max(jnp.abs(n_ranked - k)), jnp.max(jnp.abs(n_chosen - k)))

    excess = route(False)

    @pl.when(excess > 0.0)
    def _():
        route(True)


def _route_call(pq, sub_keys, tt=256, hp=4):
    t = pq.shape[0]
    n_heads, _, n_keys, half = sub_keys.shape
    sel_a, sel_b, n_cand = _candidate_tables(PEER_TOPK)
    out_spec = pl.BlockSpec((hp, n_keys, tt), lambda i, h: (h, 0, i))
    shape = lambda dt: jax.ShapeDtypeStruct((n_heads, n_keys, t), dt)
    return pl.pallas_call(
        functools.partial(_route_kernel, n_cand=n_cand),
        grid=(t // tt, n_heads // hp),
        in_specs=[pl.BlockSpec((tt, hp * 2 * half), lambda i, h: (i, h)),
                  pl.BlockSpec((hp, 2, n_keys, half), lambda i, h: (h, 0, 0, 0)),
                  pl.BlockSpec(sel_a.shape, lambda i, h: (0, 0)),
                  pl.BlockSpec(sel_b.shape, lambda i, h: (0, 0))],
        out_specs=[out_spec] * 4,
        out_shape=[shape(BF16), shape(BF16), shape(jnp.uint32), shape(jnp.uint32)],
        compiler_params=_cparams("arbitrary", "arbitrary"),
        name="peer_route",
    )(pq, sub_keys, sel_a, sel_b)


def _peer_kernel(h2t_ref, u_ref, vt_ref, rank2_ref, e2_ref, cnt1_ref, w1_ref, x1_ref, mod_ref,
                 o_ref, act_even, act_odd, acc_scr, *, n_heads, n_keys, ec, group, n_chunks, n_items):
    s = pl.program_id(0)
    chunk = jnp.clip(s - 1, 0, n_items - 1) % n_chunks

    @pl.when(s == 0)
    def _():
        act_odd[...] = jnp.zeros_like(act_odd)

    @pl.when((chunk == 0) | (s == 0))
    def _():
        acc_scr[...] = jnp.zeros_like(acc_scr)

    tt = h2t_ref.shape[1]
    zero = jnp.zeros((), BF16)
    n_groups = ec // group
    subs_per_group = group // n_keys

    def coef_of(grp, act):
        coefs = []
        for s_loc in range(subs_per_group):
            sub = grp * subs_per_group + s_loc
            gate = None
            for h in range(n_heads):
                cnt = pltpu.bitcast(jnp.broadcast_to(cnt1_ref[h, sub:sub + 1, :], (8, tt)), BF16)
                w1 = pltpu.bitcast(jnp.broadcast_to(w1_ref[h, sub:sub + 1, :], (8, tt)), BF16)
                term = jnp.where(rank2_ref[h] < cnt[None], e2_ref[h], zero) * w1[None]
                gate = term if gate is None else gate + term
            a = act[s_loc * n_keys:(s_loc + 1) * n_keys, :].astype(BF16)
            gelu = (0.5 * a) * (1.0 + lax.erf(a * (2.0 ** -0.5)))
            coefs.append(gate.reshape(n_keys, tt) * gelu)
        return jnp.concatenate(coefs, axis=0)

    def stages(act_w, act_r):
        total = acc_scr[...]
        half_t = tt // 2
        new_parts = []
        for grp in range(n_groups):
            rows = slice(grp * group, (grp + 1) * group)
            coef = coef_of(grp, act_r[rows, :])
            if grp < 2:
                cols = slice(grp * half_t, (grp + 1) * half_t)
                new_parts.append(jnp.dot(u_ref[...], h2t_ref[:, cols], preferred_element_type=F32))
            total = total + jnp.dot(vt_ref[:, rows], coef, preferred_element_type=F32)
        acc_scr[...] = total
        act_w[...] = jnp.concatenate(new_parts, axis=1)

    @pl.when(s % 2 == 0)
    def _():
        stages(act_even, act_odd)

    @pl.when(s % 2 == 1)
    def _():
        stages(act_odd, act_even)

    @pl.when((chunk == n_chunks - 1) & (s >= 1))
    def _():
        gate2 = mod_ref[0, 5:6, :]
        o_ref[...] = x1_ref[...] + gate2 * acc_scr[...].T


def _peer_call(h2t, u_b, vt_b, rank2, e2, cnt1, w1, x1, mod, seq_len, tt=512, ec=2048, group=1024):
    d, t = h2t.shape
    n_exp = u_b.shape[0]
    n_heads, n_keys, _ = rank2.shape
    tiles_per_seq = seq_len // tt
    n_chunks = n_exp // ec
    n_items = (t // tt) * n_chunks
    first = lambda s: jnp.minimum(s, n_items - 1)
    second = lambda s: jnp.clip(s - 1, 0, n_items - 1)
    tile_rows = 16
    rank2 = rank2.reshape(n_heads, n_keys // tile_rows, tile_rows, t)
    e2 = e2.reshape(n_heads, n_keys // tile_rows, tile_rows, t)
    route_spec = pl.BlockSpec((n_heads, n_keys // tile_rows, tile_rows, tt), lambda s: (0, 0, 0, second(s) // n_chunks))
    row_spec = pl.BlockSpec((n_heads, ec // n_keys, tt), lambda s: (0, second(s) % n_chunks, second(s) // n_chunks))
    return pl.pallas_call(
        functools.partial(_peer_kernel, n_heads=n_heads, n_keys=n_keys, ec=ec, group=group,
                          n_chunks=n_chunks, n_items=n_items),
        grid=(n_items + 1,),
        in_specs=[pl.BlockSpec((d, tt), lambda s: (0, first(s) // n_chunks)),
                  pl.BlockSpec((ec, d), lambda s: (first(s) % n_chunks, 0)),
                  pl.BlockSpec((d, ec), lambda s: (0, second(s) % n_chunks)),
                  route_spec, route_spec, row_spec, row_spec,
                  pl.BlockSpec((tt, d), lambda s: (second(s) // n_chunks, 0)),
                  pl.BlockSpec((1, 6, d), lambda s: (second(s) // n_chunks // tiles_per_seq, 0, 0))],
        out_specs=pl.BlockSpec((tt, d), lambda s: (second(s) // n_chunks, 0)),
        out_shape=jax.ShapeDtypeStruct((t, d), F32),
        scratch_shapes=[pltpu.VMEM((ec, tt), F32), pltpu.VMEM((ec, tt), F32), pltpu.VMEM((d, tt), F32)],
        compiler_params=_cparams("arbitrary"),
        name="peer_experts",
    )(h2t, u_b, vt_b, rank2, e2, cnt1, w1, x1, mod)


def _pack_in_proj(w_in, sb_w, gdn_qk_w, gdn_v_w, n_gdn_heads, d_model, tn):
    o_sbq, o_sbk, o_sbv = 0, sb_w, 2 * sb_w
    o_gdn = 3 * sb_w
    conv_w = 2 * gdn_qk_w + gdn_v_w
    o_a = o_gdn + conv_w
    o_b = o_a + n_gdn_heads
    o_z = o_b + n_gdn_heads
    o_gate = o_z + gdn_v_w
    pad = (-w_in.shape[1]) % tn
    packed = jnp.concatenate([
        w_in[:, o_sbq:o_a],
        w_in[:, o_gate:o_gate + 2 * d_model],
        w_in[:, o_z:o_z + gdn_v_w],
        w_in[:, o_a:o_z],
        jnp.zeros((w_in.shape[0], pad), w_in.dtype)], axis=1).astype(BF16)
    return packed


def _block(x2d, c, w_ada, b_ada, norm1_w, w_in, sb_q_norm_w, sb_k_norm_w, gdn_conv_w, gdn_A_log,
           gdn_dt_bias, gdn_o_norm_w, w_proj_sb, w_proj_gdn, w_o, norm2_w, peer_w_q, peer_sub_keys,
           peer_u, peer_v, batch, seq_len):
    t, d = x2d.shape
    sb_w = w_proj_sb.shape[0]
    gdn_v_w = w_proj_gdn.shape[0]
    gdn_qk_w = (gdn_conv_w.shape[1] - gdn_v_w) // 2
    n_sb_heads = sb_w // HEAD_DIM
    n_gdn_heads = gdn_v_w // HEAD_DIM
    assert gdn_qk_w == gdn_v_w == sb_w and d % (4 * LANES) == 0

    c_pad = jnp.zeros((8, d), F32).at[:batch].set(c)
    mod = _ada_call(c_pad, w_ada, b_ada)[:batch].reshape(batch, 6, d)

    tn = sb_w
    w_packed = _pack_in_proj(w_in, sb_w, gdn_qk_w, gdn_v_w, n_gdn_heads, d, tn)
    heads_per_tile = tn // HEAD_DIM
    q_tiles = sb_w // tn
    qkw = jnp.concatenate([jnp.tile(sb_q_norm_w * (HEAD_DIM ** -0.5 * LOG2E), (q_tiles, heads_per_tile)),
                           jnp.tile(sb_k_norm_w, (q_tiles, heads_per_tile))], axis=0).reshape(2 * q_tiles, 1, tn)
    proj = _inproj_call(x2d, mod, norm1_w.reshape(1, d), w_packed, qkw, seq_len, tn=tn)

    ysb = _sb_call(proj, batch, seq_len, n_sb_heads // 2)

    col_gdn = 3 * sb_w // gdn_v_w
    col_gate = (3 * sb_w + 3 * gdn_v_w) // d
    col_z = (3 * sb_w + 3 * gdn_v_w + 2 * d) // gdn_v_w
    col_ab = (3 * sb_w + 4 * gdn_v_w + 2 * d) // (2 * LANES)
    prep = _gdn_prep_call(proj, gdn_conv_w, gdn_A_log, gdn_dt_bias, seq_len, n_gdn_heads,
                          (col_gdn, col_gdn + 1, col_gdn + 2), col_ab, GDN_BLOCK)
    ogdn = _gdn_core_call(prep, batch, seq_len, n_gdn_heads, GDN_BLOCK)

    x1, h2, pq = _merge_call(
        x2d, ysb, ogdn, proj, mod, jnp.tile(gdn_o_norm_w, n_gdn_heads).reshape(1, gdn_v_w), norm2_w.reshape(1, d),
        w_proj_sb.astype(BF16), w_proj_gdn.astype(BF16), w_o.astype(BF16), peer_w_q.astype(BF16),
        seq_len, col_gate, col_gate + 1, col_z)

    rank2, e2, cnt1, w1 = _route_call(pq, peer_sub_keys)
    return _peer_call(h2, peer_u.astype(BF16), peer_v.T.astype(BF16), rank2, e2, cnt1, w1, x1, mod, seq_len)


def kernel(x, c, w_ada, b_ada, norm1_w, w_in, sb_q_norm_w, sb_k_norm_w, gdn_conv_w, gdn_A_log, gdn_dt_bias,
           gdn_o_norm_w, w_proj_sb, w_proj_gdn, w_o, norm2_w, peer_w_q, peer_sub_keys, peer_u, peer_v):
    batch, seq_len, d = x.shape
    x2d = x.reshape(batch * seq_len, d)
    for l in range(w_ada.shape[0]):
        x2d = _block(x2d, c, w_ada[l], b_ada[l], norm1_w[l], w_in[l], sb_q_norm_w[l], sb_k_norm_w[l],
                     gdn_conv_w[l], gdn_A_log[l], gdn_dt_bias[l], gdn_o_norm_w[l], w_proj_sb[l], w_proj_gdn[l],
                     w_o[l], norm2_w[l], peer_w_q[l], peer_sub_keys[l], peer_u[l], peer_v[l], batch, seq_len)
    return x2d.reshape(batch, seq_len, d)
```

```python
import functools

import jax
import jax.numpy as jnp
from jax import lax
from jax.experimental import pallas as pl
from jax.experimental.pallas import tpu as pltpu

F32 = jnp.float32
BF16 = jnp.bfloat16
HI = lax.Precision.HIGHEST
EPS = 1e-6
LOG2E = 1.4426950408889634

LANES = 128
HEAD_DIM = 64
GDN_BLOCK = 256
GDN_CONV = 4
PEER_TOPK = 16
NOT_RANKED = 99.0
VMEM_LIMIT = 56 * 1024 * 1024

NT_DIMS = (((1,), (1,)), ((), ()))
TN_DIMS = (((0,), (0,)), ((), ()))


def _cparams(*sem, flags=None):
    return pltpu.CompilerParams(dimension_semantics=sem, vmem_limit_bytes=VMEM_LIMIT, flags=flags)


def _sigmoid(x):
    return 1.0 / (1.0 + jnp.exp(-x))


def _silu(x):
    return x * _sigmoid(x)


def _softplus(x):
    return jnp.maximum(x, 0.0) + jnp.log(1.0 + jnp.exp(-jnp.abs(x)))


def _block_diag_ones(n, group, value=1.0, dtype=F32):
    r = jnp.arange(n) // group
    return jnp.where(r[:, None] == r[None, :], value, 0.0).astype(dtype)


def _bf16_pieces(x):
    hi = x.astype(BF16)
    rest = x - hi.astype(F32)
    mid = rest.astype(BF16)
    return hi, mid, (rest - mid.astype(F32)).astype(BF16)


def _group_sum_sq(x, bd):
    slab = bd.shape[0]
    outs = []
    for c0 in range(0, x.shape[1], slab):
        sq = x[:, c0:c0 + slab] * x[:, c0:c0 + slab]
        hi = sq.astype(BF16)
        lo = (sq - hi.astype(F32)).astype(BF16)
        outs.append(jnp.dot(hi, bd, preferred_element_type=F32) + jnp.dot(lo, bd, preferred_element_type=F32))
    return outs[0] if len(outs) == 1 else jnp.concatenate(outs, axis=1)


def _ada_kernel(c_ref, w_ref, b_ref, o_ref):
    c = c_ref[...]
    o_ref[...] = jnp.dot(_silu(c), w_ref[...], preferred_element_type=F32, precision=HI) + b_ref[...]


def _ada_call(c_pad, w_ada, b_ada):
    rows, d = c_pad.shape
    n = w_ada.shape[1]
    tn = 512
    return pl.pallas_call(
        _ada_kernel,
        grid=(n // tn,),
        in_specs=[pl.BlockSpec((rows, d), lambda j: (0, 0)),
                  pl.BlockSpec((d, tn), lambda j: (0, j)),
                  pl.BlockSpec((1, tn), lambda j: (0, j))],
        out_specs=pl.BlockSpec((rows, tn), lambda j: (0, j)),
        out_shape=jax.ShapeDtypeStruct((rows, n), F32),
        compiler_params=_cparams("arbitrary"),
        name="ada_mod",
    )(c_pad, w_ada, b_ada.reshape(1, n))


def _inproj_kernel(x_ref, mod_ref, n1_ref, w_ref, qkw_ref, bd_ref, o_ref, h_scr, *, n_qk_tiles):
    j = pl.program_id(1)

    @pl.when(j == 0)
    def _():
        x = x_ref[...]
        ms = jnp.mean(x * x, axis=-1, keepdims=True)
        y = x * lax.rsqrt(ms + EPS) * n1_ref[...]
        shift = mod_ref[0, 0:1, :]
        scale = mod_ref[0, 1:2, :]
        h_scr[...] = (y * (1.0 + scale) + shift).astype(BF16)

    acc = jnp.dot(h_scr[...], w_ref[...], preferred_element_type=F32)

    @pl.when(j < n_qk_tiles)
    def _():
        o_ref[...] = acc * lax.rsqrt(_group_sum_sq(acc, bd_ref[...]) + EPS) * qkw_ref[0]

    @pl.when(j >= n_qk_tiles)
    def _():
        o_ref[...] = acc


def _inproj_call(x2d, mod, n1w, w_packed, qkw, seq_len, tm=1024, tn=512):
    t, d = x2d.shape
    n = w_packed.shape[1]
    n_qk_tiles = qkw.shape[0]
    bd = _block_diag_ones(2 * LANES, HEAD_DIM, 1.0 / HEAD_DIM, BF16)
    tiles_per_seq = seq_len // tm
    return pl.pallas_call(
        functools.partial(_inproj_kernel, n_qk_tiles=n_qk_tiles),
        grid=(t // tm, n // tn),
        in_specs=[pl.BlockSpec((tm, d), lambda i, j: (i, 0)),
                  pl.BlockSpec((1, 6, d), lambda i, j: (i // tiles_per_seq, 0, 0)),
                  pl.BlockSpec((1, d), lambda i, j: (0, 0)),
                  pl.BlockSpec((d, tn), lambda i, j: (0, j)),
                  pl.BlockSpec((1, 1, tn), lambda i, j: (jnp.minimum(j, n_qk_tiles - 1), 0, 0)),
                  pl.BlockSpec(bd.shape, lambda i, j: (0, 0))],
        out_specs=pl.BlockSpec((tm, tn), lambda i, j: (i, j)),
        out_shape=jax.ShapeDtypeStruct((t, n), F32),
        scratch_shapes=[pltpu.VMEM((tm, d), BF16)],
        compiler_params=_cparams("arbitrary", "arbitrary"),
        name="in_proj",
    )(x2d, mod, n1w, w_packed, qkw, bd)


def _sb_kernel(q_ref, k_ref, v_ref, wlo_ref, whi_ref, o_ref, acc_ref, r_ref, *, tq):
    i = pl.program_id(2)
    half = whi_ref.shape[0]
    lane = lax.broadcasted_iota(jnp.int32, (1, LANES), 1)
    first_half = lane < HEAD_DIM
    q = q_ref[...]
    q_heads = (jnp.where(first_half, q, 0.0).astype(BF16), jnp.where(first_half, 0.0, q).astype(BF16))
    acc_ref[...] = jnp.zeros_like(acc_ref)
    r_ref[...] = jnp.zeros_like(r_ref)

    def run(blocks, masked):
        items = [(j, hd) for j in blocks for hd in range(2)]
        st = [dict() for _ in items]
        if masked:
            causal = (lax.broadcasted_iota(jnp.int32, (tq, tq), 1) < lax.broadcasted_iota(jnp.int32, (tq, tq), 0))

        def rows_of(n):
            return pl.ds(pl.multiple_of(items[n][0] * tq, tq), tq)

        def logits(n):
            kb = k_ref[rows_of(n), :].astype(BF16)
            st[n]["y"] = lax.dot_general(q_heads[items[n][1]], kb, NT_DIMS, preferred_element_type=F32)

        def softplus(n):
            y = st[n]["y"]
            neg_abs = lax.bitcast_convert_type(lax.bitcast_convert_type(y, jnp.uint32) | jnp.uint32(0x80000000), F32)
            sp = jnp.maximum(y, 0.0) + jnp.log(1.0 + jnp.exp2(neg_abs)) * LOG2E
            if masked:
                sp = jnp.where(causal, sp, 0.0)
            st[n]["log_beta"] = y - sp
            st[n]["spb"] = sp.astype(BF16)
            st[n]["sp_first"] = sp[:, 0:1]

        def later_sums(n):
            spb = st[n]["spb"]
            later_lo = jnp.dot(spb, wlo_ref[...], preferred_element_type=F32)
            later_hi = jnp.dot(spb[:, half:], whi_ref[...], preferred_element_type=F32)
            st[n]["later"] = jnp.concatenate([later_lo, later_hi], axis=1)
            st[n]["total"] = later_lo[:, 0:1] + st[n]["sp_first"]

        def weights(n):
            a = jnp.exp2(st[n]["log_beta"] - st[n]["later"])
            if masked:
                a = jnp.where(causal, a, 0.0)
            st[n]["a"] = a.astype(BF16)

        def accumulate(n):
            hd = items[n][1]
            pv = jnp.dot(st[n]["a"], v_ref[rows_of(n), :].astype(BF16), preferred_element_type=F32)
            r = r_ref[hd]
            acc_ref[hd] += jnp.exp2(-r) * pv
            r_ref[hd] = r + st[n]["total"]

        stages = (logits, softplus, later_sums, weights, accumulate)
        elementwise = (softplus, weights)
        for wave in range(len(items) + len(stages) - 1):
            todo = [(stages[wave - n], n) for n in range(len(items)) if 0 <= wave - n < len(stages)]
            for stage, n in [x for x in todo if x[0] in elementwise] + [x for x in todo if x[0] not in elementwise]:
                stage(n)

    run([i], True)

    def body(it, carry):
        j = i - 1 - 2 * it
        run([j, j - 1], False)
        return carry

    lax.fori_loop(0, i // 2, body, 0)

    @pl.when(i % 2 == 1)
    def _():
        run([0], False)

    o_ref[...] = jnp.where(first_half, acc_ref[0], acc_ref[1])


def _sb_call(proj, batch, seq_len, n_pairs, tq=512):
    t = proj.shape[0]
    nq = seq_len // tq
    half = tq // 2
    kk = jnp.arange(tq)
    later = (kk[:, None] > kk[None, :]).astype(BF16)
    w_lo = later[:, :half]
    w_hi = later[half:, half:]
    return pl.pallas_call(
        functools.partial(_sb_kernel, tq=tq),
        grid=(batch, n_pairs, nq),
        in_specs=[pl.BlockSpec((tq, LANES), lambda b, p, i: (b * nq + i, p)),
                  pl.BlockSpec((seq_len, LANES), lambda b, p, i: (b, n_pairs + p)),
                  pl.BlockSpec((seq_len, LANES), lambda b, p, i: (b, 2 * n_pairs + p)),
                  pl.BlockSpec((tq, half), lambda b, p, i: (0, 0)),
                  pl.BlockSpec((half, half), lambda b, p, i: (0, 0))],
        out_specs=pl.BlockSpec((tq, LANES), lambda b, p, i: (b * nq + i, p)),
        out_shape=jax.ShapeDtypeStruct((t, n_pairs * LANES), F32),
        scratch_shapes=[pltpu.VMEM((2, tq, LANES), F32), pltpu.VMEM((2, tq, 1), F32)],
        compiler_params=_cparams("arbitrary", "arbitrary", "arbitrary"),
        name="sb_attention",
    )(proj, proj, proj, w_lo, w_hi)


def _gdn_prep_kernel(q_ref, k_ref, v_ref, qt_ref, kt_ref, vt_ref, ab_ref, cw_ref, alog_ref, dtb_ref,
                     ea_ref, eb_ref, lc_ref, jc_ref, bd_ref,
                     qn_o, qd_o, kn_o, kd_o, kb_o, kbg_o, vb_o, gcol_o, grow_o, scr, *, tm, tiles_per_seq):
    i = pl.program_id(0)
    keep_tail = (i % tiles_per_seq != 0).astype(F32)
    w = cw_ref[...]
    width = q_ref.shape[1]

    def conv_silu(cur_ref, tail_ref, col0):
        scr[0:8, :] = tail_ref[...] * keep_tail
        scr[8:, :] = cur_ref[...]
        y = jnp.zeros((tm, width), F32)
        for tap in range(GDN_CONV):
            off = 8 - (GDN_CONV - 1) + tap
            y = y + scr[off:off + tm, :] * w[tap:tap + 1, col0:col0 + width]
        return _silu(y)

    bd = bd_ref[...]
    cq = conv_silu(q_ref, qt_ref, 0)
    qn = cq * lax.rsqrt(_group_sum_sq(cq, bd) + EPS) * (HEAD_DIM ** -0.5)
    ck = conv_silu(k_ref, kt_ref, width)
    kn = ck * lax.rsqrt(_group_sum_sq(ck, bd) + EPS)
    cv = conv_silu(v_ref, vt_ref, 2 * width)

    ab = ab_ref[...]
    g = -jnp.exp(alog_ref[...]) * _softplus(ab + dtb_ref[...])
    beta = _sigmoid(ab)
    g_parts = _bf16_pieces(g)
    lc, jc, ea = lc_ref[...], jc_ref[...], ea_ref[...]
    g_cum = sum(jnp.dot(lc, p, preferred_element_type=F32) for p in g_parts)
    g_tot = sum(jnp.dot(jc, p, preferred_element_type=F32) for p in g_parts)
    gx = sum(jnp.dot(p, ea, preferred_element_type=F32) for p in _bf16_pieces(g_cum))
    glx = sum(jnp.dot(p, ea, preferred_element_type=F32) for p in _bf16_pieces(g_tot))
    bx = sum(jnp.dot(p, eb_ref[...], preferred_element_type=F32) for p in _bf16_pieces(beta))

    e_g = jnp.exp(gx)
    kb = kn * bx
    qn_o[...] = qn
    qd_o[...] = qn * e_g
    kn_o[...] = kn
    kd_o[...] = kn * jnp.exp(glx - gx)
    kb_o[...] = kb
    kbg_o[...] = kb * e_g
    vb_o[...] = cv * bx
    g_heads = g_cum[:, :LANES]
    gcol_o[...] = g_heads
    grow_o[...] = g_heads.T[0:8, :]


def _gdn_prep_call(proj, conv_w, a_log, dt_bias, seq_len, n_heads, col_qkv, col_ab, chunk, tm=512):
    t = proj.shape[0]
    assert n_heads <= 8 and tm % chunk == 0
    width = n_heads * HEAD_DIM
    abw = 2 * LANES
    tiles_per_seq = seq_len // tm
    sub = tm // 8
    alog_pad = jnp.zeros((1, abw), F32).at[0, :n_heads].set(a_log)
    dtb_pad = jnp.zeros((1, abw), F32).at[0, :n_heads].set(dt_bias)
    head_of_lane = jnp.arange(width) // HEAD_DIM
    rows = jnp.arange(abw)
    ea = (rows[:, None] == head_of_lane[None, :]).astype(BF16)
    eb = (rows[:, None] == head_of_lane[None, :] + n_heads).astype(BF16)
    tok = jnp.arange(tm)
    same_chunk = (tok[:, None] // chunk) == (tok[None, :] // chunk)
    lc = (same_chunk & (tok[:, None] >= tok[None, :])).astype(BF16)
    jc = same_chunk.astype(BF16)
    bd = _block_diag_ones(2 * LANES, HEAD_DIM, 1.0, BF16)

    cur = lambda c: pl.BlockSpec((tm, width), lambda i: (i, c))
    tail = lambda c: pl.BlockSpec((8, width), lambda i: (jnp.maximum(i * sub - 1, 0), c))
    full = lambda a: pl.BlockSpec(a.shape, lambda i: (0,) * a.ndim)
    cq, ck, cv = col_qkv
    out_spec = pl.BlockSpec((tm, width), lambda i: (i, 0))
    out_shape = jax.ShapeDtypeStruct((t, width), F32)
    return pl.pallas_call(
        functools.partial(_gdn_prep_kernel, tm=tm, tiles_per_seq=tiles_per_seq),
        grid=(t // tm,),
        in_specs=[cur(cq), cur(ck), cur(cv), tail(cq), tail(ck), tail(cv),
                  pl.BlockSpec((tm, abw), lambda i: (i, col_ab)),
                  full(conv_w), full(alog_pad), full(dtb_pad), full(ea), full(eb), full(lc), full(jc), full(bd)],
        out_specs=[out_spec] * 7 + [pl.BlockSpec((tm, LANES), lambda i: (i, 0)), pl.BlockSpec((8, tm), lambda i: (0, i))],
        out_shape=[out_shape] * 7 + [jax.ShapeDtypeStruct((t, LANES), F32), jax.ShapeDtypeStruct((8, t), F32)],
        scratch_shapes=[pltpu.VMEM((tm + 8, width), F32)],
        compiler_params=_cparams("arbitrary"),
        name="gdn_prep",
    )(proj, proj, proj, proj, proj, proj, proj, conv_w, alog_pad, dtb_pad, ea, eb, lc, jc, bd)


def _gdn_core_kernel(qn_ref, qd_ref, kn_ref, kd_ref, kb_ref, kbg_ref, vb_ref, gcol_ref, grow_ref,
                     o_ref, s_scr, *, ts, n_heads, c_len):
    @pl.when(pl.program_id(1) == 0)
    def _():
        s_scr[...] = jnp.zeros_like(s_scr)

    row = lax.broadcasted_iota(jnp.int32, (c_len, c_len), 0)
    col = lax.broadcasted_iota(jnp.int32, (c_len, c_len), 1)
    strict = row > col
    incl = row >= col
    eye_state = (lax.broadcasted_iota(jnp.int32, (HEAD_DIM, HEAD_DIM), 0)
                 == lax.broadcasted_iota(jnp.int32, (HEAD_DIM, HEAD_DIM), 1))
    n_levels = (c_len - 1).bit_length()
    n_chunks = ts // c_len
    problems = [(c, h) for c in range(n_chunks) for h in range(n_heads)]

    def bf(x):
        return x.astype(BF16)

    def mm(a, b):
        return jnp.dot(a, b, preferred_element_type=F32)

    def head_tile(ref, c, h):
        pair, half = divmod(h, 2)
        tile = ref[c * c_len:(c + 1) * c_len, pair * LANES:(pair + 1) * LANES]
        return tile[:, half * HEAD_DIM:(half + 1) * HEAD_DIM]

    sibling = [((row >> k) ^ (col >> k)) == 1 for k in range(n_levels)]

    xs, ms, rs, qks = [], [], [], []
    for c, h in problems:
        g_col = gcol_ref[c * c_len:(c + 1) * c_len, h:h + 1]
        g_row = grow_ref[h:h + 1, c * c_len:(c + 1) * c_len]
        decay = jnp.where(incl, jnp.exp(g_col - g_row), 0.0)
        kn_b = bf(head_tile(kn_ref, c, h))
        kk = lax.dot_general(bf(head_tile(kb_ref, c, h)), kn_b, NT_DIMS, preferred_element_type=F32)
        x = jnp.where(strict, -(kk * decay), 0.0)
        xs.append(x)
        ms.append(jnp.where(row == col, 1.0, jnp.where(sibling[0], x, 0.0)))
        qks.append(bf(lax.dot_general(bf(head_tile(qn_ref, c, h)), kn_b, NT_DIMS, preferred_element_type=F32) * decay))
        rs.append(bf(jnp.concatenate([head_tile(vb_ref, c, h), head_tile(kbg_ref, c, h)], axis=1)))

    for k in range(1, n_levels):
        for idx in range(len(problems)):
            m_b = bf(ms[idx])
            left = mm(m_b, bf(jnp.where(sibling[k], xs[idx], 0.0)))
            ms[idx] = ms[idx] + mm(bf(left), m_b)

    q_eff, o_zero, p_mat, b_mat = {}, {}, {}, {}
    for idx, (c, h) in enumerate(problems):
        sol = bf(mm(bf(ms[idx]), rs[idx]))
        m1 = mm(qks[idx], sol)
        q_eff[c, h] = bf(head_tile(qd_ref, c, h) - m1[:, HEAD_DIM:])
        o_zero[c, h] = m1[:, :HEAD_DIM]
        m2 = lax.dot_general(bf(head_tile(kd_ref, c, h)), sol, TN_DIMS, preferred_element_type=F32)
        chunk_decay = jnp.exp(gcol_ref[(c + 1) * c_len - 1:(c + 1) * c_len, h:h + 1])
        p_mat[c, h] = bf(jnp.where(eye_state, chunk_decay, 0.0) - m2[:, HEAD_DIM:])
        b_mat[c, h] = m2[:, :HEAD_DIM]

    states = [s_scr[h] for h in range(n_heads)]
    for c in range(n_chunks):
        outs = []
        for h in range(n_heads):
            s_b = bf(states[h])
            outs.append(mm(q_eff[c, h], s_b) + o_zero[c, h])
            states[h] = mm(p_mat[c, h], s_b) + b_mat[c, h]
        o_ref[c * c_len:(c + 1) * c_len, :] = jnp.concatenate(outs, axis=1)
    for h in range(n_heads):
        s_scr[h] = states[h]


def _gdn_core_call(prep, batch, seq_len, n_heads, c_len, ts=512):
    t, width = prep[0].shape
    n_seq_tiles = seq_len // ts
    spec = pl.BlockSpec((ts, width), lambda b, s: (b * n_seq_tiles + s, 0))
    return pl.pallas_call(
        functools.partial(_gdn_core_kernel, ts=ts, n_heads=n_heads, c_len=c_len),
        grid=(batch, n_seq_tiles),
        in_specs=[spec] * 7 + [pl.BlockSpec((ts, LANES), lambda b, s: (b * n_seq_tiles + s, 0)),
                               pl.BlockSpec((8, ts), lambda b, s: (0, b * n_seq_tiles + s))],
        out_specs=spec,
        out_shape=jax.ShapeDtypeStruct((t, width), F32),
        scratch_shapes=[pltpu.VMEM((n_heads, HEAD_DIM, HEAD_DIM), F32)],
        compiler_params=_cparams("arbitrary", "arbitrary"),
        name="gdn_core",
    )(*prep)


def _merge_kernel(x_ref, ysb_ref, og_ref, z_ref, g0_ref, g1_ref, mod_ref, onw_ref, n2_ref, bd_ref,
                  wsb_ref, wgdn_ref, wo_ref, wq_ref, x1_o, h2t_o, pq_o):
    og = og_ref[...]
    ygdn = og * lax.rsqrt(_group_sum_sq(og, bd_ref[...]) + EPS) * onw_ref[...] * _silu(z_ref[...])
    m = (_sigmoid(g0_ref[...]) * jnp.dot(ysb_ref[...].astype(BF16), wsb_ref[...], preferred_element_type=F32)
         + _sigmoid(g1_ref[...]) * jnp.dot(ygdn.astype(BF16), wgdn_ref[...], preferred_element_type=F32))
    gate1 = mod_ref[0, 2:3, :]
    shift2 = mod_ref[0, 3:4, :]
    scale2 = mod_ref[0, 4:5, :]
    x1 = x_ref[...] + gate1 * jnp.dot(m.astype(BF16), wo_ref[...], preferred_element_type=F32)
    x1_o[...] = x1
    ms2 = jnp.mean(x1 * x1, axis=-1, keepdims=True)
    h2 = x1 * lax.rsqrt(ms2 + EPS) * n2_ref[...] * (1.0 + scale2) + shift2
    h2t_o[...] = h2.T.astype(BF16)
    pq_o[...] = jnp.dot(h2.astype(BF16), wq_ref[...], preferred_element_type=F32)


def _merge_call(x2d, ysb, ogdn, proj, mod, onw, n2w, wsb, wgdn, wo, wq, seq_len, col_g0, col_g1, col_z, tm=256):
    t, d = x2d.shape
    width = ysb.shape[1]
    nq = wq.shape[1]
    bd = _block_diag_ones(2 * LANES, HEAD_DIM, 1.0 / HEAD_DIM, BF16)
    tiles_per_seq = seq_len // tm
    full = lambda a: pl.BlockSpec(a.shape, lambda i: (0,) * a.ndim)
    return pl.pallas_call(
        _merge_kernel,
        grid=(t // tm,),
        in_specs=[pl.BlockSpec((tm, d), lambda i: (i, 0)),
                  pl.BlockSpec((tm, width), lambda i: (i, 0)),
                  pl.BlockSpec((tm, width), lambda i: (i, 0)),
                  pl.BlockSpec((tm, width), lambda i: (i, col_z)),
                  pl.BlockSpec((tm, d), lambda i: (i, col_g0)),
                  pl.BlockSpec((tm, d), lambda i: (i, col_g1)),
                  pl.BlockSpec((1, 6, d), lambda i: (i // tiles_per_seq, 0, 0)),
                  full(onw), full(n2w), full(bd), full(wsb), full(wgdn), full(wo), full(wq)],
        out_specs=[pl.BlockSpec((tm, d), lambda i: (i, 0)),
                   pl.BlockSpec((d, tm), lambda i: (0, i)),
                   pl.BlockSpec((tm, nq), lambda i: (i, 0))],
        out_shape=[jax.ShapeDtypeStruct((t, d), F32),
                   jax.ShapeDtypeStruct((d, t), BF16),
                   jax.ShapeDtypeStruct((t, nq), F32)],
        compiler_params=_cparams("arbitrary"),
        name="merge_proj",
    )(x2d, ysb, ogdn, proj, proj, proj, mod, onw, n2w, bd, wsb, wgdn, wo, wq)


def _extract_topk(s, k, break_ties):
    n = s.shape[0]
    if not break_ties:
        lowest_bits = -8388609
        vals = []
        for r in range(k):
            m = jnp.max(s, axis=0, keepdims=True)
            marker = lax.bitcast_convert_type(jnp.int32(lowest_bits - r), F32)
            s = jnp.where(s == m, marker, s)
            vals.append(m)
        took = jnp.int32(lowest_bits) - lax.bitcast_convert_type(s, jnp.int32)
        rank = jnp.where((took >= 0) & (took < k), took.astype(F32), NOT_RANKED)
        return vals, rank
    iota = lax.broadcasted_iota(jnp.int32, s.shape, 0).astype(F32)
    rank = jnp.full(s.shape, NOT_RANKED, F32)
    vals = []
    for r in range(k):
        m = jnp.max(s, axis=0, keepdims=True)
        hit = iota == jnp.min(jnp.where(s == m, iota, float(n)), axis=0, keepdims=True)
        rank = jnp.where(hit, float(r), rank)
        s = jnp.where(hit, -jnp.inf, s)
        vals.append(m)
    return vals, rank


def _candidate_tables(k):
    pairs = [(a, b) for a in range(k) for b in range(k) if (a + 1) * (b + 1) <= k]
    n_pad = -(-len(pairs) // 8) * 8
    sel_a = jnp.zeros((n_pad, k), F32).at[jnp.arange(len(pairs)), jnp.array([a for a, _ in pairs])].set(1.0)
    sel_b = jnp.zeros((n_pad, k), F32).at[jnp.arange(len(pairs)), jnp.array([b for _, b in pairs])].set(1.0)
    return sel_a, sel_b, len(pairs)


def _bf16_pair_words(x):
    bits = lax.bitcast_convert_type(x.astype(BF16).astype(F32), jnp.uint32)
    return bits | (bits >> 16)


def _route_kernel(pq_ref, keys_ref, sela_ref, selb_ref, rank2_o, e2_o, cnt1_o, w1_o, *, n_cand):
    k = PEER_TOPK
    hp, _, _, half = keys_ref.shape
    tt = pq_ref.shape[0]
    part_scores = [[lax.dot_general(keys_ref[hh, part], pq_ref[:, (2 * hh + part) * half:(2 * hh + part + 1) * half],
                                    NT_DIMS, preferred_element_type=F32, precision=HI) for hh in range(hp)]
                   for part in range(2)]
    sel_a = sela_ref[...]

    def route(break_ties, heads):
        w = len(heads) * tt
        iota_k = lax.broadcasted_iota(jnp.int32, (k, w), 0).astype(F32)
        s_all = jnp.concatenate([part_scores[part][hh] for part in range(2) for hh in heads], axis=1)
        scores = (s_all[:, :w], s_all[:, w:])
        vals, rank = _extract_topk(s_all, k, break_ties)
        top = jnp.concatenate(vals, axis=0)
        top1, top2 = top[:, :w], top[:, w:]
        v1, v2 = [v[:, :w] for v in vals], [v[:, w:] for v in vals]
        rank1, rank2 = rank[:, :w], rank[:, w:]
        cand = (jnp.dot(sel_a, top1, preferred_element_type=F32, precision=HI)
                + jnp.dot(selb_ref[...], top2, preferred_element_type=F32, precision=HI))
        cand_row = lax.broadcasted_iota(jnp.int32, cand.shape, 0)
        _, cand_rank = _extract_topk(jnp.where(cand_row < n_cand, cand, -jnp.inf), k, break_ties)
        chosen = (cand_rank < float(k)).astype(BF16)
        count = lax.dot_general(sel_a.astype(BF16), chosen, TN_DIMS, preferred_element_type=F32)
        e1 = jnp.exp(top1 - v1[0])
        e2 = jnp.exp(top2 - v2[0])
        z = jnp.zeros_like(v1[0])
        for a in range(k):
            z = z + e1[a:a + 1] * jnp.sum(jnp.where(iota_k < count[a:a + 1], e2, 0.0), axis=0, keepdims=True)
        inv_z = 1.0 / z
        cnt1 = jnp.zeros_like(rank1)
        for a in range(k):
            cnt1 = jnp.where(rank1 == float(a), count[a:a + 1], cnt1)
        e2_all = jnp.exp(scores[1] - v2[0]).astype(BF16)
        w1_all = jnp.where(rank1 < float(k), jnp.exp(scores[0] - v1[0]) * inv_z, 0.0)
        cnt1_words = _bf16_pair_words(cnt1)
        w1_words = _bf16_pair_words(w1_all)
        n_ranked = jnp.sum((rank < float(k)).astype(F32), axis=0, keepdims=True)
        n_chosen = jnp.sum((cand_rank < float(k)).astype(F32), axis=0, keepdims=True)
        off_by = jnp.maximum(jnp.maximum(jnp.abs(n_ranked[:, :w] - k), jnp.abs(n_ranked[:, w:] - k)),
                             jnp.abs(n_chosen - k))
        excess = []
        for pos, hh in enumerate(heads):
            lanes = slice(pos * tt, (pos + 1) * tt)
            rank2_o[hh] = rank2[:, lanes].astype(BF16)
            e2_o[hh] = e2_all[:, lanes]
            cnt1_o[hh] = cnt1_words[:, lanes]
            w1_o[hh] = w1_words[:, lanes]
            excess.append(jnp.max(off_by[:, lanes]))
        return excess

    excess = route(False, list(range(hp)))
    for hh in range(hp):
        @pl.when(excess[hh] > 0.0)
        def _(hh=hh):
            route(True, [hh])


def _route_call(pq, sub_keys, tt=256, hp=4):
    t = pq.shape[0]
    n_heads, _, n_keys, half = sub_keys.shape
    sel_a, sel_b, n_cand = _candidate_tables(PEER_TOPK)
    out_spec = pl.BlockSpec((hp, n_keys, tt), lambda i, h: (h, 0, i))
    shape = lambda dt: jax.ShapeDtypeStruct((n_heads, n_keys, t), dt)
    return pl.pallas_call(
        functools.partial(_route_kernel, n_cand=n_cand),
        grid=(t // tt, n_heads // hp),
        in_specs=[pl.BlockSpec((tt, hp * 2 * half), lambda i, h: (i, h)),
                  pl.BlockSpec((hp, 2, n_keys, half), lambda i, h: (h, 0, 0, 0)),
                  pl.BlockSpec(sel_a.shape, lambda i, h: (0, 0)),
                  pl.BlockSpec(sel_b.shape, lambda i, h: (0, 0))],
        out_specs=[out_spec] * 4,
        out_shape=[shape(BF16), shape(BF16), shape(jnp.uint32), shape(jnp.uint32)],
        compiler_params=_cparams("arbitrary", "arbitrary"),
        name="peer_route",
    )(pq, sub_keys, sel_a, sel_b)


def _peer_kernel(h2t_ref, u_ref, vt_ref, rank2_ref, e2_ref, cnt1_ref, w1_ref, x1_ref, mod_ref,
                 o_ref, act_even, act_odd, acc_scr, *, n_heads, n_keys, ec, group, n_chunks, n_items):
    s = pl.program_id(0)
    chunk = jnp.clip(s - 1, 0, n_items - 1) % n_chunks

    @pl.when(s == 0)
    def _():
        act_odd[...] = jnp.zeros_like(act_odd)

    @pl.when((chunk == 0) | (s == 0))
    def _():
        acc_scr[...] = jnp.zeros_like(acc_scr)

    tt = h2t_ref.shape[1]
    zero = jnp.zeros((), BF16)
    n_groups = ec // group
    subs_per_group = group // n_keys

    def coef_of(grp, act):
        coefs = []
        for s_loc in range(subs_per_group):
            sub = grp * subs_per_group + s_loc
            gate = None
            for h in range(n_heads):
                cnt = pltpu.bitcast(jnp.broadcast_to(cnt1_ref[h, sub:sub + 1, :], (8, tt)), BF16)
                w1 = pltpu.bitcast(jnp.broadcast_to(w1_ref[h, sub:sub + 1, :], (8, tt)), BF16)
                term = jnp.where(rank2_ref[h] < cnt[None], e2_ref[h], zero) * w1[None]
                gate = term if gate is None else gate + term
            a = act[s_loc * n_keys:(s_loc + 1) * n_keys, :].astype(BF16)
            gelu = (0.5 * a) * (1.0 + lax.erf(a * (2.0 ** -0.5)))
            coefs.append(gate.reshape(n_keys, tt) * gelu)
        return jnp.concatenate(coefs, axis=0)

    def stages(act_w, act_r):
        total = acc_scr[...]
        half_t = tt // 2
        new_parts = []
        for grp in range(n_groups):
            rows = slice(grp * group, (grp + 1) * group)
            coef = coef_of(grp, act_r[rows, :])
            if grp < 2:
                cols = slice(grp * half_t, (grp + 1) * half_t)
                new_parts.append(jnp.dot(u_ref[...], h2t_ref[:, cols], preferred_element_type=F32))
            total = total + jnp.dot(vt_ref[:, rows], coef, preferred_element_type=F32)
        acc_scr[...] = total
        act_w[...] = jnp.concatenate(new_parts, axis=1)

    @pl.when(s % 2 == 0)
    def _():
        stages(act_even, act_odd)

    @pl.when(s % 2 == 1)
    def _():
        stages(act_odd, act_even)

    @pl.when((chunk == n_chunks - 1) & (s >= 1))
    def _():
        gate2 = mod_ref[0, 5:6, :]
        o_ref[...] = x1_ref[...] + gate2 * acc_scr[...].T


def _peer_call(h2t, u_b, vt_b, rank2, e2, cnt1, w1, x1, mod, seq_len, tt=512, ec=2048, group=1024):
    d, t = h2t.shape
    n_exp = u_b.shape[0]
    n_heads, n_keys, _ = rank2.shape
    tiles_per_seq = seq_len // tt
    n_chunks = n_exp // ec
    n_items = (t // tt) * n_chunks
    first = lambda s: jnp.minimum(s, n_items - 1)
    second = lambda s: jnp.clip(s - 1, 0, n_items - 1)
    tile_rows = 16
    rank2 = rank2.reshape(n_heads, n_keys // tile_rows, tile_rows, t)
    e2 = e2.reshape(n_heads, n_keys // tile_rows, tile_rows, t)
    route_spec = pl.BlockSpec((n_heads, n_keys // tile_rows, tile_rows, tt), lambda s: (0, 0, 0, second(s) // n_chunks))
    row_spec = pl.BlockSpec((n_heads, ec // n_keys, tt), lambda s: (0, second(s) % n_chunks, second(s) // n_chunks))
    return pl.pallas_call(
        functools.partial(_peer_kernel, n_heads=n_heads, n_keys=n_keys, ec=ec, group=group,
                          n_chunks=n_chunks, n_items=n_items),
        grid=(n_items + 1,),
        in_specs=[pl.BlockSpec((d, tt), lambda s: (0, first(s) // n_chunks)),
                  pl.BlockSpec((ec, d), lambda s: (first(s) % n_chunks, 0)),
                  pl.BlockSpec((d, ec), lambda s: (0, second(s) % n_chunks)),
                  route_spec, route_spec, row_spec, row_spec,
                  pl.BlockSpec((tt, d), lambda s: (second(s) // n_chunks, 0)),
                  pl.BlockSpec((1, 6, d), lambda s: (second(s) // n_chunks // tiles_per_seq, 0, 0))],
        out_specs=pl.BlockSpec((tt, d), lambda s: (second(s) // n_chunks, 0)),
        out_shape=jax.ShapeDtypeStruct((t, d), F32),
        scratch_shapes=[pltpu.VMEM((ec, tt), F32), pltpu.VMEM((ec, tt), F32), pltpu.VMEM((d, tt), F32)],
        compiler_params=_cparams("arbitrary"),
        name="peer_experts",
    )(h2t, u_b, vt_b, rank2, e2, cnt1, w1, x1, mod)


def _pack_in_proj(w_in, sb_w, gdn_qk_w, gdn_v_w, n_gdn_heads, d_model, tn):
    o_sbq, o_sbk, o_sbv = 0, sb_w, 2 * sb_w
    o_gdn = 3 * sb_w
    conv_w = 2 * gdn_qk_w + gdn_v_w
    o_a = o_gdn + conv_w
    o_b = o_a + n_gdn_heads
    o_z = o_b + n_gdn_heads
    o_gate = o_z + gdn_v_w
    pad = (-w_in.shape[1]) % tn
    packed = jnp.concatenate([
        w_in[:, o_sbq:o_a],
        w_in[:, o_gate:o_gate + 2 * d_model],
        w_in[:, o_z:o_z + gdn_v_w],
        w_in[:, o_a:o_z],
        jnp.zeros((w_in.shape[0], pad), w_in.dtype)], axis=1).astype(BF16)
    return packed


def _block(x2d, c, w_ada, b_ada, norm1_w, w_in, sb_q_norm_w, sb_k_norm_w, gdn_conv_w, gdn_A_log,
           gdn_dt_bias, gdn_o_norm_w, w_proj_sb, w_proj_gdn, w_o, norm2_w, peer_w_q, peer_sub_keys,
           peer_u, peer_v, batch, seq_len):
    t, d = x2d.shape
    sb_w = w_proj_sb.shape[0]
    gdn_v_w = w_proj_gdn.shape[0]
    gdn_qk_w = (gdn_conv_w.shape[1] - gdn_v_w) // 2
    n_sb_heads = sb_w // HEAD_DIM
    n_gdn_heads = gdn_v_w // HEAD_DIM
    assert gdn_qk_w == gdn_v_w == sb_w and d % (4 * LANES) == 0

    c_pad = jnp.zeros((8, d), F32).at[:batch].set(c)
    mod = _ada_call(c_pad, w_ada, b_ada)[:batch].reshape(batch, 6, d)

    tn = sb_w
    w_packed = _pack_in_proj(w_in, sb_w, gdn_qk_w, gdn_v_w, n_gdn_heads, d, tn)
    heads_per_tile = tn // HEAD_DIM
    q_tiles = sb_w // tn
    qkw = jnp.concatenate([jnp.tile(sb_q_norm_w * (HEAD_DIM ** -0.5 * LOG2E), (q_tiles, heads_per_tile)),
                           jnp.tile(sb_k_norm_w, (q_tiles, heads_per_tile))], axis=0).reshape(2 * q_tiles, 1, tn)
    proj = _inproj_call(x2d, mod, norm1_w.reshape(1, d), w_packed, qkw, seq_len, tn=tn)

    ysb = _sb_call(proj, batch, seq_len, n_sb_heads // 2)

    col_gdn = 3 * sb_w // gdn_v_w
    col_gate = (3 * sb_w + 3 * gdn_v_w) // d
    col_z = (3 * sb_w + 3 * gdn_v_w + 2 * d) // gdn_v_w
    col_ab = (3 * sb_w + 4 * gdn_v_w + 2 * d) // (2 * LANES)
    prep = _gdn_prep_call(proj, gdn_conv_w, gdn_A_log, gdn_dt_bias, seq_len, n_gdn_heads,
                          (col_gdn, col_gdn + 1, col_gdn + 2), col_ab, GDN_BLOCK)
    ogdn = _gdn_core_call(prep, batch, seq_len, n_gdn_heads, GDN_BLOCK)

    x1, h2, pq = _merge_call(
        x2d, ysb, ogdn, proj, mod, jnp.tile(gdn_o_norm_w, n_gdn_heads).reshape(1, gdn_v_w), norm2_w.reshape(1, d),
        w_proj_sb.astype(BF16), w_proj_gdn.astype(BF16), w_o.astype(BF16), peer_w_q.astype(BF16),
        seq_len, col_gate, col_gate + 1, col_z)

    rank2, e2, cnt1, w1 = _route_call(pq, peer_sub_keys)
    return _peer_call(h2, peer_u.astype(BF16), peer_v.T.astype(BF16), rank2, e2, cnt1, w1, x1, mod, seq_len)


def kernel(x, c, w_ada, b_ada, norm1_w, w_in, sb_q_norm_w, sb_k_norm_w, gdn_conv_w, gdn_A_log, gdn_dt_bias,
           gdn_o_norm_w, w_proj_sb, w_proj_gdn, w_o, norm2_w, peer_w_q, peer_sub_keys, peer_u, peer_v):
    batch, seq_len, d = x.shape
    x2d = x.reshape(batch * seq_len, d)
    for l in range(w_ada.shape[0]):
        x2d = _block(x2d, c, w_ada[l], b_ada[l], norm1_w[l], w_in[l], sb_q_norm_w[l], sb_k_norm_w[l],
                     gdn_conv_w[l], gdn_A_log[l], gdn_dt_bias[l], gdn_o_norm_w[l], w_proj_sb[l], w_proj_gdn[l],
                     w_o[l], norm2_w[l], peer_w_q[l], peer_sub_keys[l], peer_u[l], peer_v[l], batch, seq_len)
    return x2d.reshape(batch, seq_len, d)
```

```python
import functools

import jax
import jax.numpy as jnp
from jax import lax
from jax.experimental import pallas as pl
from jax.experimental.pallas import tpu as pltpu

F32 = jnp.float32
BF16 = jnp.bfloat16
HI = lax.Precision.HIGHEST
EPS = 1e-6
LOG2E = 1.4426950408889634

LANES = 128
HEAD_DIM = 64
GDN_BLOCK = 128
GDN_CONV = 4
PEER_TOPK = 16
NOT_RANKED = 99.0
VMEM_LIMIT = 56 * 1024 * 1024

NT_DIMS = (((1,), (1,)), ((), ()))
TN_DIMS = (((0,), (0,)), ((), ()))


def _cparams(*sem, flags=None):
    return pltpu.CompilerParams(dimension_semantics=sem, vmem_limit_bytes=VMEM_LIMIT, flags=flags)


def _sigmoid(x):
    return 1.0 / (1.0 + jnp.exp(-x))


def _silu(x):
    return x * _sigmoid(x)


def _softplus(x):
    return jnp.maximum(x, 0.0) + jnp.log(1.0 + jnp.exp(-jnp.abs(x)))


def _block_diag_ones(n, group, value=1.0, dtype=F32):
    r = jnp.arange(n) // group
    return jnp.where(r[:, None] == r[None, :], value, 0.0).astype(dtype)


def _bf16_pieces(x):
    hi = x.astype(BF16)
    rest = x - hi.astype(F32)
    mid = rest.astype(BF16)
    return hi, mid, (rest - mid.astype(F32)).astype(BF16)


def _group_sum_sq(x, bd):
    slab = bd.shape[0]
    outs = []
    for c0 in range(0, x.shape[1], slab):
        sq = x[:, c0:c0 + slab] * x[:, c0:c0 + slab]
        hi = sq.astype(BF16)
        lo = (sq - hi.astype(F32)).astype(BF16)
        outs.append(jnp.dot(hi, bd, preferred_element_type=F32) + jnp.dot(lo, bd, preferred_element_type=F32))
    return outs[0] if len(outs) == 1 else jnp.concatenate(outs, axis=1)


def _ada_kernel(c_ref, w_ref, b_ref, o_ref):
    c = c_ref[...]
    o_ref[...] = jnp.dot(_silu(c), w_ref[...], preferred_element_type=F32, precision=HI) + b_ref[...]


def _ada_call(c_pad, w_ada, b_ada):
    rows, d = c_pad.shape
    n = w_ada.shape[1]
    tn = 512
    return pl.pallas_call(
        _ada_kernel,
        grid=(n // tn,),
        in_specs=[pl.BlockSpec((rows, d), lambda j: (0, 0)),
                  pl.BlockSpec((d, tn), lambda j: (0, j)),
                  pl.BlockSpec((1, tn), lambda j: (0, j))],
        out_specs=pl.BlockSpec((rows, tn), lambda j: (0, j)),
        out_shape=jax.ShapeDtypeStruct((rows, n), F32),
        compiler_params=_cparams("arbitrary"),
        name="ada_mod",
    )(c_pad, w_ada, b_ada.reshape(1, n))


def _inproj_kernel(x_ref, mod_ref, n1_ref, w_ref, qkw_ref, bd_ref, o_ref, h_scr, *, n_qk_tiles):
    j = pl.program_id(1)

    @pl.when(j == 0)
    def _():
        x = x_ref[...]
        ms = jnp.mean(x * x, axis=-1, keepdims=True)
        y = x * lax.rsqrt(ms + EPS) * n1_ref[...]
        shift = mod_ref[0, 0:1, :]
        scale = mod_ref[0, 1:2, :]
        h_scr[...] = (y * (1.0 + scale) + shift).astype(BF16)

    acc = jnp.dot(h_scr[...], w_ref[...], preferred_element_type=F32)

    @pl.when(j < n_qk_tiles)
    def _():
        o_ref[...] = acc * lax.rsqrt(_group_sum_sq(acc, bd_ref[...]) + EPS) * qkw_ref[0]

    @pl.when(j >= n_qk_tiles)
    def _():
        o_ref[...] = acc


def _inproj_call(x2d, mod, n1w, w_packed, qkw, seq_len, tm=2048, tn=512):
    t, d = x2d.shape
    n = w_packed.shape[1]
    n_qk_tiles = qkw.shape[0]
    bd = _block_diag_ones(2 * LANES, HEAD_DIM, 1.0 / HEAD_DIM, BF16)
    tiles_per_seq = seq_len // tm
    return pl.pallas_call(
        functools.partial(_inproj_kernel, n_qk_tiles=n_qk_tiles),
        grid=(t // tm, n // tn),
        in_specs=[pl.BlockSpec((tm, d), lambda i, j: (i, 0)),
                  pl.BlockSpec((1, 6, d), lambda i, j: (i // tiles_per_seq, 0, 0)),
                  pl.BlockSpec((1, d), lambda i, j: (0, 0)),
                  pl.BlockSpec((d, tn), lambda i, j: (0, j)),
                  pl.BlockSpec((1, 1, tn), lambda i, j: (jnp.minimum(j, n_qk_tiles - 1), 0, 0)),
                  pl.BlockSpec(bd.shape, lambda i, j: (0, 0))],
        out_specs=pl.BlockSpec((tm, tn), lambda i, j: (i, j)),
        out_shape=jax.ShapeDtypeStruct((t, n), F32),
        scratch_shapes=[pltpu.VMEM((tm, d), BF16)],
        compiler_params=_cparams("arbitrary", "arbitrary"),
        name="in_proj",
    )(x2d, mod, n1w, w_packed, qkw, bd)


def _sb_kernel(q_ref, k_ref, v_ref, tri_ref, o_ref, acc_ref, r_ref, *, tq):
    i = pl.program_id(2)
    half = tri_ref.shape[0]
    lane = lax.broadcasted_iota(jnp.int32, (1, LANES), 1)
    first_half = lane < HEAD_DIM
    q = q_ref[...]
    q_heads = (jnp.where(first_half, q, 0.0).astype(BF16), jnp.where(first_half, 0.0, q).astype(BF16))
    acc_ref[...] = jnp.zeros_like(acc_ref)
    r_ref[...] = jnp.zeros_like(r_ref)

    def run(blocks, masked):
        items = [(j, hd) for j in blocks for hd in range(2)]
        st = [dict() for _ in items]
        if masked:
            causal = (lax.broadcasted_iota(jnp.int32, (tq, tq), 1) < lax.broadcasted_iota(jnp.int32, (tq, tq), 0))

        def rows_of(n):
            return pl.ds(pl.multiple_of(items[n][0] * tq, tq), tq)

        def logits(n):
            kb = k_ref[rows_of(n), :].astype(BF16)
            st[n]["y"] = lax.dot_general(q_heads[items[n][1]], kb, NT_DIMS, preferred_element_type=F32)

        def softplus(n):
            y = st[n]["y"]
            neg_abs = lax.bitcast_convert_type(lax.bitcast_convert_type(y, jnp.uint32) | jnp.uint32(0x80000000), F32)
            sp = jnp.maximum(y, 0.0) + jnp.log(1.0 + jnp.exp2(neg_abs)) * LOG2E
            if masked:
                sp = jnp.where(causal, sp, 0.0)
            st[n]["log_beta"] = y - sp
            st[n]["spb"] = sp.astype(BF16)
            st[n]["sp_first"] = (sp[:, 0:1], sp[:, half:half + 1])

        def later_sums(n):
            spb = st[n]["spb"]
            first_lo, first_hi = st[n]["sp_first"]
            later_hi = jnp.dot(spb[:, half:], tri_ref[...], preferred_element_type=F32)
            total_hi = later_hi[:, 0:1] + first_hi
            later_lo = jnp.dot(spb[:, :half], tri_ref[...], preferred_element_type=F32) + total_hi
            st[n]["later"] = jnp.concatenate([later_lo, later_hi], axis=1)
            st[n]["total"] = later_lo[:, 0:1] + first_lo

        def weights(n):
            a = jnp.exp2(st[n]["log_beta"] - st[n]["later"])
            if masked:
                a = jnp.where(causal, a, 0.0)
            st[n]["a"] = a.astype(BF16)

        def accumulate(n):
            hd = items[n][1]
            pv = jnp.dot(st[n]["a"], v_ref[rows_of(n), :].astype(BF16), preferred_element_type=F32)
            r = r_ref[hd]
            acc_ref[hd] += jnp.exp2(-r) * pv
            r_ref[hd] = r + st[n]["total"]

        stages = (logits, softplus, later_sums, weights, accumulate)
        elementwise = (softplus, weights)
        for wave in range(len(items) + len(stages) - 1):
            todo = [(stages[wave - n], n) for n in range(len(items)) if 0 <= wave - n < len(stages)]
            for stage, n in [x for x in todo if x[0] in elementwise] + [x for x in todo if x[0] not in elementwise]:
                stage(n)

    run([i], True)

    def body(it, carry):
        j = i - 1 - 2 * it
        run([j, j - 1], False)
        return carry

    lax.fori_loop(0, i // 2, body, 0)

    @pl.when(i % 2 == 1)
    def _():
        run([0], False)

    o_ref[...] = jnp.where(first_half, acc_ref[0], acc_ref[1])


def _sb_call(proj, batch, seq_len, n_pairs, tq=512):
    t = proj.shape[0]
    nq = seq_len // tq
    half = tq // 2
    kk = jnp.arange(half)
    later = (kk[:, None] > kk[None, :]).astype(BF16)
    return pl.pallas_call(
        functools.partial(_sb_kernel, tq=tq),
        grid=(batch, n_pairs, nq),
        in_specs=[pl.BlockSpec((tq, LANES), lambda b, p, i: (b * nq + i, p)),
                  pl.BlockSpec((seq_len, LANES), lambda b, p, i: (b, n_pairs + p)),
                  pl.BlockSpec((seq_len, LANES), lambda b, p, i: (b, 2 * n_pairs + p)),
                  pl.BlockSpec((half, half), lambda b, p, i: (0, 0))],
        out_specs=pl.BlockSpec((tq, LANES), lambda b, p, i: (b * nq + i, p)),
        out_shape=jax.ShapeDtypeStruct((t, n_pairs * LANES), F32),
        scratch_shapes=[pltpu.VMEM((2, tq, LANES), F32), pltpu.VMEM((2, tq, 1), F32)],
        compiler_params=_cparams("arbitrary", "arbitrary", "arbitrary"),
        name="sb_attention",
    )(proj, proj, proj, later)


def _gdn_prep_kernel(q_ref, k_ref, v_ref, qt_ref, kt_ref, vt_ref, ab_ref, cw_ref, alog_ref, dtb_ref,
                     ea_ref, eb_ref, lc_ref, jc_ref, bd_ref,
                     qn_o, qd_o, kn_o, kd_o, kb_o, kbg_o, vb_o, gcol_o, grow_o, scr, *, tm, tiles_per_seq):
    i = pl.program_id(0)
    keep_tail = (i % tiles_per_seq != 0).astype(F32)
    w = cw_ref[...]
    width = q_ref.shape[1]

    def conv_silu(cur_ref, tail_ref, col0):
        scr[0:8, :] = tail_ref[...] * keep_tail
        scr[8:, :] = cur_ref[...]
        y = jnp.zeros((tm, width), F32)
        for tap in range(GDN_CONV):
            off = 8 - (GDN_CONV - 1) + tap
            y = y + scr[off:off + tm, :] * w[tap:tap + 1, col0:col0 + width]
        return _silu(y)

    bd = bd_ref[...]
    cq = conv_silu(q_ref, qt_ref, 0)
    qn = cq * lax.rsqrt(_group_sum_sq(cq, bd) + EPS) * (HEAD_DIM ** -0.5)
    ck = conv_silu(k_ref, kt_ref, width)
    kn = ck * lax.rsqrt(_group_sum_sq(ck, bd) + EPS)
    cv = conv_silu(v_ref, vt_ref, 2 * width)

    ab = ab_ref[...]
    g = -jnp.exp(alog_ref[...]) * _softplus(ab + dtb_ref[...])
    beta = _sigmoid(ab)
    g_parts = _bf16_pieces(g)
    lc, jc, ea = lc_ref[...], jc_ref[...], ea_ref[...]
    g_cum = sum(jnp.dot(lc, p, preferred_element_type=F32) for p in g_parts)
    g_tot = sum(jnp.dot(jc, p, preferred_element_type=F32) for p in g_parts)
    gx = sum(jnp.dot(p, ea, preferred_element_type=F32) for p in _bf16_pieces(g_cum))
    glx = sum(jnp.dot(p, ea, preferred_element_type=F32) for p in _bf16_pieces(g_tot))
    bx = sum(jnp.dot(p, eb_ref[...], preferred_element_type=F32) for p in _bf16_pieces(beta))

    e_g = jnp.exp(gx)
    kb = kn * bx
    qn_o[...] = qn
    qd_o[...] = qn * e_g
    kn_o[...] = kn
    kd_o[...] = kn * jnp.exp(glx - gx)
    kb_o[...] = kb
    kbg_o[...] = kb * e_g
    vb_o[...] = cv * bx
    g_heads = g_cum[:, :LANES]
    gcol_o[...] = g_heads
    grow_o[...] = g_heads.T[0:8, :]


def _gdn_prep_call(proj, conv_w, a_log, dt_bias, seq_len, n_heads, col_qkv, col_ab, chunk, tm=512):
    t = proj.shape[0]
    assert n_heads <= 8 and tm % chunk == 0
    width = n_heads * HEAD_DIM
    abw = 2 * LANES
    tiles_per_seq = seq_len // tm
    sub = tm // 8
    alog_pad = jnp.zeros((1, abw), F32).at[0, :n_heads].set(a_log)
    dtb_pad = jnp.zeros((1, abw), F32).at[0, :n_heads].set(dt_bias)
    head_of_lane = jnp.arange(width) // HEAD_DIM
    rows = jnp.arange(abw)
    ea = (rows[:, None] == head_of_lane[None, :]).astype(BF16)
    eb = (rows[:, None] == head_of_lane[None, :] + n_heads).astype(BF16)
    tok = jnp.arange(tm)
    same_chunk = (tok[:, None] // chunk) == (tok[None, :] // chunk)
    lc = (same_chunk & (tok[:, None] >= tok[None, :])).astype(BF16)
    jc = same_chunk.astype(BF16)
    bd = _block_diag_ones(2 * LANES, HEAD_DIM, 1.0, BF16)

    cur = lambda c: pl.BlockSpec((tm, width), lambda i: (i, c))
    tail = lambda c: pl.BlockSpec((8, width), lambda i: (jnp.maximum(i * sub - 1, 0), c))
    full = lambda a: pl.BlockSpec(a.shape, lambda i: (0,) * a.ndim)
    cq, ck, cv = col_qkv
    out_spec = pl.BlockSpec((tm, width), lambda i: (i, 0))
    out_shape = jax.ShapeDtypeStruct((t, width), F32)
    return pl.pallas_call(
        functools.partial(_gdn_prep_kernel, tm=tm, tiles_per_seq=tiles_per_seq),
        grid=(t // tm,),
        in_specs=[cur(cq), cur(ck), cur(cv), tail(cq), tail(ck), tail(cv),
                  pl.BlockSpec((tm, abw), lambda i: (i, col_ab)),
                  full(conv_w), full(alog_pad), full(dtb_pad), full(ea), full(eb), full(lc), full(jc), full(bd)],
        out_specs=[out_spec] * 7 + [pl.BlockSpec((tm, LANES), lambda i: (i, 0)), pl.BlockSpec((8, tm), lambda i: (0, i))],
        out_shape=[out_shape] * 7 + [jax.ShapeDtypeStruct((t, LANES), F32), jax.ShapeDtypeStruct((8, t), F32)],
        scratch_shapes=[pltpu.VMEM((tm + 8, width), F32)],
        compiler_params=_cparams("arbitrary"),
        name="gdn_prep",
    )(proj, proj, proj, proj, proj, proj, proj, conv_w, alog_pad, dtb_pad, ea, eb, lc, jc, bd)


def _gdn_core_kernel(qn_ref, qd_ref, kn_ref, kd_ref, kb_ref, kbg_ref, vb_ref, gcol_ref, grow_ref,
                     o_ref, s_scr, *, ts, n_heads, c_len):
    @pl.when(pl.program_id(1) == 0)
    def _():
        s_scr[...] = jnp.zeros_like(s_scr)

    row = lax.broadcasted_iota(jnp.int32, (c_len, c_len), 0)
    col = lax.broadcasted_iota(jnp.int32, (c_len, c_len), 1)
    strict = row > col
    incl = row >= col
    eye_state = (lax.broadcasted_iota(jnp.int32, (HEAD_DIM, HEAD_DIM), 0)
                 == lax.broadcasted_iota(jnp.int32, (HEAD_DIM, HEAD_DIM), 1))
    n_levels = (c_len - 1).bit_length()
    n_chunks = ts // c_len
    problems = [(c, h) for c in range(n_chunks) for h in range(n_heads)]

    def bf(x):
        return x.astype(BF16)

    def mm(a, b):
        return jnp.dot(a, b, preferred_element_type=F32)

    def head_tile(ref, c, h):
        pair, half = divmod(h, 2)
        tile = ref[c * c_len:(c + 1) * c_len, pair * LANES:(pair + 1) * LANES]
        return tile[:, half * HEAD_DIM:(half + 1) * HEAD_DIM]

    sibling = [((row >> k) ^ (col >> k)) == 1 for k in range(n_levels)]

    xs, ms, rs, qks = [], [], [], []
    for c, h in problems:
        g_col = gcol_ref[c * c_len:(c + 1) * c_len, h:h + 1]
        g_row = grow_ref[h:h + 1, c * c_len:(c + 1) * c_len]
        decay = jnp.where(incl, jnp.exp(g_col - g_row), 0.0)
        kn_b = bf(head_tile(kn_ref, c, h))
        kk = lax.dot_general(bf(head_tile(kb_ref, c, h)), kn_b, NT_DIMS, preferred_element_type=F32)
        x = jnp.where(strict, -(kk * decay), 0.0)
        xs.append(x)
        ms.append(jnp.where(row == col, 1.0, jnp.where(sibling[0], x, 0.0)))
        qks.append(bf(lax.dot_general(bf(head_tile(qn_ref, c, h)), kn_b, NT_DIMS, preferred_element_type=F32) * decay))
        rs.append(bf(jnp.concatenate([head_tile(vb_ref, c, h), head_tile(kbg_ref, c, h)], axis=1)))

    for k in range(1, n_levels):
        for idx in range(len(problems)):
            m_b = bf(ms[idx])
            left = mm(m_b, bf(jnp.where(sibling[k], xs[idx], 0.0)))
            ms[idx] = ms[idx] + mm(bf(left), m_b)

    q_eff, o_zero, p_mat, b_mat = {}, {}, {}, {}
    for idx, (c, h) in enumerate(problems):
        sol = bf(mm(bf(ms[idx]), rs[idx]))
        m1 = mm(qks[idx], sol)
        q_eff[c, h] = bf(head_tile(qd_ref, c, h) - m1[:, HEAD_DIM:])
        o_zero[c, h] = m1[:, :HEAD_DIM]
        m2 = lax.dot_general(bf(head_tile(kd_ref, c, h)), sol, TN_DIMS, preferred_element_type=F32)
        chunk_decay = jnp.exp(gcol_ref[(c + 1) * c_len - 1:(c + 1) * c_len, h:h + 1])
        p_mat[c, h] = bf(jnp.where(eye_state, chunk_decay, 0.0) - m2[:, HEAD_DIM:])
        b_mat[c, h] = m2[:, :HEAD_DIM]

    states = [s_scr[h] for h in range(n_heads)]
    for c in range(n_chunks):
        outs = []
        for h in range(n_heads):
            s_b = bf(states[h])
            outs.append(mm(q_eff[c, h], s_b) + o_zero[c, h])
            states[h] = mm(p_mat[c, h], s_b) + b_mat[c, h]
        o_ref[c * c_len:(c + 1) * c_len, :] = jnp.concatenate(outs, axis=1)
    for h in range(n_heads):
        s_scr[h] = states[h]


def _gdn_core_call(prep, batch, seq_len, n_heads, c_len, ts=512):
    t, width = prep[0].shape
    n_seq_tiles = seq_len // ts
    spec = pl.BlockSpec((ts, width), lambda b, s: (b * n_seq_tiles + s, 0))
    return pl.pallas_call(
        functools.partial(_gdn_core_kernel, ts=ts, n_heads=n_heads, c_len=c_len),
        grid=(batch, n_seq_tiles),
        in_specs=[spec] * 7 + [pl.BlockSpec((ts, LANES), lambda b, s: (b * n_seq_tiles + s, 0)),
                               pl.BlockSpec((8, ts), lambda b, s: (0, b * n_seq_tiles + s))],
        out_specs=spec,
        out_shape=jax.ShapeDtypeStruct((t, width), F32),
        scratch_shapes=[pltpu.VMEM((n_heads, HEAD_DIM, HEAD_DIM), F32)],
        compiler_params=_cparams("arbitrary", "arbitrary"),
        name="gdn_core",
    )(*prep)


def _merge_kernel(x_ref, ysb_ref, og_ref, z_ref, g0_ref, g1_ref, mod_ref, onw_ref, n2_ref, bd_ref,
                  wsb_ref, wgdn_ref, wo_ref, wq_ref, x1_o, h2t_o, pq_o):
    og = og_ref[...]
    ygdn = og * lax.rsqrt(_group_sum_sq(og, bd_ref[...]) + EPS) * onw_ref[...] * _silu(z_ref[...])
    m = (_sigmoid(g0_ref[...]) * jnp.dot(ysb_ref[...].astype(BF16), wsb_ref[...], preferred_element_type=F32)
         + _sigmoid(g1_ref[...]) * jnp.dot(ygdn.astype(BF16), wgdn_ref[...], preferred_element_type=F32))
    gate1 = mod_ref[0, 2:3, :]
    shift2 = mod_ref[0, 3:4, :]
    scale2 = mod_ref[0, 4:5, :]
    x1 = x_ref[...] + gate1 * jnp.dot(m.astype(BF16), wo_ref[...], preferred_element_type=F32)
    x1_o[...] = x1
    ms2 = jnp.mean(x1 * x1, axis=-1, keepdims=True)
    h2 = x1 * lax.rsqrt(ms2 + EPS) * n2_ref[...] * (1.0 + scale2) + shift2
    h2t_o[...] = h2.T.astype(BF16)
    pq_o[...] = jnp.dot(h2.astype(BF16), wq_ref[...], preferred_element_type=F32)


def _merge_call(x2d, ysb, ogdn, proj, mod, onw, n2w, wsb, wgdn, wo, wq, seq_len, col_g0, col_g1, col_z, tm=512):
    t, d = x2d.shape
    width = ysb.shape[1]
    nq = wq.shape[1]
    bd = _block_diag_ones(2 * LANES, HEAD_DIM, 1.0 / HEAD_DIM, BF16)
    tiles_per_seq = seq_len // tm
    full = lambda a: pl.BlockSpec(a.shape, lambda i: (0,) * a.ndim)
    once = lambda a: pl.BlockSpec(a.shape, lambda i: (0,) * a.ndim, pipeline_mode=pl.Buffered(1))
    return pl.pallas_call(
        _merge_kernel,
        grid=(t // tm,),
        in_specs=[pl.BlockSpec((tm, d), lambda i: (i, 0)),
                  pl.BlockSpec((tm, width), lambda i: (i, 0)),
                  pl.BlockSpec((tm, width), lambda i: (i, 0)),
                  pl.BlockSpec((tm, width), lambda i: (i, col_z)),
                  pl.BlockSpec((tm, d), lambda i: (i, col_g0)),
                  pl.BlockSpec((tm, d), lambda i: (i, col_g1)),
                  pl.BlockSpec((1, 6, d), lambda i: (i // tiles_per_seq, 0, 0)),
                  full(onw), full(n2w), full(bd), once(wsb), once(wgdn), once(wo), once(wq)],
        out_specs=[pl.BlockSpec((tm, d), lambda i: (i, 0)),
                   pl.BlockSpec((d, tm), lambda i: (0, i)),
                   pl.BlockSpec((tm, nq), lambda i: (i, 0))],
        out_shape=[jax.ShapeDtypeStruct((t, d), F32),
                   jax.ShapeDtypeStruct((d, t), BF16),
                   jax.ShapeDtypeStruct((t, nq), F32)],
        compiler_params=_cparams("arbitrary"),
        name="merge_proj",
    )(x2d, ysb, ogdn, proj, proj, proj, mod, onw, n2w, bd, wsb, wgdn, wo, wq)


def _extract_topk(s, k, break_ties):
    n = s.shape[0]
    if not break_ties:
        lowest_bits = -8388609
        vals = []
        for r in range(k):
            m = jnp.max(s, axis=0, keepdims=True)
            marker = lax.bitcast_convert_type(jnp.int32(lowest_bits - r), F32)
            s = jnp.where(s == m, marker, s)
            vals.append(m)
        took = jnp.int32(lowest_bits) - lax.bitcast_convert_type(s, jnp.int32)
        rank = jnp.where((took >= 0) & (took < k), took.astype(F32), NOT_RANKED)
        return vals, rank
    iota = lax.broadcasted_iota(jnp.int32, s.shape, 0).astype(F32)
    rank = jnp.full(s.shape, NOT_RANKED, F32)
    vals = []
    for r in range(k):
        m = jnp.max(s, axis=0, keepdims=True)
        hit = iota == jnp.min(jnp.where(s == m, iota, float(n)), axis=0, keepdims=True)
        rank = jnp.where(hit, float(r), rank)
        s = jnp.where(hit, -jnp.inf, s)
        vals.append(m)
    return vals, rank


def _candidate_tables(k):
    pairs = [(a, b) for a in range(k) for b in range(k) if (a + 1) * (b + 1) <= k]
    n_pad = -(-len(pairs) // 8) * 8
    sel_a = jnp.zeros((n_pad, k), F32).at[jnp.arange(len(pairs)), jnp.array([a for a, _ in pairs])].set(1.0)
    sel_b = jnp.zeros((n_pad, k), F32).at[jnp.arange(len(pairs)), jnp.array([b for _, b in pairs])].set(1.0)
    return sel_a, sel_b, len(pairs)


def _bf16_pair_words(x):
    bits = lax.bitcast_convert_type(x.astype(BF16).astype(F32), jnp.uint32)
    return bits | (bits >> 16)


def _route_kernel(pq_ref, keys_ref, sela_ref, selb_ref, rank2_o, e2_o, cnt1_o, w1_o, *, n_cand):
    k = PEER_TOPK
    hp, _, _, half = keys_ref.shape
    tt = pq_ref.shape[0]
    part_scores = [[lax.dot_general(keys_ref[hh, part], pq_ref[:, (2 * hh + part) * half:(2 * hh + part + 1) * half],
                                    NT_DIMS, preferred_element_type=F32, precision=HI) for hh in range(hp)]
                   for part in range(2)]
    sel_a = sela_ref[...]

    def route(break_ties, heads):
        w = len(heads) * tt
        iota_k = lax.broadcasted_iota(jnp.int32, (k, w), 0).astype(F32)
        s_all = jnp.concatenate([part_scores[part][hh] for part in range(2) for hh in heads], axis=1)
        scores = (s_all[:, :w], s_all[:, w:])
        vals, rank = _extract_topk(s_all, k, break_ties)
        top = jnp.concatenate(vals, axis=0)
        top1, top2 = top[:, :w], top[:, w:]
        v1, v2 = [v[:, :w] for v in vals], [v[:, w:] for v in vals]
        rank1, rank2 = rank[:, :w], rank[:, w:]
        cand = (jnp.dot(sel_a, top1, preferred_element_type=F32, precision=HI)
                + jnp.dot(selb_ref[...], top2, preferred_element_type=F32, precision=HI))
        cand_row = lax.broadcasted_iota(jnp.int32, cand.shape, 0)
        _, cand_rank = _extract_topk(jnp.where(cand_row < n_cand, cand, -jnp.inf), k, break_ties)
        chosen = (cand_rank < float(k)).astype(BF16)
        count = lax.dot_general(sel_a.astype(BF16), chosen, TN_DIMS, preferred_element_type=F32)
        e1 = jnp.exp(top1 - v1[0])
        e2 = jnp.exp(top2 - v2[0])
        z = jnp.zeros_like(v1[0])
        for a in range(k):
            z = z + e1[a:a + 1] * jnp.sum(jnp.where(iota_k < count[a:a + 1], e2, 0.0), axis=0, keepdims=True)
        inv_z = 1.0 / z
        cnt1 = jnp.zeros_like(rank1)
        for a in range(k):
            cnt1 = jnp.where(rank1 == float(a), count[a:a + 1], cnt1)
        e2_all = jnp.exp(scores[1] - v2[0]).astype(BF16)
        w1_all = jnp.where(rank1 < float(k), jnp.exp(scores[0] - v1[0]) * inv_z, 0.0)
        cnt1_words = _bf16_pair_words(cnt1)
        w1_words = _bf16_pair_words(w1_all)
        n_ranked = jnp.sum((rank < float(k)).astype(F32), axis=0, keepdims=True)
        n_chosen = jnp.sum((cand_rank < float(k)).astype(F32), axis=0, keepdims=True)
        off_by = jnp.maximum(jnp.maximum(jnp.abs(n_ranked[:, :w] - k), jnp.abs(n_ranked[:, w:] - k)),
                             jnp.abs(n_chosen - k))
        excess = []
        for pos, hh in enumerate(heads):
            lanes = slice(pos * tt, (pos + 1) * tt)
            rank2_o[hh] = rank2[:, lanes].astype(BF16)
            e2_o[hh] = e2_all[:, lanes]
            cnt1_o[hh] = cnt1_words[:, lanes]
            w1_o[hh] = w1_words[:, lanes]
            excess.append(jnp.max(off_by[:, lanes]))
        return excess

    excess = route(False, list(range(hp)))
    for hh in range(hp):
        @pl.when(excess[hh] > 0.0)
        def _(hh=hh):
            route(True, [hh])


def _route_call(pq, sub_keys, tt=256, hp=4):
    t = pq.shape[0]
    n_heads, _, n_keys, half = sub_keys.shape
    sel_a, sel_b, n_cand = _candidate_tables(PEER_TOPK)
    out_spec = pl.BlockSpec((hp, n_keys, tt), lambda i, h: (h, 0, i))
    shape = lambda dt: jax.ShapeDtypeStruct((n_heads, n_keys, t), dt)
    return pl.pallas_call(
        functools.partial(_route_kernel, n_cand=n_cand),
        grid=(t // tt, n_heads // hp),
        in_specs=[pl.BlockSpec((tt, hp * 2 * half), lambda i, h: (i, h)),
                  pl.BlockSpec((hp, 2, n_keys, half), lambda i, h: (h, 0, 0, 0)),
                  pl.BlockSpec(sel_a.shape, lambda i, h: (0, 0)),
                  pl.BlockSpec(sel_b.shape, lambda i, h: (0, 0))],
        out_specs=[out_spec] * 4,
        out_shape=[shape(BF16), shape(BF16), shape(jnp.uint32), shape(jnp.uint32)],
        compiler_params=_cparams("arbitrary", "arbitrary"),
        name="peer_route",
    )(pq, sub_keys, sel_a, sel_b)


def _peer_kernel(h2t_ref, u_ref, vt_ref, rank2_ref, e2_ref, cnt1_ref, w1_ref, x1_ref, mod_ref,
                 o_ref, act_even, act_odd, acc_scr, *, n_heads, n_keys, ec, group, n_chunks, n_items):
    s = pl.program_id(0)
    chunk = jnp.clip(s - 1, 0, n_items - 1) % n_chunks

    @pl.when(s == 0)
    def _():
        act_odd[...] = jnp.zeros_like(act_odd)

    @pl.when((chunk == 0) | (s == 0))
    def _():
        acc_scr[...] = jnp.zeros_like(acc_scr)

    tt = h2t_ref.shape[1]
    zero = jnp.zeros((), BF16)
    n_groups = ec // group
    subs_per_group = group // n_keys

    def coef_of(grp, act):
        coefs = []
        for s_loc in range(subs_per_group):
            sub = grp * subs_per_group + s_loc
            gate = None
            for h in range(n_heads):
                cnt = pltpu.bitcast(jnp.broadcast_to(cnt1_ref[h, sub:sub + 1, :], (8, tt)), BF16)
                w1 = pltpu.bitcast(jnp.broadcast_to(w1_ref[h, sub:sub + 1, :], (8, tt)), BF16)
                term = jnp.where(rank2_ref[h] < cnt[None], e2_ref[h], zero) * w1[None]
                gate = term if gate is None else gate + term
            a = act[s_loc * n_keys:(s_loc + 1) * n_keys, :].astype(BF16)
            gelu = (0.5 * a) * (1.0 + lax.erf(a * (2.0 ** -0.5)))
            coefs.append(gate.reshape(n_keys, tt) * gelu)
        return jnp.concatenate(coefs, axis=0)

    def stages(act_w, act_r):
        total = acc_scr[...]
        half_t = tt // 2
        new_parts = []
        for grp in range(n_groups):
            rows = slice(grp * group, (grp + 1) * group)
            coef = coef_of(grp, act_r[rows, :])
            if grp < 2:
                cols = slice(grp * half_t, (grp + 1) * half_t)
                new_parts.append(jnp.dot(u_ref[...], h2t_ref[:, cols], preferred_element_type=F32))
            total = total + jnp.dot(vt_ref[:, rows], coef, preferred_element_type=F32)
        acc_scr[...] = total
        act_w[...] = jnp.concatenate(new_parts, axis=1)

    @pl.when(s % 2 == 0)
    def _():
        stages(act_even, act_odd)

    @pl.when(s % 2 == 1)
    def _():
        stages(act_odd, act_even)

    @pl.when((chunk == n_chunks - 1) & (s >= 1))
    def _():
        gate2 = mod_ref[0, 5:6, :]
        o_ref[...] = x1_ref[...] + gate2 * acc_scr[...].T


def _peer_call(h2t, u_b, vt_b, rank2, e2, cnt1, w1, x1, mod, seq_len, tt=512, ec=2048, group=1024):
    d, t = h2t.shape
    n_exp = u_b.shape[0]
    n_heads, n_keys, _ = rank2.shape
    tiles_per_seq = seq_len // tt
    n_chunks = n_exp // ec
    n_items = (t // tt) * n_chunks
    first = lambda s: jnp.minimum(s, n_items - 1)
    second = lambda s: jnp.clip(s - 1, 0, n_items - 1)
    tile_rows = 16
    rank2 = rank2.reshape(n_heads, n_keys // tile_rows, tile_rows, t)
    e2 = e2.reshape(n_heads, n_keys // tile_rows, tile_rows, t)
    route_spec = pl.BlockSpec((n_heads, n_keys // tile_rows, tile_rows, tt), lambda s: (0, 0, 0, second(s) // n_chunks))
    row_spec = pl.BlockSpec((n_heads, ec // n_keys, tt), lambda s: (0, second(s) % n_chunks, second(s) // n_chunks))
    return pl.pallas_call(
        functools.partial(_peer_kernel, n_heads=n_heads, n_keys=n_keys, ec=ec, group=group,
                          n_chunks=n_chunks, n_items=n_items),
        grid=(n_items + 1,),
        in_specs=[pl.BlockSpec((d, tt), lambda s: (0, first(s) // n_chunks)),
                  pl.BlockSpec((ec, d), lambda s: (first(s) % n_chunks, 0)),
                  pl.BlockSpec((d, ec), lambda s: (0, second(s) % n_chunks)),
                  route_spec, route_spec, row_spec, row_spec,
                  pl.BlockSpec((tt, d), lambda s: (second(s) // n_chunks, 0)),
                  pl.BlockSpec((1, 6, d), lambda s: (second(s) // n_chunks // tiles_per_seq, 0, 0))],
        out_specs=pl.BlockSpec((tt, d), lambda s: (second(s) // n_chunks, 0)),
        out_shape=jax.ShapeDtypeStruct((t, d), F32),
        scratch_shapes=[pltpu.VMEM((ec, tt), F32), pltpu.VMEM((ec, tt), F32), pltpu.VMEM((d, tt), F32)],
        compiler_params=_cparams("arbitrary"),
        name="peer_experts",
    )(h2t, u_b, vt_b, rank2, e2, cnt1, w1, x1, mod)


def _pack_in_proj(w_in, sb_w, gdn_qk_w, gdn_v_w, n_gdn_heads, d_model, tn):
    o_sbq, o_sbk, o_sbv = 0, sb_w, 2 * sb_w
    o_gdn = 3 * sb_w
    conv_w = 2 * gdn_qk_w + gdn_v_w
    o_a = o_gdn + conv_w
    o_b = o_a + n_gdn_heads
    o_z = o_b + n_gdn_heads
    o_gate = o_z + gdn_v_w
    pad = (-w_in.shape[1]) % tn
    packed = jnp.concatenate([
        w_in[:, o_sbq:o_a],
        w_in[:, o_gate:o_gate + 2 * d_model],
        w_in[:, o_z:o_z + gdn_v_w],
        w_in[:, o_a:o_z],
        jnp.zeros((w_in.shape[0], pad), w_in.dtype)], axis=1).astype(BF16)
    return packed


def _block(x2d, c, w_ada, b_ada, norm1_w, w_in, sb_q_norm_w, sb_k_norm_w, gdn_conv_w, gdn_A_log,
           gdn_dt_bias, gdn_o_norm_w, w_proj_sb, w_proj_gdn, w_o, norm2_w, peer_w_q, peer_sub_keys,
           peer_u, peer_v, batch, seq_len):
    t, d = x2d.shape
    sb_w = w_proj_sb.shape[0]
    gdn_v_w = w_proj_gdn.shape[0]
    gdn_qk_w = (gdn_conv_w.shape[1] - gdn_v_w) // 2
    n_sb_heads = sb_w // HEAD_DIM
    n_gdn_heads = gdn_v_w // HEAD_DIM
    assert gdn_qk_w == gdn_v_w == sb_w and d % (4 * LANES) == 0

    c_pad = jnp.zeros((8, d), F32).at[:batch].set(c)
    mod = _ada_call(c_pad, w_ada, b_ada)[:batch].reshape(batch, 6, d)

    tn = sb_w
    w_packed = _pack_in_proj(w_in, sb_w, gdn_qk_w, gdn_v_w, n_gdn_heads, d, tn)
    heads_per_tile = tn // HEAD_DIM
    q_tiles = sb_w // tn
    qkw = jnp.concatenate([jnp.tile(sb_q_norm_w * (HEAD_DIM ** -0.5 * LOG2E), (q_tiles, heads_per_tile)),
                           jnp.tile(sb_k_norm_w, (q_tiles, heads_per_tile))], axis=0).reshape(2 * q_tiles, 1, tn)
    proj = _inproj_call(x2d, mod, norm1_w.reshape(1, d), w_packed, qkw, seq_len, tn=tn)

    ysb = _sb_call(proj, batch, seq_len, n_sb_heads // 2)

    col_gdn = 3 * sb_w // gdn_v_w
    col_gate = (3 * sb_w + 3 * gdn_v_w) // d
    col_z = (3 * sb_w + 3 * gdn_v_w + 2 * d) // gdn_v_w
    col_ab = (3 * sb_w + 4 * gdn_v_w + 2 * d) // (2 * LANES)
    prep = _gdn_prep_call(proj, gdn_conv_w, gdn_A_log, gdn_dt_bias, seq_len, n_gdn_heads,
                          (col_gdn, col_gdn + 1, col_gdn + 2), col_ab, GDN_BLOCK)
    ogdn = _gdn_core_call(prep, batch, seq_len, n_gdn_heads, GDN_BLOCK)

    x1, h2, pq = _merge_call(
        x2d, ysb, ogdn, proj, mod, jnp.tile(gdn_o_norm_w, n_gdn_heads).reshape(1, gdn_v_w), norm2_w.reshape(1, d),
        w_proj_sb.astype(BF16), w_proj_gdn.astype(BF16), w_o.astype(BF16), peer_w_q.astype(BF16),
        seq_len, col_gate, col_gate + 1, col_z)

    rank2, e2, cnt1, w1 = _route_call(pq, peer_sub_keys)
    return _peer_call(h2, peer_u.astype(BF16), peer_v.T.astype(BF16), rank2, e2, cnt1, w1, x1, mod, seq_len)


def kernel(x, c, w_ada, b_ada, norm1_w, w_in, sb_q_norm_w, sb_k_norm_w, gdn_conv_w, gdn_A_log, gdn_dt_bias,
           gdn_o_norm_w, w_proj_sb, w_proj_gdn, w_o, norm2_w, peer_w_q, peer_sub_keys, peer_u, peer_v):
    batch, seq_len, d = x.shape
    x2d = x.reshape(batch * seq_len, d)
    for l in range(w_ada.shape[0]):
        x2d = _block(x2d, c, w_ada[l], b_ada[l], norm1_w[l], w_in[l], sb_q_norm_w[l], sb_k_norm_w[l],
                     gdn_conv_w[l], gdn_A_log[l], gdn_dt_bias[l], gdn_o_norm_w[l], w_proj_sb[l], w_proj_gdn[l],
                     w_o[l], norm2_w[l], peer_w_q[l], peer_sub_keys[l], peer_u[l], peer_v[l], batch, seq_len)
    return x2d.reshape(batch, seq_len, d)
```

```python
import functools

import jax
import jax.numpy as jnp
from jax import lax
from jax.experimental import pallas as pl
from jax.experimental.pallas import tpu as pltpu

F32 = jnp.float32
BF16 = jnp.bfloat16
HI = lax.Precision.HIGHEST
EPS = 1e-6
LOG2E = 1.4426950408889634

LANES = 128
BF16_TILE_ROWS = 16
HEAD_DIM = 64
GDN_BLOCK = 128
GDN_CONV = 4
PEER_TOPK = 16
NOT_RANKED = 99.0
VMEM_LIMIT = 56 * 1024 * 1024

NT_DIMS = (((1,), (1,)), ((), ()))
TN_DIMS = (((0,), (0,)), ((), ()))


def _cparams(*sem, flags=None):
    return pltpu.CompilerParams(dimension_semantics=sem, vmem_limit_bytes=VMEM_LIMIT, flags=flags)


def _sigmoid(x):
    return 1.0 / (1.0 + jnp.exp(-x))


def _silu(x):
    return x * _sigmoid(x)


def _softplus(x):
    return jnp.maximum(x, 0.0) + jnp.log(1.0 + jnp.exp(-jnp.abs(x)))


def _block_diag_ones(n, group, value=1.0, dtype=F32):
    r = jnp.arange(n) // group
    return jnp.where(r[:, None] == r[None, :], value, 0.0).astype(dtype)


def _bf16_pieces(x):
    hi = x.astype(BF16)
    rest = x - hi.astype(F32)
    mid = rest.astype(BF16)
    return hi, mid, (rest - mid.astype(F32)).astype(BF16)


def _group_sum_sq(x, bd):
    slab = bd.shape[0]
    outs = []
    for c0 in range(0, x.shape[1], slab):
        sq = x[:, c0:c0 + slab] * x[:, c0:c0 + slab]
        hi = sq.astype(BF16)
        lo = (sq - hi.astype(F32)).astype(BF16)
        outs.append(jnp.dot(hi, bd, preferred_element_type=F32) + jnp.dot(lo, bd, preferred_element_type=F32))
    return outs[0] if len(outs) == 1 else jnp.concatenate(outs, axis=1)


def _ada_kernel(c_ref, w_ref, b_ref, o_ref):
    c = c_ref[...]
    o_ref[...] = jnp.dot(_silu(c), w_ref[...], preferred_element_type=F32, precision=HI) + b_ref[...]


def _ada_call(c_pad, w_ada, b_ada):
    rows, d = c_pad.shape
    n = w_ada.shape[1]
    tn = 512
    return pl.pallas_call(
        _ada_kernel,
        grid=(n // tn,),
        in_specs=[pl.BlockSpec((rows, d), lambda j: (0, 0)),
                  pl.BlockSpec((d, tn), lambda j: (0, j)),
                  pl.BlockSpec((1, tn), lambda j: (0, j))],
        out_specs=pl.BlockSpec((rows, tn), lambda j: (0, j)),
        out_shape=jax.ShapeDtypeStruct((rows, n), F32),
        compiler_params=_cparams("arbitrary"),
        name="ada_mod",
    )(c_pad, w_ada, b_ada.reshape(1, n))


def _inproj_kernel(x_ref, mod_ref, n1_ref, w_ref, qkw_ref, bd_ref, o_ref, tail_ref, h_scr, *, n_qk_tiles):
    j = pl.program_id(1)
    last = pl.num_programs(1) - 1

    @pl.when(j == 0)
    def _():
        x = x_ref[...]
        ms = jnp.mean(x * x, axis=-1, keepdims=True)
        y = x * lax.rsqrt(ms + EPS) * n1_ref[...]
        shift = mod_ref[0, 0:1, :]
        scale = mod_ref[0, 1:2, :]
        h_scr[...] = (y * (1.0 + scale) + shift).astype(BF16)

    acc = jnp.dot(h_scr[...], w_ref[...], preferred_element_type=F32)

    @pl.when(j < n_qk_tiles)
    def _():
        o_ref[...] = (acc * lax.rsqrt(_group_sum_sq(acc, bd_ref[...]) + EPS) * qkw_ref[0]).astype(BF16)

    @pl.when((j >= n_qk_tiles) & (j < last))
    def _():
        o_ref[...] = acc.astype(BF16)

    @pl.when(j == last)
    def _():
        tail_ref[...] = acc


def _inproj_call(x2d, mod, n1w, w_packed, qkw, seq_len, tm=2048, tn=512):
    t, d = x2d.shape
    n = w_packed.shape[1]
    n_tiles = n // tn
    n_qk_tiles = qkw.shape[0]
    bd = _block_diag_ones(2 * LANES, HEAD_DIM, 1.0 / HEAD_DIM, BF16)
    tiles_per_seq = seq_len // tm
    return pl.pallas_call(
        functools.partial(_inproj_kernel, n_qk_tiles=n_qk_tiles),
        grid=(t // tm, n // tn),
        in_specs=[pl.BlockSpec((tm, d), lambda i, j: (i, 0)),
                  pl.BlockSpec((1, 6, d), lambda i, j: (i // tiles_per_seq, 0, 0)),
                  pl.BlockSpec((1, d), lambda i, j: (0, 0)),
                  pl.BlockSpec((d, tn), lambda i, j: (0, j)),
                  pl.BlockSpec((1, 1, tn), lambda i, j: (jnp.minimum(j, n_qk_tiles - 1), 0, 0)),
                  pl.BlockSpec(bd.shape, lambda i, j: (0, 0))],
        out_specs=[pl.BlockSpec((tm, tn), lambda i, j: (i, jnp.minimum(j, n_tiles - 2))),
                   pl.BlockSpec((tm, tn), lambda i, j: (i, 0))],
        out_shape=[jax.ShapeDtypeStruct((t, n - tn), BF16), jax.ShapeDtypeStruct((t, tn), F32)],
        scratch_shapes=[pltpu.VMEM((tm, d), BF16)],
        compiler_params=_cparams("arbitrary", "arbitrary"),
        name="in_proj",
    )(x2d, mod, n1w, w_packed, qkw, bd)


def _sb_kernel(q_ref, k_ref, v_ref, tri_ref, o_ref, acc_ref, r_ref, *, tq):
    i = pl.program_id(2)
    half = tri_ref.shape[0]
    lane = lax.broadcasted_iota(jnp.int32, (1, LANES), 1)
    first_half = lane < HEAD_DIM
    q = q_ref[...].astype(F32)
    q_heads = (jnp.where(first_half, q, 0.0).astype(BF16), jnp.where(first_half, 0.0, q).astype(BF16))
    acc_ref[...] = jnp.zeros_like(acc_ref)
    r_ref[...] = jnp.zeros_like(r_ref)

    def run(blocks, masked):
        items = [(j, hd) for j in blocks for hd in range(2)]
        st = [dict() for _ in items]
        if masked:
            causal = (lax.broadcasted_iota(jnp.int32, (tq, tq), 1) < lax.broadcasted_iota(jnp.int32, (tq, tq), 0))

        def rows_of(n):
            return pl.ds(pl.multiple_of(items[n][0] * tq, tq), tq)

        def logits(n):
            kb = k_ref[rows_of(n), :].astype(BF16)
            st[n]["y"] = lax.dot_general(q_heads[items[n][1]], kb, NT_DIMS, preferred_element_type=F32)

        def softplus(n):
            y = st[n]["y"]
            neg_abs = lax.bitcast_convert_type(lax.bitcast_convert_type(y, jnp.uint32) | jnp.uint32(0x80000000), F32)
            sp = jnp.maximum(y, 0.0) + jnp.log(1.0 + jnp.exp2(neg_abs)) * LOG2E
            if masked:
                sp = jnp.where(causal, sp, 0.0)
            st[n]["log_beta"] = y - sp
            st[n]["spb"] = sp.astype(BF16)
            st[n]["sp_first"] = (sp[:, 0:1], sp[:, half:half + 1])

        def later_sums(n):
            spb = st[n]["spb"]
            first_lo, first_hi = st[n]["sp_first"]
            later_hi = jnp.dot(spb[:, half:], tri_ref[...], preferred_element_type=F32)
            total_hi = later_hi[:, 0:1] + first_hi
            later_lo = jnp.dot(spb[:, :half], tri_ref[...], preferred_element_type=F32) + total_hi
            st[n]["later"] = jnp.concatenate([later_lo, later_hi], axis=1)
            st[n]["total"] = later_lo[:, 0:1] + first_lo

        def weights(n):
            a = jnp.exp2(st[n]["log_beta"] - st[n]["later"])
            if masked:
                a = jnp.where(causal, a, 0.0)
            st[n]["a"] = a.astype(BF16)

        def accumulate(n):
            hd = items[n][1]
            pv = jnp.dot(st[n]["a"], v_ref[rows_of(n), :].astype(BF16), preferred_element_type=F32)
            r = r_ref[hd]
            acc_ref[hd] += jnp.exp2(-r) * pv
            r_ref[hd] = r + st[n]["total"]

        stages = (logits, softplus, later_sums, weights, accumulate)
        elementwise = (softplus, weights)
        for wave in range(len(items) + len(stages) - 1):
            todo = [(stages[wave - n], n) for n in range(len(items)) if 0 <= wave - n < len(stages)]
            for stage, n in [x for x in todo if x[0] in elementwise] + [x for x in todo if x[0] not in elementwise]:
                stage(n)

    run([i], True)

    def body(it, carry):
        j = i - 1 - 2 * it
        run([j, j - 1], False)
        return carry

    lax.fori_loop(0, i // 2, body, 0)

    @pl.when(i % 2 == 1)
    def _():
        run([0], False)

    o_ref[...] = jnp.where(first_half, acc_ref[0], acc_ref[1])


def _sb_call(proj, batch, seq_len, n_pairs, tq=512):
    t = proj.shape[0]
    nq = seq_len // tq
    half = tq // 2
    kk = jnp.arange(half)
    later = (kk[:, None] > kk[None, :]).astype(BF16)
    return pl.pallas_call(
        functools.partial(_sb_kernel, tq=tq),
        grid=(batch, n_pairs, nq),
        in_specs=[pl.BlockSpec((tq, LANES), lambda b, p, i: (b * nq + i, p)),
                  pl.BlockSpec((seq_len, LANES), lambda b, p, i: (b, n_pairs + p)),
                  pl.BlockSpec((seq_len, LANES), lambda b, p, i: (b, 2 * n_pairs + p)),
                  pl.BlockSpec((half, half), lambda b, p, i: (0, 0))],
        out_specs=pl.BlockSpec((tq, LANES), lambda b, p, i: (b * nq + i, p)),
        out_shape=jax.ShapeDtypeStruct((t, n_pairs * LANES), F32),
        scratch_shapes=[pltpu.VMEM((2, tq, LANES), F32), pltpu.VMEM((2, tq, 1), F32)],
        compiler_params=_cparams("arbitrary", "arbitrary", "arbitrary"),
        name="sb_attention",
    )(proj, proj, proj, later)


def _gdn_prep_kernel(q_ref, k_ref, v_ref, qt_ref, kt_ref, vt_ref, ab_ref, cw_ref, alog_ref, dtb_ref,
                     ea_ref, eb_ref, lc_ref, jc_ref, bd_ref,
                     qn_o, qd_o, kn_o, kd_o, kb_o, kbg_o, vb_o, gcol_o, grow_o, scr, *, tm, tiles_per_seq):
    i = pl.program_id(0)
    keep_tail = (i % tiles_per_seq != 0).astype(F32)
    w = cw_ref[...]
    width = q_ref.shape[1]

    halo = qt_ref.shape[0]

    def conv_silu(cur_ref, tail_ref, col0):
        scr[0:halo, :] = tail_ref[...].astype(F32) * keep_tail
        scr[halo:, :] = cur_ref[...].astype(F32)
        y = jnp.zeros((tm, width), F32)
        for tap in range(GDN_CONV):
            off = halo - (GDN_CONV - 1) + tap
            y = y + scr[off:off + tm, :] * w[tap:tap + 1, col0:col0 + width]
        return _silu(y)

    bd = bd_ref[...]
    cq = conv_silu(q_ref, qt_ref, 0)
    qn = cq * lax.rsqrt(_group_sum_sq(cq, bd) + EPS) * (HEAD_DIM ** -0.5)
    ck = conv_silu(k_ref, kt_ref, width)
    kn = ck * lax.rsqrt(_group_sum_sq(ck, bd) + EPS)
    cv = conv_silu(v_ref, vt_ref, 2 * width)

    ab = ab_ref[...]
    g = -jnp.exp(alog_ref[...]) * _softplus(ab + dtb_ref[...])
    beta = _sigmoid(ab)
    g_parts = _bf16_pieces(g)
    lc, jc, ea = lc_ref[...], jc_ref[...], ea_ref[...]
    g_cum = sum(jnp.dot(lc, p, preferred_element_type=F32) for p in g_parts)
    g_tot = sum(jnp.dot(jc, p, preferred_element_type=F32) for p in g_parts)
    gx = sum(jnp.dot(p, ea, preferred_element_type=F32) for p in _bf16_pieces(g_cum))
    glx = sum(jnp.dot(p, ea, preferred_element_type=F32) for p in _bf16_pieces(g_tot))
    bx = sum(jnp.dot(p, eb_ref[...], preferred_element_type=F32) for p in _bf16_pieces(beta))

    e_g = jnp.exp(gx)
    kb = kn * bx
    qn_o[...] = qn
    qd_o[...] = qn * e_g
    kn_o[...] = kn
    kd_o[...] = kn * jnp.exp(glx - gx)
    kb_o[...] = kb
    kbg_o[...] = kb * e_g
    vb_o[...] = cv * bx
    g_heads = g_cum[:, :LANES]
    gcol_o[...] = g_heads
    grow_o[...] = g_heads.T[0:8, :]


def _gdn_prep_call(proj, ab_logits, conv_w, a_log, dt_bias, seq_len, n_heads, col_qkv, chunk, tm=512):
    t = proj.shape[0]
    assert n_heads <= 8 and tm % chunk == 0
    width = n_heads * HEAD_DIM
    abw = 2 * LANES
    tiles_per_seq = seq_len // tm
    halo = 16
    sub = tm // halo
    alog_pad = jnp.zeros((1, abw), F32).at[0, :n_heads].set(a_log)
    dtb_pad = jnp.zeros((1, abw), F32).at[0, :n_heads].set(dt_bias)
    head_of_lane = jnp.arange(width) // HEAD_DIM
    rows = jnp.arange(abw)
    ea = (rows[:, None] == head_of_lane[None, :]).astype(BF16)
    eb = (rows[:, None] == head_of_lane[None, :] + n_heads).astype(BF16)
    tok = jnp.arange(tm)
    same_chunk = (tok[:, None] // chunk) == (tok[None, :] // chunk)
    lc = (same_chunk & (tok[:, None] >= tok[None, :])).astype(BF16)
    jc = same_chunk.astype(BF16)
    bd = _block_diag_ones(2 * LANES, HEAD_DIM, 1.0, BF16)

    cur = lambda c: pl.BlockSpec((tm, width), lambda i: (i, c))
    tail = lambda c: pl.BlockSpec((halo, width), lambda i: (jnp.maximum(i * sub - 1, 0), c))
    full = lambda a: pl.BlockSpec(a.shape, lambda i: (0,) * a.ndim)
    cq, ck, cv = col_qkv
    out_spec = pl.BlockSpec((tm, width), lambda i: (i, 0))
    out_shape = jax.ShapeDtypeStruct((t, width), F32)
    return pl.pallas_call(
        functools.partial(_gdn_prep_kernel, tm=tm, tiles_per_seq=tiles_per_seq),
        grid=(t // tm,),
        in_specs=[cur(cq), cur(ck), cur(cv), tail(cq), tail(ck), tail(cv),
                  pl.BlockSpec((tm, abw), lambda i: (i, 0)),
                  full(conv_w), full(alog_pad), full(dtb_pad), full(ea), full(eb), full(lc), full(jc), full(bd)],
        out_specs=[out_spec] * 7 + [pl.BlockSpec((tm, LANES), lambda i: (i, 0)), pl.BlockSpec((8, tm), lambda i: (0, i))],
        out_shape=[out_shape] * 7 + [jax.ShapeDtypeStruct((t, LANES), F32), jax.ShapeDtypeStruct((8, t), F32)],
        scratch_shapes=[pltpu.VMEM((tm + halo, width), F32)],
        compiler_params=_cparams("arbitrary"),
        name="gdn_prep",
    )(proj, proj, proj, proj, proj, proj, ab_logits, conv_w, alog_pad, dtb_pad, ea, eb, lc, jc, bd)


def _gdn_core_kernel(qn_ref, qd_ref, kn_ref, kd_ref, kb_ref, kbg_ref, vb_ref, gcol_ref, grow_ref,
                     o_ref, s_scr, *, ts, n_heads, c_len):
    @pl.when(pl.program_id(1) == 0)
    def _():
        s_scr[...] = jnp.zeros_like(s_scr)

    row = lax.broadcasted_iota(jnp.int32, (c_len, c_len), 0)
    col = lax.broadcasted_iota(jnp.int32, (c_len, c_len), 1)
    strict = row > col
    incl = row >= col
    eye_state = (lax.broadcasted_iota(jnp.int32, (HEAD_DIM, HEAD_DIM), 0)
                 == lax.broadcasted_iota(jnp.int32, (HEAD_DIM, HEAD_DIM), 1))
    n_levels = (c_len - 1).bit_length()
    n_chunks = ts // c_len
    problems = [(c, h) for c in range(n_chunks) for h in range(n_heads)]

    def bf(x):
        return x.astype(BF16)

    def mm(a, b):
        return jnp.dot(a, b, preferred_element_type=F32)

    def head_tile(ref, c, h):
        pair, half = divmod(h, 2)
        tile = ref[c * c_len:(c + 1) * c_len, pair * LANES:(pair + 1) * LANES]
        return tile[:, half * HEAD_DIM:(half + 1) * HEAD_DIM]

    sibling = [((row >> k) ^ (col >> k)) == 1 for k in range(n_levels)]

    xs, ms, rs, qks = [], [], [], []
    for c, h in problems:
        g_col = gcol_ref[c * c_len:(c + 1) * c_len, h:h + 1]
        g_row = grow_ref[h:h + 1, c * c_len:(c + 1) * c_len]
        decay = jnp.where(incl, jnp.exp(g_col - g_row), 0.0)
        kn_b = bf(head_tile(kn_ref, c, h))
        kk = lax.dot_general(bf(head_tile(kb_ref, c, h)), kn_b, NT_DIMS, preferred_element_type=F32)
        x = jnp.where(strict, -(kk * decay), 0.0)
        xs.append(x)
        ms.append(jnp.where(row == col, 1.0, jnp.where(sibling[0], x, 0.0)))
        qks.append(bf(lax.dot_general(bf(head_tile(qn_ref, c, h)), kn_b, NT_DIMS, preferred_element_type=F32) * decay))
        rs.append(bf(jnp.concatenate([head_tile(vb_ref, c, h), head_tile(kbg_ref, c, h)], axis=1)))

    for k in range(1, n_levels):
        for idx in range(len(problems)):
            m_b = bf(ms[idx])
            left = mm(m_b, bf(jnp.where(sibling[k], xs[idx], 0.0)))
            ms[idx] = ms[idx] + mm(bf(left), m_b)

    q_eff, o_zero, p_mat, b_mat = {}, {}, {}, {}
    for idx, (c, h) in enumerate(problems):
        sol = bf(mm(bf(ms[idx]), rs[idx]))
        m1 = mm(qks[idx], sol)
        q_eff[c, h] = bf(head_tile(qd_ref, c, h) - m1[:, HEAD_DIM:])
        o_zero[c, h] = m1[:, :HEAD_DIM]
        m2 = lax.dot_general(bf(head_tile(kd_ref, c, h)), sol, TN_DIMS, preferred_element_type=F32)
        chunk_decay = jnp.exp(gcol_ref[(c + 1) * c_len - 1:(c + 1) * c_len, h:h + 1])
        p_mat[c, h] = bf(jnp.where(eye_state, chunk_decay, 0.0) - m2[:, HEAD_DIM:])
        b_mat[c, h] = m2[:, :HEAD_DIM]

    states = [s_scr[h] for h in range(n_heads)]
    for c in range(n_chunks):
        outs = []
        for h in range(n_heads):
            s_b = bf(states[h])
            outs.append(mm(q_eff[c, h], s_b) + o_zero[c, h])
            states[h] = mm(p_mat[c, h], s_b) + b_mat[c, h]
        o_ref[c * c_len:(c + 1) * c_len, :] = jnp.concatenate(outs, axis=1)
    for h in range(n_heads):
        s_scr[h] = states[h]


def _gdn_core_call(prep, batch, seq_len, n_heads, c_len, ts=512):
    t, width = prep[0].shape
    n_seq_tiles = seq_len // ts
    spec = pl.BlockSpec((ts, width), lambda b, s: (b * n_seq_tiles + s, 0))
    return pl.pallas_call(
        functools.partial(_gdn_core_kernel, ts=ts, n_heads=n_heads, c_len=c_len),
        grid=(batch, n_seq_tiles),
        in_specs=[spec] * 7 + [pl.BlockSpec((ts, LANES), lambda b, s: (b * n_seq_tiles + s, 0)),
                               pl.BlockSpec((8, ts), lambda b, s: (0, b * n_seq_tiles + s))],
        out_specs=spec,
        out_shape=jax.ShapeDtypeStruct((t, width), F32),
        scratch_shapes=[pltpu.VMEM((n_heads, HEAD_DIM, HEAD_DIM), F32)],
        compiler_params=_cparams("arbitrary", "arbitrary"),
        name="gdn_core",
    )(*prep)


def _merge_kernel(x_ref, ysb_ref, og_ref, z_ref, g0_ref, g1_ref, mod_ref, onw_ref, n2_ref, bd_ref,
                  wsb_ref, wgdn_ref, wo_ref, wq_ref, x1_o, h2t_o, pq_o):
    og = og_ref[...]
    ygdn = og * lax.rsqrt(_group_sum_sq(og, bd_ref[...]) + EPS) * onw_ref[...] * _silu(z_ref[...].astype(F32))
    m = (_sigmoid(g0_ref[...].astype(F32)) * jnp.dot(ysb_ref[...].astype(BF16), wsb_ref[...],
                                                      preferred_element_type=F32)
         + _sigmoid(g1_ref[...].astype(F32)) * jnp.dot(ygdn.astype(BF16), wgdn_ref[...], preferred_element_type=F32))
    gate1 = mod_ref[0, 2:3, :]
    shift2 = mod_ref[0, 3:4, :]
    scale2 = mod_ref[0, 4:5, :]
    x1 = x_ref[...] + gate1 * jnp.dot(m.astype(BF16), wo_ref[...], preferred_element_type=F32)
    x1_o[...] = x1
    ms2 = jnp.mean(x1 * x1, axis=-1, keepdims=True)
    h2 = x1 * lax.rsqrt(ms2 + EPS) * n2_ref[...] * (1.0 + scale2) + shift2
    h2t_o[...] = h2.T.astype(BF16)
    pq_o[...] = jnp.dot(h2.astype(BF16), wq_ref[...], preferred_element_type=F32)


def _merge_call(x2d, ysb, ogdn, proj, mod, onw, n2w, wsb, wgdn, wo, wq, seq_len, col_g0, col_g1, col_z, tm=512):
    t, d = x2d.shape
    width = ysb.shape[1]
    nq = wq.shape[1]
    bd = _block_diag_ones(2 * LANES, HEAD_DIM, 1.0 / HEAD_DIM, BF16)
    tiles_per_seq = seq_len // tm
    full = lambda a: pl.BlockSpec(a.shape, lambda i: (0,) * a.ndim)
    once = lambda a: pl.BlockSpec(a.shape, lambda i: (0,) * a.ndim, pipeline_mode=pl.Buffered(1))
    return pl.pallas_call(
        _merge_kernel,
        grid=(t // tm,),
        in_specs=[pl.BlockSpec((tm, d), lambda i: (i, 0)),
                  pl.BlockSpec((tm, width), lambda i: (i, 0)),
                  pl.BlockSpec((tm, width), lambda i: (i, 0)),
                  pl.BlockSpec((tm, width), lambda i: (i, col_z)),
                  pl.BlockSpec((tm, d), lambda i: (i, col_g0)),
                  pl.BlockSpec((tm, d), lambda i: (i, col_g1)),
                  pl.BlockSpec((1, 6, d), lambda i: (i // tiles_per_seq, 0, 0)),
                  full(onw), full(n2w), full(bd), once(wsb), once(wgdn), once(wo), once(wq)],
        out_specs=[pl.BlockSpec((tm, d), lambda i: (i, 0)),
                   pl.BlockSpec((d, tm), lambda i: (0, i)),
                   pl.BlockSpec((tm, nq), lambda i: (i, 0))],
        out_shape=[jax.ShapeDtypeStruct((t, d), F32),
                   jax.ShapeDtypeStruct((d, t), BF16),
                   jax.ShapeDtypeStruct((t, nq), F32)],
        compiler_params=_cparams("arbitrary"),
        name="merge_proj",
    )(x2d, ysb, ogdn, proj, proj, proj, mod, onw, n2w, bd, wsb, wgdn, wo, wq)


def _extract_topk(s, k, break_ties):
    n = s.shape[0]
    if not break_ties:
        lowest_bits = -8388609
        vals = []
        for r in range(k):
            m = jnp.max(s, axis=0, keepdims=True)
            marker = lax.bitcast_convert_type(jnp.int32(lowest_bits - r), F32)
            s = jnp.where(s == m, marker, s)
            vals.append(m)
        took = jnp.int32(lowest_bits) - lax.bitcast_convert_type(s, jnp.int32)
        rank = jnp.where((took >= 0) & (took < k), took.astype(F32), NOT_RANKED)
        return vals, rank
    iota = lax.broadcasted_iota(jnp.int32, s.shape, 0).astype(F32)
    rank = jnp.full(s.shape, NOT_RANKED, F32)
    vals = []
    for r in range(k):
        m = jnp.max(s, axis=0, keepdims=True)
        hit = iota == jnp.min(jnp.where(s == m, iota, float(n)), axis=0, keepdims=True)
        rank = jnp.where(hit, float(r), rank)
        s = jnp.where(hit, -jnp.inf, s)
        vals.append(m)
    return vals, rank


def _candidate_tables(k):
    pairs = [(a, b) for a in range(k) for b in range(k) if (a + 1) * (b + 1) <= k]
    n_pad = -(-len(pairs) // 8) * 8
    sel_a = jnp.zeros((n_pad, k), F32).at[jnp.arange(len(pairs)), jnp.array([a for a, _ in pairs])].set(1.0)
    sel_b = jnp.zeros((n_pad, k), F32).at[jnp.arange(len(pairs)), jnp.array([b for _, b in pairs])].set(1.0)
    return sel_a, sel_b, len(pairs)


def _bf16_pair_words(x):
    bits = lax.bitcast_convert_type(x.astype(BF16).astype(F32), jnp.uint32)
    return bits | (bits >> 16)


def _route_kernel(pq_ref, keys_ref, sela_ref, selb_ref, rank2_o, e2_o, cnt1_o, w1_o, *, n_cand):
    k = PEER_TOPK
    hp, _, _, half = keys_ref.shape
    tt = pq_ref.shape[0]
    part_scores = [[lax.dot_general(keys_ref[hh, part], pq_ref[:, (2 * hh + part) * half:(2 * hh + part + 1) * half],
                                    NT_DIMS, preferred_element_type=F32, precision=HI) for hh in range(hp)]
                   for part in range(2)]
    sel_a = sela_ref[...]

    def route(break_ties, heads):
        w = len(heads) * tt
        iota_k = lax.broadcasted_iota(jnp.int32, (k, w), 0).astype(F32)
        s_all = jnp.concatenate([part_scores[part][hh] for part in range(2) for hh in heads], axis=1)
        scores = (s_all[:, :w], s_all[:, w:])
        vals, rank = _extract_topk(s_all, k, break_ties)
        top = jnp.concatenate(vals, axis=0)
        top1, top2 = top[:, :w], top[:, w:]
        v1, v2 = [v[:, :w] for v in vals], [v[:, w:] for v in vals]
        rank1, rank2 = rank[:, :w], rank[:, w:]
        cand = (jnp.dot(sel_a, top1, preferred_element_type=F32, precision=HI)
                + jnp.dot(selb_ref[...], top2, preferred_element_type=F32, precision=HI))
        cand_row = lax.broadcasted_iota(jnp.int32, cand.shape, 0)
        _, cand_rank = _extract_topk(jnp.where(cand_row < n_cand, cand, -jnp.inf), k, break_ties)
        chosen = (cand_rank < float(k)).astype(BF16)
        count = lax.dot_general(sel_a.astype(BF16), chosen, TN_DIMS, preferred_element_type=F32)
        e1 = jnp.exp(top1 - v1[0])
        e2 = jnp.exp(top2 - v2[0])
        z = jnp.zeros_like(v1[0])
        for a in range(k):
            z = z + e1[a:a + 1] * jnp.sum(jnp.where(iota_k < count[a:a + 1], e2, 0.0), axis=0, keepdims=True)
        inv_z = 1.0 / z
        cnt1 = jnp.zeros_like(rank1)
        for a in range(k):
            cnt1 = jnp.where(rank1 == float(a), count[a:a + 1], cnt1)
        e2_all = jnp.exp(scores[1] - v2[0]).astype(BF16)
        w1_all = jnp.where(rank1 < float(k), jnp.exp(scores[0] - v1[0]) * inv_z, 0.0)
        cnt1_words = _bf16_pair_words(cnt1)
        w1_words = _bf16_pair_words(w1_all)
        n_ranked = jnp.sum((rank < float(k)).astype(F32), axis=0, keepdims=True)
        n_chosen = jnp.sum((cand_rank < float(k)).astype(F32), axis=0, keepdims=True)
        off_by = jnp.maximum(jnp.maximum(jnp.abs(n_ranked[:, :w] - k), jnp.abs(n_ranked[:, w:] - k)),
                             jnp.abs(n_chosen - k))
        excess = []
        for pos, hh in enumerate(heads):
            lanes = slice(pos * tt, (pos + 1) * tt)
            rank2_o[hh] = rank2[:, lanes].astype(BF16).reshape(rank2_o.shape[1:])
            e2_o[hh] = e2_all[:, lanes].reshape(e2_o.shape[1:])
            cnt1_o[hh] = cnt1_words[:, lanes]
            w1_o[hh] = w1_words[:, lanes]
            excess.append(jnp.max(off_by[:, lanes]))
        return excess

    excess = route(False, list(range(hp)))
    for hh in range(hp):
        @pl.when(excess[hh] > 0.0)
        def _(hh=hh):
            route(True, [hh])


def _route_call(pq, sub_keys, tt=256, hp=4):
    t = pq.shape[0]
    n_heads, _, n_keys, half = sub_keys.shape
    sel_a, sel_b, n_cand = _candidate_tables(PEER_TOPK)
    out_spec = pl.BlockSpec((hp, n_keys, tt), lambda i, h: (h, 0, i))
    shape = lambda dt: jax.ShapeDtypeStruct((n_heads, n_keys, t), dt)
    tiled_spec = pl.BlockSpec((hp, n_keys // BF16_TILE_ROWS, BF16_TILE_ROWS, tt), lambda i, h: (h, 0, 0, i))
    tiled_shape = jax.ShapeDtypeStruct((n_heads, n_keys // BF16_TILE_ROWS, BF16_TILE_ROWS, t), BF16)
    return pl.pallas_call(
        functools.partial(_route_kernel, n_cand=n_cand),
        grid=(t // tt, n_heads // hp),
        in_specs=[pl.BlockSpec((tt, hp * 2 * half), lambda i, h: (i, h)),
                  pl.BlockSpec((hp, 2, n_keys, half), lambda i, h: (h, 0, 0, 0)),
                  pl.BlockSpec(sel_a.shape, lambda i, h: (0, 0)),
                  pl.BlockSpec(sel_b.shape, lambda i, h: (0, 0))],
        out_specs=[tiled_spec, tiled_spec, out_spec, out_spec],
        out_shape=[tiled_shape, tiled_shape, shape(jnp.uint32), shape(jnp.uint32)],
        compiler_params=_cparams("arbitrary", "arbitrary"),
        name="peer_route",
    )(pq, sub_keys, sel_a, sel_b)


def _peer_kernel(h2t_ref, u_ref, vt_ref, rank2_ref, e2_ref, cnt1_ref, w1_ref, x1_ref, mod_ref,
                 o_ref, act_even, act_odd, acc_scr, *, n_heads, n_keys, ec, group, n_chunks, n_items):
    s = pl.program_id(0)
    chunk = jnp.clip(s - 1, 0, n_items - 1) % n_chunks

    @pl.when(s == 0)
    def _():
        act_odd[...] = jnp.zeros_like(act_odd)

    @pl.when((chunk == 0) | (s == 0))
    def _():
        acc_scr[...] = jnp.zeros_like(acc_scr)

    tt = h2t_ref.shape[1]
    zero = jnp.zeros((), BF16)
    n_groups = ec // group
    subs_per_group = group // n_keys

    def coef_of(grp, act):
        coefs = []
        for s_loc in range(subs_per_group):
            sub = grp * subs_per_group + s_loc
            gate = None
            for h in range(n_heads):
                cnt = pltpu.bitcast(jnp.broadcast_to(cnt1_ref[h, sub:sub + 1, :], (8, tt)), BF16)
                w1 = pltpu.bitcast(jnp.broadcast_to(w1_ref[h, sub:sub + 1, :], (8, tt)), BF16)
                term = jnp.where(rank2_ref[h] < cnt[None], e2_ref[h], zero) * w1[None]
                gate = term if gate is None else gate + term
            a = act[s_loc * n_keys:(s_loc + 1) * n_keys, :].astype(BF16)
            gelu = (0.5 * a) * (1.0 + lax.erf(a * (2.0 ** -0.5)))
            coefs.append(gate.reshape(n_keys, tt) * gelu)
        return jnp.concatenate(coefs, axis=0)

    def stages(act_w, act_r):
        total = acc_scr[...]
        half_t = tt // 2
        new_parts = []
        for grp in range(n_groups):
            rows = slice(grp * group, (grp + 1) * group)
            coef = coef_of(grp, act_r[rows, :])
            if grp < 2:
                cols = slice(grp * half_t, (grp + 1) * half_t)
                new_parts.append(jnp.dot(u_ref[...], h2t_ref[:, cols], preferred_element_type=F32))
            total = total + jnp.dot(vt_ref[:, rows], coef, preferred_element_type=F32)
        acc_scr[...] = total
        act_w[...] = jnp.concatenate(new_parts, axis=1)

    @pl.when(s % 2 == 0)
    def _():
        stages(act_even, act_odd)

    @pl.when(s % 2 == 1)
    def _():
        stages(act_odd, act_even)

    @pl.when((chunk == n_chunks - 1) & (s >= 1))
    def _():
        gate2 = mod_ref[0, 5:6, :]
        o_ref[...] = x1_ref[...] + gate2 * acc_scr[...].T


def _peer_call(h2t, u_b, vt_b, rank2, e2, cnt1, w1, x1, mod, seq_len, tt=512, ec=2048, group=1024):
    d, t = h2t.shape
    n_exp = u_b.shape[0]
    n_heads, n_key_tiles, tile_rows, _ = rank2.shape
    n_keys = n_key_tiles * tile_rows
    tiles_per_seq = seq_len // tt
    n_chunks = n_exp // ec
    n_items = (t // tt) * n_chunks
    first = lambda s: jnp.minimum(s, n_items - 1)
    second = lambda s: jnp.clip(s - 1, 0, n_items - 1)
    route_spec = pl.BlockSpec((n_heads, n_key_tiles, tile_rows, tt), lambda s: (0, 0, 0, second(s) // n_chunks))
    row_spec = pl.BlockSpec((n_heads, ec // n_keys, tt), lambda s: (0, second(s) % n_chunks, second(s) // n_chunks))
    return pl.pallas_call(
        functools.partial(_peer_kernel, n_heads=n_heads, n_keys=n_keys, ec=ec, group=group,
                          n_chunks=n_chunks, n_items=n_items),
        grid=(n_items + 1,),
        in_specs=[pl.BlockSpec((d, tt), lambda s: (0, first(s) // n_chunks)),
                  pl.BlockSpec((ec, d), lambda s: (first(s) % n_chunks, 0)),
                  pl.BlockSpec((d, ec), lambda s: (0, second(s) % n_chunks)),
                  route_spec, route_spec, row_spec, row_spec,
                  pl.BlockSpec((tt, d), lambda s: (second(s) // n_chunks, 0)),
                  pl.BlockSpec((1, 6, d), lambda s: (second(s) // n_chunks // tiles_per_seq, 0, 0))],
        out_specs=pl.BlockSpec((tt, d), lambda s: (second(s) // n_chunks, 0)),
        out_shape=jax.ShapeDtypeStruct((t, d), F32),
        scratch_shapes=[pltpu.VMEM((ec, tt), F32), pltpu.VMEM((ec, tt), F32), pltpu.VMEM((d, tt), F32)],
        compiler_params=_cparams("arbitrary"),
        name="peer_experts",
    )(h2t, u_b, vt_b, rank2, e2, cnt1, w1, x1, mod)


def _pack_in_proj(w_in, sb_w, gdn_qk_w, gdn_v_w, n_gdn_heads, d_model, tn):
    o_sbq, o_sbk, o_sbv = 0, sb_w, 2 * sb_w
    o_gdn = 3 * sb_w
    conv_w = 2 * gdn_qk_w + gdn_v_w
    o_a = o_gdn + conv_w
    o_b = o_a + n_gdn_heads
    o_z = o_b + n_gdn_heads
    o_gate = o_z + gdn_v_w
    pad = (-w_in.shape[1]) % tn
    packed = jnp.concatenate([
        w_in[:, o_sbq:o_a],
        w_in[:, o_gate:o_gate + 2 * d_model],
        w_in[:, o_z:o_z + gdn_v_w],
        w_in[:, o_a:o_z],
        jnp.zeros((w_in.shape[0], pad), w_in.dtype)], axis=1).astype(BF16)
    return packed


def _block(x2d, c, w_ada, b_ada, norm1_w, w_in, sb_q_norm_w, sb_k_norm_w, gdn_conv_w, gdn_A_log,
           gdn_dt_bias, gdn_o_norm_w, w_proj_sb, w_proj_gdn, w_o, norm2_w, peer_w_q, peer_sub_keys,
           peer_u, peer_v, batch, seq_len):
    t, d = x2d.shape
    sb_w = w_proj_sb.shape[0]
    gdn_v_w = w_proj_gdn.shape[0]
    gdn_qk_w = (gdn_conv_w.shape[1] - gdn_v_w) // 2
    n_sb_heads = sb_w // HEAD_DIM
    n_gdn_heads = gdn_v_w // HEAD_DIM
    assert gdn_qk_w == gdn_v_w == sb_w and d % (4 * LANES) == 0

    c_pad = jnp.zeros((8, d), F32).at[:batch].set(c)
    mod = _ada_call(c_pad, w_ada, b_ada)[:batch].reshape(batch, 6, d)

    tn = sb_w
    w_packed = _pack_in_proj(w_in, sb_w, gdn_qk_w, gdn_v_w, n_gdn_heads, d, tn)
    heads_per_tile = tn // HEAD_DIM
    q_tiles = sb_w // tn
    qkw = jnp.concatenate([jnp.tile(sb_q_norm_w * (HEAD_DIM ** -0.5 * LOG2E), (q_tiles, heads_per_tile)),
                           jnp.tile(sb_k_norm_w, (q_tiles, heads_per_tile))], axis=0).reshape(2 * q_tiles, 1, tn)
    proj, ab_logits = _inproj_call(x2d, mod, norm1_w.reshape(1, d), w_packed, qkw, seq_len, tn=tn)

    ysb = _sb_call(proj, batch, seq_len, n_sb_heads // 2)

    col_gdn = 3 * sb_w // gdn_v_w
    col_gate = (3 * sb_w + 3 * gdn_v_w) // d
    col_z = (3 * sb_w + 3 * gdn_v_w + 2 * d) // gdn_v_w
    prep = _gdn_prep_call(proj, ab_logits, gdn_conv_w, gdn_A_log, gdn_dt_bias, seq_len, n_gdn_heads,
                          (col_gdn, col_gdn + 1, col_gdn + 2), GDN_BLOCK)
    ogdn = _gdn_core_call(prep, batch, seq_len, n_gdn_heads, GDN_BLOCK)

    x1, h2, pq = _merge_call(
        x2d, ysb, ogdn, proj, mod, jnp.tile(gdn_o_norm_w, n_gdn_heads).reshape(1, gdn_v_w), norm2_w.reshape(1, d),
        w_proj_sb.astype(BF16), w_proj_gdn.astype(BF16), w_o.astype(BF16), peer_w_q.astype(BF16),
        seq_len, col_gate, col_gate + 1, col_z)

    rank2, e2, cnt1, w1 = _route_call(pq, peer_sub_keys)
    return _peer_call(h2, peer_u.astype(BF16), peer_v.T.astype(BF16), rank2, e2, cnt1, w1, x1, mod, seq_len)


def kernel(x, c, w_ada, b_ada, norm1_w, w_in, sb_q_norm_w, sb_k_norm_w, gdn_conv_w, gdn_A_log, gdn_dt_bias,
           gdn_o_norm_w, w_proj_sb, w_proj_gdn, w_o, norm2_w, peer_w_q, peer_sub_keys, peer_u, peer_v):
    batch, seq_len, d = x.shape
    x2d = x.reshape(batch * seq_len, d)
    for l in range(w_ada.shape[0]):
        x2d = _block(x2d, c, w_ada[l], b_ada[l], norm1_w[l], w_in[l], sb_q_norm_w[l], sb_k_norm_w[l],
                     gdn_conv_w[l], gdn_A_log[l], gdn_dt_bias[l], gdn_o_norm_w[l], w_proj_sb[l], w_proj_gdn[l],
                     w_o[l], norm2_w[l], peer_w_q[l], peer_sub_keys[l], peer_u[l], peer_v[l], batch, seq_len)
    return x2d.reshape(batch, seq_len, d)
```

```python
import functools

import jax
import jax.numpy as jnp
from jax import lax
from jax.experimental import pallas as pl
from jax.experimental.pallas import tpu as pltpu

F32 = jnp.float32
BF16 = jnp.bfloat16
HI = lax.Precision.HIGHEST
EPS = 1e-6
LOG2E = 1.4426950408889634

LANES = 128
BF16_TILE_ROWS = 16
HEAD_DIM = 64
GDN_BLOCK = 128
GDN_CONV = 4
PEER_TOPK = 16
NOT_RANKED = 99.0
VMEM_LIMIT = 56 * 1024 * 1024

NT_DIMS = (((1,), (1,)), ((), ()))
TN_DIMS = (((0,), (0,)), ((), ()))


def _cparams(*sem, flags=None):
    return pltpu.CompilerParams(dimension_semantics=sem, vmem_limit_bytes=VMEM_LIMIT, flags=flags)


def _sigmoid(x):
    return 1.0 / (1.0 + jnp.exp(-x))


def _silu(x):
    return x * _sigmoid(x)


def _softplus(x):
    return jnp.maximum(x, 0.0) + jnp.log(1.0 + jnp.exp(-jnp.abs(x)))


def _block_diag_ones(n, group, value=1.0, dtype=F32):
    r = jnp.arange(n) // group
    return jnp.where(r[:, None] == r[None, :], value, 0.0).astype(dtype)


def _bf16_pieces(x):
    hi = x.astype(BF16)
    rest = x - hi.astype(F32)
    mid = rest.astype(BF16)
    return hi, mid, (rest - mid.astype(F32)).astype(BF16)


def _group_sum_sq(x, bd):
    slab = bd.shape[0]
    outs = []
    for c0 in range(0, x.shape[1], slab):
        sq = x[:, c0:c0 + slab] * x[:, c0:c0 + slab]
        hi = sq.astype(BF16)
        lo = (sq - hi.astype(F32)).astype(BF16)
        outs.append(jnp.dot(hi, bd, preferred_element_type=F32) + jnp.dot(lo, bd, preferred_element_type=F32))
    return outs[0] if len(outs) == 1 else jnp.concatenate(outs, axis=1)


def _ada_kernel(c_ref, w_ref, b_ref, o_ref):
    c = c_ref[...]
    o_ref[...] = jnp.dot(_silu(c), w_ref[...], preferred_element_type=F32, precision=HI) + b_ref[...]


def _ada_call(c_pad, w_ada, b_ada):
    rows, d = c_pad.shape
    n = w_ada.shape[1]
    tn = 2048
    return pl.pallas_call(
        _ada_kernel,
        grid=(n // tn,),
        in_specs=[pl.BlockSpec((rows, d), lambda j: (0, 0)),
                  pl.BlockSpec((d, tn), lambda j: (0, j)),
                  pl.BlockSpec((1, tn), lambda j: (0, j))],
        out_specs=pl.BlockSpec((rows, tn), lambda j: (0, j)),
        out_shape=jax.ShapeDtypeStruct((rows, n), F32),
        compiler_params=_cparams("arbitrary"),
        name="ada_mod",
    )(c_pad, w_ada, b_ada.reshape(1, n))


def _inproj_kernel(x_ref, mod_ref, n1_ref, w_ref, qkw_ref, bd_ref, o_ref, tail_ref, h_scr, *, n_qk_tiles):
    j = pl.program_id(1)
    last = pl.num_programs(1) - 1

    @pl.when(j == 0)
    def _():
        x = x_ref[...]
        ms = jnp.mean(x * x, axis=-1, keepdims=True)
        y = x * lax.rsqrt(ms + EPS) * n1_ref[...]
        shift = mod_ref[0, 0:1, :]
        scale = mod_ref[0, 1:2, :]
        h_scr[...] = (y * (1.0 + scale) + shift).astype(BF16)

    acc = jnp.dot(h_scr[...], w_ref[...], preferred_element_type=F32)

    @pl.when(j < n_qk_tiles)
    def _():
        o_ref[...] = (acc * lax.rsqrt(_group_sum_sq(acc, bd_ref[...]) + EPS) * qkw_ref[0]).astype(BF16)

    @pl.when((j >= n_qk_tiles) & (j < last))
    def _():
        o_ref[...] = acc.astype(BF16)

    @pl.when(j == last)
    def _():
        tail_ref[...] = acc


def _inproj_call(x2d, mod, n1w, w_packed, qkw, seq_len, tm=2048, tn=512):
    t, d = x2d.shape
    n = w_packed.shape[1]
    n_tiles = n // tn
    n_qk_tiles = qkw.shape[0]
    bd = _block_diag_ones(2 * LANES, HEAD_DIM, 1.0 / HEAD_DIM, BF16)
    tiles_per_seq = seq_len // tm
    return pl.pallas_call(
        functools.partial(_inproj_kernel, n_qk_tiles=n_qk_tiles),
        grid=(t // tm, n // tn),
        in_specs=[pl.BlockSpec((tm, d), lambda i, j: (i, 0)),
                  pl.BlockSpec((1, 6, d), lambda i, j: (i // tiles_per_seq, 0, 0)),
                  pl.BlockSpec((1, d), lambda i, j: (0, 0)),
                  pl.BlockSpec((d, tn), lambda i, j: (0, j)),
                  pl.BlockSpec((1, 1, tn), lambda i, j: (jnp.minimum(j, n_qk_tiles - 1), 0, 0)),
                  pl.BlockSpec(bd.shape, lambda i, j: (0, 0))],
        out_specs=[pl.BlockSpec((tm, tn), lambda i, j: (i, jnp.minimum(j, n_tiles - 2))),
                   pl.BlockSpec((tm, tn), lambda i, j: (i, 0))],
        out_shape=[jax.ShapeDtypeStruct((t, n - tn), BF16), jax.ShapeDtypeStruct((t, tn), F32)],
        scratch_shapes=[pltpu.VMEM((tm, d), BF16)],
        compiler_params=_cparams("arbitrary", "arbitrary"),
        name="in_proj",
    )(x2d, mod, n1w, w_packed, qkw, bd)


def _sb_kernel(q_ref, k_ref, v_ref, tri_ref, o_ref, acc_ref, r_ref, *, tq):
    i = pl.program_id(2)
    half = tri_ref.shape[0]
    lane = lax.broadcasted_iota(jnp.int32, (1, LANES), 1)
    first_half = lane < HEAD_DIM
    q = q_ref[...].astype(F32)
    q_heads = (jnp.where(first_half, q, 0.0).astype(BF16), jnp.where(first_half, 0.0, q).astype(BF16))
    acc_ref[...] = jnp.zeros_like(acc_ref)
    r_ref[...] = jnp.zeros_like(r_ref)

    def run(blocks, masked):
        items = [(j, hd) for j in blocks for hd in range(2)]
        st = [dict() for _ in items]
        if masked:
            causal = (lax.broadcasted_iota(jnp.int32, (tq, tq), 1) < lax.broadcasted_iota(jnp.int32, (tq, tq), 0))

        def rows_of(n):
            return pl.ds(pl.multiple_of(items[n][0] * tq, tq), tq)

        def logits(n):
            kb = k_ref[rows_of(n), :].astype(BF16)
            st[n]["y"] = lax.dot_general(q_heads[items[n][1]], kb, NT_DIMS, preferred_element_type=F32)

        def softplus(n):
            y = st[n]["y"]
            neg_abs = lax.bitcast_convert_type(lax.bitcast_convert_type(y, jnp.uint32) | jnp.uint32(0x80000000), F32)
            sp = jnp.maximum(y, 0.0) + jnp.log(1.0 + jnp.exp2(neg_abs)) * LOG2E
            if masked:
                sp = jnp.where(causal, sp, 0.0)
            st[n]["log_beta"] = y - sp
            st[n]["spb"] = sp.astype(BF16)
            st[n]["sp_first"] = (sp[:, 0:1], sp[:, half:half + 1])

        def later_sums(n):
            spb = st[n]["spb"]
            first_lo, first_hi = st[n]["sp_first"]
            later_hi = jnp.dot(spb[:, half:], tri_ref[...], preferred_element_type=F32)
            total_hi = later_hi[:, 0:1] + first_hi
            later_lo = jnp.dot(spb[:, :half], tri_ref[...], preferred_element_type=F32) + total_hi
            st[n]["later"] = jnp.concatenate([later_lo, later_hi], axis=1)
            st[n]["total"] = later_lo[:, 0:1] + first_lo

        def weights(n):
            a = jnp.exp2(st[n]["log_beta"] - st[n]["later"])
            if masked:
                a = jnp.where(causal, a, 0.0)
            st[n]["a"] = a.astype(BF16)

        def accumulate(n):
            hd = items[n][1]
            pv = jnp.dot(st[n]["a"], v_ref[rows_of(n), :].astype(BF16), preferred_element_type=F32)
            r = r_ref[hd]
            acc_ref[hd] += jnp.exp2(-r) * pv
            r_ref[hd] = r + st[n]["total"]

        stages = (logits, softplus, later_sums, weights, accumulate)
        elementwise = (softplus, weights)
        for wave in range(len(items) + len(stages) - 1):
            todo = [(stages[wave - n], n) for n in range(len(items)) if 0 <= wave - n < len(stages)]
            for stage, n in [x for x in todo if x[0] in elementwise] + [x for x in todo if x[0] not in elementwise]:
                stage(n)

    run([i], True)

    per_trip = 4

    def body(it, carry):
        j = i - 1 - per_trip * it
        run([j - d for d in range(per_trip)], False)
        return carry

    lax.fori_loop(0, i // per_trip, body, 0)

    def leftover(it, carry):
        run([i % per_trip - 1 - it], False)
        return carry

    lax.fori_loop(0, i % per_trip, leftover, 0)

    o_ref[...] = jnp.where(first_half, acc_ref[0], acc_ref[1])


def _sb_call(proj, batch, seq_len, n_pairs, tq=512):
    t = proj.shape[0]
    nq = seq_len // tq
    half = tq // 2
    kk = jnp.arange(half)
    later = (kk[:, None] > kk[None, :]).astype(BF16)
    return pl.pallas_call(
        functools.partial(_sb_kernel, tq=tq),
        grid=(batch, n_pairs, nq),
        in_specs=[pl.BlockSpec((tq, LANES), lambda b, p, i: (b * nq + i, p)),
                  pl.BlockSpec((seq_len, LANES), lambda b, p, i: (b, n_pairs + p)),
                  pl.BlockSpec((seq_len, LANES), lambda b, p, i: (b, 2 * n_pairs + p)),
                  pl.BlockSpec((half, half), lambda b, p, i: (0, 0))],
        out_specs=pl.BlockSpec((tq, LANES), lambda b, p, i: (b * nq + i, p)),
        out_shape=jax.ShapeDtypeStruct((t, n_pairs * LANES), F32),
        scratch_shapes=[pltpu.VMEM((2, tq, LANES), F32), pltpu.VMEM((2, tq, 1), F32)],
        compiler_params=_cparams("arbitrary", "arbitrary", "arbitrary"),
        name="sb_attention",
    )(proj, proj, proj, later)


def _gdn_prep_kernel(q_ref, k_ref, v_ref, qt_ref, kt_ref, vt_ref, ab_ref, cw_ref, alog_ref, dtb_ref,
                     ea_ref, eb_ref, lc_ref, jc_ref, bd_ref,
                     qn_o, qd_o, kn_o, kd_o, kb_o, kbg_o, vb_o, gcol_o, grow_o, scr, *, tm, tiles_per_seq):
    i = pl.program_id(0)
    keep_tail = (i % tiles_per_seq != 0).astype(F32)
    w = cw_ref[...]
    width = q_ref.shape[1]

    halo = qt_ref.shape[0]

    def conv_silu(cur_ref, tail_ref, col0):
        scr[0:halo, :] = tail_ref[...].astype(F32) * keep_tail
        scr[halo:, :] = cur_ref[...].astype(F32)
        y = jnp.zeros((tm, width), F32)
        for tap in range(GDN_CONV):
            off = halo - (GDN_CONV - 1) + tap
            y = y + scr[off:off + tm, :] * w[tap:tap + 1, col0:col0 + width]
        return _silu(y)

    bd = bd_ref[...]
    cq = conv_silu(q_ref, qt_ref, 0)
    qn = cq * lax.rsqrt(_group_sum_sq(cq, bd) + EPS) * (HEAD_DIM ** -0.5)
    ck = conv_silu(k_ref, kt_ref, width)
    kn = ck * lax.rsqrt(_group_sum_sq(ck, bd) + EPS)
    cv = conv_silu(v_ref, vt_ref, 2 * width)

    ab = ab_ref[...]
    g = -jnp.exp(alog_ref[...]) * _softplus(ab + dtb_ref[...])
    beta = _sigmoid(ab)
    g_parts = _bf16_pieces(g)
    lc, jc, ea = lc_ref[...], jc_ref[...], ea_ref[...]
    g_cum = sum(jnp.dot(lc, p, preferred_element_type=F32) for p in g_parts)
    g_tot = sum(jnp.dot(jc, p, preferred_element_type=F32) for p in g_parts)
    gx = sum(jnp.dot(p, ea, preferred_element_type=F32) for p in _bf16_pieces(g_cum))
    glx = sum(jnp.dot(p, ea, preferred_element_type=F32) for p in _bf16_pieces(g_tot))
    bx = sum(jnp.dot(p, eb_ref[...], preferred_element_type=F32) for p in _bf16_pieces(beta))

    e_g = jnp.exp(gx)
    kb = kn * bx
    qn_o[...] = qn
    qd_o[...] = qn * e_g
    kn_o[...] = kn
    kd_o[...] = kn * jnp.exp(glx - gx)
    kb_o[...] = kb
    kbg_o[...] = kb * e_g
    vb_o[...] = cv * bx
    g_heads = g_cum[:, :LANES]
    gcol_o[...] = g_heads
    grow_o[...] = g_heads.T[0:8, :]


def _gdn_prep_call(proj, ab_logits, conv_w, a_log, dt_bias, seq_len, n_heads, col_qkv, chunk, tm=512):
    t = proj.shape[0]
    assert n_heads <= 8 and tm % chunk == 0
    width = n_heads * HEAD_DIM
    abw = 2 * LANES
    tiles_per_seq = seq_len // tm
    halo = 16
    sub = tm // halo
    alog_pad = jnp.zeros((1, abw), F32).at[0, :n_heads].set(a_log)
    dtb_pad = jnp.zeros((1, abw), F32).at[0, :n_heads].set(dt_bias)
    head_of_lane = jnp.arange(width) // HEAD_DIM
    rows = jnp.arange(abw)
    ea = (rows[:, None] == head_of_lane[None, :]).astype(BF16)
    eb = (rows[:, None] == head_of_lane[None, :] + n_heads).astype(BF16)
    tok = jnp.arange(tm)
    same_chunk = (tok[:, None] // chunk) == (tok[None, :] // chunk)
    lc = (same_chunk & (tok[:, None] >= tok[None, :])).astype(BF16)
    jc = same_chunk.astype(BF16)
    bd = _block_diag_ones(2 * LANES, HEAD_DIM, 1.0, BF16)

    cur = lambda c: pl.BlockSpec((tm, width), lambda i: (i, c))
    tail = lambda c: pl.BlockSpec((halo, width), lambda i: (jnp.maximum(i * sub - 1, 0), c))
    full = lambda a: pl.BlockSpec(a.shape, lambda i: (0,) * a.ndim)
    cq, ck, cv = col_qkv
    out_spec = pl.BlockSpec((tm, width), lambda i: (i, 0))
    out_shape = jax.ShapeDtypeStruct((t, width), F32)
    return pl.pallas_call(
        functools.partial(_gdn_prep_kernel, tm=tm, tiles_per_seq=tiles_per_seq),
        grid=(t // tm,),
        in_specs=[cur(cq), cur(ck), cur(cv), tail(cq), tail(ck), tail(cv),
                  pl.BlockSpec((tm, abw), lambda i: (i, 0)),
                  full(conv_w), full(alog_pad), full(dtb_pad), full(ea), full(eb), full(lc), full(jc), full(bd)],
        out_specs=[out_spec] * 7 + [pl.BlockSpec((tm, LANES), lambda i: (i, 0)), pl.BlockSpec((8, tm), lambda i: (0, i))],
        out_shape=[out_shape] * 7 + [jax.ShapeDtypeStruct((t, LANES), F32), jax.ShapeDtypeStruct((8, t), F32)],
        scratch_shapes=[pltpu.VMEM((tm + halo, width), F32)],
        compiler_params=_cparams("arbitrary"),
        name="gdn_prep",
    )(proj, proj, proj, proj, proj, proj, ab_logits, conv_w, alog_pad, dtb_pad, ea, eb, lc, jc, bd)


def _gdn_core_kernel(qn_ref, qd_ref, kn_ref, kd_ref, kb_ref, kbg_ref, vb_ref, gcol_ref, grow_ref,
                     o_ref, s_scr, *, ts, n_heads, c_len):
    @pl.when(pl.program_id(1) == 0)
    def _():
        s_scr[...] = jnp.zeros_like(s_scr)

    row = lax.broadcasted_iota(jnp.int32, (c_len, c_len), 0)
    col = lax.broadcasted_iota(jnp.int32, (c_len, c_len), 1)
    strict = row > col
    incl = row >= col
    eye_state = (lax.broadcasted_iota(jnp.int32, (HEAD_DIM, HEAD_DIM), 0)
                 == lax.broadcasted_iota(jnp.int32, (HEAD_DIM, HEAD_DIM), 1))
    n_levels = (c_len - 1).bit_length()
    n_chunks = ts // c_len
    problems = [(c, h) for c in range(n_chunks) for h in range(n_heads)]

    def bf(x):
        return x.astype(BF16)

    def mm(a, b):
        return jnp.dot(a, b, preferred_element_type=F32)

    def head_tile(ref, c, h):
        pair, half = divmod(h, 2)
        tile = ref[c * c_len:(c + 1) * c_len, pair * LANES:(pair + 1) * LANES]
        return tile[:, half * HEAD_DIM:(half + 1) * HEAD_DIM]

    sibling = [((row >> k) ^ (col >> k)) == 1 for k in range(n_levels)]

    xs, ms, rs, qks = [], [], [], []
    for c, h in problems:
        g_col = gcol_ref[c * c_len:(c + 1) * c_len, h:h + 1]
        g_row = grow_ref[h:h + 1, c * c_len:(c + 1) * c_len]
        decay = jnp.where(incl, jnp.exp(g_col - g_row), 0.0)
        kn_b = bf(head_tile(kn_ref, c, h))
        kk = lax.dot_general(bf(head_tile(kb_ref, c, h)), kn_b, NT_DIMS, preferred_element_type=F32)
        x = jnp.where(strict, -(kk * decay), 0.0)
        xs.append(x)
        ms.append(jnp.where(row == col, 1.0, jnp.where(sibling[0], x, 0.0)))
        qks.append(bf(lax.dot_general(bf(head_tile(qn_ref, c, h)), kn_b, NT_DIMS, preferred_element_type=F32) * decay))
        rs.append(bf(jnp.concatenate([head_tile(vb_ref, c, h), head_tile(kbg_ref, c, h)], axis=1)))

    for k in range(1, n_levels):
        for idx in range(len(problems)):
            m_b = bf(ms[idx])
            left = mm(m_b, bf(jnp.where(sibling[k], xs[idx], 0.0)))
            ms[idx] = ms[idx] + mm(bf(left), m_b)

    q_eff, o_zero, p_mat, b_mat = {}, {}, {}, {}
    for idx, (c, h) in enumerate(problems):
        sol = bf(mm(bf(ms[idx]), rs[idx]))
        m1 = mm(qks[idx], sol)
        q_eff[c, h] = bf(head_tile(qd_ref, c, h) - m1[:, HEAD_DIM:])
        o_zero[c, h] = m1[:, :HEAD_DIM]
        m2 = lax.dot_general(bf(head_tile(kd_ref, c, h)), sol, TN_DIMS, preferred_element_type=F32)
        chunk_decay = jnp.exp(gcol_ref[(c + 1) * c_len - 1:(c + 1) * c_len, h:h + 1])
        p_mat[c, h] = bf(jnp.where(eye_state, chunk_decay, 0.0) - m2[:, HEAD_DIM:])
        b_mat[c, h] = m2[:, :HEAD_DIM]

    states = [s_scr[h] for h in range(n_heads)]
    for c in range(n_chunks):
        outs = []
        for h in range(n_heads):
            s_b = bf(states[h])
            outs.append(mm(q_eff[c, h], s_b) + o_zero[c, h])
            states[h] = mm(p_mat[c, h], s_b) + b_mat[c, h]
        o_ref[c * c_len:(c + 1) * c_len, :] = jnp.concatenate(outs, axis=1)
    for h in range(n_heads):
        s_scr[h] = states[h]


def _gdn_core_call(prep, batch, seq_len, n_heads, c_len, ts=512):
    t, width = prep[0].shape
    n_seq_tiles = seq_len // ts
    spec = pl.BlockSpec((ts, width), lambda b, s: (b * n_seq_tiles + s, 0))
    return pl.pallas_call(
        functools.partial(_gdn_core_kernel, ts=ts, n_heads=n_heads, c_len=c_len),
        grid=(batch, n_seq_tiles),
        in_specs=[spec] * 7 + [pl.BlockSpec((ts, LANES), lambda b, s: (b * n_seq_tiles + s, 0)),
                               pl.BlockSpec((8, ts), lambda b, s: (0, b * n_seq_tiles + s))],
        out_specs=spec,
        out_shape=jax.ShapeDtypeStruct((t, width), F32),
        scratch_shapes=[pltpu.VMEM((n_heads, HEAD_DIM, HEAD_DIM), F32)],
        compiler_params=_cparams("arbitrary", "arbitrary"),
        name="gdn_core",
    )(*prep)


def _merge_kernel(x_ref, ysb_ref, og_ref, z_ref, g0_ref, g1_ref, mod_ref, onw_ref, n2_ref, bd_ref,
                  wsb_ref, wgdn_ref, wo_ref, wq_ref, x1_o, h2t_o, pq_o):
    og = og_ref[...]
    ygdn = og * lax.rsqrt(_group_sum_sq(og, bd_ref[...]) + EPS) * onw_ref[...] * _silu(z_ref[...].astype(F32))
    m = (_sigmoid(g0_ref[...].astype(F32)) * jnp.dot(ysb_ref[...].astype(BF16), wsb_ref[...],
                                                      preferred_element_type=F32)
         + _sigmoid(g1_ref[...].astype(F32)) * jnp.dot(ygdn.astype(BF16), wgdn_ref[...], preferred_element_type=F32))
    gate1 = mod_ref[0, 2:3, :]
    shift2 = mod_ref[0, 3:4, :]
    scale2 = mod_ref[0, 4:5, :]
    x1 = x_ref[...] + gate1 * jnp.dot(m.astype(BF16), wo_ref[...], preferred_element_type=F32)
    x1_o[...] = x1
    ms2 = jnp.mean(x1 * x1, axis=-1, keepdims=True)
    h2 = x1 * lax.rsqrt(ms2 + EPS) * n2_ref[...] * (1.0 + scale2) + shift2
    h2t_o[...] = h2.T.astype(BF16)
    pq_o[...] = jnp.dot(h2.astype(BF16), wq_ref[...], preferred_element_type=F32)


def _merge_call(x2d, ysb, ogdn, proj, mod, onw, n2w, wsb, wgdn, wo, wq, seq_len, col_g0, col_g1, col_z, tm=512):
    t, d = x2d.shape
    width = ysb.shape[1]
    nq = wq.shape[1]
    bd = _block_diag_ones(2 * LANES, HEAD_DIM, 1.0 / HEAD_DIM, BF16)
    tiles_per_seq = seq_len // tm
    full = lambda a: pl.BlockSpec(a.shape, lambda i: (0,) * a.ndim)
    once = lambda a: pl.BlockSpec(a.shape, lambda i: (0,) * a.ndim, pipeline_mode=pl.Buffered(1))
    return pl.pallas_call(
        _merge_kernel,
        grid=(t // tm,),
        in_specs=[pl.BlockSpec((tm, d), lambda i: (i, 0)),
                  pl.BlockSpec((tm, width), lambda i: (i, 0)),
                  pl.BlockSpec((tm, width), lambda i: (i, 0)),
                  pl.BlockSpec((tm, width), lambda i: (i, col_z)),
                  pl.BlockSpec((tm, d), lambda i: (i, col_g0)),
                  pl.BlockSpec((tm, d), lambda i: (i, col_g1)),
                  pl.BlockSpec((1, 6, d), lambda i: (i // tiles_per_seq, 0, 0)),
                  full(onw), full(n2w), full(bd), once(wsb), once(wgdn), once(wo), once(wq)],
        out_specs=[pl.BlockSpec((tm, d), lambda i: (i, 0)),
                   pl.BlockSpec((d, tm), lambda i: (0, i)),
                   pl.BlockSpec((tm, nq), lambda i: (i, 0))],
        out_shape=[jax.ShapeDtypeStruct((t, d), F32),
                   jax.ShapeDtypeStruct((d, t), BF16),
                   jax.ShapeDtypeStruct((t, nq), F32)],
        compiler_params=_cparams("arbitrary"),
        name="merge_proj",
    )(x2d, ysb, ogdn, proj, proj, proj, mod, onw, n2w, bd, wsb, wgdn, wo, wq)


def _extract_topk(s, k, break_ties):
    n = s.shape[0]
    if not break_ties:
        lowest_bits = -8388609
        vals = []
        for r in range(k):
            m = jnp.max(s, axis=0, keepdims=True)
            marker = lax.bitcast_convert_type(jnp.int32(lowest_bits - r), F32)
            s = jnp.where(s == m, marker, s)
            vals.append(m)
        took = jnp.int32(lowest_bits) - lax.bitcast_convert_type(s, jnp.int32)
        rank = jnp.where((took >= 0) & (took < k), took.astype(F32), NOT_RANKED)
        return vals, rank
    iota = lax.broadcasted_iota(jnp.int32, s.shape, 0).astype(F32)
    rank = jnp.full(s.shape, NOT_RANKED, F32)
    vals = []
    for r in range(k):
        m = jnp.max(s, axis=0, keepdims=True)
        hit = iota == jnp.min(jnp.where(s == m, iota, float(n)), axis=0, keepdims=True)
        rank = jnp.where(hit, float(r), rank)
        s = jnp.where(hit, -jnp.inf, s)
        vals.append(m)
    return vals, rank


def _candidate_tables(k):
    pairs = [(a, b) for a in range(k) for b in range(k) if (a + 1) * (b + 1) <= k]
    n_pad = -(-len(pairs) // 8) * 8
    sel_a = jnp.zeros((n_pad, k), F32).at[jnp.arange(len(pairs)), jnp.array([a for a, _ in pairs])].set(1.0)
    sel_b = jnp.zeros((n_pad, k), F32).at[jnp.arange(len(pairs)), jnp.array([b for _, b in pairs])].set(1.0)
    return sel_a, sel_b, len(pairs)


def _bf16_pair_words(x):
    bits = lax.bitcast_convert_type(x.astype(BF16).astype(F32), jnp.uint32)
    return bits | (bits >> 16)


def _route_kernel(pq_ref, keys_ref, sela_ref, selb_ref, rank2_o, e2_o, cnt1_o, w1_o, *, n_cand):
    k = PEER_TOPK
    hp, _, _, half = keys_ref.shape
    tt = pq_ref.shape[0]
    part_scores = [[lax.dot_general(keys_ref[hh, part], pq_ref[:, (2 * hh + part) * half:(2 * hh + part + 1) * half],
                                    NT_DIMS, preferred_element_type=F32, precision=HI) for hh in range(hp)]
                   for part in range(2)]
    sel_a = sela_ref[...]

    def route(break_ties, heads):
        w = len(heads) * tt
        iota_k = lax.broadcasted_iota(jnp.int32, (k, w), 0).astype(F32)
        s_all = jnp.concatenate([part_scores[part][hh] for part in range(2) for hh in heads], axis=1)
        scores = (s_all[:, :w], s_all[:, w:])
        vals, rank = _extract_topk(s_all, k, break_ties)
        top = jnp.concatenate(vals, axis=0)
        top1, top2 = top[:, :w], top[:, w:]
        v1, v2 = [v[:, :w] for v in vals], [v[:, w:] for v in vals]
        rank1, rank2 = rank[:, :w], rank[:, w:]
        cand = (jnp.dot(sel_a, top1, preferred_element_type=F32, precision=HI)
                + jnp.dot(selb_ref[...], top2, preferred_element_type=F32, precision=HI))
        cand_row = lax.broadcasted_iota(jnp.int32, cand.shape, 0)
        _, cand_rank = _extract_topk(jnp.where(cand_row < n_cand, cand, -jnp.inf), k, break_ties)
        chosen = (cand_rank < float(k)).astype(BF16)
        count = lax.dot_general(sel_a.astype(BF16), chosen, TN_DIMS, preferred_element_type=F32)
        e1 = jnp.exp(top1 - v1[0])
        e2 = jnp.exp(top2 - v2[0])
        z = jnp.zeros_like(v1[0])
        for a in range(k):
            z = z + e1[a:a + 1] * jnp.sum(jnp.where(iota_k < count[a:a + 1], e2, 0.0), axis=0, keepdims=True)
        inv_z = 1.0 / z
        cnt1 = jnp.zeros_like(rank1)
        for a in range(k):
            cnt1 = jnp.where(rank1 == float(a), count[a:a + 1], cnt1)
        e2_all = jnp.exp(scores[1] - v2[0]).astype(BF16)
        w1_all = jnp.where(rank1 < float(k), jnp.exp(scores[0] - v1[0]) * inv_z, 0.0)
        cnt1_words = _bf16_pair_words(cnt1)
        w1_words = _bf16_pair_words(w1_all)
        n_ranked = jnp.sum((rank < float(k)).astype(F32), axis=0, keepdims=True)
        n_chosen = jnp.sum((cand_rank < float(k)).astype(F32), axis=0, keepdims=True)
        off_by = jnp.maximum(jnp.maximum(jnp.abs(n_ranked[:, :w] - k), jnp.abs(n_ranked[:, w:] - k)),
                             jnp.abs(n_chosen - k))
        excess = []
        for pos, hh in enumerate(heads):
            lanes = slice(pos * tt, (pos + 1) * tt)
            rank2_o[hh] = rank2[:, lanes].astype(BF16).reshape(rank2_o.shape[1:])
            e2_o[hh] = e2_all[:, lanes].reshape(e2_o.shape[1:])
            cnt1_o[hh] = cnt1_words[:, lanes]
            w1_o[hh] = w1_words[:, lanes]
            excess.append(jnp.max(off_by[:, lanes]))
        return excess

    excess = route(False, list(range(hp)))
    for hh in range(hp):
        @pl.when(excess[hh] > 0.0)
        def _(hh=hh):
            route(True, [hh])


def _route_call(pq, sub_keys, tt=256, hp=4):
    t = pq.shape[0]
    n_heads, _, n_keys, half = sub_keys.shape
    sel_a, sel_b, n_cand = _candidate_tables(PEER_TOPK)
    out_spec = pl.BlockSpec((hp, n_keys, tt), lambda i, h: (h, 0, i))
    shape = lambda dt: jax.ShapeDtypeStruct((n_heads, n_keys, t), dt)
    tiled_spec = pl.BlockSpec((hp, n_keys // BF16_TILE_ROWS, BF16_TILE_ROWS, tt), lambda i, h: (h, 0, 0, i))
    tiled_shape = jax.ShapeDtypeStruct((n_heads, n_keys // BF16_TILE_ROWS, BF16_TILE_ROWS, t), BF16)
    return pl.pallas_call(
        functools.partial(_route_kernel, n_cand=n_cand),
        grid=(t // tt, n_heads // hp),
        in_specs=[pl.BlockSpec((tt, hp * 2 * half), lambda i, h: (i, h)),
                  pl.BlockSpec((hp, 2, n_keys, half), lambda i, h: (h, 0, 0, 0)),
                  pl.BlockSpec(sel_a.shape, lambda i, h: (0, 0)),
                  pl.BlockSpec(sel_b.shape, lambda i, h: (0, 0))],
        out_specs=[tiled_spec, tiled_spec, out_spec, out_spec],
        out_shape=[tiled_shape, tiled_shape, shape(jnp.uint32), shape(jnp.uint32)],
        compiler_params=_cparams("arbitrary", "arbitrary"),
        name="peer_route",
    )(pq, sub_keys, sel_a, sel_b)


def _peer_kernel(h2t_ref, u_ref, vt_ref, rank2_ref, e2_ref, cnt1_ref, w1_ref, x1_ref, mod_ref,
                 o_ref, act_even, act_odd, acc_scr, *, n_heads, n_keys, ec, group, n_chunks, n_items):
    s = pl.program_id(0)
    chunk = jnp.clip(s - 1, 0, n_items - 1) % n_chunks

    @pl.when(s == 0)
    def _():
        act_odd[...] = jnp.zeros_like(act_odd)

    @pl.when((chunk == 0) | (s == 0))
    def _():
        acc_scr[...] = jnp.zeros_like(acc_scr)

    tt = h2t_ref.shape[1]
    zero = jnp.zeros((), BF16)
    n_groups = ec // group
    subs_per_group = group // n_keys

    def coef_of(grp, act):
        coefs = []
        for s_loc in range(subs_per_group):
            sub = grp * subs_per_group + s_loc
            gate = None
            for h in range(n_heads):
                cnt = pltpu.bitcast(jnp.broadcast_to(cnt1_ref[h, sub:sub + 1, :], (8, tt)), BF16)
                w1 = pltpu.bitcast(jnp.broadcast_to(w1_ref[h, sub:sub + 1, :], (8, tt)), BF16)
                term = jnp.where(rank2_ref[h] < cnt[None], e2_ref[h], zero) * w1[None]
                gate = term if gate is None else gate + term
            a = act[s_loc * n_keys:(s_loc + 1) * n_keys, :].astype(BF16)
            gelu = (0.5 * a) * (1.0 + lax.erf(a * (2.0 ** -0.5)))
            coefs.append(gate.reshape(n_keys, tt) * gelu)
        return jnp.concatenate(coefs, axis=0)

    def stages(act_w, act_r):
        total = acc_scr[...]
        half_t = tt // 2
        new_parts = []
        for grp in range(n_groups):
            rows = slice(grp * group, (grp + 1) * group)
            coef = coef_of(grp, act_r[rows, :])
            if grp < 2:
                cols = slice(grp * half_t, (grp + 1) * half_t)
                new_parts.append(jnp.dot(u_ref[...], h2t_ref[:, cols], preferred_element_type=F32))
            total = total + jnp.dot(vt_ref[:, rows], coef, preferred_element_type=F32)
        acc_scr[...] = total
        act_w[...] = jnp.concatenate(new_parts, axis=1)

    @pl.when(s % 2 == 0)
    def _():
        stages(act_even, act_odd)

    @pl.when(s % 2 == 1)
    def _():
        stages(act_odd, act_even)

    @pl.when((chunk == n_chunks - 1) & (s >= 1))
    def _():
        gate2 = mod_ref[0, 5:6, :]
        o_ref[...] = x1_ref[...] + gate2 * acc_scr[...].T


def _peer_call(h2t, u_b, vt_b, rank2, e2, cnt1, w1, x1, mod, seq_len, tt=512, ec=2048, group=1024):
    d, t = h2t.shape
    n_exp = u_b.shape[0]
    n_heads, n_key_tiles, tile_rows, _ = rank2.shape
    n_keys = n_key_tiles * tile_rows
    tiles_per_seq = seq_len // tt
    n_chunks = n_exp // ec
    n_items = (t // tt) * n_chunks
    first = lambda s: jnp.minimum(s, n_items - 1)
    second = lambda s: jnp.clip(s - 1, 0, n_items - 1)
    route_spec = pl.BlockSpec((n_heads, n_key_tiles, tile_rows, tt), lambda s: (0, 0, 0, second(s) // n_chunks))
    row_spec = pl.BlockSpec((n_heads, ec // n_keys, tt), lambda s: (0, second(s) % n_chunks, second(s) // n_chunks))
    return pl.pallas_call(
        functools.partial(_peer_kernel, n_heads=n_heads, n_keys=n_keys, ec=ec, group=group,
                          n_chunks=n_chunks, n_items=n_items),
        grid=(n_items + 1,),
        in_specs=[pl.BlockSpec((d, tt), lambda s: (0, first(s) // n_chunks)),
                  pl.BlockSpec((ec, d), lambda s: (first(s) % n_chunks, 0)),
                  pl.BlockSpec((d, ec), lambda s: (0, second(s) % n_chunks)),
                  route_spec, route_spec, row_spec, row_spec,
                  pl.BlockSpec((tt, d), lambda s: (second(s) // n_chunks, 0)),
                  pl.BlockSpec((1, 6, d), lambda s: (second(s) // n_chunks // tiles_per_seq, 0, 0))],
        out_specs=pl.BlockSpec((tt, d), lambda s: (second(s) // n_chunks, 0)),
        out_shape=jax.ShapeDtypeStruct((t, d), F32),
        scratch_shapes=[pltpu.VMEM((ec, tt), F32), pltpu.VMEM((ec, tt), F32), pltpu.VMEM((d, tt), F32)],
        compiler_params=_cparams("arbitrary"),
        name="peer_experts",
    )(h2t, u_b, vt_b, rank2, e2, cnt1, w1, x1, mod)


def _pack_in_proj(w_in, sb_w, gdn_qk_w, gdn_v_w, n_gdn_heads, d_model, tn):
    o_sbq, o_sbk, o_sbv = 0, sb_w, 2 * sb_w
    o_gdn = 3 * sb_w
    conv_w = 2 * gdn_qk_w + gdn_v_w
    o_a = o_gdn + conv_w
    o_b = o_a + n_gdn_heads
    o_z = o_b + n_gdn_heads
    o_gate = o_z + gdn_v_w
    pad = (-w_in.shape[1]) % tn
    packed = jnp.concatenate([
        w_in[:, o_sbq:o_a],
        w_in[:, o_gate:o_gate + 2 * d_model],
        w_in[:, o_z:o_z + gdn_v_w],
        w_in[:, o_a:o_z],
        jnp.zeros((w_in.shape[0], pad), w_in.dtype)], axis=1).astype(BF16)
    return packed


def _block(x2d, c, w_ada, b_ada, norm1_w, w_in, sb_q_norm_w, sb_k_norm_w, gdn_conv_w, gdn_A_log,
           gdn_dt_bias, gdn_o_norm_w, w_proj_sb, w_proj_gdn, w_o, norm2_w, peer_w_q, peer_sub_keys,
           peer_u, peer_v, batch, seq_len):
    t, d = x2d.shape
    sb_w = w_proj_sb.shape[0]
    gdn_v_w = w_proj_gdn.shape[0]
    gdn_qk_w = (gdn_conv_w.shape[1] - gdn_v_w) // 2
    n_sb_heads = sb_w // HEAD_DIM
    n_gdn_heads = gdn_v_w // HEAD_DIM
    assert gdn_qk_w == gdn_v_w == sb_w and d % (4 * LANES) == 0

    c_pad = jnp.zeros((8, d), F32).at[:batch].set(c)
    mod = _ada_call(c_pad, w_ada, b_ada)[:batch].reshape(batch, 6, d)

    tn = sb_w
    w_packed = _pack_in_proj(w_in, sb_w, gdn_qk_w, gdn_v_w, n_gdn_heads, d, tn)
    heads_per_tile = tn // HEAD_DIM
    q_tiles = sb_w // tn
    qkw = jnp.concatenate([jnp.tile(sb_q_norm_w * (HEAD_DIM ** -0.5 * LOG2E), (q_tiles, heads_per_tile)),
                           jnp.tile(sb_k_norm_w, (q_tiles, heads_per_tile))], axis=0).reshape(2 * q_tiles, 1, tn)
    proj, ab_logits = _inproj_call(x2d, mod, norm1_w.reshape(1, d), w_packed, qkw, seq_len, tn=tn)

    ysb = _sb_call(proj, batch, seq_len, n_sb_heads // 2)

    col_gdn = 3 * sb_w // gdn_v_w
    col_gate = (3 * sb_w + 3 * gdn_v_w) // d
    col_z = (3 * sb_w + 3 * gdn_v_w + 2 * d) // gdn_v_w
    prep = _gdn_prep_call(proj, ab_logits, gdn_conv_w, gdn_A_log, gdn_dt_bias, seq_len, n_gdn_heads,
                          (col_gdn, col_gdn + 1, col_gdn + 2), GDN_BLOCK)
    ogdn = _gdn_core_call(prep, batch, seq_len, n_gdn_heads, GDN_BLOCK)

    x1, h2, pq = _merge_call(
        x2d, ysb, ogdn, proj, mod, jnp.tile(gdn_o_norm_w, n_gdn_heads).reshape(1, gdn_v_w), norm2_w.reshape(1, d),
        w_proj_sb.astype(BF16), w_proj_gdn.astype(BF16), w_o.astype(BF16), peer_w_q.astype(BF16),
        seq_len, col_gate, col_gate + 1, col_z)

    rank2, e2, cnt1, w1 = _route_call(pq, peer_sub_keys)
    return _peer_call(h2, peer_u.astype(BF16), peer_v.T.astype(BF16), rank2, e2, cnt1, w1, x1, mod, seq_len)


def kernel(x, c, w_ada, b_ada, norm1_w, w_in, sb_q_norm_w, sb_k_norm_w, gdn_conv_w, gdn_A_log, gdn_dt_bias,
           gdn_o_norm_w, w_proj_sb, w_proj_gdn, w_o, norm2_w, peer_w_q, peer_sub_keys, peer_u, peer_v):
    batch, seq_len, d = x.shape
    x2d = x.reshape(batch * seq_len, d)
    for l in range(w_ada.shape[0]):
        x2d = _block(x2d, c, w_ada[l], b_ada[l], norm1_w[l], w_in[l], sb_q_norm_w[l], sb_k_norm_w[l],
                     gdn_conv_w[l], gdn_A_log[l], gdn_dt_bias[l], gdn_o_norm_w[l], w_proj_sb[l], w_proj_gdn[l],
                     w_o[l], norm2_w[l], peer_w_q[l], peer_sub_keys[l], peer_u[l], peer_v[l], batch, seq_len)
    return x2d.reshape(batch, seq_len, d)
```

```python
import functools

import jax
import jax.numpy as jnp
from jax import lax
from jax.experimental import pallas as pl
from jax.experimental.pallas import tpu as pltpu

F32 = jnp.float32
BF16 = jnp.bfloat16
HI = lax.Precision.HIGHEST
EPS = 1e-6
LOG2E = 1.4426950408889634

LANES = 128
BF16_TILE_ROWS = 16
HEAD_DIM = 64
GDN_BLOCK = 128
GDN_CONV = 4
PEER_TOPK = 16
NOT_RANKED = 99.0
VMEM_LIMIT = 56 * 1024 * 1024

NT_DIMS = (((1,), (1,)), ((), ()))
TN_DIMS = (((0,), (0,)), ((), ()))


def _cparams(*sem, flags=None):
    return pltpu.CompilerParams(dimension_semantics=sem, vmem_limit_bytes=VMEM_LIMIT, flags=flags)


def _sigmoid(x):
    return 1.0 / (1.0 + jnp.exp(-x))


def _silu(x):
    return x * _sigmoid(x)


def _softplus(x):
    return jnp.maximum(x, 0.0) + jnp.log(1.0 + jnp.exp(-jnp.abs(x)))


def _block_diag_ones(n, group, value=1.0, dtype=F32):
    r = jnp.arange(n) // group
    return jnp.where(r[:, None] == r[None, :], value, 0.0).astype(dtype)


def _bf16_pieces(x):
    hi = x.astype(BF16)
    rest = x - hi.astype(F32)
    mid = rest.astype(BF16)
    return hi, mid, (rest - mid.astype(F32)).astype(BF16)


def _group_sum_sq(x, bd):
    slab = bd.shape[0]
    outs = []
    for c0 in range(0, x.shape[1], slab):
        sq = x[:, c0:c0 + slab] * x[:, c0:c0 + slab]
        hi = sq.astype(BF16)
        lo = (sq - hi.astype(F32)).astype(BF16)
        outs.append(jnp.dot(hi, bd, preferred_element_type=F32) + jnp.dot(lo, bd, preferred_element_type=F32))
    return outs[0] if len(outs) == 1 else jnp.concatenate(outs, axis=1)


def _ada_kernel(c_ref, w_ref, b_ref, o_ref):
    c = c_ref[...]
    o_ref[...] = jnp.dot(_silu(c), w_ref[...], preferred_element_type=F32, precision=HI) + b_ref[...]


def _ada_call(c_pad, w_ada, b_ada):
    rows, d = c_pad.shape
    n = w_ada.shape[1]
    tn = 2048
    return pl.pallas_call(
        _ada_kernel,
        grid=(n // tn,),
        in_specs=[pl.BlockSpec((rows, d), lambda j: (0, 0)),
                  pl.BlockSpec((d, tn), lambda j: (0, j)),
                  pl.BlockSpec((1, tn), lambda j: (0, j))],
        out_specs=pl.BlockSpec((rows, tn), lambda j: (0, j)),
        out_shape=jax.ShapeDtypeStruct((rows, n), F32),
        compiler_params=_cparams("arbitrary"),
        name="ada_mod",
    )(c_pad, w_ada, b_ada.reshape(1, n))


def _inproj_kernel(x_ref, mod_ref, n1_ref, w_ref, qkw_ref, bd_ref, o_ref, tail_ref, h_scr, *, n_qk_tiles):
    j = pl.program_id(1)
    last = pl.num_programs(1) - 1

    @pl.when(j == 0)
    def _():
        x = x_ref[...]
        ms = jnp.mean(x * x, axis=-1, keepdims=True)
        y = x * lax.rsqrt(ms + EPS) * n1_ref[...]
        shift = mod_ref[0, 0:1, :]
        scale = mod_ref[0, 1:2, :]
        h_scr[...] = (y * (1.0 + scale) + shift).astype(BF16)

    acc = jnp.dot(h_scr[...], w_ref[...], preferred_element_type=F32)

    @pl.when(j < n_qk_tiles)
    def _():
        o_ref[...] = (acc * lax.rsqrt(_group_sum_sq(acc, bd_ref[...]) + EPS) * qkw_ref[0]).astype(BF16)

    @pl.when((j >= n_qk_tiles) & (j < last))
    def _():
        o_ref[...] = acc.astype(BF16)

    @pl.when(j == last)
    def _():
        tail_ref[...] = acc


def _inproj_call(x2d, mod, n1w, w_packed, qkw, seq_len, tm=2048, tn=512):
    t, d = x2d.shape
    n = w_packed.shape[1]
    n_tiles = n // tn
    n_qk_tiles = qkw.shape[0]
    assert seq_len % tm == 0, "a row tile must not straddle two sequences (per-sequence modulation)"
    bd = _block_diag_ones(2 * LANES, HEAD_DIM, 1.0 / HEAD_DIM, BF16)
    tiles_per_seq = seq_len // tm
    return pl.pallas_call(
        functools.partial(_inproj_kernel, n_qk_tiles=n_qk_tiles),
        grid=(t // tm, n // tn),
        in_specs=[pl.BlockSpec((tm, d), lambda i, j: (i, 0)),
                  pl.BlockSpec((1, 6, d), lambda i, j: (i // tiles_per_seq, 0, 0)),
                  pl.BlockSpec((1, d), lambda i, j: (0, 0)),
                  pl.BlockSpec((d, tn), lambda i, j: (0, j)),
                  pl.BlockSpec((1, 1, tn), lambda i, j: (jnp.minimum(j, n_qk_tiles - 1), 0, 0)),
                  pl.BlockSpec(bd.shape, lambda i, j: (0, 0))],
        out_specs=[pl.BlockSpec((tm, tn), lambda i, j: (i, jnp.minimum(j, n_tiles - 2))),
                   pl.BlockSpec((tm, tn), lambda i, j: (i, 0))],
        out_shape=[jax.ShapeDtypeStruct((t, n - tn), BF16), jax.ShapeDtypeStruct((t, tn), F32)],
        scratch_shapes=[pltpu.VMEM((tm, d), BF16)],
        compiler_params=_cparams("arbitrary", "arbitrary"),
        name="in_proj",
    )(x2d, mod, n1w, w_packed, qkw, bd)


def _sb_kernel(q_ref, k_ref, v_ref, tri_ref, o_ref, acc_ref, r_ref, *, tq):
    i = pl.program_id(2)
    half = tri_ref.shape[0]
    lane = lax.broadcasted_iota(jnp.int32, (1, LANES), 1)
    first_half = lane < HEAD_DIM
    q = q_ref[...].astype(F32)
    q_heads = (jnp.where(first_half, q, 0.0).astype(BF16), jnp.where(first_half, 0.0, q).astype(BF16))
    acc_ref[...] = jnp.zeros_like(acc_ref)
    r_ref[...] = jnp.zeros_like(r_ref)

    def run(blocks, masked):
        items = [(j, hd) for j in blocks for hd in range(2)]
        st = [dict() for _ in items]
        if masked:
            causal = (lax.broadcasted_iota(jnp.int32, (tq, tq), 1) < lax.broadcasted_iota(jnp.int32, (tq, tq), 0))

        def rows_of(n):
            return pl.ds(pl.multiple_of(items[n][0] * tq, tq), tq)

        def logits(n):
            kb = k_ref[rows_of(n), :].astype(BF16)
            st[n]["y"] = lax.dot_general(q_heads[items[n][1]], kb, NT_DIMS, preferred_element_type=F32)

        def softplus(n):
            y = st[n]["y"]
            neg_abs = lax.bitcast_convert_type(lax.bitcast_convert_type(y, jnp.uint32) | jnp.uint32(0x80000000), F32)
            sp = jnp.maximum(y, 0.0) + jnp.log(1.0 + jnp.exp2(neg_abs)) * LOG2E
            if masked:
                sp = jnp.where(causal, sp, 0.0)
            st[n]["log_beta"] = y - sp
            st[n]["spb"] = sp.astype(BF16)
            st[n]["sp_first"] = (sp[:, 0:1], sp[:, half:half + 1])

        def later_sums(n):
            spb = st[n]["spb"]
            first_lo, first_hi = st[n]["sp_first"]
            later_hi = jnp.dot(spb[:, half:], tri_ref[...], preferred_element_type=F32)
            total_hi = later_hi[:, 0:1] + first_hi
            later_lo = jnp.dot(spb[:, :half], tri_ref[...], preferred_element_type=F32) + total_hi
            st[n]["later"] = jnp.concatenate([later_lo, later_hi], axis=1)
            st[n]["total"] = later_lo[:, 0:1] + first_lo

        def weights(n):
            a = jnp.exp2(st[n]["log_beta"] - st[n]["later"])
            if masked:
                a = jnp.where(causal, a, 0.0)
            st[n]["a"] = a.astype(BF16)

        def accumulate(n):
            hd = items[n][1]
            pv = jnp.dot(st[n]["a"], v_ref[rows_of(n), :].astype(BF16), preferred_element_type=F32)
            r = r_ref[hd]
            acc_ref[hd] += jnp.exp2(-r) * pv
            r_ref[hd] = r + st[n]["total"]

        stages = (logits, softplus, later_sums, weights, accumulate)
        elementwise = (softplus, weights)
        for wave in range(len(items) + len(stages) - 1):
            todo = [(stages[wave - n], n) for n in range(len(items)) if 0 <= wave - n < len(stages)]
            for stage, n in [x for x in todo if x[0] in elementwise] + [x for x in todo if x[0] not in elementwise]:
                stage(n)

    run([i], True)

    per_trip = 4

    def body(it, carry):
        j = i - 1 - per_trip * it
        run([j - d for d in range(per_trip)], False)
        return carry

    lax.fori_loop(0, i // per_trip, body, 0)

    left = i % per_trip

    @pl.when(left >= 2)
    def _():
        run([left - 1, left - 2], False)

    @pl.when(left % 2 == 1)
    def _():
        run([0], False)

    o_ref[...] = jnp.where(first_half, acc_ref[0], acc_ref[1])


def _sb_call(proj, batch, seq_len, n_pairs, tq=512):
    t = proj.shape[0]
    nq = seq_len // tq
    half = tq // 2
    kk = jnp.arange(half)
    later = (kk[:, None] > kk[None, :]).astype(BF16)
    return pl.pallas_call(
        functools.partial(_sb_kernel, tq=tq),
        grid=(batch, n_pairs, nq),
        in_specs=[pl.BlockSpec((tq, LANES), lambda b, p, i: (b * nq + i, p)),
                  pl.BlockSpec((seq_len, LANES), lambda b, p, i: (b, n_pairs + p)),
                  pl.BlockSpec((seq_len, LANES), lambda b, p, i: (b, 2 * n_pairs + p)),
                  pl.BlockSpec((half, half), lambda b, p, i: (0, 0))],
        out_specs=pl.BlockSpec((tq, LANES), lambda b, p, i: (b * nq + i, p)),
        out_shape=jax.ShapeDtypeStruct((t, n_pairs * LANES), F32),
        scratch_shapes=[pltpu.VMEM((2, tq, LANES), F32), pltpu.VMEM((2, tq, 1), F32)],
        compiler_params=_cparams("arbitrary", "arbitrary", "arbitrary"),
        name="sb_attention",
    )(proj, proj, proj, later)


def _gdn_prep_kernel(q_ref, k_ref, v_ref, qt_ref, kt_ref, vt_ref, ab_ref, cw_ref, alog_ref, dtb_ref,
                     ea_ref, eb_ref, lc_ref, jc_ref, bd_ref,
                     qn_o, qd_o, kn_o, kd_o, kb_o, kbg_o, vb_o, gcol_o, grow_o, scr, *, tm, tiles_per_seq):
    i = pl.program_id(0)
    keep_tail = (i % tiles_per_seq != 0).astype(F32)
    w = cw_ref[...]
    width = q_ref.shape[1]

    halo = qt_ref.shape[0]

    def conv_silu(cur_ref, tail_ref, col0):
        scr[0:halo, :] = tail_ref[...].astype(F32) * keep_tail
        scr[halo:, :] = cur_ref[...].astype(F32)
        y = jnp.zeros((tm, width), F32)
        for tap in range(GDN_CONV):
            off = halo - (GDN_CONV - 1) + tap
            y = y + scr[off:off + tm, :] * w[tap:tap + 1, col0:col0 + width]
        return _silu(y)

    bd = bd_ref[...]
    cq = conv_silu(q_ref, qt_ref, 0)
    qn = cq * lax.rsqrt(_group_sum_sq(cq, bd) + EPS) * (HEAD_DIM ** -0.5)
    ck = conv_silu(k_ref, kt_ref, width)
    kn = ck * lax.rsqrt(_group_sum_sq(ck, bd) + EPS)
    cv = conv_silu(v_ref, vt_ref, 2 * width)

    ab = ab_ref[...]
    g = -jnp.exp(alog_ref[...]) * _softplus(ab + dtb_ref[...])
    beta = _sigmoid(ab)
    g_parts = _bf16_pieces(g)
    lc, jc, ea = lc_ref[...], jc_ref[...], ea_ref[...]
    g_cum = sum(jnp.dot(lc, p, preferred_element_type=F32) for p in g_parts)
    g_tot = sum(jnp.dot(jc, p, preferred_element_type=F32) for p in g_parts)
    gx = sum(jnp.dot(p, ea, preferred_element_type=F32) for p in _bf16_pieces(g_cum))
    glx = sum(jnp.dot(p, ea, preferred_element_type=F32) for p in _bf16_pieces(g_tot))
    bx = sum(jnp.dot(p, eb_ref[...], preferred_element_type=F32) for p in _bf16_pieces(beta))

    e_g = jnp.exp(gx)
    kb = kn * bx
    qn_o[...] = qn
    qd_o[...] = qn * e_g
    kn_o[...] = kn
    kd_o[...] = kn * jnp.exp(glx - gx)
    kb_o[...] = kb
    kbg_o[...] = kb * e_g
    vb_o[...] = cv * bx
    g_heads = g_cum[:, :LANES]
    gcol_o[...] = g_heads
    grow_o[...] = g_heads.T[0:8, :]


def _gdn_prep_call(proj, ab_logits, conv_w, a_log, dt_bias, seq_len, n_heads, col_qkv, chunk, tm=512):
    t = proj.shape[0]
    assert n_heads <= 8 and tm % chunk == 0
    width = n_heads * HEAD_DIM
    abw = 2 * LANES
    tiles_per_seq = seq_len // tm
    halo = 16
    sub = tm // halo
    alog_pad = jnp.zeros((1, abw), F32).at[0, :n_heads].set(a_log)
    dtb_pad = jnp.zeros((1, abw), F32).at[0, :n_heads].set(dt_bias)
    head_of_lane = jnp.arange(width) // HEAD_DIM
    rows = jnp.arange(abw)
    ea = (rows[:, None] == head_of_lane[None, :]).astype(BF16)
    eb = (rows[:, None] == head_of_lane[None, :] + n_heads).astype(BF16)
    tok = jnp.arange(tm)
    same_chunk = (tok[:, None] // chunk) == (tok[None, :] // chunk)
    lc = (same_chunk & (tok[:, None] >= tok[None, :])).astype(BF16)
    jc = same_chunk.astype(BF16)
    bd = _block_diag_ones(2 * LANES, HEAD_DIM, 1.0, BF16)

    cur = lambda c: pl.BlockSpec((tm, width), lambda i: (i, c))
    tail = lambda c: pl.BlockSpec((halo, width), lambda i: (jnp.maximum(i * sub - 1, 0), c))
    full = lambda a: pl.BlockSpec(a.shape, lambda i: (0,) * a.ndim)
    cq, ck, cv = col_qkv
    out_spec = pl.BlockSpec((tm, width), lambda i: (i, 0))
    out_shape = jax.ShapeDtypeStruct((t, width), F32)
    return pl.pallas_call(
        functools.partial(_gdn_prep_kernel, tm=tm, tiles_per_seq=tiles_per_seq),
        grid=(t // tm,),
        in_specs=[cur(cq), cur(ck), cur(cv), tail(cq), tail(ck), tail(cv),
                  pl.BlockSpec((tm, abw), lambda i: (i, 0)),
                  full(conv_w), full(alog_pad), full(dtb_pad), full(ea), full(eb), full(lc), full(jc), full(bd)],
        out_specs=[out_spec] * 7 + [pl.BlockSpec((tm, LANES), lambda i: (i, 0)), pl.BlockSpec((8, tm), lambda i: (0, i))],
        out_shape=[out_shape] * 7 + [jax.ShapeDtypeStruct((t, LANES), F32), jax.ShapeDtypeStruct((8, t), F32)],
        scratch_shapes=[pltpu.VMEM((tm + halo, width), F32)],
        compiler_params=_cparams("arbitrary"),
        name="gdn_prep",
    )(proj, proj, proj, proj, proj, proj, ab_logits, conv_w, alog_pad, dtb_pad, ea, eb, lc, jc, bd)


def _gdn_core_kernel(qn_ref, qd_ref, kn_ref, kd_ref, kb_ref, kbg_ref, vb_ref, gcol_ref, grow_ref,
                     o_ref, s_scr, *, ts, n_heads, c_len):
    @pl.when(pl.program_id(1) == 0)
    def _():
        s_scr[...] = jnp.zeros_like(s_scr)

    row = lax.broadcasted_iota(jnp.int32, (c_len, c_len), 0)
    col = lax.broadcasted_iota(jnp.int32, (c_len, c_len), 1)
    strict = row > col
    incl = row >= col
    eye_state = (lax.broadcasted_iota(jnp.int32, (HEAD_DIM, HEAD_DIM), 0)
                 == lax.broadcasted_iota(jnp.int32, (HEAD_DIM, HEAD_DIM), 1))
    n_levels = (c_len - 1).bit_length()
    n_chunks = ts // c_len
    problems = [(c, h) for c in range(n_chunks) for h in range(n_heads)]

    def bf(x):
        return x.astype(BF16)

    def mm(a, b):
        return jnp.dot(a, b, preferred_element_type=F32)

    def head_tile(ref, c, h):
        pair, half = divmod(h, 2)
        tile = ref[c * c_len:(c + 1) * c_len, pair * LANES:(pair + 1) * LANES]
        return tile[:, half * HEAD_DIM:(half + 1) * HEAD_DIM]

    sibling = [((row >> k) ^ (col >> k)) == 1 for k in range(n_levels)]

    xs, ms, rs, qks = [], [], [], []
    for c, h in problems:
        g_col = gcol_ref[c * c_len:(c + 1) * c_len, h:h + 1]
        g_row = grow_ref[h:h + 1, c * c_len:(c + 1) * c_len]
        decay = jnp.where(incl, jnp.exp(g_col - g_row), 0.0)
        kn_b = bf(head_tile(kn_ref, c, h))
        kk = lax.dot_general(bf(head_tile(kb_ref, c, h)), kn_b, NT_DIMS, preferred_element_type=F32)
        x = jnp.where(strict, -(kk * decay), 0.0)
        xs.append(x)
        ms.append(jnp.where(row == col, 1.0, jnp.where(sibling[0], x, 0.0)))
        qks.append(bf(lax.dot_general(bf(head_tile(qn_ref, c, h)), kn_b, NT_DIMS, preferred_element_type=F32) * decay))
        rs.append(bf(jnp.concatenate([head_tile(vb_ref, c, h), head_tile(kbg_ref, c, h)], axis=1)))

    for k in range(1, n_levels):
        for idx in range(len(problems)):
            m_b = bf(ms[idx])
            left = mm(m_b, bf(jnp.where(sibling[k], xs[idx], 0.0)))
            ms[idx] = ms[idx] + mm(bf(left), m_b)

    q_eff, o_zero, p_mat, b_mat = {}, {}, {}, {}
    for idx, (c, h) in enumerate(problems):
        sol = bf(mm(bf(ms[idx]), rs[idx]))
        m1 = mm(qks[idx], sol)
        q_eff[c, h] = bf(head_tile(qd_ref, c, h) - m1[:, HEAD_DIM:])
        o_zero[c, h] = m1[:, :HEAD_DIM]
        m2 = lax.dot_general(bf(head_tile(kd_ref, c, h)), sol, TN_DIMS, preferred_element_type=F32)
        chunk_decay = jnp.exp(gcol_ref[(c + 1) * c_len - 1:(c + 1) * c_len, h:h + 1])
        p_mat[c, h] = bf(jnp.where(eye_state, chunk_decay, 0.0) - m2[:, HEAD_DIM:])
        b_mat[c, h] = m2[:, :HEAD_DIM]

    states = [s_scr[h] for h in range(n_heads)]
    for c in range(n_chunks):
        outs = []
        for h in range(n_heads):
            s_b = bf(states[h])
            outs.append(mm(q_eff[c, h], s_b) + o_zero[c, h])
            states[h] = mm(p_mat[c, h], s_b) + b_mat[c, h]
        o_ref[c * c_len:(c + 1) * c_len, :] = jnp.concatenate(outs, axis=1)
    for h in range(n_heads):
        s_scr[h] = states[h]


def _gdn_core_call(prep, batch, seq_len, n_heads, c_len, ts=512):
    t, width = prep[0].shape
    n_seq_tiles = seq_len // ts
    spec = pl.BlockSpec((ts, width), lambda b, s: (b * n_seq_tiles + s, 0))
    return pl.pallas_call(
        functools.partial(_gdn_core_kernel, ts=ts, n_heads=n_heads, c_len=c_len),
        grid=(batch, n_seq_tiles),
        in_specs=[spec] * 7 + [pl.BlockSpec((ts, LANES), lambda b, s: (b * n_seq_tiles + s, 0)),
                               pl.BlockSpec((8, ts), lambda b, s: (0, b * n_seq_tiles + s))],
        out_specs=spec,
        out_shape=jax.ShapeDtypeStruct((t, width), F32),
        scratch_shapes=[pltpu.VMEM((n_heads, HEAD_DIM, HEAD_DIM), F32)],
        compiler_params=_cparams("arbitrary", "arbitrary"),
        name="gdn_core",
    )(*prep)


def _merge_kernel(x_ref, ysb_ref, og_ref, z_ref, g0_ref, g1_ref, mod_ref, onw_ref, n2_ref, bd_ref,
                  wsb_ref, wgdn_ref, wo_ref, wq_ref, x1_o, h2t_o, pq_o):
    og = og_ref[...]
    ygdn = og * lax.rsqrt(_group_sum_sq(og, bd_ref[...]) + EPS) * onw_ref[...] * _silu(z_ref[...].astype(F32))
    m = (_sigmoid(g0_ref[...].astype(F32)) * jnp.dot(ysb_ref[...].astype(BF16), wsb_ref[...],
                                                      preferred_element_type=F32)
         + _sigmoid(g1_ref[...].astype(F32)) * jnp.dot(ygdn.astype(BF16), wgdn_ref[...], preferred_element_type=F32))
    gate1 = mod_ref[0, 2:3, :]
    shift2 = mod_ref[0, 3:4, :]
    scale2 = mod_ref[0, 4:5, :]
    x1 = x_ref[...] + gate1 * jnp.dot(m.astype(BF16), wo_ref[...], preferred_element_type=F32)
    x1_o[...] = x1
    ms2 = jnp.mean(x1 * x1, axis=-1, keepdims=True)
    h2 = x1 * lax.rsqrt(ms2 + EPS) * n2_ref[...] * (1.0 + scale2) + shift2
    h2t_o[...] = h2.T.astype(BF16)
    pq_o[...] = jnp.dot(h2.astype(BF16), wq_ref[...], preferred_element_type=F32)


def _merge_call(x2d, ysb, ogdn, proj, mod, onw, n2w, wsb, wgdn, wo, wq, seq_len, col_g0, col_g1, col_z, tm=512):
    t, d = x2d.shape
    width = ysb.shape[1]
    nq = wq.shape[1]
    bd = _block_diag_ones(2 * LANES, HEAD_DIM, 1.0 / HEAD_DIM, BF16)
    tiles_per_seq = seq_len // tm
    full = lambda a: pl.BlockSpec(a.shape, lambda i: (0,) * a.ndim)
    once = lambda a: pl.BlockSpec(a.shape, lambda i: (0,) * a.ndim, pipeline_mode=pl.Buffered(1))
    return pl.pallas_call(
        _merge_kernel,
        grid=(t // tm,),
        in_specs=[pl.BlockSpec((tm, d), lambda i: (i, 0)),
                  pl.BlockSpec((tm, width), lambda i: (i, 0)),
                  pl.BlockSpec((tm, width), lambda i: (i, 0)),
                  pl.BlockSpec((tm, width), lambda i: (i, col_z)),
                  pl.BlockSpec((tm, d), lambda i: (i, col_g0)),
                  pl.BlockSpec((tm, d), lambda i: (i, col_g1)),
                  pl.BlockSpec((1, 6, d), lambda i: (i // tiles_per_seq, 0, 0)),
                  full(onw), full(n2w), full(bd), once(wsb), once(wgdn), once(wo), once(wq)],
        out_specs=[pl.BlockSpec((tm, d), lambda i: (i, 0)),
                   pl.BlockSpec((d, tm), lambda i: (0, i)),
                   pl.BlockSpec((tm, nq), lambda i: (i, 0))],
        out_shape=[jax.ShapeDtypeStruct((t, d), F32),
                   jax.ShapeDtypeStruct((d, t), BF16),
                   jax.ShapeDtypeStruct((t, nq), F32)],
        compiler_params=_cparams("arbitrary"),
        name="merge_proj",
    )(x2d, ysb, ogdn, proj, proj, proj, mod, onw, n2w, bd, wsb, wgdn, wo, wq)


def _extract_topk(s, k, break_ties):
    n = s.shape[0]
    if not break_ties:
        lowest_bits = -8388609
        vals = []
        for r in range(k):
            m = jnp.max(s, axis=0, keepdims=True)
            marker = lax.bitcast_convert_type(jnp.int32(lowest_bits - r), F32)
            s = jnp.where(s == m, marker, s)
            vals.append(m)
        took = jnp.int32(lowest_bits) - lax.bitcast_convert_type(s, jnp.int32)
        rank = jnp.where((took >= 0) & (took < k), took.astype(F32), NOT_RANKED)
        return vals, rank
    iota = lax.broadcasted_iota(jnp.int32, s.shape, 0).astype(F32)
    rank = jnp.full(s.shape, NOT_RANKED, F32)
    vals = []
    for r in range(k):
        m = jnp.max(s, axis=0, keepdims=True)
        hit = iota == jnp.min(jnp.where(s == m, iota, float(n)), axis=0, keepdims=True)
        rank = jnp.where(hit, float(r), rank)
        s = jnp.where(hit, -jnp.inf, s)
        vals.append(m)
    return vals, rank


def _candidate_tables(k):
    pairs = [(a, b) for a in range(k) for b in range(k) if (a + 1) * (b + 1) <= k]
    n_pad = -(-len(pairs) // 8) * 8
    sel_a = jnp.zeros((n_pad, k), F32).at[jnp.arange(len(pairs)), jnp.array([a for a, _ in pairs])].set(1.0)
    sel_b = jnp.zeros((n_pad, k), F32).at[jnp.arange(len(pairs)), jnp.array([b for _, b in pairs])].set(1.0)
    return sel_a, sel_b, len(pairs)


def _bf16_pair_words(x):
    bits = lax.bitcast_convert_type(x.astype(BF16).astype(F32), jnp.uint32)
    return bits | (bits >> 16)


def _route_kernel(pq_ref, keys_ref, sela_ref, selb_ref, rank2_o, e2_o, cnt1_o, w1_o, *, n_cand):
    k = PEER_TOPK
    hp, _, _, half = keys_ref.shape
    tt = pq_ref.shape[0]
    part_scores = [[lax.dot_general(keys_ref[hh, part], pq_ref[:, (2 * hh + part) * half:(2 * hh + part + 1) * half],
                                    NT_DIMS, preferred_element_type=F32, precision=HI) for hh in range(hp)]
                   for part in range(2)]
    sel_a = sela_ref[...]

    def route(break_ties, heads):
        w = len(heads) * tt
        iota_k = lax.broadcasted_iota(jnp.int32, (k, w), 0).astype(F32)
        s_all = jnp.concatenate([part_scores[part][hh] for part in range(2) for hh in heads], axis=1)
        scores = (s_all[:, :w], s_all[:, w:])
        vals, rank = _extract_topk(s_all, k, break_ties)
        top = jnp.concatenate(vals, axis=0)
        top1, top2 = top[:, :w], top[:, w:]
        v1, v2 = [v[:, :w] for v in vals], [v[:, w:] for v in vals]
        rank1, rank2 = rank[:, :w], rank[:, w:]
        cand = (jnp.dot(sel_a, top1, preferred_element_type=F32, precision=HI)
                + jnp.dot(selb_ref[...], top2, preferred_element_type=F32, precision=HI))
        cand_row = lax.broadcasted_iota(jnp.int32, cand.shape, 0)
        _, cand_rank = _extract_topk(jnp.where(cand_row < n_cand, cand, -jnp.inf), k, break_ties)
        chosen = (cand_rank < float(k)).astype(BF16)
        count = lax.dot_general(sel_a.astype(BF16), chosen, TN_DIMS, preferred_element_type=F32)
        e1 = jnp.exp(top1 - v1[0])
        e2 = jnp.exp(top2 - v2[0])
        z = jnp.zeros_like(v1[0])
        for a in range(k):
            z = z + e1[a:a + 1] * jnp.sum(jnp.where(iota_k < count[a:a + 1], e2, 0.0), axis=0, keepdims=True)
        inv_z = 1.0 / z
        cnt1 = jnp.zeros_like(rank1)
        for a in range(k):
            cnt1 = jnp.where(rank1 == float(a), count[a:a + 1], cnt1)
        e2_all = jnp.exp(scores[1] - v2[0]).astype(BF16)
        w1_all = jnp.where(rank1 < float(k), jnp.exp(scores[0] - v1[0]) * inv_z, 0.0)
        cnt1_words = _bf16_pair_words(cnt1)
        w1_words = _bf16_pair_words(w1_all)
        n_ranked = jnp.sum((rank < float(k)).astype(F32), axis=0, keepdims=True)
        n_chosen = jnp.sum((cand_rank < float(k)).astype(F32), axis=0, keepdims=True)
        off_by = jnp.maximum(jnp.maximum(jnp.abs(n_ranked[:, :w] - k), jnp.abs(n_ranked[:, w:] - k)),
                             jnp.abs(n_chosen - k))
        excess = []
        for pos, hh in enumerate(heads):
            lanes = slice(pos * tt, (pos + 1) * tt)
            rank2_o[hh] = rank2[:, lanes].astype(BF16).reshape(rank2_o.shape[1:])
            e2_o[hh] = e2_all[:, lanes].reshape(e2_o.shape[1:])
            cnt1_o[hh] = cnt1_words[:, lanes]
            w1_o[hh] = w1_words[:, lanes]
            excess.append(jnp.max(off_by[:, lanes]))
        return excess

    excess = route(False, list(range(hp)))
    for hh in range(hp):
        @pl.when(excess[hh] > 0.0)
        def _(hh=hh):
            route(True, [hh])


def _route_call(pq, sub_keys, tt=256, hp=4):
    t = pq.shape[0]
    n_heads, _, n_keys, half = sub_keys.shape
    sel_a, sel_b, n_cand = _candidate_tables(PEER_TOPK)
    out_spec = pl.BlockSpec((hp, n_keys, tt), lambda i, h: (h, 0, i))
    shape = lambda dt: jax.ShapeDtypeStruct((n_heads, n_keys, t), dt)
    tiled_spec = pl.BlockSpec((hp, n_keys // BF16_TILE_ROWS, BF16_TILE_ROWS, tt), lambda i, h: (h, 0, 0, i))
    tiled_shape = jax.ShapeDtypeStruct((n_heads, n_keys // BF16_TILE_ROWS, BF16_TILE_ROWS, t), BF16)
    return pl.pallas_call(
        functools.partial(_route_kernel, n_cand=n_cand),
        grid=(t // tt, n_heads // hp),
        in_specs=[pl.BlockSpec((tt, hp * 2 * half), lambda i, h: (i, h)),
                  pl.BlockSpec((hp, 2, n_keys, half), lambda i, h: (h, 0, 0, 0)),
                  pl.BlockSpec(sel_a.shape, lambda i, h: (0, 0)),
                  pl.BlockSpec(sel_b.shape, lambda i, h: (0, 0))],
        out_specs=[tiled_spec, tiled_spec, out_spec, out_spec],
        out_shape=[tiled_shape, tiled_shape, shape(jnp.uint32), shape(jnp.uint32)],
        compiler_params=_cparams("arbitrary", "arbitrary"),
        name="peer_route",
    )(pq, sub_keys, sel_a, sel_b)


def _peer_kernel(h2t_ref, u_ref, vt_ref, rank2_ref, e2_ref, cnt1_ref, w1_ref, x1_ref, mod_ref,
                 o_ref, act_even, act_odd, acc_scr, *, n_heads, n_keys, ec, group, n_chunks, n_items):
    s = pl.program_id(0)
    chunk = jnp.clip(s - 1, 0, n_items - 1) % n_chunks

    @pl.when(s == 0)
    def _():
        act_odd[...] = jnp.zeros_like(act_odd)

    @pl.when((chunk == 0) | (s == 0))
    def _():
        acc_scr[...] = jnp.zeros_like(acc_scr)

    tt = h2t_ref.shape[1]
    zero = jnp.zeros((), BF16)
    n_groups = ec // group
    subs_per_group = group // n_keys

    def coef_of(grp, act):
        coefs = []
        for s_loc in range(subs_per_group):
            sub = grp * subs_per_group + s_loc
            gate = None
            for h in range(n_heads):
                cnt = pltpu.bitcast(jnp.broadcast_to(cnt1_ref[h, sub:sub + 1, :], (8, tt)), BF16)
                w1 = pltpu.bitcast(jnp.broadcast_to(w1_ref[h, sub:sub + 1, :], (8, tt)), BF16)
                term = jnp.where(rank2_ref[h] < cnt[None], e2_ref[h], zero) * w1[None]
                gate = term if gate is None else gate + term
            a = act[s_loc * n_keys:(s_loc + 1) * n_keys, :].astype(BF16)
            gelu = (0.5 * a) * (1.0 + lax.erf(a * (2.0 ** -0.5)))
            coefs.append(gate.reshape(n_keys, tt) * gelu)
        return jnp.concatenate(coefs, axis=0)

    def stages(act_w, act_r):
        total = acc_scr[...]
        half_t = tt // 2
        new_parts = []
        for grp in range(n_groups):
            rows = slice(grp * group, (grp + 1) * group)
            coef = coef_of(grp, act_r[rows, :])
            if grp < 2:
                cols = slice(grp * half_t, (grp + 1) * half_t)
                new_parts.append(jnp.dot(u_ref[...], h2t_ref[:, cols], preferred_element_type=F32))
            total = total + jnp.dot(vt_ref[:, rows], coef, preferred_element_type=F32)
        acc_scr[...] = total
        act_w[...] = jnp.concatenate(new_parts, axis=1)

    @pl.when(s % 2 == 0)
    def _():
        stages(act_even, act_odd)

    @pl.when(s % 2 == 1)
    def _():
        stages(act_odd, act_even)

    @pl.when((chunk == n_chunks - 1) & (s >= 1))
    def _():
        gate2 = mod_ref[0, 5:6, :]
        o_ref[...] = x1_ref[...] + gate2 * acc_scr[...].T


def _peer_call(h2t, u_b, vt_b, rank2, e2, cnt1, w1, x1, mod, seq_len, tt=512, ec=2048, group=1024):
    d, t = h2t.shape
    n_exp = u_b.shape[0]
    n_heads, n_key_tiles, tile_rows, _ = rank2.shape
    n_keys = n_key_tiles * tile_rows
    tiles_per_seq = seq_len // tt
    n_chunks = n_exp // ec
    n_items = (t // tt) * n_chunks
    first = lambda s: jnp.minimum(s, n_items - 1)
    second = lambda s: jnp.clip(s - 1, 0, n_items - 1)
    route_spec = pl.BlockSpec((n_heads, n_key_tiles, tile_rows, tt), lambda s: (0, 0, 0, second(s) // n_chunks))
    row_spec = pl.BlockSpec((n_heads, ec // n_keys, tt), lambda s: (0, second(s) % n_chunks, second(s) // n_chunks))
    return pl.pallas_call(
        functools.partial(_peer_kernel, n_heads=n_heads, n_keys=n_keys, ec=ec, group=group,
                          n_chunks=n_chunks, n_items=n_items),
        grid=(n_items + 1,),
        in_specs=[pl.BlockSpec((d, tt), lambda s: (0, first(s) // n_chunks)),
                  pl.BlockSpec((ec, d), lambda s: (first(s) % n_chunks, 0)),
                  pl.BlockSpec((d, ec), lambda s: (0, second(s) % n_chunks)),
                  route_spec, route_spec, row_spec, row_spec,
                  pl.BlockSpec((tt, d), lambda s: (second(s) // n_chunks, 0)),
                  pl.BlockSpec((1, 6, d), lambda s: (second(s) // n_chunks // tiles_per_seq, 0, 0))],
        out_specs=pl.BlockSpec((tt, d), lambda s: (second(s) // n_chunks, 0)),
        out_shape=jax.ShapeDtypeStruct((t, d), F32),
        scratch_shapes=[pltpu.VMEM((ec, tt), F32), pltpu.VMEM((ec, tt), F32), pltpu.VMEM((d, tt), F32)],
        compiler_params=_cparams("arbitrary"),
        name="peer_experts",
    )(h2t, u_b, vt_b, rank2, e2, cnt1, w1, x1, mod)


def _pack_in_proj(w_in, sb_w, gdn_qk_w, gdn_v_w, n_gdn_heads, d_model, tn):
    o_sbq, o_sbk, o_sbv = 0, sb_w, 2 * sb_w
    o_gdn = 3 * sb_w
    conv_w = 2 * gdn_qk_w + gdn_v_w
    o_a = o_gdn + conv_w
    o_b = o_a + n_gdn_heads
    o_z = o_b + n_gdn_heads
    o_gate = o_z + gdn_v_w
    pad = (-w_in.shape[1]) % tn
    packed = jnp.concatenate([
        w_in[:, o_sbq:o_a],
        w_in[:, o_gate:o_gate + 2 * d_model],
        w_in[:, o_z:o_z + gdn_v_w],
        w_in[:, o_a:o_z],
        jnp.zeros((w_in.shape[0], pad), w_in.dtype)], axis=1).astype(BF16)
    return packed


def _block(x2d, c, w_ada, b_ada, norm1_w, w_in, sb_q_norm_w, sb_k_norm_w, gdn_conv_w, gdn_A_log,
           gdn_dt_bias, gdn_o_norm_w, w_proj_sb, w_proj_gdn, w_o, norm2_w, peer_w_q, peer_sub_keys,
           peer_u, peer_v, batch, seq_len):
    t, d = x2d.shape
    sb_w = w_proj_sb.shape[0]
    gdn_v_w = w_proj_gdn.shape[0]
    gdn_qk_w = (gdn_conv_w.shape[1] - gdn_v_w) // 2
    n_sb_heads = sb_w // HEAD_DIM
    n_gdn_heads = gdn_v_w // HEAD_DIM
    assert gdn_qk_w == gdn_v_w == sb_w and d % (4 * LANES) == 0

    c_pad = jnp.zeros((8, d), F32).at[:batch].set(c)
    mod = _ada_call(c_pad, w_ada, b_ada)[:batch].reshape(batch, 6, d)

    tn = sb_w
    w_packed = _pack_in_proj(w_in, sb_w, gdn_qk_w, gdn_v_w, n_gdn_heads, d, tn)
    heads_per_tile = tn // HEAD_DIM
    q_tiles = sb_w // tn
    qkw = jnp.concatenate([jnp.tile(sb_q_norm_w * (HEAD_DIM ** -0.5 * LOG2E), (q_tiles, heads_per_tile)),
                           jnp.tile(sb_k_norm_w, (q_tiles, heads_per_tile))], axis=0).reshape(2 * q_tiles, 1, tn)
    proj, ab_logits = _inproj_call(x2d, mod, norm1_w.reshape(1, d), w_packed, qkw, seq_len, tn=tn)

    ysb = _sb_call(proj, batch, seq_len, n_sb_heads // 2)

    col_gdn = 3 * sb_w // gdn_v_w
    col_gate = (3 * sb_w + 3 * gdn_v_w) // d
    col_z = (3 * sb_w + 3 * gdn_v_w + 2 * d) // gdn_v_w
    prep = _gdn_prep_call(proj, ab_logits, gdn_conv_w, gdn_A_log, gdn_dt_bias, seq_len, n_gdn_heads,
                          (col_gdn, col_gdn + 1, col_gdn + 2), GDN_BLOCK)
    ogdn = _gdn_core_call(prep, batch, seq_len, n_gdn_heads, GDN_BLOCK)

    x1, h2, pq = _merge_call(
        x2d, ysb, ogdn, proj, mod, jnp.tile(gdn_o_norm_w, n_gdn_heads).reshape(1, gdn_v_w), norm2_w.reshape(1, d),
        w_proj_sb.astype(BF16), w_proj_gdn.astype(BF16), w_o.astype(BF16), peer_w_q.astype(BF16),
        seq_len, col_gate, col_gate + 1, col_z)

    rank2, e2, cnt1, w1 = _route_call(pq, peer_sub_keys)
    return _peer_call(h2, peer_u.astype(BF16), peer_v.T.astype(BF16), rank2, e2, cnt1, w1, x1, mod, seq_len)


def kernel(x, c, w_ada, b_ada, norm1_w, w_in, sb_q_norm_w, sb_k_norm_w, gdn_conv_w, gdn_A_log, gdn_dt_bias,
           gdn_o_norm_w, w_proj_sb, w_proj_gdn, w_o, norm2_w, peer_w_q, peer_sub_keys, peer_u, peer_v):
    batch, seq_len, d = x.shape
    x2d = x.reshape(batch * seq_len, d)
    params = (w_ada, b_ada, norm1_w, w_in, sb_q_norm_w, sb_k_norm_w, gdn_conv_w, gdn_A_log, gdn_dt_bias,
              gdn_o_norm_w, w_proj_sb, w_proj_gdn, w_o, norm2_w, peer_w_q, peer_sub_keys, peer_u, peer_v)
    depth = w_ada.shape[0]
    for l in range(depth):
        layer = [p.reshape(p.shape[1:]) if depth == 1 else p[l] for p in params]
        x2d = _block(x2d, c, *layer, batch, seq_len)
    return x2d.reshape(batch, seq_len, d)
```

```python
import functools

import jax
import jax.numpy as jnp
from jax import lax
from jax.experimental import pallas as pl
from jax.experimental.pallas import tpu as pltpu

F32 = jnp.float32
BF16 = jnp.bfloat16
HI = lax.Precision.HIGHEST
EPS = 1e-6
LOG2E = 1.4426950408889634

LANES = 128
BF16_TILE_ROWS = 16
HEAD_DIM = 64
GDN_BLOCK = 128
GDN_CONV = 4
PEER_TOPK = 16
NOT_RANKED = 99.0
VMEM_LIMIT = 56 * 1024 * 1024

NT_DIMS = (((1,), (1,)), ((), ()))
TN_DIMS = (((0,), (0,)), ((), ()))


def _cparams(*sem):
    return pltpu.CompilerParams(dimension_semantics=sem, vmem_limit_bytes=VMEM_LIMIT)


def _sigmoid(x):
    return 1.0 / (1.0 + jnp.exp(-x))


def _silu(x):
    return x * _sigmoid(x)


def _softplus(x):
    return jnp.maximum(x, 0.0) + jnp.log(1.0 + jnp.exp(-jnp.abs(x)))


def _block_diag_ones(n, group, value=1.0, dtype=F32):
    r = jnp.arange(n) // group
    return jnp.where(r[:, None] == r[None, :], value, 0.0).astype(dtype)


def _bf16_pieces(x):
    hi = x.astype(BF16)
    rest = x - hi.astype(F32)
    mid = rest.astype(BF16)
    return hi, mid, (rest - mid.astype(F32)).astype(BF16)


def _group_sum_sq(x, bd):
    slab = bd.shape[0]
    outs = []
    for c0 in range(0, x.shape[1], slab):
        sq = x[:, c0:c0 + slab] * x[:, c0:c0 + slab]
        hi = sq.astype(BF16)
        lo = (sq - hi.astype(F32)).astype(BF16)
        outs.append(jnp.dot(hi, bd, preferred_element_type=F32) + jnp.dot(lo, bd, preferred_element_type=F32))
    return outs[0] if len(outs) == 1 else jnp.concatenate(outs, axis=1)


def _ada_kernel(c_ref, w_ref, b_ref, o_ref):
    c = c_ref[...]
    o_ref[...] = jnp.dot(_silu(c), w_ref[...], preferred_element_type=F32, precision=HI) + b_ref[...]


def _ada_call(c_pad, w_ada, b_ada):
    rows, d = c_pad.shape
    n = w_ada.shape[1]
    tn = 2048
    return pl.pallas_call(
        _ada_kernel,
        grid=(n // tn,),
        in_specs=[pl.BlockSpec((rows, d), lambda j: (0, 0)),
                  pl.BlockSpec((d, tn), lambda j: (0, j)),
                  pl.BlockSpec((1, tn), lambda j: (0, j))],
        out_specs=pl.BlockSpec((rows, tn), lambda j: (0, j)),
        out_shape=jax.ShapeDtypeStruct((rows, n), F32),
        compiler_params=_cparams("arbitrary"),
        name="ada_mod",
    )(c_pad, w_ada, b_ada.reshape(1, n))


def _inproj_kernel(x_ref, mod_ref, n1_ref, w_ref, qkw_ref, bd_ref, o_ref, tail_ref, h_scr, *, n_qk_tiles):
    j = pl.program_id(1)
    last = pl.num_programs(1) - 1

    @pl.when(j == 0)
    def _():
        x = x_ref[...]
        ms = jnp.mean(x * x, axis=-1, keepdims=True)
        y = x * lax.rsqrt(ms + EPS) * n1_ref[...]
        shift = mod_ref[0, 0:1, :]
        scale = mod_ref[0, 1:2, :]
        h_scr[...] = (y * (1.0 + scale) + shift).astype(BF16)

    acc = jnp.dot(h_scr[...], w_ref[...], preferred_element_type=F32)

    @pl.when(j < n_qk_tiles)
    def _():
        o_ref[...] = (acc * lax.rsqrt(_group_sum_sq(acc, bd_ref[...]) + EPS) * qkw_ref[0]).astype(BF16)

    @pl.when((j >= n_qk_tiles) & (j < last))
    def _():
        o_ref[...] = acc.astype(BF16)

    @pl.when(j == last)
    def _():
        tail_ref[...] = acc


def _inproj_call(x2d, mod, n1w, w_packed, qkw, seq_len, tm=2048, tn=512):
    t, d = x2d.shape
    n = w_packed.shape[1]
    n_tiles = n // tn
    n_qk_tiles = qkw.shape[0]
    assert seq_len % tm == 0, "a row tile must not straddle two sequences (per-sequence modulation)"
    bd = _block_diag_ones(2 * LANES, HEAD_DIM, 1.0 / HEAD_DIM, BF16)
    tiles_per_seq = seq_len // tm
    return pl.pallas_call(
        functools.partial(_inproj_kernel, n_qk_tiles=n_qk_tiles),
        grid=(t // tm, n // tn),
        in_specs=[pl.BlockSpec((tm, d), lambda i, j: (i, 0)),
                  pl.BlockSpec((1, 6, d), lambda i, j: (i // tiles_per_seq, 0, 0)),
                  pl.BlockSpec((1, d), lambda i, j: (0, 0)),
                  pl.BlockSpec((d, tn), lambda i, j: (0, j)),
                  pl.BlockSpec((1, 1, tn), lambda i, j: (jnp.minimum(j, n_qk_tiles - 1), 0, 0)),
                  pl.BlockSpec(bd.shape, lambda i, j: (0, 0))],
        out_specs=[pl.BlockSpec((tm, tn), lambda i, j: (i, jnp.minimum(j, n_tiles - 2))),
                   pl.BlockSpec((tm, tn), lambda i, j: (i, 0))],
        out_shape=[jax.ShapeDtypeStruct((t, n - tn), BF16), jax.ShapeDtypeStruct((t, tn), F32)],
        scratch_shapes=[pltpu.VMEM((tm, d), BF16)],
        compiler_params=_cparams("arbitrary", "arbitrary"),
        name="in_proj",
    )(x2d, mod, n1w, w_packed, qkw, bd)


def _sb_kernel(q_ref, k_ref, v_ref, tri_ref, o_ref, acc_ref, r_ref, *, tq):
    i = pl.program_id(2)
    half = tri_ref.shape[0]
    lane = lax.broadcasted_iota(jnp.int32, (1, LANES), 1)
    first_half = lane < HEAD_DIM
    q = q_ref[...].astype(F32)
    q_heads = (jnp.where(first_half, q, 0.0).astype(BF16), jnp.where(first_half, 0.0, q).astype(BF16))
    acc_ref[...] = jnp.zeros_like(acc_ref)
    r_ref[...] = jnp.zeros_like(r_ref)

    def run(blocks, masked):
        items = [(j, hd) for j in blocks for hd in range(2)]
        st = [dict() for _ in items]
        if masked:
            causal = (lax.broadcasted_iota(jnp.int32, (tq, tq), 1) < lax.broadcasted_iota(jnp.int32, (tq, tq), 0))

        def rows_of(n):
            return pl.ds(pl.multiple_of(items[n][0] * tq, tq), tq)

        def logits(n):
            kb = k_ref[rows_of(n), :].astype(BF16)
            st[n]["y"] = lax.dot_general(q_heads[items[n][1]], kb, NT_DIMS, preferred_element_type=F32)

        def softplus(n):
            y = st[n]["y"]
            neg_abs = lax.bitcast_convert_type(lax.bitcast_convert_type(y, jnp.uint32) | jnp.uint32(0x80000000), F32)
            sp = jnp.maximum(y, 0.0) + jnp.log(1.0 + jnp.exp2(neg_abs)) * LOG2E
            if masked:
                sp = jnp.where(causal, sp, 0.0)
            st[n]["log_beta"] = y - sp
            st[n]["spb"] = sp.astype(BF16)
            st[n]["sp_first"] = (sp[:, 0:1], sp[:, half:half + 1])

        def later_sums(n):
            spb = st[n]["spb"]
            first_lo, first_hi = st[n]["sp_first"]
            later_hi = jnp.dot(spb[:, half:], tri_ref[...], preferred_element_type=F32)
            total_hi = later_hi[:, 0:1] + first_hi
            later_lo = jnp.dot(spb[:, :half], tri_ref[...], preferred_element_type=F32) + total_hi
            st[n]["later"] = jnp.concatenate([later_lo, later_hi], axis=1)
            st[n]["total"] = later_lo[:, 0:1] + first_lo

        def weights(n):
            a = jnp.exp2(st[n]["log_beta"] - st[n]["later"])
            if masked:
                a = jnp.where(causal, a, 0.0)
            st[n]["a"] = a.astype(BF16)

        def accumulate(n):
            hd = items[n][1]
            pv = jnp.dot(st[n]["a"], v_ref[rows_of(n), :].astype(BF16), preferred_element_type=F32)
            r = r_ref[hd]
            acc_ref[hd] += jnp.exp2(-r) * pv
            r_ref[hd] = r + st[n]["total"]

        stages = (logits, softplus, later_sums, weights, accumulate)
        elementwise = (softplus, weights)
        for wave in range(len(items) + len(stages) - 1):
            todo = [(stages[wave - n], n) for n in range(len(items)) if 0 <= wave - n < len(stages)]
            for stage, n in [x for x in todo if x[0] in elementwise] + [x for x in todo if x[0] not in elementwise]:
                stage(n)

    run([i], True)

    per_trip = 4

    def body(it, carry):
        j = i - 1 - per_trip * it
        run([j - d for d in range(per_trip)], False)
        return carry

    lax.fori_loop(0, i // per_trip, body, 0)

    left = i % per_trip

    @pl.when(left >= 2)
    def _():
        run([left - 1, left - 2], False)

    @pl.when(left % 2 == 1)
    def _():
        run([0], False)

    o_ref[...] = jnp.where(first_half, acc_ref[0], acc_ref[1])


def _sb_call(proj, batch, seq_len, n_pairs, tq=512):
    t = proj.shape[0]
    nq = seq_len // tq
    half = tq // 2
    kk = jnp.arange(half)
    later = (kk[:, None] > kk[None, :]).astype(BF16)
    return pl.pallas_call(
        functools.partial(_sb_kernel, tq=tq),
        grid=(batch, n_pairs, nq),
        in_specs=[pl.BlockSpec((tq, LANES), lambda b, p, i: (b * nq + i, p)),
                  pl.BlockSpec((seq_len, LANES), lambda b, p, i: (b, n_pairs + p)),
                  pl.BlockSpec((seq_len, LANES), lambda b, p, i: (b, 2 * n_pairs + p)),
                  pl.BlockSpec((half, half), lambda b, p, i: (0, 0))],
        out_specs=pl.BlockSpec((tq, LANES), lambda b, p, i: (b * nq + i, p)),
        out_shape=jax.ShapeDtypeStruct((t, n_pairs * LANES), F32),
        scratch_shapes=[pltpu.VMEM((2, tq, LANES), F32), pltpu.VMEM((2, tq, 1), F32)],
        compiler_params=_cparams("arbitrary", "arbitrary", "arbitrary"),
        name="sb_attention",
    )(proj, proj, proj, later)


def _gdn_prep_kernel(q_ref, k_ref, v_ref, qt_ref, kt_ref, vt_ref, ab_ref, cw_ref, alog_ref, dtb_ref,
                     ea_ref, eb_ref, lc_ref, jc_ref, bd_ref,
                     qn_o, qd_o, kn_o, kd_o, kb_o, kbg_o, vb_o, gcol_o, grow_o, scr, *, tm, tiles_per_seq):
    i = pl.program_id(0)
    keep_tail = (i % tiles_per_seq != 0).astype(F32)
    w = cw_ref[...]
    width = q_ref.shape[1]

    halo = qt_ref.shape[0]

    def conv_silu(cur_ref, tail_ref, col0):
        scr[0:halo, :] = tail_ref[...].astype(F32) * keep_tail
        scr[halo:, :] = cur_ref[...].astype(F32)
        y = jnp.zeros((tm, width), F32)
        for tap in range(GDN_CONV):
            off = halo - (GDN_CONV - 1) + tap
            y = y + scr[off:off + tm, :] * w[tap:tap + 1, col0:col0 + width]
        return _silu(y)

    bd = bd_ref[...]
    cq = conv_silu(q_ref, qt_ref, 0)
    qn = cq * lax.rsqrt(_group_sum_sq(cq, bd) + EPS) * (HEAD_DIM ** -0.5)
    ck = conv_silu(k_ref, kt_ref, width)
    kn = ck * lax.rsqrt(_group_sum_sq(ck, bd) + EPS)
    cv = conv_silu(v_ref, vt_ref, 2 * width)

    ab = ab_ref[...]
    g = -jnp.exp(alog_ref[...]) * _softplus(ab + dtb_ref[...])
    beta = _sigmoid(ab)
    g_parts = _bf16_pieces(g)
    lc, jc, ea = lc_ref[...], jc_ref[...], ea_ref[...]
    g_cum = sum(jnp.dot(lc, p, preferred_element_type=F32) for p in g_parts)
    g_tot = sum(jnp.dot(jc, p, preferred_element_type=F32) for p in g_parts)
    gx = sum(jnp.dot(p, ea, preferred_element_type=F32) for p in _bf16_pieces(g_cum))
    glx = sum(jnp.dot(p, ea, preferred_element_type=F32) for p in _bf16_pieces(g_tot))
    bx = sum(jnp.dot(p, eb_ref[...], preferred_element_type=F32) for p in _bf16_pieces(beta))

    e_g = jnp.exp(gx)
    kb = kn * bx
    qn_o[...] = qn
    qd_o[...] = qn * e_g
    kn_o[...] = kn
    kd_o[...] = kn * jnp.exp(glx - gx)
    kb_o[...] = kb
    kbg_o[...] = kb * e_g
    vb_o[...] = cv * bx
    g_heads = g_cum[:, :LANES]
    gcol_o[...] = g_heads
    grow_o[...] = g_heads.T[0:8, :]


def _gdn_prep_call(proj, ab_logits, conv_w, a_log, dt_bias, seq_len, n_heads, col_qkv, chunk, tm=512):
    t = proj.shape[0]
    assert n_heads <= 8 and tm % chunk == 0
    width = n_heads * HEAD_DIM
    abw = 2 * LANES
    tiles_per_seq = seq_len // tm
    halo = 16
    sub = tm // halo
    alog_pad = jnp.zeros((1, abw), F32).at[0, :n_heads].set(a_log)
    dtb_pad = jnp.zeros((1, abw), F32).at[0, :n_heads].set(dt_bias)
    head_of_lane = jnp.arange(width) // HEAD_DIM
    rows = jnp.arange(abw)
    ea = (rows[:, None] == head_of_lane[None, :]).astype(BF16)
    eb = (rows[:, None] == head_of_lane[None, :] + n_heads).astype(BF16)
    tok = jnp.arange(tm)
    same_chunk = (tok[:, None] // chunk) == (tok[None, :] // chunk)
    lc = (same_chunk & (tok[:, None] >= tok[None, :])).astype(BF16)
    jc = same_chunk.astype(BF16)
    bd = _block_diag_ones(2 * LANES, HEAD_DIM, 1.0, BF16)

    cur = lambda c: pl.BlockSpec((tm, width), lambda i: (i, c))
    tail = lambda c: pl.BlockSpec((halo, width), lambda i: (jnp.maximum(i * sub - 1, 0), c))
    full = lambda a: pl.BlockSpec(a.shape, lambda i: (0,) * a.ndim)
    cq, ck, cv = col_qkv
    out_spec = pl.BlockSpec((tm, width), lambda i: (i, 0))
    out_shape = jax.ShapeDtypeStruct((t, width), F32)
    return pl.pallas_call(
        functools.partial(_gdn_prep_kernel, tm=tm, tiles_per_seq=tiles_per_seq),
        grid=(t // tm,),
        in_specs=[cur(cq), cur(ck), cur(cv), tail(cq), tail(ck), tail(cv),
                  pl.BlockSpec((tm, abw), lambda i: (i, 0)),
                  full(conv_w), full(alog_pad), full(dtb_pad), full(ea), full(eb), full(lc), full(jc), full(bd)],
        out_specs=[out_spec] * 7 + [pl.BlockSpec((tm, LANES), lambda i: (i, 0)), pl.BlockSpec((8, tm), lambda i: (0, i))],
        out_shape=[out_shape] * 7 + [jax.ShapeDtypeStruct((t, LANES), F32), jax.ShapeDtypeStruct((8, t), F32)],
        scratch_shapes=[pltpu.VMEM((tm + halo, width), F32)],
        compiler_params=_cparams("arbitrary"),
        name="gdn_prep",
    )(proj, proj, proj, proj, proj, proj, ab_logits, conv_w, alog_pad, dtb_pad, ea, eb, lc, jc, bd)


def _gdn_core_kernel(qn_ref, qd_ref, kn_ref, kd_ref, kb_ref, kbg_ref, vb_ref, gcol_ref, grow_ref,
                     o_ref, s_scr, *, ts, n_heads, c_len):
    @pl.when(pl.program_id(1) == 0)
    def _():
        s_scr[...] = jnp.zeros_like(s_scr)

    row = lax.broadcasted_iota(jnp.int32, (c_len, c_len), 0)
    col = lax.broadcasted_iota(jnp.int32, (c_len, c_len), 1)
    strict = row > col
    incl = row >= col
    eye_state = (lax.broadcasted_iota(jnp.int32, (HEAD_DIM, HEAD_DIM), 0)
                 == lax.broadcasted_iota(jnp.int32, (HEAD_DIM, HEAD_DIM), 1))
    n_levels = (c_len - 1).bit_length()
    n_chunks = ts // c_len
    problems = [(c, h) for c in range(n_chunks) for h in range(n_heads)]

    def bf(x):
        return x.astype(BF16)

    def mm(a, b):
        return jnp.dot(a, b, preferred_element_type=F32)

    def head_tile(ref, c, h):
        pair, half = divmod(h, 2)
        tile = ref[c * c_len:(c + 1) * c_len, pair * LANES:(pair + 1) * LANES]
        return tile[:, half * HEAD_DIM:(half + 1) * HEAD_DIM]

    sibling = [((row >> k) ^ (col >> k)) == 1 for k in range(n_levels)]

    xs, ms, rs, qks = [], [], [], []
    for c, h in problems:
        g_col = gcol_ref[c * c_len:(c + 1) * c_len, h:h + 1]
        g_row = grow_ref[h:h + 1, c * c_len:(c + 1) * c_len]
        decay = jnp.where(incl, jnp.exp(g_col - g_row), 0.0)
        kn_b = bf(head_tile(kn_ref, c, h))
        kk = lax.dot_general(bf(head_tile(kb_ref, c, h)), kn_b, NT_DIMS, preferred_element_type=F32)
        x = jnp.where(strict, -(kk * decay), 0.0)
        xs.append(x)
        ms.append(jnp.where(row == col, 1.0, jnp.where(sibling[0], x, 0.0)))
        qks.append(bf(lax.dot_general(bf(head_tile(qn_ref, c, h)), kn_b, NT_DIMS, preferred_element_type=F32) * decay))
        rs.append(bf(jnp.concatenate([head_tile(vb_ref, c, h), head_tile(kbg_ref, c, h)], axis=1)))

    for k in range(1, n_levels):
        for idx in range(len(problems)):
            m_b = bf(ms[idx])
            left = mm(m_b, bf(jnp.where(sibling[k], xs[idx], 0.0)))
            ms[idx] = ms[idx] + mm(bf(left), m_b)

    q_eff, o_zero, p_mat, b_mat = {}, {}, {}, {}
    for idx, (c, h) in enumerate(problems):
        sol = bf(mm(bf(ms[idx]), rs[idx]))
        m1 = mm(qks[idx], sol)
        q_eff[c, h] = bf(head_tile(qd_ref, c, h) - m1[:, HEAD_DIM:])
        o_zero[c, h] = m1[:, :HEAD_DIM]
        m2 = lax.dot_general(bf(head_tile(kd_ref, c, h)), sol, TN_DIMS, preferred_element_type=F32)
        chunk_decay = jnp.exp(gcol_ref[(c + 1) * c_len - 1:(c + 1) * c_len, h:h + 1])
        p_mat[c, h] = bf(jnp.where(eye_state, chunk_decay, 0.0) - m2[:, HEAD_DIM:])
        b_mat[c, h] = m2[:, :HEAD_DIM]

    states = [s_scr[h] for h in range(n_heads)]
    for c in range(n_chunks):
        outs = []
        for h in range(n_heads):
            s_b = bf(states[h])
            outs.append(mm(q_eff[c, h], s_b) + o_zero[c, h])
            states[h] = mm(p_mat[c, h], s_b) + b_mat[c, h]
        o_ref[c * c_len:(c + 1) * c_len, :] = jnp.concatenate(outs, axis=1)
    for h in range(n_heads):
        s_scr[h] = states[h]


def _gdn_core_call(prep, batch, seq_len, n_heads, c_len, ts=512):
    t, width = prep[0].shape
    n_seq_tiles = seq_len // ts
    spec = pl.BlockSpec((ts, width), lambda b, s: (b * n_seq_tiles + s, 0))
    return pl.pallas_call(
        functools.partial(_gdn_core_kernel, ts=ts, n_heads=n_heads, c_len=c_len),
        grid=(batch, n_seq_tiles),
        in_specs=[spec] * 7 + [pl.BlockSpec((ts, LANES), lambda b, s: (b * n_seq_tiles + s, 0)),
                               pl.BlockSpec((8, ts), lambda b, s: (0, b * n_seq_tiles + s))],
        out_specs=spec,
        out_shape=jax.ShapeDtypeStruct((t, width), F32),
        scratch_shapes=[pltpu.VMEM((n_heads, HEAD_DIM, HEAD_DIM), F32)],
        compiler_params=_cparams("arbitrary", "arbitrary"),
        name="gdn_core",
    )(*prep)


def _merge_kernel(x_ref, ysb_ref, og_ref, z_ref, g0_ref, g1_ref, mod_ref, onw_ref, n2_ref, bd_ref,
                  wsb_ref, wgdn_ref, wo_ref, wq_ref, x1_o, h2t_o, pq_o):
    og = og_ref[...]
    ygdn = og * lax.rsqrt(_group_sum_sq(og, bd_ref[...]) + EPS) * onw_ref[...] * _silu(z_ref[...].astype(F32))
    m = (_sigmoid(g0_ref[...].astype(F32)) * jnp.dot(ysb_ref[...].astype(BF16), wsb_ref[...],
                                                      preferred_element_type=F32)
         + _sigmoid(g1_ref[...].astype(F32)) * jnp.dot(ygdn.astype(BF16), wgdn_ref[...], preferred_element_type=F32))
    gate1 = mod_ref[0, 2:3, :]
    shift2 = mod_ref[0, 3:4, :]
    scale2 = mod_ref[0, 4:5, :]
    x1 = x_ref[...] + gate1 * jnp.dot(m.astype(BF16), wo_ref[...], preferred_element_type=F32)
    x1_o[...] = x1
    ms2 = jnp.mean(x1 * x1, axis=-1, keepdims=True)
    h2 = x1 * lax.rsqrt(ms2 + EPS) * n2_ref[...] * (1.0 + scale2) + shift2
    h2t_o[...] = h2.T.astype(BF16)
    pq_o[...] = jnp.dot(h2.astype(BF16), wq_ref[...], preferred_element_type=F32)


def _merge_call(x2d, ysb, ogdn, proj, mod, onw, n2w, wsb, wgdn, wo, wq, seq_len, col_g0, col_g1, col_z, tm=512):
    t, d = x2d.shape
    width = ysb.shape[1]
    nq = wq.shape[1]
    bd = _block_diag_ones(2 * LANES, HEAD_DIM, 1.0 / HEAD_DIM, BF16)
    tiles_per_seq = seq_len // tm
    full = lambda a: pl.BlockSpec(a.shape, lambda i: (0,) * a.ndim)
    once = lambda a: pl.BlockSpec(a.shape, lambda i: (0,) * a.ndim, pipeline_mode=pl.Buffered(1))
    return pl.pallas_call(
        _merge_kernel,
        grid=(t // tm,),
        in_specs=[pl.BlockSpec((tm, d), lambda i: (i, 0)),
                  pl.BlockSpec((tm, width), lambda i: (i, 0)),
                  pl.BlockSpec((tm, width), lambda i: (i, 0)),
                  pl.BlockSpec((tm, width), lambda i: (i, col_z)),
                  pl.BlockSpec((tm, d), lambda i: (i, col_g0)),
                  pl.BlockSpec((tm, d), lambda i: (i, col_g1)),
                  pl.BlockSpec((1, 6, d), lambda i: (i // tiles_per_seq, 0, 0)),
                  full(onw), full(n2w), full(bd), once(wsb), once(wgdn), once(wo), once(wq)],
        out_specs=[pl.BlockSpec((tm, d), lambda i: (i, 0)),
                   pl.BlockSpec((d, tm), lambda i: (0, i)),
                   pl.BlockSpec((tm, nq), lambda i: (i, 0))],
        out_shape=[jax.ShapeDtypeStruct((t, d), F32),
                   jax.ShapeDtypeStruct((d, t), BF16),
                   jax.ShapeDtypeStruct((t, nq), F32)],
        compiler_params=_cparams("arbitrary"),
        name="merge_proj",
    )(x2d, ysb, ogdn, proj, proj, proj, mod, onw, n2w, bd, wsb, wgdn, wo, wq)


def _extract_topk(s, k, break_ties):
    n = s.shape[0]
    if not break_ties:
        lowest_bits = -8388609
        vals = []
        for r in range(k):
            m = jnp.max(s, axis=0, keepdims=True)
            marker = lax.bitcast_convert_type(jnp.int32(lowest_bits - r), F32)
            s = jnp.where(s == m, marker, s)
            vals.append(m)
        took = jnp.int32(lowest_bits) - lax.bitcast_convert_type(s, jnp.int32)
        rank = jnp.where((took >= 0) & (took < k), took.astype(F32), NOT_RANKED)
        return vals, rank
    iota = lax.broadcasted_iota(jnp.int32, s.shape, 0).astype(F32)
    rank = jnp.full(s.shape, NOT_RANKED, F32)
    vals = []
    for r in range(k):
        m = jnp.max(s, axis=0, keepdims=True)
        hit = iota == jnp.min(jnp.where(s == m, iota, float(n)), axis=0, keepdims=True)
        rank = jnp.where(hit, float(r), rank)
        s = jnp.where(hit, -jnp.inf, s)
        vals.append(m)
    return vals, rank


def _candidate_tables(k):
    pairs = [(a, b) for a in range(k) for b in range(k) if (a + 1) * (b + 1) <= k]
    n_pad = -(-len(pairs) // 8) * 8
    sel_a = jnp.zeros((n_pad, k), F32).at[jnp.arange(len(pairs)), jnp.array([a for a, _ in pairs])].set(1.0)
    sel_b = jnp.zeros((n_pad, k), F32).at[jnp.arange(len(pairs)), jnp.array([b for _, b in pairs])].set(1.0)
    return sel_a, sel_b, len(pairs)


def _bf16_pair_words(x):
    bits = lax.bitcast_convert_type(x.astype(BF16).astype(F32), jnp.uint32)
    return bits | (bits >> 16)


def _route_kernel(pq_ref, keys_ref, sela_ref, selb_ref, rank2_o, e2_o, cnt1_o, w1_o, *, n_cand):
    k = PEER_TOPK
    hp, _, _, half = keys_ref.shape
    tt = pq_ref.shape[0]
    part_scores = [[lax.dot_general(keys_ref[hh, part], pq_ref[:, (2 * hh + part) * half:(2 * hh + part + 1) * half],
                                    NT_DIMS, preferred_element_type=F32, precision=HI) for hh in range(hp)]
                   for part in range(2)]
    sel_a = sela_ref[...]

    def route(break_ties, heads):
        w = len(heads) * tt
        iota_k = lax.broadcasted_iota(jnp.int32, (k, w), 0).astype(F32)
        s_all = jnp.concatenate([part_scores[part][hh] for part in range(2) for hh in heads], axis=1)
        scores = (s_all[:, :w], s_all[:, w:])
        vals, rank = _extract_topk(s_all, k, break_ties)
        top = jnp.concatenate(vals, axis=0)
        top1, top2 = top[:, :w], top[:, w:]
        v1, v2 = [v[:, :w] for v in vals], [v[:, w:] for v in vals]
        rank1, rank2 = rank[:, :w], rank[:, w:]
        cand = (jnp.dot(sel_a, top1, preferred_element_type=F32, precision=HI)
                + jnp.dot(selb_ref[...], top2, preferred_element_type=F32, precision=HI))
        cand_row = lax.broadcasted_iota(jnp.int32, cand.shape, 0)
        _, cand_rank = _extract_topk(jnp.where(cand_row < n_cand, cand, -jnp.inf), k, break_ties)
        chosen = (cand_rank < float(k)).astype(BF16)
        count = lax.dot_general(sel_a.astype(BF16), chosen, TN_DIMS, preferred_element_type=F32)
        e1 = jnp.exp(top1 - v1[0])
        e2 = jnp.exp(top2 - v2[0])
        z = jnp.zeros_like(v1[0])
        for a in range(k):
            z = z + e1[a:a + 1] * jnp.sum(jnp.where(iota_k < count[a:a + 1], e2, 0.0), axis=0, keepdims=True)
        inv_z = 1.0 / z
        cnt1 = jnp.zeros_like(rank1)
        for a in range(k):
            cnt1 = jnp.where(rank1 == float(a), count[a:a + 1], cnt1)
        e2_all = jnp.exp(scores[1] - v2[0]).astype(BF16)
        w1_all = jnp.where(rank1 < float(k), jnp.exp(scores[0] - v1[0]) * inv_z, 0.0)
        cnt1_words = _bf16_pair_words(cnt1)
        w1_words = _bf16_pair_words(w1_all)
        n_ranked = jnp.sum((rank < float(k)).astype(F32), axis=0, keepdims=True)
        n_chosen = jnp.sum((cand_rank < float(k)).astype(F32), axis=0, keepdims=True)
        off_by = jnp.maximum(jnp.maximum(jnp.abs(n_ranked[:, :w] - k), jnp.abs(n_ranked[:, w:] - k)),
                             jnp.abs(n_chosen - k))
        excess = []
        for pos, hh in enumerate(heads):
            lanes = slice(pos * tt, (pos + 1) * tt)
            rank2_o[hh] = rank2[:, lanes].astype(BF16).reshape(rank2_o.shape[1:])
            e2_o[hh] = e2_all[:, lanes].reshape(e2_o.shape[1:])
            cnt1_o[hh] = cnt1_words[:, lanes]
            w1_o[hh] = w1_words[:, lanes]
            excess.append(jnp.max(off_by[:, lanes]))
        return excess

    excess = route(False, list(range(hp)))
    for hh in range(hp):
        @pl.when(excess[hh] > 0.0)
        def _(hh=hh):
            route(True, [hh])


def _route_call(pq, sub_keys, tt=256, hp=4):
    t = pq.shape[0]
    n_heads, _, n_keys, half = sub_keys.shape
    sel_a, sel_b, n_cand = _candidate_tables(PEER_TOPK)
    out_spec = pl.BlockSpec((hp, n_keys, tt), lambda i, h: (h, 0, i))
    shape = lambda dt: jax.ShapeDtypeStruct((n_heads, n_keys, t), dt)
    tiled_spec = pl.BlockSpec((hp, n_keys // BF16_TILE_ROWS, BF16_TILE_ROWS, tt), lambda i, h: (h, 0, 0, i))
    tiled_shape = jax.ShapeDtypeStruct((n_heads, n_keys // BF16_TILE_ROWS, BF16_TILE_ROWS, t), BF16)
    return pl.pallas_call(
        functools.partial(_route_kernel, n_cand=n_cand),
        grid=(t // tt, n_heads // hp),
        in_specs=[pl.BlockSpec((tt, hp * 2 * half), lambda i, h: (i, h)),
                  pl.BlockSpec((hp, 2, n_keys, half), lambda i, h: (h, 0, 0, 0)),
                  pl.BlockSpec(sel_a.shape, lambda i, h: (0, 0)),
                  pl.BlockSpec(sel_b.shape, lambda i, h: (0, 0))],
        out_specs=[tiled_spec, tiled_spec, out_spec, out_spec],
        out_shape=[tiled_shape, tiled_shape, shape(jnp.uint32), shape(jnp.uint32)],
        compiler_params=_cparams("arbitrary", "arbitrary"),
        name="peer_route",
    )(pq, sub_keys, sel_a, sel_b)


def _peer_kernel(h2t_ref, u_ref, vt_ref, rank2_ref, e2_ref, cnt1_ref, w1_ref, x1_ref, mod_ref,
                 o_ref, act_even, act_odd, acc_scr, *, n_heads, n_keys, ec, group, n_chunks, n_items):
    s = pl.program_id(0)
    chunk = jnp.clip(s - 1, 0, n_items - 1) % n_chunks

    @pl.when(s == 0)
    def _():
        act_odd[...] = jnp.zeros_like(act_odd)

    @pl.when((chunk == 0) | (s == 0))
    def _():
        acc_scr[...] = jnp.zeros_like(acc_scr)

    tt = h2t_ref.shape[1]
    zero = jnp.zeros((), BF16)
    n_groups = ec // group
    subs_per_group = group // n_keys

    def coef_of(grp, act):
        coefs = []
        for s_loc in range(subs_per_group):
            sub = grp * subs_per_group + s_loc
            gate = None
            for h in range(n_heads):
                cnt = pltpu.bitcast(jnp.broadcast_to(cnt1_ref[h, sub:sub + 1, :], (8, tt)), BF16)
                w1 = pltpu.bitcast(jnp.broadcast_to(w1_ref[h, sub:sub + 1, :], (8, tt)), BF16)
                term = jnp.where(rank2_ref[h] < cnt[None], e2_ref[h], zero) * w1[None]
                gate = term if gate is None else gate + term
            a = act[s_loc * n_keys:(s_loc + 1) * n_keys, :].astype(BF16)
            gelu = (0.5 * a) * (1.0 + lax.erf(a * (2.0 ** -0.5)))
            coefs.append(gate.reshape(n_keys, tt) * gelu)
        return jnp.concatenate(coefs, axis=0)

    def stages(act_w, act_r):
        total = acc_scr[...]
        half_t = tt // 2
        new_parts = []
        for grp in range(n_groups):
            rows = slice(grp * group, (grp + 1) * group)
            coef = coef_of(grp, act_r[rows, :])
            if grp < 2:
                cols = slice(grp * half_t, (grp + 1) * half_t)
                new_parts.append(jnp.dot(u_ref[...], h2t_ref[:, cols], preferred_element_type=F32))
            total = total + jnp.dot(vt_ref[:, rows], coef, preferred_element_type=F32)
        acc_scr[...] = total
        act_w[...] = jnp.concatenate(new_parts, axis=1)

    @pl.when(s % 2 == 0)
    def _():
        stages(act_even, act_odd)

    @pl.when(s % 2 == 1)
    def _():
        stages(act_odd, act_even)

    @pl.when((chunk == n_chunks - 1) & (s >= 1))
    def _():
        gate2 = mod_ref[0, 5:6, :]
        o_ref[...] = x1_ref[...] + gate2 * acc_scr[...].T


def _peer_call(h2t, u_b, vt_b, rank2, e2, cnt1, w1, x1, mod, seq_len, tt=512, ec=2048, group=1024):
    d, t = h2t.shape
    n_exp = u_b.shape[0]
    n_heads, n_key_tiles, tile_rows, _ = rank2.shape
    n_keys = n_key_tiles * tile_rows
    tiles_per_seq = seq_len // tt
    n_chunks = n_exp // ec
    n_items = (t // tt) * n_chunks
    first = lambda s: jnp.minimum(s, n_items - 1)
    second = lambda s: jnp.clip(s - 1, 0, n_items - 1)
    route_spec = pl.BlockSpec((n_heads, n_key_tiles, tile_rows, tt), lambda s: (0, 0, 0, second(s) // n_chunks))
    row_spec = pl.BlockSpec((n_heads, ec // n_keys, tt), lambda s: (0, second(s) % n_chunks, second(s) // n_chunks))
    return pl.pallas_call(
        functools.partial(_peer_kernel, n_heads=n_heads, n_keys=n_keys, ec=ec, group=group,
                          n_chunks=n_chunks, n_items=n_items),
        grid=(n_items + 1,),
        in_specs=[pl.BlockSpec((d, tt), lambda s: (0, first(s) // n_chunks)),
                  pl.BlockSpec((ec, d), lambda s: (first(s) % n_chunks, 0)),
                  pl.BlockSpec((d, ec), lambda s: (0, second(s) % n_chunks)),
                  route_spec, route_spec, row_spec, row_spec,
                  pl.BlockSpec((tt, d), lambda s: (second(s) // n_chunks, 0)),
                  pl.BlockSpec((1, 6, d), lambda s: (second(s) // n_chunks // tiles_per_seq, 0, 0))],
        out_specs=pl.BlockSpec((tt, d), lambda s: (second(s) // n_chunks, 0)),
        out_shape=jax.ShapeDtypeStruct((t, d), F32),
        scratch_shapes=[pltpu.VMEM((ec, tt), F32), pltpu.VMEM((ec, tt), F32), pltpu.VMEM((d, tt), F32)],
        compiler_params=_cparams("arbitrary"),
        name="peer_experts",
    )(h2t, u_b, vt_b, rank2, e2, cnt1, w1, x1, mod)


def _pack_in_proj(w_in, sb_w, gdn_qk_w, gdn_v_w, n_gdn_heads, d_model, tn):
    o_sbq, o_sbk, o_sbv = 0, sb_w, 2 * sb_w
    o_gdn = 3 * sb_w
    conv_w = 2 * gdn_qk_w + gdn_v_w
    o_a = o_gdn + conv_w
    o_b = o_a + n_gdn_heads
    o_z = o_b + n_gdn_heads
    o_gate = o_z + gdn_v_w
    pad = (-w_in.shape[1]) % tn
    packed = jnp.concatenate([
        w_in[:, o_sbq:o_a],
        w_in[:, o_gate:o_gate + 2 * d_model],
        w_in[:, o_z:o_z + gdn_v_w],
        w_in[:, o_a:o_z],
        jnp.zeros((w_in.shape[0], pad), w_in.dtype)], axis=1).astype(BF16)
    return packed


def _block(x2d, c, w_ada, b_ada, norm1_w, w_in, sb_q_norm_w, sb_k_norm_w, gdn_conv_w, gdn_A_log,
           gdn_dt_bias, gdn_o_norm_w, w_proj_sb, w_proj_gdn, w_o, norm2_w, peer_w_q, peer_sub_keys,
           peer_u, peer_v, batch, seq_len):
    t, d = x2d.shape
    sb_w = w_proj_sb.shape[0]
    gdn_v_w = w_proj_gdn.shape[0]
    gdn_qk_w = (gdn_conv_w.shape[1] - gdn_v_w) // 2
    n_sb_heads = sb_w // HEAD_DIM
    n_gdn_heads = gdn_v_w // HEAD_DIM
    assert gdn_qk_w == gdn_v_w == sb_w and d % (4 * LANES) == 0

    c_pad = jnp.zeros((8, d), F32).at[:batch].set(c)
    mod = _ada_call(c_pad, w_ada, b_ada)[:batch].reshape(batch, 6, d)

    tn = sb_w
    w_packed = _pack_in_proj(w_in, sb_w, gdn_qk_w, gdn_v_w, n_gdn_heads, d, tn)
    heads_per_tile = tn // HEAD_DIM
    q_tiles = sb_w // tn
    qkw = jnp.concatenate([jnp.tile(sb_q_norm_w * (HEAD_DIM ** -0.5 * LOG2E), (q_tiles, heads_per_tile)),
                           jnp.tile(sb_k_norm_w, (q_tiles, heads_per_tile))], axis=0).reshape(2 * q_tiles, 1, tn)
    proj, ab_logits = _inproj_call(x2d, mod, norm1_w.reshape(1, d), w_packed, qkw, seq_len, tn=tn)

    ysb = _sb_call(proj, batch, seq_len, n_sb_heads // 2)

    col_gdn = 3 * sb_w // gdn_v_w
    col_gate = (3 * sb_w + 3 * gdn_v_w) // d
    col_z = (3 * sb_w + 3 * gdn_v_w + 2 * d) // gdn_v_w
    prep = _gdn_prep_call(proj, ab_logits, gdn_conv_w, gdn_A_log, gdn_dt_bias, seq_len, n_gdn_heads,
                          (col_gdn, col_gdn + 1, col_gdn + 2), GDN_BLOCK)
    ogdn = _gdn_core_call(prep, batch, seq_len, n_gdn_heads, GDN_BLOCK)

    x1, h2, pq = _merge_call(
        x2d, ysb, ogdn, proj, mod, jnp.tile(gdn_o_norm_w, n_gdn_heads).reshape(1, gdn_v_w), norm2_w.reshape(1, d),
        w_proj_sb.astype(BF16), w_proj_gdn.astype(BF16), w_o.astype(BF16), peer_w_q.astype(BF16),
        seq_len, col_gate, col_gate + 1, col_z)

    rank2, e2, cnt1, w1 = _route_call(pq, peer_sub_keys)
    return _peer_call(h2, peer_u.astype(BF16), peer_v.T.astype(BF16), rank2, e2, cnt1, w1, x1, mod, seq_len)


def kernel(x, c, w_ada, b_ada, norm1_w, w_in, sb_q_norm_w, sb_k_norm_w, gdn_conv_w, gdn_A_log, gdn_dt_bias,
           gdn_o_norm_w, w_proj_sb, w_proj_gdn, w_o, norm2_w, peer_w_q, peer_sub_keys, peer_u, peer_v):
    batch, seq_len, d = x.shape
    x2d = x.reshape(batch * seq_len, d)
    params = (w_ada, b_ada, norm1_w, w_in, sb_q_norm_w, sb_k_norm_w, gdn_conv_w, gdn_A_log, gdn_dt_bias,
              gdn_o_norm_w, w_proj_sb, w_proj_gdn, w_o, norm2_w, peer_w_q, peer_sub_keys, peer_u, peer_v)
    depth = w_ada.shape[0]
    for l in range(depth):
        layer = [p.reshape(p.shape[1:]) if depth == 1 else p[l] for p in params]
        x2d = _block(x2d, c, *layer, batch, seq_len)
    return x2d.reshape(batch, seq_len, d)
```

```python
import functools

import jax
import jax.numpy as jnp
from jax import lax
from jax.experimental import pallas as pl
from jax.experimental.pallas import tpu as pltpu

F32 = jnp.float32
BF16 = jnp.bfloat16
HI = lax.Precision.HIGHEST
EPS = 1e-6
LOG2E = 1.4426950408889634

LANES = 128
BF16_TILE_ROWS = 16
HEAD_DIM = 64
GDN_BLOCK = 128
GDN_CONV = 4
PEER_TOPK = 16
NOT_RANKED = 99.0
VMEM_LIMIT = 56 * 1024 * 1024

NT_DIMS = (((1,), (1,)), ((), ()))
TN_DIMS = (((0,), (0,)), ((), ()))


def _cparams(*sem):
    return pltpu.CompilerParams(dimension_semantics=sem, vmem_limit_bytes=VMEM_LIMIT)


def _sigmoid(x):
    return 1.0 / (1.0 + jnp.exp(-x))


def _silu(x):
    return x * _sigmoid(x)


def _softplus(x):
    return jnp.maximum(x, 0.0) + jnp.log(1.0 + jnp.exp(-jnp.abs(x)))


def _block_diag_ones(n, group, value=1.0, dtype=F32):
    r = jnp.arange(n) // group
    return jnp.where(r[:, None] == r[None, :], value, 0.0).astype(dtype)


def _bf16_pieces(x):
    hi = x.astype(BF16)
    rest = x - hi.astype(F32)
    mid = rest.astype(BF16)
    return hi, mid, (rest - mid.astype(F32)).astype(BF16)


def _group_sum_sq(x, bd):
    slab = bd.shape[0]
    outs = []
    for c0 in range(0, x.shape[1], slab):
        sq = x[:, c0:c0 + slab] * x[:, c0:c0 + slab]
        hi = sq.astype(BF16)
        lo = (sq - hi.astype(F32)).astype(BF16)
        outs.append(jnp.dot(hi, bd, preferred_element_type=F32) + jnp.dot(lo, bd, preferred_element_type=F32))
    return outs[0] if len(outs) == 1 else jnp.concatenate(outs, axis=1)


def _ada_kernel(c_ref, w_ref, b_ref, o_ref):
    c = c_ref[...]
    o_ref[...] = jnp.dot(_silu(c), w_ref[...], preferred_element_type=F32, precision=HI) + b_ref[...]


def _ada_call(c_pad, w_ada, b_ada):
    rows, d = c_pad.shape
    n = w_ada.shape[1]
    tn = 2048
    return pl.pallas_call(
        _ada_kernel,
        grid=(n // tn,),
        in_specs=[pl.BlockSpec((rows, d), lambda j: (0, 0)),
                  pl.BlockSpec((d, tn), lambda j: (0, j)),
                  pl.BlockSpec((1, tn), lambda j: (0, j))],
        out_specs=pl.BlockSpec((rows, tn), lambda j: (0, j)),
        out_shape=jax.ShapeDtypeStruct((rows, n), F32),
        compiler_params=_cparams("arbitrary"),
        name="ada_mod",
    )(c_pad, w_ada, b_ada.reshape(1, n))


def _inproj_kernel(x_ref, mod_ref, n1_ref, w_ref, qkw_ref, bd_ref, o_ref, tail_ref, h_scr, *, n_qk_tiles):
    j = pl.program_id(1)
    last = pl.num_programs(1) - 1

    @pl.when(j == 0)
    def _():
        x = x_ref[...]
        ms = jnp.mean(x * x, axis=-1, keepdims=True)
        y = x * lax.rsqrt(ms + EPS) * n1_ref[...]
        shift = mod_ref[0, 0:1, :]
        scale = mod_ref[0, 1:2, :]
        h_scr[...] = (y * (1.0 + scale) + shift).astype(BF16)

    acc = jnp.dot(h_scr[...], w_ref[...], preferred_element_type=F32)

    @pl.when(j < n_qk_tiles)
    def _():
        o_ref[...] = (acc * lax.rsqrt(_group_sum_sq(acc, bd_ref[...]) + EPS) * qkw_ref[0]).astype(BF16)

    @pl.when((j >= n_qk_tiles) & (j < last))
    def _():
        o_ref[...] = acc.astype(BF16)

    @pl.when(j == last)
    def _():
        tail_ref[...] = acc


def _inproj_call(x2d, mod, n1w, w_packed, qkw, seq_len, tm=2048, tn=512):
    t, d = x2d.shape
    n = w_packed.shape[1]
    n_tiles = n // tn
    n_qk_tiles = qkw.shape[0]
    assert seq_len % tm == 0, "a row tile must not straddle two sequences (per-sequence modulation)"
    bd = _block_diag_ones(2 * LANES, HEAD_DIM, 1.0 / HEAD_DIM, BF16)
    tiles_per_seq = seq_len // tm
    return pl.pallas_call(
        functools.partial(_inproj_kernel, n_qk_tiles=n_qk_tiles),
        grid=(t // tm, n // tn),
        in_specs=[pl.BlockSpec((tm, d), lambda i, j: (i, 0)),
                  pl.BlockSpec((1, 6, d), lambda i, j: (i // tiles_per_seq, 0, 0)),
                  pl.BlockSpec((1, d), lambda i, j: (0, 0)),
                  pl.BlockSpec((d, tn), lambda i, j: (0, j)),
                  pl.BlockSpec((1, 1, tn), lambda i, j: (jnp.minimum(j, n_qk_tiles - 1), 0, 0)),
                  pl.BlockSpec(bd.shape, lambda i, j: (0, 0))],
        out_specs=[pl.BlockSpec((tm, tn), lambda i, j: (i, jnp.minimum(j, n_tiles - 2))),
                   pl.BlockSpec((tm, tn), lambda i, j: (i, 0))],
        out_shape=[jax.ShapeDtypeStruct((t, n - tn), BF16), jax.ShapeDtypeStruct((t, tn), F32)],
        scratch_shapes=[pltpu.VMEM((tm, d), BF16)],
        compiler_params=_cparams("arbitrary", "arbitrary"),
        name="in_proj",
    )(x2d, mod, n1w, w_packed, qkw, bd)


def _sb_kernel(q_ref, k_ref, v_ref, tri_ref, o_ref, acc_ref, r_ref, *, tq):
    i = pl.program_id(2)
    half = tri_ref.shape[0]
    lane = lax.broadcasted_iota(jnp.int32, (1, LANES), 1)
    first_half = lane < HEAD_DIM
    q = q_ref[...].astype(F32)
    q_heads = (jnp.where(first_half, q, 0.0).astype(BF16), jnp.where(first_half, 0.0, q).astype(BF16))
    acc_ref[...] = jnp.zeros_like(acc_ref)
    r_ref[...] = jnp.zeros_like(r_ref)

    def run(blocks, masked):
        items = [(j, hd) for j in blocks for hd in range(2)]
        st = [dict() for _ in items]
        if masked:
            causal = (lax.broadcasted_iota(jnp.int32, (tq, tq), 1) < lax.broadcasted_iota(jnp.int32, (tq, tq), 0))

        def rows_of(n):
            return pl.ds(pl.multiple_of(items[n][0] * tq, tq), tq)

        def logits(n):
            kb = k_ref[rows_of(n), :].astype(BF16)
            st[n]["y"] = lax.dot_general(q_heads[items[n][1]], kb, NT_DIMS, preferred_element_type=F32)

        def softplus(n):
            y = st[n]["y"]
            yb = y.astype(BF16)
            log_term = jnp.log(1.0 + jnp.exp2(-jnp.abs(yb))) * LOG2E
            spb = jnp.maximum(yb, 0.0) + log_term
            if masked:
                spb = jnp.where(causal, spb, jnp.zeros((), BF16))
            st[n]["log_beta"] = jnp.minimum(y, 0.0) - log_term.astype(F32)
            st[n]["spb"] = spb
            st[n]["sp_first"] = (spb[:, 0:1].astype(F32), spb[:, half:half + 1].astype(F32))

        def later_sums(n):
            spb = st[n]["spb"]
            first_lo, first_hi = st[n]["sp_first"]
            later_hi = jnp.dot(spb[:, half:], tri_ref[...], preferred_element_type=F32)
            total_hi = later_hi[:, 0:1] + first_hi
            later_lo = jnp.dot(spb[:, :half], tri_ref[...], preferred_element_type=F32) + total_hi
            st[n]["later"] = jnp.concatenate([later_lo, later_hi], axis=1)
            st[n]["total"] = later_lo[:, 0:1] + first_lo

        def weights(n):
            a = jnp.exp2(st[n]["log_beta"] - st[n]["later"])
            if masked:
                a = jnp.where(causal, a, 0.0)
            st[n]["a"] = a.astype(BF16)

        def accumulate(n):
            hd = items[n][1]
            pv = jnp.dot(st[n]["a"], v_ref[rows_of(n), :].astype(BF16), preferred_element_type=F32)
            r = r_ref[hd]
            acc_ref[hd] += jnp.exp2(-r) * pv
            r_ref[hd] = r + st[n]["total"]

        stages = (logits, softplus, later_sums, weights, accumulate)
        elementwise = (softplus, weights)
        for wave in range(len(items) + len(stages) - 1):
            todo = [(stages[wave - n], n) for n in range(len(items)) if 0 <= wave - n < len(stages)]
            for stage, n in [x for x in todo if x[0] in elementwise] + [x for x in todo if x[0] not in elementwise]:
                stage(n)

    run([i], True)

    per_trip = 4

    def body(it, carry):
        j = i - 1 - per_trip * it
        run([j - d for d in range(per_trip)], False)
        return carry

    lax.fori_loop(0, i // per_trip, body, 0)

    left = i % per_trip

    @pl.when(left >= 2)
    def _():
        run([left - 1, left - 2], False)

    @pl.when(left % 2 == 1)
    def _():
        run([0], False)

    o_ref[...] = jnp.where(first_half, acc_ref[0], acc_ref[1])


def _sb_call(proj, batch, seq_len, n_pairs, tq=512):
    t = proj.shape[0]
    nq = seq_len // tq
    half = tq // 2
    kk = jnp.arange(half)
    later = (kk[:, None] > kk[None, :]).astype(BF16)
    return pl.pallas_call(
        functools.partial(_sb_kernel, tq=tq),
        grid=(batch, n_pairs, nq),
        in_specs=[pl.BlockSpec((tq, LANES), lambda b, p, i: (b * nq + i, p)),
                  pl.BlockSpec((seq_len, LANES), lambda b, p, i: (b, n_pairs + p)),
                  pl.BlockSpec((seq_len, LANES), lambda b, p, i: (b, 2 * n_pairs + p)),
                  pl.BlockSpec((half, half), lambda b, p, i: (0, 0))],
        out_specs=pl.BlockSpec((tq, LANES), lambda b, p, i: (b * nq + i, p)),
        out_shape=jax.ShapeDtypeStruct((t, n_pairs * LANES), F32),
        scratch_shapes=[pltpu.VMEM((2, tq, LANES), F32), pltpu.VMEM((2, tq, 1), F32)],
        compiler_params=_cparams("arbitrary", "arbitrary", "arbitrary"),
        name="sb_attention",
    )(proj, proj, proj, later)


def _gdn_prep_kernel(q_ref, k_ref, v_ref, qt_ref, kt_ref, vt_ref, ab_ref, cw_ref, alog_ref, dtb_ref,
                     ea_ref, eb_ref, lc_ref, jc_ref, bd_ref,
                     qn_o, qd_o, kn_o, kd_o, kb_o, kbg_o, vb_o, gcol_o, grow_o, scr, *, tm, tiles_per_seq):
    i = pl.program_id(0)
    keep_tail = (i % tiles_per_seq != 0).astype(F32)
    w = cw_ref[...]
    width = q_ref.shape[1]

    halo = qt_ref.shape[0]

    def conv_silu(cur_ref, tail_ref, col0):
        scr[0:halo, :] = tail_ref[...].astype(F32) * keep_tail
        scr[halo:, :] = cur_ref[...].astype(F32)
        y = jnp.zeros((tm, width), F32)
        for tap in range(GDN_CONV):
            off = halo - (GDN_CONV - 1) + tap
            y = y + scr[off:off + tm, :] * w[tap:tap + 1, col0:col0 + width]
        return _silu(y)

    bd = bd_ref[...]
    cq = conv_silu(q_ref, qt_ref, 0)
    qn = cq * lax.rsqrt(_group_sum_sq(cq, bd) + EPS) * (HEAD_DIM ** -0.5)
    ck = conv_silu(k_ref, kt_ref, width)
    kn = ck * lax.rsqrt(_group_sum_sq(ck, bd) + EPS)
    cv = conv_silu(v_ref, vt_ref, 2 * width)

    ab = ab_ref[...]
    g = -jnp.exp(alog_ref[...]) * _softplus(ab + dtb_ref[...])
    beta = _sigmoid(ab)
    g_parts = _bf16_pieces(g)
    lc, jc, ea = lc_ref[...], jc_ref[...], ea_ref[...]
    g_cum = sum(jnp.dot(lc, p, preferred_element_type=F32) for p in g_parts)
    g_tot = sum(jnp.dot(jc, p, preferred_element_type=F32) for p in g_parts)
    gx = sum(jnp.dot(p, ea, preferred_element_type=F32) for p in _bf16_pieces(g_cum))
    glx = sum(jnp.dot(p, ea, preferred_element_type=F32) for p in _bf16_pieces(g_tot))
    bx = sum(jnp.dot(p, eb_ref[...], preferred_element_type=F32) for p in _bf16_pieces(beta))

    e_g = jnp.exp(gx)
    kb = kn * bx
    qn_o[...] = qn
    qd_o[...] = qn * e_g
    kn_o[...] = kn
    kd_o[...] = kn * jnp.exp(glx - gx)
    kb_o[...] = kb
    kbg_o[...] = kb * e_g
    vb_o[...] = cv * bx
    g_heads = g_cum[:, :LANES]
    gcol_o[...] = g_heads
    grow_o[...] = g_heads.T[0:8, :]


def _gdn_prep_call(proj, ab_logits, conv_w, a_log, dt_bias, seq_len, n_heads, col_qkv, chunk, tm=512):
    t = proj.shape[0]
    assert n_heads <= 8 and tm % chunk == 0
    width = n_heads * HEAD_DIM
    abw = 2 * LANES
    tiles_per_seq = seq_len // tm
    halo = 16
    sub = tm // halo
    alog_pad = jnp.zeros((1, abw), F32).at[0, :n_heads].set(a_log)
    dtb_pad = jnp.zeros((1, abw), F32).at[0, :n_heads].set(dt_bias)
    head_of_lane = jnp.arange(width) // HEAD_DIM
    rows = jnp.arange(abw)
    ea = (rows[:, None] == head_of_lane[None, :]).astype(BF16)
    eb = (rows[:, None] == head_of_lane[None, :] + n_heads).astype(BF16)
    tok = jnp.arange(tm)
    same_chunk = (tok[:, None] // chunk) == (tok[None, :] // chunk)
    lc = (same_chunk & (tok[:, None] >= tok[None, :])).astype(BF16)
    jc = same_chunk.astype(BF16)
    bd = _block_diag_ones(2 * LANES, HEAD_DIM, 1.0, BF16)

    cur = lambda c: pl.BlockSpec((tm, width), lambda i: (i, c))
    tail = lambda c: pl.BlockSpec((halo, width), lambda i: (jnp.maximum(i * sub - 1, 0), c))
    full = lambda a: pl.BlockSpec(a.shape, lambda i: (0,) * a.ndim)
    cq, ck, cv = col_qkv
    out_spec = pl.BlockSpec((tm, width), lambda i: (i, 0))
    out_shape = jax.ShapeDtypeStruct((t, width), F32)
    return pl.pallas_call(
        functools.partial(_gdn_prep_kernel, tm=tm, tiles_per_seq=tiles_per_seq),
        grid=(t // tm,),
        in_specs=[cur(cq), cur(ck), cur(cv), tail(cq), tail(ck), tail(cv),
                  pl.BlockSpec((tm, abw), lambda i: (i, 0)),
                  full(conv_w), full(alog_pad), full(dtb_pad), full(ea), full(eb), full(lc), full(jc), full(bd)],
        out_specs=[out_spec] * 7 + [pl.BlockSpec((tm, LANES), lambda i: (i, 0)), pl.BlockSpec((8, tm), lambda i: (0, i))],
        out_shape=[out_shape] * 7 + [jax.ShapeDtypeStruct((t, LANES), F32), jax.ShapeDtypeStruct((8, t), F32)],
        scratch_shapes=[pltpu.VMEM((tm + halo, width), F32)],
        compiler_params=_cparams("arbitrary"),
        name="gdn_prep",
    )(proj, proj, proj, proj, proj, proj, ab_logits, conv_w, alog_pad, dtb_pad, ea, eb, lc, jc, bd)


def _gdn_core_kernel(qn_ref, qd_ref, kn_ref, kd_ref, kb_ref, kbg_ref, vb_ref, gcol_ref, grow_ref,
                     o_ref, s_scr, *, ts, n_heads, c_len):
    @pl.when(pl.program_id(1) == 0)
    def _():
        s_scr[...] = jnp.zeros_like(s_scr)

    row = lax.broadcasted_iota(jnp.int32, (c_len, c_len), 0)
    col = lax.broadcasted_iota(jnp.int32, (c_len, c_len), 1)
    strict = row > col
    incl = row >= col
    eye_state = (lax.broadcasted_iota(jnp.int32, (HEAD_DIM, HEAD_DIM), 0)
                 == lax.broadcasted_iota(jnp.int32, (HEAD_DIM, HEAD_DIM), 1))
    n_levels = (c_len - 1).bit_length()
    n_chunks = ts // c_len
    problems = [(c, h) for c in range(n_chunks) for h in range(n_heads)]

    def bf(x):
        return x.astype(BF16)

    def mm(a, b):
        return jnp.dot(a, b, preferred_element_type=F32)

    def head_tile(ref, c, h):
        pair, half = divmod(h, 2)
        tile = ref[c * c_len:(c + 1) * c_len, pair * LANES:(pair + 1) * LANES]
        return tile[:, half * HEAD_DIM:(half + 1) * HEAD_DIM]

    sibling = [((row >> k) ^ (col >> k)) == 1 for k in range(n_levels)]

    xs, ms, rs, qks = [], [], [], []
    for c, h in problems:
        g_col = gcol_ref[c * c_len:(c + 1) * c_len, h:h + 1]
        g_row = grow_ref[h:h + 1, c * c_len:(c + 1) * c_len]
        decay = jnp.where(incl, jnp.exp(g_col - g_row), 0.0)
        kn_b = bf(head_tile(kn_ref, c, h))
        kk = lax.dot_general(bf(head_tile(kb_ref, c, h)), kn_b, NT_DIMS, preferred_element_type=F32)
        x = jnp.where(strict, -(kk * decay), 0.0)
        xs.append(x)
        ms.append(jnp.where(row == col, 1.0, jnp.where(sibling[0], x, 0.0)))
        qks.append(bf(lax.dot_general(bf(head_tile(qn_ref, c, h)), kn_b, NT_DIMS, preferred_element_type=F32) * decay))
        rs.append(bf(jnp.concatenate([head_tile(vb_ref, c, h), head_tile(kbg_ref, c, h)], axis=1)))

    for k in range(1, n_levels):
        for idx in range(len(problems)):
            m_b = bf(ms[idx])
            left = mm(m_b, bf(jnp.where(sibling[k], xs[idx], 0.0)))
            ms[idx] = ms[idx] + mm(bf(left), m_b)

    q_eff, o_zero, p_mat, b_mat = {}, {}, {}, {}
    for idx, (c, h) in enumerate(problems):
        sol = bf(mm(bf(ms[idx]), rs[idx]))
        m1 = mm(qks[idx], sol)
        q_eff[c, h] = bf(head_tile(qd_ref, c, h) - m1[:, HEAD_DIM:])
        o_zero[c, h] = m1[:, :HEAD_DIM]
        m2 = lax.dot_general(bf(head_tile(kd_ref, c, h)), sol, TN_DIMS, preferred_element_type=F32)
        chunk_decay = jnp.exp(gcol_ref[(c + 1) * c_len - 1:(c + 1) * c_len, h:h + 1])
        p_mat[c, h] = bf(jnp.where(eye_state, chunk_decay, 0.0) - m2[:, HEAD_DIM:])
        b_mat[c, h] = m2[:, :HEAD_DIM]

    states = [s_scr[h] for h in range(n_heads)]
    for c in range(n_chunks):
        outs = []
        for h in range(n_heads):
            s_b = bf(states[h])
            outs.append(mm(q_eff[c, h], s_b) + o_zero[c, h])
            states[h] = mm(p_mat[c, h], s_b) + b_mat[c, h]
        o_ref[c * c_len:(c + 1) * c_len, :] = jnp.concatenate(outs, axis=1)
    for h in range(n_heads):
        s_scr[h] = states[h]


def _gdn_core_call(prep, batch, seq_len, n_heads, c_len, ts=512):
    t, width = prep[0].shape
    n_seq_tiles = seq_len // ts
    spec = pl.BlockSpec((ts, width), lambda b, s: (b * n_seq_tiles + s, 0))
    return pl.pallas_call(
        functools.partial(_gdn_core_kernel, ts=ts, n_heads=n_heads, c_len=c_len),
        grid=(batch, n_seq_tiles),
        in_specs=[spec] * 7 + [pl.BlockSpec((ts, LANES), lambda b, s: (b * n_seq_tiles + s, 0)),
                               pl.BlockSpec((8, ts), lambda b, s: (0, b * n_seq_tiles + s))],
        out_specs=spec,
        out_shape=jax.ShapeDtypeStruct((t, width), F32),
        scratch_shapes=[pltpu.VMEM((n_heads, HEAD_DIM, HEAD_DIM), F32)],
        compiler_params=_cparams("arbitrary", "arbitrary"),
        name="gdn_core",
    )(*prep)


def _merge_kernel(x_ref, ysb_ref, og_ref, z_ref, g0_ref, g1_ref, mod_ref, onw_ref, n2_ref, bd_ref,
                  wsb_ref, wgdn_ref, wo_ref, wq_ref, x1_o, h2t_o, pq_o):
    og = og_ref[...]
    ygdn = og * lax.rsqrt(_group_sum_sq(og, bd_ref[...]) + EPS) * onw_ref[...] * _silu(z_ref[...].astype(F32))
    m = (_sigmoid(g0_ref[...].astype(F32)) * jnp.dot(ysb_ref[...].astype(BF16), wsb_ref[...],
                                                      preferred_element_type=F32)
         + _sigmoid(g1_ref[...].astype(F32)) * jnp.dot(ygdn.astype(BF16), wgdn_ref[...], preferred_element_type=F32))
    gate1 = mod_ref[0, 2:3, :]
    shift2 = mod_ref[0, 3:4, :]
    scale2 = mod_ref[0, 4:5, :]
    x1 = x_ref[...] + gate1 * jnp.dot(m.astype(BF16), wo_ref[...], preferred_element_type=F32)
    x1_o[...] = x1
    ms2 = jnp.mean(x1 * x1, axis=-1, keepdims=True)
    h2 = x1 * lax.rsqrt(ms2 + EPS) * n2_ref[...] * (1.0 + scale2) + shift2
    h2t_o[...] = h2.T.astype(BF16)
    pq_o[...] = jnp.dot(h2.astype(BF16), wq_ref[...], preferred_element_type=F32)


def _merge_call(x2d, ysb, ogdn, proj, mod, onw, n2w, wsb, wgdn, wo, wq, seq_len, col_g0, col_g1, col_z, tm=512):
    t, d = x2d.shape
    width = ysb.shape[1]
    nq = wq.shape[1]
    bd = _block_diag_ones(2 * LANES, HEAD_DIM, 1.0 / HEAD_DIM, BF16)
    tiles_per_seq = seq_len // tm
    full = lambda a: pl.BlockSpec(a.shape, lambda i: (0,) * a.ndim)
    once = lambda a: pl.BlockSpec(a.shape, lambda i: (0,) * a.ndim, pipeline_mode=pl.Buffered(1))
    return pl.pallas_call(
        _merge_kernel,
        grid=(t // tm,),
        in_specs=[pl.BlockSpec((tm, d), lambda i: (i, 0)),
                  pl.BlockSpec((tm, width), lambda i: (i, 0)),
                  pl.BlockSpec((tm, width), lambda i: (i, 0)),
                  pl.BlockSpec((tm, width), lambda i: (i, col_z)),
                  pl.BlockSpec((tm, d), lambda i: (i, col_g0)),
                  pl.BlockSpec((tm, d), lambda i: (i, col_g1)),
                  pl.BlockSpec((1, 6, d), lambda i: (i // tiles_per_seq, 0, 0)),
                  full(onw), full(n2w), full(bd), once(wsb), once(wgdn), once(wo), once(wq)],
        out_specs=[pl.BlockSpec((tm, d), lambda i: (i, 0)),
                   pl.BlockSpec((d, tm), lambda i: (0, i)),
                   pl.BlockSpec((tm, nq), lambda i: (i, 0))],
        out_shape=[jax.ShapeDtypeStruct((t, d), F32),
                   jax.ShapeDtypeStruct((d, t), BF16),
                   jax.ShapeDtypeStruct((t, nq), F32)],
        compiler_params=_cparams("arbitrary"),
        name="merge_proj",
    )(x2d, ysb, ogdn, proj, proj, proj, mod, onw, n2w, bd, wsb, wgdn, wo, wq)


def _extract_topk(s, k, break_ties):
    n = s.shape[0]
    if not break_ties:
        lowest_bits = -8388609
        vals = []
        for r in range(k):
            m = jnp.max(s, axis=0, keepdims=True)
            marker = lax.bitcast_convert_type(jnp.int32(lowest_bits - r), F32)
            s = jnp.where(s == m, marker, s)
            vals.append(m)
        took = jnp.int32(lowest_bits) - lax.bitcast_convert_type(s, jnp.int32)
        rank = jnp.where((took >= 0) & (took < k), took.astype(F32), NOT_RANKED)
        return vals, rank
    iota = lax.broadcasted_iota(jnp.int32, s.shape, 0).astype(F32)
    rank = jnp.full(s.shape, NOT_RANKED, F32)
    vals = []
    for r in range(k):
        m = jnp.max(s, axis=0, keepdims=True)
        hit = iota == jnp.min(jnp.where(s == m, iota, float(n)), axis=0, keepdims=True)
        rank = jnp.where(hit, float(r), rank)
        s = jnp.where(hit, -jnp.inf, s)
        vals.append(m)
    return vals, rank


def _candidate_tables(k):
    pairs = [(a, b) for a in range(k) for b in range(k) if (a + 1) * (b + 1) <= k]
    n_pad = -(-len(pairs) // 8) * 8
    sel_a = jnp.zeros((n_pad, k), F32).at[jnp.arange(len(pairs)), jnp.array([a for a, _ in pairs])].set(1.0)
    sel_b = jnp.zeros((n_pad, k), F32).at[jnp.arange(len(pairs)), jnp.array([b for _, b in pairs])].set(1.0)
    return sel_a, sel_b, len(pairs)


def _bf16_pair_words(x):
    bits = lax.bitcast_convert_type(x.astype(BF16).astype(F32), jnp.uint32)
    return bits | (bits >> 16)


def _route_kernel(pq_ref, keys_ref, sela_ref, selb_ref, rank2_o, e2_o, cnt1_o, w1_o, *, n_cand):
    k = PEER_TOPK
    hp, _, _, half = keys_ref.shape
    tt = pq_ref.shape[0]
    part_scores = [[lax.dot_general(keys_ref[hh, part], pq_ref[:, (2 * hh + part) * half:(2 * hh + part + 1) * half],
                                    NT_DIMS, preferred_element_type=F32, precision=HI) for hh in range(hp)]
                   for part in range(2)]
    sel_a = sela_ref[...]

    def route(break_ties, heads):
        w = len(heads) * tt
        iota_k = lax.broadcasted_iota(jnp.int32, (k, w), 0).astype(F32)
        s_all = jnp.concatenate([part_scores[part][hh] for part in range(2) for hh in heads], axis=1)
        scores = (s_all[:, :w], s_all[:, w:])
        vals, rank = _extract_topk(s_all, k, break_ties)
        top = jnp.concatenate(vals, axis=0)
        top1, top2 = top[:, :w], top[:, w:]
        v1, v2 = [v[:, :w] for v in vals], [v[:, w:] for v in vals]
        rank1, rank2 = rank[:, :w], rank[:, w:]
        cand = (jnp.dot(sel_a, top1, preferred_element_type=F32, precision=HI)
                + jnp.dot(selb_ref[...], top2, preferred_element_type=F32, precision=HI))
        cand_row = lax.broadcasted_iota(jnp.int32, cand.shape, 0)
        _, cand_rank = _extract_topk(jnp.where(cand_row < n_cand, cand, -jnp.inf), k, break_ties)
        chosen = (cand_rank < float(k)).astype(BF16)
        count = lax.dot_general(sel_a.astype(BF16), chosen, TN_DIMS, preferred_element_type=F32)
        e1 = jnp.exp(top1 - v1[0])
        e2 = jnp.exp(top2 - v2[0])
        z = jnp.zeros_like(v1[0])
        for a in range(k):
            z = z + e1[a:a + 1] * jnp.sum(jnp.where(iota_k < count[a:a + 1], e2, 0.0), axis=0, keepdims=True)
        inv_z = 1.0 / z
        cnt1 = jnp.zeros_like(rank1)
        for a in range(k):
            cnt1 = jnp.where(rank1 == float(a), count[a:a + 1], cnt1)
        e2_all = jnp.exp(scores[1] - v2[0]).astype(BF16)
        w1_all = jnp.where(rank1 < float(k), jnp.exp(scores[0] - v1[0]) * inv_z, 0.0)
        cnt1_words = _bf16_pair_words(cnt1)
        w1_words = _bf16_pair_words(w1_all)
        n_ranked = jnp.sum((rank < float(k)).astype(F32), axis=0, keepdims=True)
        n_chosen = jnp.sum((cand_rank < float(k)).astype(F32), axis=0, keepdims=True)
        off_by = jnp.maximum(jnp.maximum(jnp.abs(n_ranked[:, :w] - k), jnp.abs(n_ranked[:, w:] - k)),
                             jnp.abs(n_chosen - k))
        excess = []
        for pos, hh in enumerate(heads):
            lanes = slice(pos * tt, (pos + 1) * tt)
            rank2_o[hh] = rank2[:, lanes].astype(BF16).reshape(rank2_o.shape[1:])
            e2_o[hh] = e2_all[:, lanes].reshape(e2_o.shape[1:])
            cnt1_o[hh] = cnt1_words[:, lanes]
            w1_o[hh] = w1_words[:, lanes]
            excess.append(jnp.max(off_by[:, lanes]))
        return excess

    excess = route(False, list(range(hp)))
    for hh in range(hp):
        @pl.when(excess[hh] > 0.0)
        def _(hh=hh):
            route(True, [hh])


def _route_call(pq, sub_keys, tt=256, hp=4):
    t = pq.shape[0]
    n_heads, _, n_keys, half = sub_keys.shape
    sel_a, sel_b, n_cand = _candidate_tables(PEER_TOPK)
    out_spec = pl.BlockSpec((hp, n_keys, tt), lambda i, h: (h, 0, i))
    shape = lambda dt: jax.ShapeDtypeStruct((n_heads, n_keys, t), dt)
    tiled_spec = pl.BlockSpec((hp, n_keys // BF16_TILE_ROWS, BF16_TILE_ROWS, tt), lambda i, h: (h, 0, 0, i))
    tiled_shape = jax.ShapeDtypeStruct((n_heads, n_keys // BF16_TILE_ROWS, BF16_TILE_ROWS, t), BF16)
    return pl.pallas_call(
        functools.partial(_route_kernel, n_cand=n_cand),
        grid=(t // tt, n_heads // hp),
        in_specs=[pl.BlockSpec((tt, hp * 2 * half), lambda i, h: (i, h)),
                  pl.BlockSpec((hp, 2, n_keys, half), lambda i, h: (h, 0, 0, 0)),
                  pl.BlockSpec(sel_a.shape, lambda i, h: (0, 0)),
                  pl.BlockSpec(sel_b.shape, lambda i, h: (0, 0))],
        out_specs=[tiled_spec, tiled_spec, out_spec, out_spec],
        out_shape=[tiled_shape, tiled_shape, shape(jnp.uint32), shape(jnp.uint32)],
        compiler_params=_cparams("arbitrary", "arbitrary"),
        name="peer_route",
    )(pq, sub_keys, sel_a, sel_b)


def _peer_kernel(h2t_ref, u_ref, vt_ref, rank2_ref, e2_ref, cnt1_ref, w1_ref, x1_ref, mod_ref,
                 o_ref, act_even, act_odd, acc_scr, *, n_heads, n_keys, ec, group, n_chunks, n_items):
    s = pl.program_id(0)
    chunk = jnp.clip(s - 1, 0, n_items - 1) % n_chunks

    @pl.when(s == 0)
    def _():
        act_odd[...] = jnp.zeros_like(act_odd)

    @pl.when((chunk == 0) | (s == 0))
    def _():
        acc_scr[...] = jnp.zeros_like(acc_scr)

    tt = h2t_ref.shape[1]
    zero = jnp.zeros((), BF16)
    n_groups = ec // group
    subs_per_group = group // n_keys

    def coef_of(grp, act):
        coefs = []
        for s_loc in range(subs_per_group):
            sub = grp * subs_per_group + s_loc
            gate = None
            for h in range(n_heads):
                cnt = pltpu.bitcast(jnp.broadcast_to(cnt1_ref[h, sub:sub + 1, :], (8, tt)), BF16)
                w1 = pltpu.bitcast(jnp.broadcast_to(w1_ref[h, sub:sub + 1, :], (8, tt)), BF16)
                term = jnp.where(rank2_ref[h] < cnt[None], e2_ref[h], zero) * w1[None]
                gate = term if gate is None else gate + term
            a = act[s_loc * n_keys:(s_loc + 1) * n_keys, :].astype(BF16)
            gelu = (0.5 * a) * (1.0 + lax.erf(a * (2.0 ** -0.5)))
            coefs.append(gate.reshape(n_keys, tt) * gelu)
        return jnp.concatenate(coefs, axis=0)

    def stages(act_w, act_r):
        total = acc_scr[...]
        half_t = tt // 2
        new_parts = []
        for grp in range(n_groups):
            rows = slice(grp * group, (grp + 1) * group)
            coef = coef_of(grp, act_r[rows, :])
            if grp < 2:
                cols = slice(grp * half_t, (grp + 1) * half_t)
                new_parts.append(jnp.dot(u_ref[...], h2t_ref[:, cols], preferred_element_type=F32))
            total = total + jnp.dot(vt_ref[:, rows], coef, preferred_element_type=F32)
        acc_scr[...] = total
        act_w[...] = jnp.concatenate(new_parts, axis=1)

    @pl.when(s % 2 == 0)
    def _():
        stages(act_even, act_odd)

    @pl.when(s % 2 == 1)
    def _():
        stages(act_odd, act_even)

    @pl.when((chunk == n_chunks - 1) & (s >= 1))
    def _():
        gate2 = mod_ref[0, 5:6, :]
        o_ref[...] = x1_ref[...] + gate2 * acc_scr[...].T


def _peer_call(h2t, u_b, vt_b, rank2, e2, cnt1, w1, x1, mod, seq_len, tt=512, ec=2048, group=1024):
    d, t = h2t.shape
    n_exp = u_b.shape[0]
    n_heads, n_key_tiles, tile_rows, _ = rank2.shape
    n_keys = n_key_tiles * tile_rows
    tiles_per_seq = seq_len // tt
    n_chunks = n_exp // ec
    n_items = (t // tt) * n_chunks
    first = lambda s: jnp.minimum(s, n_items - 1)
    second = lambda s: jnp.clip(s - 1, 0, n_items - 1)
    route_spec = pl.BlockSpec((n_heads, n_key_tiles, tile_rows, tt), lambda s: (0, 0, 0, second(s) // n_chunks))
    row_spec = pl.BlockSpec((n_heads, ec // n_keys, tt), lambda s: (0, second(s) % n_chunks, second(s) // n_chunks))
    return pl.pallas_call(
        functools.partial(_peer_kernel, n_heads=n_heads, n_keys=n_keys, ec=ec, group=group,
                          n_chunks=n_chunks, n_items=n_items),
        grid=(n_items + 1,),
        in_specs=[pl.BlockSpec((d, tt), lambda s: (0, first(s) // n_chunks)),
                  pl.BlockSpec((ec, d), lambda s: (first(s) % n_chunks, 0)),
                  pl.BlockSpec((d, ec), lambda s: (0, second(s) % n_chunks)),
                  route_spec, route_spec, row_spec, row_spec,
                  pl.BlockSpec((tt, d), lambda s: (second(s) // n_chunks, 0)),
                  pl.BlockSpec((1, 6, d), lambda s: (second(s) // n_chunks // tiles_per_seq, 0, 0))],
        out_specs=pl.BlockSpec((tt, d), lambda s: (second(s) // n_chunks, 0)),
        out_shape=jax.ShapeDtypeStruct((t, d), F32),
        scratch_shapes=[pltpu.VMEM((ec, tt), F32), pltpu.VMEM((ec, tt), F32), pltpu.VMEM((d, tt), F32)],
        compiler_params=_cparams("arbitrary"),
        name="peer_experts",
    )(h2t, u_b, vt_b, rank2, e2, cnt1, w1, x1, mod)


def _pack_in_proj(w_in, sb_w, gdn_qk_w, gdn_v_w, n_gdn_heads, d_model, tn):
    o_sbq, o_sbk, o_sbv = 0, sb_w, 2 * sb_w
    o_gdn = 3 * sb_w
    conv_w = 2 * gdn_qk_w + gdn_v_w
    o_a = o_gdn + conv_w
    o_b = o_a + n_gdn_heads
    o_z = o_b + n_gdn_heads
    o_gate = o_z + gdn_v_w
    pad = (-w_in.shape[1]) % tn
    packed = jnp.concatenate([
        w_in[:, o_sbq:o_a],
        w_in[:, o_gate:o_gate + 2 * d_model],
        w_in[:, o_z:o_z + gdn_v_w],
        w_in[:, o_a:o_z],
        jnp.zeros((w_in.shape[0], pad), w_in.dtype)], axis=1).astype(BF16)
    return packed


def _block(x2d, c, w_ada, b_ada, norm1_w, w_in, sb_q_norm_w, sb_k_norm_w, gdn_conv_w, gdn_A_log,
           gdn_dt_bias, gdn_o_norm_w, w_proj_sb, w_proj_gdn, w_o, norm2_w, peer_w_q, peer_sub_keys,
           peer_u, peer_v, batch, seq_len):
    t, d = x2d.shape
    sb_w = w_proj_sb.shape[0]
    gdn_v_w = w_proj_gdn.shape[0]
    gdn_qk_w = (gdn_conv_w.shape[1] - gdn_v_w) // 2
    n_sb_heads = sb_w // HEAD_DIM
    n_gdn_heads = gdn_v_w // HEAD_DIM
    assert gdn_qk_w == gdn_v_w == sb_w and d % (4 * LANES) == 0

    c_pad = jnp.zeros((8, d), F32).at[:batch].set(c)
    mod = _ada_call(c_pad, w_ada, b_ada)[:batch].reshape(batch, 6, d)

    tn = sb_w
    w_packed = _pack_in_proj(w_in, sb_w, gdn_qk_w, gdn_v_w, n_gdn_heads, d, tn)
    heads_per_tile = tn // HEAD_DIM
    q_tiles = sb_w // tn
    qkw = jnp.concatenate([jnp.tile(sb_q_norm_w * (HEAD_DIM ** -0.5 * LOG2E), (q_tiles, heads_per_tile)),
                           jnp.tile(sb_k_norm_w, (q_tiles, heads_per_tile))], axis=0).reshape(2 * q_tiles, 1, tn)
    proj, ab_logits = _inproj_call(x2d, mod, norm1_w.reshape(1, d), w_packed, qkw, seq_len, tn=tn)

    ysb = _sb_call(proj, batch, seq_len, n_sb_heads // 2)

    col_gdn = 3 * sb_w // gdn_v_w
    col_gate = (3 * sb_w + 3 * gdn_v_w) // d
    col_z = (3 * sb_w + 3 * gdn_v_w + 2 * d) // gdn_v_w
    prep = _gdn_prep_call(proj, ab_logits, gdn_conv_w, gdn_A_log, gdn_dt_bias, seq_len, n_gdn_heads,
                          (col_gdn, col_gdn + 1, col_gdn + 2), GDN_BLOCK)
    ogdn = _gdn_core_call(prep, batch, seq_len, n_gdn_heads, GDN_BLOCK)

    x1, h2, pq = _merge_call(
        x2d, ysb, ogdn, proj, mod, jnp.tile(gdn_o_norm_w, n_gdn_heads).reshape(1, gdn_v_w), norm2_w.reshape(1, d),
        w_proj_sb.astype(BF16), w_proj_gdn.astype(BF16), w_o.astype(BF16), peer_w_q.astype(BF16),
        seq_len, col_gate, col_gate + 1, col_z)

    rank2, e2, cnt1, w1 = _route_call(pq, peer_sub_keys)
    return _peer_call(h2, peer_u.astype(BF16), peer_v.T.astype(BF16), rank2, e2, cnt1, w1, x1, mod, seq_len)


def kernel(x, c, w_ada, b_ada, norm1_w, w_in, sb_q_norm_w, sb_k_norm_w, gdn_conv_w, gdn_A_log, gdn_dt_bias,
           gdn_o_norm_w, w_proj_sb, w_proj_gdn, w_o, norm2_w, peer_w_q, peer_sub_keys, peer_u, peer_v):
    batch, seq_len, d = x.shape
    x2d = x.reshape(batch * seq_len, d)
    params = (w_ada, b_ada, norm1_w, w_in, sb_q_norm_w, sb_k_norm_w, gdn_conv_w, gdn_A_log, gdn_dt_bias,
              gdn_o_norm_w, w_proj_sb, w_proj_gdn, w_o, norm2_w, peer_w_q, peer_sub_keys, peer_u, peer_v)
    depth = w_ada.shape[0]
    for l in range(depth):
        layer = [p.reshape(p.shape[1:]) if depth == 1 else p[l] for p in params]
        x2d = _block(x2d, c, *layer, batch, seq_len)
    return x2d.reshape(batch, seq_len, d)
```

```python
import functools

import jax
import jax.numpy as jnp
from jax import lax
from jax.experimental import pallas as pl
from jax.experimental.pallas import tpu as pltpu

F32 = jnp.float32
BF16 = jnp.bfloat16
HI = lax.Precision.HIGHEST
EPS = 1e-6
LOG2E = 1.4426950408889634

LANES = 128
BF16_TILE_ROWS = 16
HEAD_DIM = 64
GDN_BLOCK = 128
GDN_CONV = 4
PEER_TOPK = 16
NOT_RANKED = 99.0
VMEM_LIMIT = 56 * 1024 * 1024

NT_DIMS = (((1,), (1,)), ((), ()))
TN_DIMS = (((0,), (0,)), ((), ()))


def _cparams(*sem):
    return pltpu.CompilerParams(dimension_semantics=sem, vmem_limit_bytes=VMEM_LIMIT)


def _sigmoid(x):
    return 1.0 / (1.0 + jnp.exp(-x))


def _silu(x):
    return x * _sigmoid(x)


def _softplus(x):
    return jnp.maximum(x, 0.0) + jnp.log(1.0 + jnp.exp(-jnp.abs(x)))


def _block_diag_ones(n, group, value=1.0, dtype=F32):
    r = jnp.arange(n) // group
    return jnp.where(r[:, None] == r[None, :], value, 0.0).astype(dtype)


def _bf16_pieces(x):
    hi = x.astype(BF16)
    rest = x - hi.astype(F32)
    mid = rest.astype(BF16)
    return hi, mid, (rest - mid.astype(F32)).astype(BF16)


def _group_sum_sq(x, bd):
    slab = bd.shape[0]
    outs = []
    for c0 in range(0, x.shape[1], slab):
        sq = x[:, c0:c0 + slab] * x[:, c0:c0 + slab]
        hi = sq.astype(BF16)
        lo = (sq - hi.astype(F32)).astype(BF16)
        outs.append(jnp.dot(hi, bd, preferred_element_type=F32) + jnp.dot(lo, bd, preferred_element_type=F32))
    return outs[0] if len(outs) == 1 else jnp.concatenate(outs, axis=1)


def _ada_kernel(c_ref, w_ref, b_ref, o_ref):
    c = c_ref[...]
    o_ref[...] = jnp.dot(_silu(c), w_ref[...], preferred_element_type=F32, precision=HI) + b_ref[...]


def _ada_call(c_pad, w_ada, b_ada):
    rows, d = c_pad.shape
    n = w_ada.shape[1]
    tn = 2048
    return pl.pallas_call(
        _ada_kernel,
        grid=(n // tn,),
        in_specs=[pl.BlockSpec((rows, d), lambda j: (0, 0)),
                  pl.BlockSpec((d, tn), lambda j: (0, j)),
                  pl.BlockSpec((1, tn), lambda j: (0, j))],
        out_specs=pl.BlockSpec((rows, tn), lambda j: (0, j)),
        out_shape=jax.ShapeDtypeStruct((rows, n), F32),
        compiler_params=_cparams("arbitrary"),
        name="ada_mod",
    )(c_pad, w_ada, b_ada.reshape(1, n))


def _inproj_kernel(x_ref, mod_ref, n1_ref, w_ref, qkw_ref, bd_ref, o_ref, tail_ref, h_scr, *, n_qk_tiles):
    j = pl.program_id(1)
    last = pl.num_programs(1) - 1

    @pl.when(j == 0)
    def _():
        x = x_ref[...]
        ms = jnp.mean(x * x, axis=-1, keepdims=True)
        y = x * lax.rsqrt(ms + EPS) * n1_ref[...]
        shift = mod_ref[0, 0:1, :]
        scale = mod_ref[0, 1:2, :]
        h_scr[...] = (y * (1.0 + scale) + shift).astype(BF16)

    acc = jnp.dot(h_scr[...], w_ref[...], preferred_element_type=F32)

    @pl.when(j < n_qk_tiles)
    def _():
        o_ref[...] = (acc * lax.rsqrt(_group_sum_sq(acc, bd_ref[...]) + EPS) * qkw_ref[0]).astype(BF16)

    @pl.when((j >= n_qk_tiles) & (j < last))
    def _():
        o_ref[...] = acc.astype(BF16)

    @pl.when(j == last)
    def _():
        tail_ref[...] = acc


def _inproj_call(x2d, mod, n1w, w_packed, qkw, seq_len, tm=2048, tn=512):
    t, d = x2d.shape
    n = w_packed.shape[1]
    n_tiles = n // tn
    n_qk_tiles = qkw.shape[0]
    assert seq_len % tm == 0, "a row tile must not straddle two sequences (per-sequence modulation)"
    bd = _block_diag_ones(2 * LANES, HEAD_DIM, 1.0 / HEAD_DIM, BF16)
    tiles_per_seq = seq_len // tm
    return pl.pallas_call(
        functools.partial(_inproj_kernel, n_qk_tiles=n_qk_tiles),
        grid=(t // tm, n // tn),
        in_specs=[pl.BlockSpec((tm, d), lambda i, j: (i, 0)),
                  pl.BlockSpec((1, 6, d), lambda i, j: (i // tiles_per_seq, 0, 0)),
                  pl.BlockSpec((1, d), lambda i, j: (0, 0)),
                  pl.BlockSpec((d, tn), lambda i, j: (0, j)),
                  pl.BlockSpec((1, 1, tn), lambda i, j: (jnp.minimum(j, n_qk_tiles - 1), 0, 0)),
                  pl.BlockSpec(bd.shape, lambda i, j: (0, 0))],
        out_specs=[pl.BlockSpec((tm, tn), lambda i, j: (i, jnp.minimum(j, n_tiles - 2))),
                   pl.BlockSpec((tm, tn), lambda i, j: (i, 0))],
        out_shape=[jax.ShapeDtypeStruct((t, n - tn), BF16), jax.ShapeDtypeStruct((t, tn), F32)],
        scratch_shapes=[pltpu.VMEM((tm, d), BF16)],
        compiler_params=_cparams("arbitrary", "arbitrary"),
        name="in_proj",
    )(x2d, mod, n1w, w_packed, qkw, bd)


def _sb_kernel(q_ref, k_ref, v_ref, tri_ref, o_ref, acc_ref, r_ref, *, tq):
    i = pl.program_id(2)
    half = tri_ref.shape[0]
    lane = lax.broadcasted_iota(jnp.int32, (1, LANES), 1)
    first_half = lane < HEAD_DIM
    q = q_ref[...].astype(F32)
    q_heads = (jnp.where(first_half, q, 0.0).astype(BF16), jnp.where(first_half, 0.0, q).astype(BF16))
    acc_ref[...] = jnp.zeros_like(acc_ref)
    r_ref[...] = jnp.zeros_like(r_ref)

    def run(blocks, masked):
        items = [(j, hd) for j in blocks for hd in range(2)]
        st = [dict() for _ in items]
        if masked:
            causal = (lax.broadcasted_iota(jnp.int32, (tq, tq), 1) < lax.broadcasted_iota(jnp.int32, (tq, tq), 0))

        def rows_of(n):
            return pl.ds(pl.multiple_of(items[n][0] * tq, tq), tq)

        def logits(n):
            kb = k_ref[rows_of(n), :].astype(BF16)
            st[n]["y"] = lax.dot_general(q_heads[items[n][1]], kb, NT_DIMS, preferred_element_type=F32)

        def softplus(n):
            y = st[n]["y"]
            neg_abs = lax.bitcast_convert_type(lax.bitcast_convert_type(y, jnp.uint32) | jnp.uint32(0x80000000), F32)
            sp = jnp.maximum(y, 0.0) + jnp.log(1.0 + jnp.exp2(neg_abs)) * LOG2E
            if masked:
                sp = jnp.where(causal, sp, 0.0)
            st[n]["log_beta"] = y - sp
            st[n]["spb"] = sp.astype(BF16)
            st[n]["sp_first"] = (sp[:, 0:1], sp[:, half:half + 1])

        def later_sums(n):
            spb = st[n]["spb"]
            first_lo, first_hi = st[n]["sp_first"]
            later_hi = jnp.dot(spb[:, half:], tri_ref[...], preferred_element_type=F32)
            total_hi = later_hi[:, 0:1] + first_hi
            later_lo = jnp.dot(spb[:, :half], tri_ref[...], preferred_element_type=F32) + total_hi
            st[n]["later"] = jnp.concatenate([later_lo, later_hi], axis=1)
            st[n]["total"] = later_lo[:, 0:1] + first_lo

        def weights(n):
            a = jnp.exp2(st[n]["log_beta"] - st[n]["later"])
            if masked:
                a = jnp.where(causal, a, 0.0)
            st[n]["a"] = a.astype(BF16)

        def accumulate(n):
            hd = items[n][1]
            pv = jnp.dot(st[n]["a"], v_ref[rows_of(n), :].astype(BF16), preferred_element_type=F32)
            r = r_ref[hd]
            acc_ref[hd] += jnp.exp2(-r) * pv
            r_ref[hd] = r + st[n]["total"]

        stages = (logits, softplus, later_sums, weights, accumulate)
        elementwise = (softplus, weights)
        for wave in range(len(items) + len(stages) - 1):
            todo = [(stages[wave - n], n) for n in range(len(items)) if 0 <= wave - n < len(stages)]
            for stage, n in [x for x in todo if x[0] in elementwise] + [x for x in todo if x[0] not in elementwise]:
                stage(n)

    run([i], True)

    per_trip = 4

    def body(it, carry):
        j = i - 1 - per_trip * it
        run([j - d for d in range(per_trip)], False)
        return carry

    lax.fori_loop(0, i // per_trip, body, 0)

    left = i % per_trip

    @pl.when(left >= 2)
    def _():
        run([left - 1, left - 2], False)

    @pl.when(left % 2 == 1)
    def _():
        run([0], False)

    o_ref[...] = jnp.where(first_half, acc_ref[0], acc_ref[1])


def _sb_call(proj, batch, seq_len, n_pairs, tq=512):
    t = proj.shape[0]
    nq = seq_len // tq
    half = tq // 2
    kk = jnp.arange(half)
    later = (kk[:, None] > kk[None, :]).astype(BF16)
    return pl.pallas_call(
        functools.partial(_sb_kernel, tq=tq),
        grid=(batch, n_pairs, nq),
        in_specs=[pl.BlockSpec((tq, LANES), lambda b, p, i: (b * nq + i, p)),
                  pl.BlockSpec((seq_len, LANES), lambda b, p, i: (b, n_pairs + p)),
                  pl.BlockSpec((seq_len, LANES), lambda b, p, i: (b, 2 * n_pairs + p)),
                  pl.BlockSpec((half, half), lambda b, p, i: (0, 0))],
        out_specs=pl.BlockSpec((tq, LANES), lambda b, p, i: (b * nq + i, p)),
        out_shape=jax.ShapeDtypeStruct((t, n_pairs * LANES), F32),
        scratch_shapes=[pltpu.VMEM((2, tq, LANES), F32), pltpu.VMEM((2, tq, 1), F32)],
        compiler_params=_cparams("arbitrary", "arbitrary", "arbitrary"),
        name="sb_attention",
    )(proj, proj, proj, later)


def _gdn_prep_kernel(q_ref, k_ref, v_ref, qt_ref, kt_ref, vt_ref, ab_ref, cw_ref, alog_ref, dtb_ref,
                     ea_ref, eb_ref, lc_ref, jc_ref, bd_ref,
                     qn_o, qd_o, kn_o, kd_o, kb_o, kbg_o, vb_o, gcol_o, grow_o, scr, *, tm, tiles_per_seq):
    i = pl.program_id(0)
    keep_tail = (i % tiles_per_seq != 0).astype(F32)
    w = cw_ref[...]
    width = q_ref.shape[1]

    halo = qt_ref.shape[0]

    def conv_silu(cur_ref, tail_ref, col0):
        scr[0:halo, :] = tail_ref[...].astype(F32) * keep_tail
        scr[halo:, :] = cur_ref[...].astype(F32)
        y = jnp.zeros((tm, width), F32)
        for tap in range(GDN_CONV):
            off = halo - (GDN_CONV - 1) + tap
            y = y + scr[off:off + tm, :] * w[tap:tap + 1, col0:col0 + width]
        return _silu(y)

    bd = bd_ref[...]
    cq = conv_silu(q_ref, qt_ref, 0)
    qn = cq * lax.rsqrt(_group_sum_sq(cq, bd) + EPS) * (HEAD_DIM ** -0.5)
    ck = conv_silu(k_ref, kt_ref, width)
    kn = ck * lax.rsqrt(_group_sum_sq(ck, bd) + EPS)
    cv = conv_silu(v_ref, vt_ref, 2 * width)

    ab = ab_ref[...]
    g = -jnp.exp(alog_ref[...]) * _softplus(ab + dtb_ref[...])
    beta = _sigmoid(ab)
    g_parts = _bf16_pieces(g)
    lc, jc, ea = lc_ref[...], jc_ref[...], ea_ref[...]
    g_cum = sum(jnp.dot(lc, p, preferred_element_type=F32) for p in g_parts)
    g_tot = sum(jnp.dot(jc, p, preferred_element_type=F32) for p in g_parts)
    gx = sum(jnp.dot(p, ea, preferred_element_type=F32) for p in _bf16_pieces(g_cum))
    glx = sum(jnp.dot(p, ea, preferred_element_type=F32) for p in _bf16_pieces(g_tot))
    bx = sum(jnp.dot(p, eb_ref[...], preferred_element_type=F32) for p in _bf16_pieces(beta))

    e_g = jnp.exp(gx)
    kb = kn * bx
    qn_o[...] = qn.astype(BF16)
    qd_o[...] = (qn * e_g).astype(BF16)
    kn_o[...] = kn.astype(BF16)
    kd_o[...] = (kn * jnp.exp(glx - gx)).astype(BF16)
    kb_o[...] = kb.astype(BF16)
    kbg_o[...] = (kb * e_g).astype(BF16)
    vb_o[...] = (cv * bx).astype(BF16)
    g_heads = g_cum[:, :LANES]
    gcol_o[...] = g_heads
    grow_o[...] = g_heads.T[0:8, :]


def _gdn_prep_call(proj, ab_logits, conv_w, a_log, dt_bias, seq_len, n_heads, col_qkv, chunk, tm=512):
    t = proj.shape[0]
    assert n_heads <= 8 and tm % chunk == 0
    width = n_heads * HEAD_DIM
    abw = 2 * LANES
    tiles_per_seq = seq_len // tm
    halo = 16
    sub = tm // halo
    alog_pad = jnp.zeros((1, abw), F32).at[0, :n_heads].set(a_log)
    dtb_pad = jnp.zeros((1, abw), F32).at[0, :n_heads].set(dt_bias)
    head_of_lane = jnp.arange(width) // HEAD_DIM
    rows = jnp.arange(abw)
    ea = (rows[:, None] == head_of_lane[None, :]).astype(BF16)
    eb = (rows[:, None] == head_of_lane[None, :] + n_heads).astype(BF16)
    tok = jnp.arange(tm)
    same_chunk = (tok[:, None] // chunk) == (tok[None, :] // chunk)
    lc = (same_chunk & (tok[:, None] >= tok[None, :])).astype(BF16)
    jc = same_chunk.astype(BF16)
    bd = _block_diag_ones(2 * LANES, HEAD_DIM, 1.0, BF16)

    cur = lambda c: pl.BlockSpec((tm, width), lambda i: (i, c))
    tail = lambda c: pl.BlockSpec((halo, width), lambda i: (jnp.maximum(i * sub - 1, 0), c))
    full = lambda a: pl.BlockSpec(a.shape, lambda i: (0,) * a.ndim)
    cq, ck, cv = col_qkv
    out_spec = pl.BlockSpec((tm, width), lambda i: (i, 0))
    out_shape = jax.ShapeDtypeStruct((t, width), BF16)
    return pl.pallas_call(
        functools.partial(_gdn_prep_kernel, tm=tm, tiles_per_seq=tiles_per_seq),
        grid=(t // tm,),
        in_specs=[cur(cq), cur(ck), cur(cv), tail(cq), tail(ck), tail(cv),
                  pl.BlockSpec((tm, abw), lambda i: (i, 0)),
                  full(conv_w), full(alog_pad), full(dtb_pad), full(ea), full(eb), full(lc), full(jc), full(bd)],
        out_specs=[out_spec] * 7 + [pl.BlockSpec((tm, LANES), lambda i: (i, 0)), pl.BlockSpec((8, tm), lambda i: (0, i))],
        out_shape=[out_shape] * 7 + [jax.ShapeDtypeStruct((t, LANES), F32), jax.ShapeDtypeStruct((8, t), F32)],
        scratch_shapes=[pltpu.VMEM((tm + halo, width), F32)],
        compiler_params=_cparams("arbitrary"),
        name="gdn_prep",
    )(proj, proj, proj, proj, proj, proj, ab_logits, conv_w, alog_pad, dtb_pad, ea, eb, lc, jc, bd)


def _gdn_core_kernel(qn_ref, qd_ref, kn_ref, kd_ref, kb_ref, kbg_ref, vb_ref, gcol_ref, grow_ref,
                     o_ref, s_scr, *, ts, n_heads, c_len):
    @pl.when(pl.program_id(1) == 0)
    def _():
        s_scr[...] = jnp.zeros_like(s_scr)

    row = lax.broadcasted_iota(jnp.int32, (c_len, c_len), 0)
    col = lax.broadcasted_iota(jnp.int32, (c_len, c_len), 1)
    strict = row > col
    incl = row >= col
    eye_state = (lax.broadcasted_iota(jnp.int32, (HEAD_DIM, HEAD_DIM), 0)
                 == lax.broadcasted_iota(jnp.int32, (HEAD_DIM, HEAD_DIM), 1))
    n_levels = (c_len - 1).bit_length()
    n_chunks = ts // c_len
    problems = [(c, h) for c in range(n_chunks) for h in range(n_heads)]

    def bf(x):
        return x.astype(BF16)

    def mm(a, b):
        return jnp.dot(a, b, preferred_element_type=F32)

    def head_tile(ref, c, h):
        pair, half = divmod(h, 2)
        tile = ref[c * c_len:(c + 1) * c_len, pair * LANES:(pair + 1) * LANES]
        return tile[:, half * HEAD_DIM:(half + 1) * HEAD_DIM]

    sibling = [((row >> k) ^ (col >> k)) == 1 for k in range(n_levels)]

    xs, ms, rs, qks = [], [], [], []
    for c, h in problems:
        g_col = gcol_ref[c * c_len:(c + 1) * c_len, h:h + 1]
        g_row = grow_ref[h:h + 1, c * c_len:(c + 1) * c_len]
        decay = jnp.where(incl, jnp.exp(g_col - g_row), 0.0)
        kn_b = bf(head_tile(kn_ref, c, h))
        kk = lax.dot_general(bf(head_tile(kb_ref, c, h)), kn_b, NT_DIMS, preferred_element_type=F32)
        x = jnp.where(strict, -(kk * decay), 0.0)
        xs.append(x)
        ms.append(jnp.where(row == col, 1.0, jnp.where(sibling[0], x, 0.0)))
        qks.append(bf(lax.dot_general(bf(head_tile(qn_ref, c, h)), kn_b, NT_DIMS, preferred_element_type=F32) * decay))
        rs.append(bf(jnp.concatenate([head_tile(vb_ref, c, h), head_tile(kbg_ref, c, h)], axis=1)))

    for k in range(1, n_levels):
        for idx in range(len(problems)):
            m_b = bf(ms[idx])
            left = mm(m_b, bf(jnp.where(sibling[k], xs[idx], 0.0)))
            ms[idx] = ms[idx] + mm(bf(left), m_b)

    q_eff, o_zero, p_mat, b_mat = {}, {}, {}, {}
    for idx, (c, h) in enumerate(problems):
        sol = bf(mm(bf(ms[idx]), rs[idx]))
        m1 = mm(qks[idx], sol)
        q_eff[c, h] = bf(head_tile(qd_ref, c, h) - m1[:, HEAD_DIM:])
        o_zero[c, h] = m1[:, :HEAD_DIM]
        m2 = lax.dot_general(bf(head_tile(kd_ref, c, h)), sol, TN_DIMS, preferred_element_type=F32)
        chunk_decay = jnp.exp(gcol_ref[(c + 1) * c_len - 1:(c + 1) * c_len, h:h + 1])
        p_mat[c, h] = bf(jnp.where(eye_state, chunk_decay, 0.0) - m2[:, HEAD_DIM:])
        b_mat[c, h] = m2[:, :HEAD_DIM]

    states = [s_scr[h] for h in range(n_heads)]
    for c in range(n_chunks):
        outs = []
        for h in range(n_heads):
            s_b = bf(states[h])
            outs.append(mm(q_eff[c, h], s_b) + o_zero[c, h])
            states[h] = mm(p_mat[c, h], s_b) + b_mat[c, h]
        o_ref[c * c_len:(c + 1) * c_len, :] = jnp.concatenate(outs, axis=1)
    for h in range(n_heads):
        s_scr[h] = states[h]


def _gdn_core_call(prep, batch, seq_len, n_heads, c_len, ts=512):
    t, width = prep[0].shape
    n_seq_tiles = seq_len // ts
    spec = pl.BlockSpec((ts, width), lambda b, s: (b * n_seq_tiles + s, 0))
    return pl.pallas_call(
        functools.partial(_gdn_core_kernel, ts=ts, n_heads=n_heads, c_len=c_len),
        grid=(batch, n_seq_tiles),
        in_specs=[spec] * 7 + [pl.BlockSpec((ts, LANES), lambda b, s: (b * n_seq_tiles + s, 0)),
                               pl.BlockSpec((8, ts), lambda b, s: (0, b * n_seq_tiles + s))],
        out_specs=spec,
        out_shape=jax.ShapeDtypeStruct((t, width), F32),
        scratch_shapes=[pltpu.VMEM((n_heads, HEAD_DIM, HEAD_DIM), F32)],
        compiler_params=_cparams("arbitrary", "arbitrary"),
        name="gdn_core",
    )(*prep)


def _merge_kernel(x_ref, ysb_ref, og_ref, z_ref, g0_ref, g1_ref, mod_ref, onw_ref, n2_ref, bd_ref,
                  wsb_ref, wgdn_ref, wo_ref, wq_ref, x1_o, h2t_o, pq_o):
    og = og_ref[...]
    ygdn = og * lax.rsqrt(_group_sum_sq(og, bd_ref[...]) + EPS) * onw_ref[...] * _silu(z_ref[...].astype(F32))
    m = (_sigmoid(g0_ref[...].astype(F32)) * jnp.dot(ysb_ref[...].astype(BF16), wsb_ref[...],
                                                      preferred_element_type=F32)
         + _sigmoid(g1_ref[...].astype(F32)) * jnp.dot(ygdn.astype(BF16), wgdn_ref[...], preferred_element_type=F32))
    gate1 = mod_ref[0, 2:3, :]
    shift2 = mod_ref[0, 3:4, :]
    scale2 = mod_ref[0, 4:5, :]
    x1 = x_ref[...] + gate1 * jnp.dot(m.astype(BF16), wo_ref[...], preferred_element_type=F32)
    x1_o[...] = x1
    ms2 = jnp.mean(x1 * x1, axis=-1, keepdims=True)
    h2 = x1 * lax.rsqrt(ms2 + EPS) * n2_ref[...] * (1.0 + scale2) + shift2
    h2t_o[...] = h2.T.astype(BF16)
    pq_o[...] = jnp.dot(h2.astype(BF16), wq_ref[...], preferred_element_type=F32)


def _merge_call(x2d, ysb, ogdn, proj, mod, onw, n2w, wsb, wgdn, wo, wq, seq_len, col_g0, col_g1, col_z, tm=512):
    t, d = x2d.shape
    width = ysb.shape[1]
    nq = wq.shape[1]
    bd = _block_diag_ones(2 * LANES, HEAD_DIM, 1.0 / HEAD_DIM, BF16)
    tiles_per_seq = seq_len // tm
    full = lambda a: pl.BlockSpec(a.shape, lambda i: (0,) * a.ndim)
    once = lambda a: pl.BlockSpec(a.shape, lambda i: (0,) * a.ndim, pipeline_mode=pl.Buffered(1))
    return pl.pallas_call(
        _merge_kernel,
        grid=(t // tm,),
        in_specs=[pl.BlockSpec((tm, d), lambda i: (i, 0)),
                  pl.BlockSpec((tm, width), lambda i: (i, 0)),
                  pl.BlockSpec((tm, width), lambda i: (i, 0)),
                  pl.BlockSpec((tm, width), lambda i: (i, col_z)),
                  pl.BlockSpec((tm, d), lambda i: (i, col_g0)),
                  pl.BlockSpec((tm, d), lambda i: (i, col_g1)),
                  pl.BlockSpec((1, 6, d), lambda i: (i // tiles_per_seq, 0, 0)),
                  full(onw), full(n2w), full(bd), once(wsb), once(wgdn), once(wo), once(wq)],
        out_specs=[pl.BlockSpec((tm, d), lambda i: (i, 0)),
                   pl.BlockSpec((d, tm), lambda i: (0, i)),
                   pl.BlockSpec((tm, nq), lambda i: (i, 0))],
        out_shape=[jax.ShapeDtypeStruct((t, d), F32),
                   jax.ShapeDtypeStruct((d, t), BF16),
                   jax.ShapeDtypeStruct((t, nq), F32)],
        compiler_params=_cparams("arbitrary"),
        name="merge_proj",
    )(x2d, ysb, ogdn, proj, proj, proj, mod, onw, n2w, bd, wsb, wgdn, wo, wq)


def _extract_topk(s, k, break_ties):
    n = s.shape[0]
    if not break_ties:
        lowest_bits = -8388609
        vals = []
        for r in range(k):
            m = jnp.max(s, axis=0, keepdims=True)
            marker = lax.bitcast_convert_type(jnp.int32(lowest_bits - r), F32)
            s = jnp.where(s == m, marker, s)
            vals.append(m)
        took = jnp.int32(lowest_bits) - lax.bitcast_convert_type(s, jnp.int32)
        rank = jnp.where((took >= 0) & (took < k), took.astype(F32), NOT_RANKED)
        return vals, rank
    iota = lax.broadcasted_iota(jnp.int32, s.shape, 0).astype(F32)
    rank = jnp.full(s.shape, NOT_RANKED, F32)
    vals = []
    for r in range(k):
        m = jnp.max(s, axis=0, keepdims=True)
        hit = iota == jnp.min(jnp.where(s == m, iota, float(n)), axis=0, keepdims=True)
        rank = jnp.where(hit, float(r), rank)
        s = jnp.where(hit, -jnp.inf, s)
        vals.append(m)
    return vals, rank


def _candidate_tables(k):
    pairs = [(a, b) for a in range(k) for b in range(k) if (a + 1) * (b + 1) <= k]
    n_pad = -(-len(pairs) // 8) * 8
    sel_a = jnp.zeros((n_pad, k), F32).at[jnp.arange(len(pairs)), jnp.array([a for a, _ in pairs])].set(1.0)
    sel_b = jnp.zeros((n_pad, k), F32).at[jnp.arange(len(pairs)), jnp.array([b for _, b in pairs])].set(1.0)
    return sel_a, sel_b, len(pairs)


def _bf16_pair_words(x):
    bits = lax.bitcast_convert_type(x.astype(BF16).astype(F32), jnp.uint32)
    return bits | (bits >> 16)


def _route_kernel(pq_ref, keys_ref, sela_ref, selb_ref, rank2_o, e2_o, cnt1_o, w1_o, *, n_cand):
    k = PEER_TOPK
    hp, _, _, half = keys_ref.shape
    tt = pq_ref.shape[0]
    part_scores = [[lax.dot_general(keys_ref[hh, part], pq_ref[:, (2 * hh + part) * half:(2 * hh + part + 1) * half],
                                    NT_DIMS, preferred_element_type=F32, precision=HI) for hh in range(hp)]
                   for part in range(2)]
    sel_a = sela_ref[...]

    def route(break_ties, heads):
        w = len(heads) * tt
        iota_k = lax.broadcasted_iota(jnp.int32, (k, w), 0).astype(F32)
        s_all = jnp.concatenate([part_scores[part][hh] for part in range(2) for hh in heads], axis=1)
        scores = (s_all[:, :w], s_all[:, w:])
        vals, rank = _extract_topk(s_all, k, break_ties)
        top = jnp.concatenate(vals, axis=0)
        top1, top2 = top[:, :w], top[:, w:]
        v1, v2 = [v[:, :w] for v in vals], [v[:, w:] for v in vals]
        rank1, rank2 = rank[:, :w], rank[:, w:]
        cand = (jnp.dot(sel_a, top1, preferred_element_type=F32, precision=HI)
                + jnp.dot(selb_ref[...], top2, preferred_element_type=F32, precision=HI))
        cand_row = lax.broadcasted_iota(jnp.int32, cand.shape, 0)
        _, cand_rank = _extract_topk(jnp.where(cand_row < n_cand, cand, -jnp.inf), k, break_ties)
        chosen = (cand_rank < float(k)).astype(BF16)
        count = lax.dot_general(sel_a.astype(BF16), chosen, TN_DIMS, preferred_element_type=F32)
        e1 = jnp.exp(top1 - v1[0])
        e2 = jnp.exp(top2 - v2[0])
        z = jnp.zeros_like(v1[0])
        for a in range(k):
            z = z + e1[a:a + 1] * jnp.sum(jnp.where(iota_k < count[a:a + 1], e2, 0.0), axis=0, keepdims=True)
        inv_z = 1.0 / z
        cnt1 = jnp.zeros_like(rank1)
        for a in range(k):
            cnt1 = jnp.where(rank1 == float(a), count[a:a + 1], cnt1)
        e2_all = jnp.exp(scores[1] - v2[0]).astype(BF16)
        w1_all = jnp.where(rank1 < float(k), jnp.exp(scores[0] - v1[0]) * inv_z, 0.0)
        cnt1_words = _bf16_pair_words(cnt1)
        w1_words = _bf16_pair_words(w1_all)
        n_ranked = jnp.sum((rank < float(k)).astype(F32), axis=0, keepdims=True)
        n_chosen = jnp.sum((cand_rank < float(k)).astype(F32), axis=0, keepdims=True)
        off_by = jnp.maximum(jnp.maximum(jnp.abs(n_ranked[:, :w] - k), jnp.abs(n_ranked[:, w:] - k)),
                             jnp.abs(n_chosen - k))
        excess = []
        for pos, hh in enumerate(heads):
            lanes = slice(pos * tt, (pos + 1) * tt)
            rank2_o[hh] = rank2[:, lanes].astype(BF16).reshape(rank2_o.shape[1:])
            e2_o[hh] = e2_all[:, lanes].reshape(e2_o.shape[1:])
            cnt1_o[hh] = cnt1_words[:, lanes]
            w1_o[hh] = w1_words[:, lanes]
            excess.append(jnp.max(off_by[:, lanes]))
        return excess

    excess = route(False, list(range(hp)))
    for hh in range(hp):
        @pl.when(excess[hh] > 0.0)
        def _(hh=hh):
            route(True, [hh])


def _route_call(pq, sub_keys, tt=256, hp=4):
    t = pq.shape[0]
    n_heads, _, n_keys, half = sub_keys.shape
    sel_a, sel_b, n_cand = _candidate_tables(PEER_TOPK)
    out_spec = pl.BlockSpec((hp, n_keys, tt), lambda i, h: (h, 0, i))
    shape = lambda dt: jax.ShapeDtypeStruct((n_heads, n_keys, t), dt)
    tiled_spec = pl.BlockSpec((hp, n_keys // BF16_TILE_ROWS, BF16_TILE_ROWS, tt), lambda i, h: (h, 0, 0, i))
    tiled_shape = jax.ShapeDtypeStruct((n_heads, n_keys // BF16_TILE_ROWS, BF16_TILE_ROWS, t), BF16)
    return pl.pallas_call(
        functools.partial(_route_kernel, n_cand=n_cand),
        grid=(t // tt, n_heads // hp),
        in_specs=[pl.BlockSpec((tt, hp * 2 * half), lambda i, h: (i, h)),
                  pl.BlockSpec((hp, 2, n_keys, half), lambda i, h: (h, 0, 0, 0)),
                  pl.BlockSpec(sel_a.shape, lambda i, h: (0, 0)),
                  pl.BlockSpec(sel_b.shape, lambda i, h: (0, 0))],
        out_specs=[tiled_spec, tiled_spec, out_spec, out_spec],
        out_shape=[tiled_shape, tiled_shape, shape(jnp.uint32), shape(jnp.uint32)],
        compiler_params=_cparams("arbitrary", "arbitrary"),
        name="peer_route",
    )(pq, sub_keys, sel_a, sel_b)


def _peer_kernel(h2t_ref, u_ref, vt_ref, rank2_ref, e2_ref, cnt1_ref, w1_ref, x1_ref, mod_ref,
                 o_ref, act_even, act_odd, acc_scr, *, n_heads, n_keys, ec, group, n_chunks, n_items):
    s = pl.program_id(0)
    chunk = jnp.clip(s - 1, 0, n_items - 1) % n_chunks

    @pl.when(s == 0)
    def _():
        act_odd[...] = jnp.zeros_like(act_odd)

    @pl.when((chunk == 0) | (s == 0))
    def _():
        acc_scr[...] = jnp.zeros_like(acc_scr)

    tt = h2t_ref.shape[1]
    zero = jnp.zeros((), BF16)
    n_groups = ec // group
    subs_per_group = group // n_keys

    def coef_of(grp, act):
        coefs = []
        for s_loc in range(subs_per_group):
            sub = grp * subs_per_group + s_loc
            gate = None
            for h in range(n_heads):
                cnt = pltpu.bitcast(jnp.broadcast_to(cnt1_ref[h, sub:sub + 1, :], (8, tt)), BF16)
                w1 = pltpu.bitcast(jnp.broadcast_to(w1_ref[h, sub:sub + 1, :], (8, tt)), BF16)
                term = jnp.where(rank2_ref[h] < cnt[None], e2_ref[h], zero) * w1[None]
                gate = term if gate is None else gate + term
            a = act[s_loc * n_keys:(s_loc + 1) * n_keys, :].astype(BF16)
            gelu = (0.5 * a) * (1.0 + lax.erf(a * (2.0 ** -0.5)))
            coefs.append(gate.reshape(n_keys, tt) * gelu)
        return jnp.concatenate(coefs, axis=0)

    def stages(act_w, act_r):
        total = acc_scr[...]
        half_t = tt // 2
        new_parts = []
        for grp in range(n_groups):
            rows = slice(grp * group, (grp + 1) * group)
            coef = coef_of(grp, act_r[rows, :])
            if grp < 2:
                cols = slice(grp * half_t, (grp + 1) * half_t)
                new_parts.append(jnp.dot(u_ref[...], h2t_ref[:, cols], preferred_element_type=F32))
            total = total + jnp.dot(vt_ref[:, rows], coef, preferred_element_type=F32)
        acc_scr[...] = total
        act_w[...] = jnp.concatenate(new_parts, axis=1)

    @pl.when(s % 2 == 0)
    def _():
        stages(act_even, act_odd)

    @pl.when(s % 2 == 1)
    def _():
        stages(act_odd, act_even)

    @pl.when((chunk == n_chunks - 1) & (s >= 1))
    def _():
        gate2 = mod_ref[0, 5:6, :]
        o_ref[...] = x1_ref[...] + gate2 * acc_scr[...].T


def _peer_call(h2t, u_b, vt_b, rank2, e2, cnt1, w1, x1, mod, seq_len, tt=512, ec=2048, group=1024):
    d, t = h2t.shape
    n_exp = u_b.shape[0]
    n_heads, n_key_tiles, tile_rows, _ = rank2.shape
    n_keys = n_key_tiles * tile_rows
    tiles_per_seq = seq_len // tt
    n_chunks = n_exp // ec
    n_items = (t // tt) * n_chunks
    first = lambda s: jnp.minimum(s, n_items - 1)
    second = lambda s: jnp.clip(s - 1, 0, n_items - 1)
    route_spec = pl.BlockSpec((n_heads, n_key_tiles, tile_rows, tt), lambda s: (0, 0, 0, second(s) // n_chunks))
    row_spec = pl.BlockSpec((n_heads, ec // n_keys, tt), lambda s: (0, second(s) % n_chunks, second(s) // n_chunks))
    return pl.pallas_call(
        functools.partial(_peer_kernel, n_heads=n_heads, n_keys=n_keys, ec=ec, group=group,
                          n_chunks=n_chunks, n_items=n_items),
        grid=(n_items + 1,),
        in_specs=[pl.BlockSpec((d, tt), lambda s: (0, first(s) // n_chunks)),
                  pl.BlockSpec((ec, d), lambda s: (first(s) % n_chunks, 0)),
                  pl.BlockSpec((d, ec), lambda s: (0, second(s) % n_chunks)),
                  route_spec, route_spec, row_spec, row_spec,
                  pl.BlockSpec((tt, d), lambda s: (second(s) // n_chunks, 0)),
                  pl.BlockSpec((1, 6, d), lambda s: (second(s) // n_chunks // tiles_per_seq, 0, 0))],
        out_specs=pl.BlockSpec((tt, d), lambda s: (second(s) // n_chunks, 0)),
        out_shape=jax.ShapeDtypeStruct((t, d), F32),
        scratch_shapes=[pltpu.VMEM((ec, tt), F32), pltpu.VMEM((ec, tt), F32), pltpu.VMEM((d, tt), F32)],
        compiler_params=_cparams("arbitrary"),
        name="peer_experts",
    )(h2t, u_b, vt_b, rank2, e2, cnt1, w1, x1, mod)


def _pack_in_proj(w_in, sb_w, gdn_qk_w, gdn_v_w, n_gdn_heads, d_model, tn):
    o_sbq, o_sbk, o_sbv = 0, sb_w, 2 * sb_w
    o_gdn = 3 * sb_w
    conv_w = 2 * gdn_qk_w + gdn_v_w
    o_a = o_gdn + conv_w
    o_b = o_a + n_gdn_heads
    o_z = o_b + n_gdn_heads
    o_gate = o_z + gdn_v_w
    pad = (-w_in.shape[1]) % tn
    packed = jnp.concatenate([
        w_in[:, o_sbq:o_a],
        w_in[:, o_gate:o_gate + 2 * d_model],
        w_in[:, o_z:o_z + gdn_v_w],
        w_in[:, o_a:o_z],
        jnp.zeros((w_in.shape[0], pad), w_in.dtype)], axis=1).astype(BF16)
    return packed


def _block(x2d, c, w_ada, b_ada, norm1_w, w_in, sb_q_norm_w, sb_k_norm_w, gdn_conv_w, gdn_A_log,
           gdn_dt_bias, gdn_o_norm_w, w_proj_sb, w_proj_gdn, w_o, norm2_w, peer_w_q, peer_sub_keys,
           peer_u, peer_v, batch, seq_len):
    t, d = x2d.shape
    sb_w = w_proj_sb.shape[0]
    gdn_v_w = w_proj_gdn.shape[0]
    gdn_qk_w = (gdn_conv_w.shape[1] - gdn_v_w) // 2
    n_sb_heads = sb_w // HEAD_DIM
    n_gdn_heads = gdn_v_w // HEAD_DIM
    assert gdn_qk_w == gdn_v_w == sb_w and d % (4 * LANES) == 0

    c_pad = jnp.zeros((8, d), F32).at[:batch].set(c)
    mod = _ada_call(c_pad, w_ada, b_ada)[:batch].reshape(batch, 6, d)

    tn = sb_w
    w_packed = _pack_in_proj(w_in, sb_w, gdn_qk_w, gdn_v_w, n_gdn_heads, d, tn)
    heads_per_tile = tn // HEAD_DIM
    q_tiles = sb_w // tn
    qkw = jnp.concatenate([jnp.tile(sb_q_norm_w * (HEAD_DIM ** -0.5 * LOG2E), (q_tiles, heads_per_tile)),
                           jnp.tile(sb_k_norm_w, (q_tiles, heads_per_tile))], axis=0).reshape(2 * q_tiles, 1, tn)
    proj, ab_logits = _inproj_call(x2d, mod, norm1_w.reshape(1, d), w_packed, qkw, seq_len, tn=tn)

    ysb = _sb_call(proj, batch, seq_len, n_sb_heads // 2)

    col_gdn = 3 * sb_w // gdn_v_w
    col_gate = (3 * sb_w + 3 * gdn_v_w) // d
    col_z = (3 * sb_w + 3 * gdn_v_w + 2 * d) // gdn_v_w
    prep = _gdn_prep_call(proj, ab_logits, gdn_conv_w, gdn_A_log, gdn_dt_bias, seq_len, n_gdn_heads,
                          (col_gdn, col_gdn + 1, col_gdn + 2), GDN_BLOCK)
    ogdn = _gdn_core_call(prep, batch, seq_len, n_gdn_heads, GDN_BLOCK)

    x1, h2, pq = _merge_call(
        x2d, ysb, ogdn, proj, mod, jnp.tile(gdn_o_norm_w, n_gdn_heads).reshape(1, gdn_v_w), norm2_w.reshape(1, d),
        w_proj_sb.astype(BF16), w_proj_gdn.astype(BF16), w_o.astype(BF16), peer_w_q.astype(BF16),
        seq_len, col_gate, col_gate + 1, col_z)

    rank2, e2, cnt1, w1 = _route_call(pq, peer_sub_keys)
    return _peer_call(h2, peer_u.astype(BF16), peer_v.T.astype(BF16), rank2, e2, cnt1, w1, x1, mod, seq_len)


def kernel(x, c, w_ada, b_ada, norm1_w, w_in, sb_q_norm_w, sb_k_norm_w, gdn_conv_w, gdn_A_log, gdn_dt_bias,
           gdn_o_norm_w, w_proj_sb, w_proj_gdn, w_o, norm2_w, peer_w_q, peer_sub_keys, peer_u, peer_v):
    batch, seq_len, d = x.shape
    x2d = x.reshape(batch * seq_len, d)
    params = (w_ada, b_ada, norm1_w, w_in, sb_q_norm_w, sb_k_norm_w, gdn_conv_w, gdn_A_log, gdn_dt_bias,
              gdn_o_norm_w, w_proj_sb, w_proj_gdn, w_o, norm2_w, peer_w_q, peer_sub_keys, peer_u, peer_v)
    depth = w_ada.shape[0]
    for l in range(depth):
        layer = [p.reshape(p.shape[1:]) if depth == 1 else p[l] for p in params]
        x2d = _block(x2d, c, *layer, batch, seq_len)
    return x2d.reshape(batch, seq_len, d)
```

```python
import functools

import jax
import jax.numpy as jnp
from jax import lax
from jax.experimental import pallas as pl
from jax.experimental.pallas import tpu as pltpu

F32 = jnp.float32
BF16 = jnp.bfloat16
HI = lax.Precision.HIGHEST
EPS = 1e-6
LOG2E = 1.4426950408889634

LANES = 128
BF16_TILE_ROWS = 16
HEAD_DIM = 64
GDN_BLOCK = 128
GDN_CONV = 4
PEER_TOPK = 16
NOT_RANKED = 99.0
VMEM_LIMIT = 56 * 1024 * 1024

NT_DIMS = (((1,), (1,)), ((), ()))
TN_DIMS = (((0,), (0,)), ((), ()))


def _cparams(*sem):
    return pltpu.CompilerParams(dimension_semantics=sem, vmem_limit_bytes=VMEM_LIMIT)


def _sigmoid(x):
    return 1.0 / (1.0 + jnp.exp(-x))


def _silu(x):
    return x * _sigmoid(x)


def _softplus(x):
    return jnp.maximum(x, 0.0) + jnp.log(1.0 + jnp.exp(-jnp.abs(x)))


def _block_diag_ones(n, group, value=1.0, dtype=F32):
    r = jnp.arange(n) // group
    return jnp.where(r[:, None] == r[None, :], value, 0.0).astype(dtype)


def _bf16_pieces(x):
    hi = x.astype(BF16)
    rest = x - hi.astype(F32)
    mid = rest.astype(BF16)
    return hi, mid, (rest - mid.astype(F32)).astype(BF16)


def _group_sum_sq(x, bd):
    slab = bd.shape[0]
    outs = []
    for c0 in range(0, x.shape[1], slab):
        sq = x[:, c0:c0 + slab] * x[:, c0:c0 + slab]
        hi = sq.astype(BF16)
        lo = (sq - hi.astype(F32)).astype(BF16)
        outs.append(jnp.dot(hi, bd, preferred_element_type=F32) + jnp.dot(lo, bd, preferred_element_type=F32))
    return outs[0] if len(outs) == 1 else jnp.concatenate(outs, axis=1)


def _ada_kernel(c_ref, w_ref, b_ref, o_ref):
    c = c_ref[...]
    o_ref[...] = jnp.dot(_silu(c), w_ref[...], preferred_element_type=F32, precision=HI) + b_ref[...]


def _ada_call(c_pad, w_ada, b_ada):
    rows, d = c_pad.shape
    n = w_ada.shape[1]
    tn = 2048
    return pl.pallas_call(
        _ada_kernel,
        grid=(n // tn,),
        in_specs=[pl.BlockSpec((rows, d), lambda j: (0, 0)),
                  pl.BlockSpec((d, tn), lambda j: (0, j)),
                  pl.BlockSpec((1, tn), lambda j: (0, j))],
        out_specs=pl.BlockSpec((rows, tn), lambda j: (0, j)),
        out_shape=jax.ShapeDtypeStruct((rows, n), F32),
        compiler_params=_cparams("arbitrary"),
        name="ada_mod",
    )(c_pad, w_ada, b_ada.reshape(1, n))


def _inproj_kernel(x_ref, mod_ref, n1_ref, w_ref, qkw_ref, bd_ref, o_ref, tail_ref, h_scr, *, n_qk_tiles):
    j = pl.program_id(1)
    last = pl.num_programs(1) - 1

    @pl.when(j == 0)
    def _():
        x = x_ref[...]
        ms = jnp.mean(x * x, axis=-1, keepdims=True)
        y = x * lax.rsqrt(ms + EPS) * n1_ref[...]
        shift = mod_ref[0, 0:1, :]
        scale = mod_ref[0, 1:2, :]
        h_scr[...] = (y * (1.0 + scale) + shift).astype(BF16)

    acc = jnp.dot(h_scr[...], w_ref[...], preferred_element_type=F32)

    @pl.when(j < n_qk_tiles)
    def _():
        o_ref[...] = (acc * lax.rsqrt(_group_sum_sq(acc, bd_ref[...]) + EPS) * qkw_ref[0]).astype(BF16)

    @pl.when((j >= n_qk_tiles) & (j < last))
    def _():
        o_ref[...] = acc.astype(BF16)

    @pl.when(j == last)
    def _():
        tail_ref[...] = acc


def _inproj_call(x2d, mod, n1w, w_packed, qkw, seq_len, tm=2048, tn=512):
    t, d = x2d.shape
    n = w_packed.shape[1]
    n_tiles = n // tn
    n_qk_tiles = qkw.shape[0]
    assert seq_len % tm == 0, "a row tile must not straddle two sequences (per-sequence modulation)"
    bd = _block_diag_ones(2 * LANES, HEAD_DIM, 1.0 / HEAD_DIM, BF16)
    tiles_per_seq = seq_len // tm
    return pl.pallas_call(
        functools.partial(_inproj_kernel, n_qk_tiles=n_qk_tiles),
        grid=(t // tm, n // tn),
        in_specs=[pl.BlockSpec((tm, d), lambda i, j: (i, 0)),
                  pl.BlockSpec((1, 6, d), lambda i, j: (i // tiles_per_seq, 0, 0)),
                  pl.BlockSpec((1, d), lambda i, j: (0, 0)),
                  pl.BlockSpec((d, tn), lambda i, j: (0, j)),
                  pl.BlockSpec((1, 1, tn), lambda i, j: (jnp.minimum(j, n_qk_tiles - 1), 0, 0)),
                  pl.BlockSpec(bd.shape, lambda i, j: (0, 0))],
        out_specs=[pl.BlockSpec((tm, tn), lambda i, j: (i, jnp.minimum(j, n_tiles - 2))),
                   pl.BlockSpec((tm, tn), lambda i, j: (i, 0))],
        out_shape=[jax.ShapeDtypeStruct((t, n - tn), BF16), jax.ShapeDtypeStruct((t, tn), F32)],
        scratch_shapes=[pltpu.VMEM((tm, d), BF16)],
        compiler_params=_cparams("arbitrary", "arbitrary"),
        name="in_proj",
    )(x2d, mod, n1w, w_packed, qkw, bd)


def _sb_kernel(q_ref, k_ref, v_ref, tri_ref, o_ref, acc_ref, r_ref, *, tq):
    i = pl.program_id(2)
    half = tri_ref.shape[0]
    lane = lax.broadcasted_iota(jnp.int32, (1, LANES), 1)
    first_half = lane < HEAD_DIM
    q = q_ref[...].astype(F32)
    q_both = jnp.concatenate([jnp.where(first_half, q, 0.0), jnp.where(first_half, 0.0, q)], axis=0).astype(BF16)
    acc_ref[...] = jnp.zeros_like(acc_ref)
    r_ref[...] = jnp.zeros_like(r_ref)

    def run(blocks, masked):
        items = [(j, hd) for j in blocks for hd in range(2)]
        st = [dict() for _ in items]
        if masked:
            causal = (lax.broadcasted_iota(jnp.int32, (tq, tq), 1) < lax.broadcasted_iota(jnp.int32, (tq, tq), 0))

        def rows_of(n):
            return pl.ds(pl.multiple_of(items[n][0] * tq, tq), tq)

        def logits(n):
            if items[n][1] == 0:
                kb = k_ref[rows_of(n), :].astype(BF16)
                y_both = lax.dot_general(q_both, kb, NT_DIMS, preferred_element_type=F32)
                st[n]["y"] = y_both[:tq]
                st[n + 1]["y"] = y_both[tq:]

        def softplus(n):
            y = st[n]["y"]
            neg_abs = lax.bitcast_convert_type(lax.bitcast_convert_type(y, jnp.uint32) | jnp.uint32(0x80000000), F32)
            sp = jnp.maximum(y, 0.0) + jnp.log(1.0 + jnp.exp2(neg_abs)) * LOG2E
            if masked:
                sp = jnp.where(causal, sp, 0.0)
            st[n]["log_beta"] = y - sp
            st[n]["spb"] = sp.astype(BF16)
            st[n]["sp_first"] = (sp[:, 0:1], sp[:, half:half + 1])

        def later_sums(n):
            spb = st[n]["spb"]
            first_lo, first_hi = st[n]["sp_first"]
            later_hi = jnp.dot(spb[:, half:], tri_ref[...], preferred_element_type=F32)
            total_hi = later_hi[:, 0:1] + first_hi
            later_lo = jnp.dot(spb[:, :half], tri_ref[...], preferred_element_type=F32) + total_hi
            st[n]["later"] = jnp.concatenate([later_lo, later_hi], axis=1)
            st[n]["total"] = later_lo[:, 0:1] + first_lo

        def weights(n):
            a = jnp.exp2(st[n]["log_beta"] - st[n]["later"])
            if masked:
                a = jnp.where(causal, a, 0.0)
            st[n]["a"] = a.astype(BF16)

        def accumulate(n):
            hd = items[n][1]
            pv = jnp.dot(st[n]["a"], v_ref[rows_of(n), :].astype(BF16), preferred_element_type=F32)
            r = r_ref[hd]
            acc_ref[hd] += jnp.exp2(-r) * pv
            r_ref[hd] = r + st[n]["total"]

        stages = (logits, softplus, later_sums, weights, accumulate)
        elementwise = (softplus, weights)
        for wave in range(len(items) + len(stages) - 1):
            todo = [(stages[wave - n], n) for n in range(len(items)) if 0 <= wave - n < len(stages)]
            for stage, n in [x for x in todo if x[0] in elementwise] + [x for x in todo if x[0] not in elementwise]:
                stage(n)

    run([i], True)

    per_trip = 4

    def body(it, carry):
        j = i - 1 - per_trip * it
        run([j - d for d in range(per_trip)], False)
        return carry

    lax.fori_loop(0, i // per_trip, body, 0)

    left = i % per_trip

    @pl.when(left >= 2)
    def _():
        run([left - 1, left - 2], False)

    @pl.when(left % 2 == 1)
    def _():
        run([0], False)

    o_ref[...] = jnp.where(first_half, acc_ref[0], acc_ref[1])


def _sb_call(proj, batch, seq_len, n_pairs, tq=512):
    t = proj.shape[0]
    nq = seq_len // tq
    half = tq // 2
    kk = jnp.arange(half)
    later = (kk[:, None] > kk[None, :]).astype(BF16)
    return pl.pallas_call(
        functools.partial(_sb_kernel, tq=tq),
        grid=(batch, n_pairs, nq),
        in_specs=[pl.BlockSpec((tq, LANES), lambda b, p, i: (b * nq + i, p)),
                  pl.BlockSpec((seq_len, LANES), lambda b, p, i: (b, n_pairs + p)),
                  pl.BlockSpec((seq_len, LANES), lambda b, p, i: (b, 2 * n_pairs + p)),
                  pl.BlockSpec((half, half), lambda b, p, i: (0, 0))],
        out_specs=pl.BlockSpec((tq, LANES), lambda b, p, i: (b * nq + i, p)),
        out_shape=jax.ShapeDtypeStruct((t, n_pairs * LANES), F32),
        scratch_shapes=[pltpu.VMEM((2, tq, LANES), F32), pltpu.VMEM((2, tq, 1), F32)],
        compiler_params=_cparams("arbitrary", "arbitrary", "arbitrary"),
        name="sb_attention",
    )(proj, proj, proj, later)


def _gdn_prep_kernel(q_ref, k_ref, v_ref, qt_ref, kt_ref, vt_ref, ab_ref, cw_ref, alog_ref, dtb_ref,
                     ea_ref, eb_ref, lc_ref, jc_ref, bd_ref,
                     qn_o, qd_o, kn_o, kd_o, kb_o, kbg_o, vb_o, gcol_o, grow_o, scr, *, tm, tiles_per_seq):
    i = pl.program_id(0)
    keep_tail = (i % tiles_per_seq != 0).astype(F32)
    w = cw_ref[...]
    width = q_ref.shape[1]

    halo = qt_ref.shape[0]

    def conv_silu(cur_ref, tail_ref, col0):
        scr[0:halo, :] = tail_ref[...].astype(F32) * keep_tail
        scr[halo:, :] = cur_ref[...].astype(F32)
        y = jnp.zeros((tm, width), F32)
        for tap in range(GDN_CONV):
            off = halo - (GDN_CONV - 1) + tap
            y = y + scr[off:off + tm, :] * w[tap:tap + 1, col0:col0 + width]
        return _silu(y)

    bd = bd_ref[...]
    cq = conv_silu(q_ref, qt_ref, 0)
    qn = cq * lax.rsqrt(_group_sum_sq(cq, bd) + EPS) * (HEAD_DIM ** -0.5)
    ck = conv_silu(k_ref, kt_ref, width)
    kn = ck * lax.rsqrt(_group_sum_sq(ck, bd) + EPS)
    cv = conv_silu(v_ref, vt_ref, 2 * width)

    ab = ab_ref[...]
    g = -jnp.exp(alog_ref[...]) * _softplus(ab + dtb_ref[...])
    beta = _sigmoid(ab)
    g_parts = _bf16_pieces(g)
    lc, jc, ea = lc_ref[...], jc_ref[...], ea_ref[...]
    g_cum = sum(jnp.dot(lc, p, preferred_element_type=F32) for p in g_parts)
    g_tot = sum(jnp.dot(jc, p, preferred_element_type=F32) for p in g_parts)
    gx = sum(jnp.dot(p, ea, preferred_element_type=F32) for p in _bf16_pieces(g_cum))
    glx = sum(jnp.dot(p, ea, preferred_element_type=F32) for p in _bf16_pieces(g_tot))
    bx = sum(jnp.dot(p, eb_ref[...], preferred_element_type=F32) for p in _bf16_pieces(beta))

    e_g = jnp.exp(gx)
    kb = kn * bx
    qn_o[...] = qn
    qd_o[...] = qn * e_g
    kn_o[...] = kn
    kd_o[...] = kn * jnp.exp(glx - gx)
    kb_o[...] = kb
    kbg_o[...] = kb * e_g
    vb_o[...] = cv * bx
    g_heads = g_cum[:, :LANES]
    gcol_o[...] = g_heads
    grow_o[...] = g_heads.T[0:8, :]


def _gdn_prep_call(proj, ab_logits, conv_w, a_log, dt_bias, seq_len, n_heads, col_qkv, chunk, tm=512):
    t = proj.shape[0]
    assert n_heads <= 8 and tm % chunk == 0
    width = n_heads * HEAD_DIM
    abw = 2 * LANES
    tiles_per_seq = seq_len // tm
    halo = 16
    sub = tm // halo
    alog_pad = jnp.zeros((1, abw), F32).at[0, :n_heads].set(a_log)
    dtb_pad = jnp.zeros((1, abw), F32).at[0, :n_heads].set(dt_bias)
    head_of_lane = jnp.arange(width) // HEAD_DIM
    rows = jnp.arange(abw)
    ea = (rows[:, None] == head_of_lane[None, :]).astype(BF16)
    eb = (rows[:, None] == head_of_lane[None, :] + n_heads).astype(BF16)
    tok = jnp.arange(tm)
    same_chunk = (tok[:, None] // chunk) == (tok[None, :] // chunk)
    lc = (same_chunk & (tok[:, None] >= tok[None, :])).astype(BF16)
    jc = same_chunk.astype(BF16)
    bd = _block_diag_ones(2 * LANES, HEAD_DIM, 1.0, BF16)

    cur = lambda c: pl.BlockSpec((tm, width), lambda i: (i, c))
    tail = lambda c: pl.BlockSpec((halo, width), lambda i: (jnp.maximum(i * sub - 1, 0), c))
    full = lambda a: pl.BlockSpec(a.shape, lambda i: (0,) * a.ndim)
    cq, ck, cv = col_qkv
    out_spec = pl.BlockSpec((tm, width), lambda i: (i, 0))
    out_shape = jax.ShapeDtypeStruct((t, width), F32)
    return pl.pallas_call(
        functools.partial(_gdn_prep_kernel, tm=tm, tiles_per_seq=tiles_per_seq),
        grid=(t // tm,),
        in_specs=[cur(cq), cur(ck), cur(cv), tail(cq), tail(ck), tail(cv),
                  pl.BlockSpec((tm, abw), lambda i: (i, 0)),
                  full(conv_w), full(alog_pad), full(dtb_pad), full(ea), full(eb), full(lc), full(jc), full(bd)],
        out_specs=[out_spec] * 7 + [pl.BlockSpec((tm, LANES), lambda i: (i, 0)), pl.BlockSpec((8, tm), lambda i: (0, i))],
        out_shape=[out_shape] * 7 + [jax.ShapeDtypeStruct((t, LANES), F32), jax.ShapeDtypeStruct((8, t), F32)],
        scratch_shapes=[pltpu.VMEM((tm + halo, width), F32)],
        compiler_params=_cparams("arbitrary"),
        name="gdn_prep",
    )(proj, proj, proj, proj, proj, proj, ab_logits, conv_w, alog_pad, dtb_pad, ea, eb, lc, jc, bd)


def _gdn_core_kernel(qn_ref, qd_ref, kn_ref, kd_ref, kb_ref, kbg_ref, vb_ref, gcol_ref, grow_ref,
                     o_ref, s_scr, *, ts, n_heads, c_len):
    @pl.when(pl.program_id(1) == 0)
    def _():
        s_scr[...] = jnp.zeros_like(s_scr)

    row = lax.broadcasted_iota(jnp.int32, (c_len, c_len), 0)
    col = lax.broadcasted_iota(jnp.int32, (c_len, c_len), 1)
    strict = row > col
    incl = row >= col
    eye_state = (lax.broadcasted_iota(jnp.int32, (HEAD_DIM, HEAD_DIM), 0)
                 == lax.broadcasted_iota(jnp.int32, (HEAD_DIM, HEAD_DIM), 1))
    n_levels = (c_len - 1).bit_length()
    n_chunks = ts // c_len
    problems = [(c, h) for c in range(n_chunks) for h in range(n_heads)]

    def bf(x):
        return x.astype(BF16)

    def mm(a, b):
        return jnp.dot(a, b, preferred_element_type=F32)

    def head_tile(ref, c, h):
        pair, half = divmod(h, 2)
        tile = ref[c * c_len:(c + 1) * c_len, pair * LANES:(pair + 1) * LANES]
        return tile[:, half * HEAD_DIM:(half + 1) * HEAD_DIM]

    sibling = [((row >> k) ^ (col >> k)) == 1 for k in range(n_levels)]

    xs, ms, rs, qks = [], [], [], []
    for c, h in problems:
        g_col = gcol_ref[c * c_len:(c + 1) * c_len, h:h + 1]
        g_row = grow_ref[h:h + 1, c * c_len:(c + 1) * c_len]
        decay = jnp.where(incl, jnp.exp(g_col - g_row), 0.0)
        kn_b = bf(head_tile(kn_ref, c, h))
        kk = lax.dot_general(bf(head_tile(kb_ref, c, h)), kn_b, NT_DIMS, preferred_element_type=F32)
        x = jnp.where(strict, -(kk * decay), 0.0)
        xs.append(x)
        ms.append(jnp.where(row == col, 1.0, jnp.where(sibling[0], x, 0.0)))
        qks.append(bf(lax.dot_general(bf(head_tile(qn_ref, c, h)), kn_b, NT_DIMS, preferred_element_type=F32) * decay))
        rs.append(bf(jnp.concatenate([head_tile(vb_ref, c, h), head_tile(kbg_ref, c, h)], axis=1)))

    for k in range(1, n_levels):
        for idx in range(len(problems)):
            m_b = bf(ms[idx])
            left = mm(m_b, bf(jnp.where(sibling[k], xs[idx], 0.0)))
            ms[idx] = ms[idx] + mm(bf(left), m_b)

    q_eff, o_zero, p_mat, b_mat = {}, {}, {}, {}
    for idx, (c, h) in enumerate(problems):
        sol = bf(mm(bf(ms[idx]), rs[idx]))
        m1 = mm(qks[idx], sol)
        q_eff[c, h] = bf(head_tile(qd_ref, c, h) - m1[:, HEAD_DIM:])
        o_zero[c, h] = m1[:, :HEAD_DIM]
        m2 = lax.dot_general(bf(head_tile(kd_ref, c, h)), sol, TN_DIMS, preferred_element_type=F32)
        chunk_decay = jnp.exp(gcol_ref[(c + 1) * c_len - 1:(c + 1) * c_len, h:h + 1])
        p_mat[c, h] = bf(jnp.where(eye_state, chunk_decay, 0.0) - m2[:, HEAD_DIM:])
        b_mat[c, h] = m2[:, :HEAD_DIM]

    states = [s_scr[h] for h in range(n_heads)]
    for c in range(n_chunks):
        outs = []
        for h in range(n_heads):
            s_b = bf(states[h])
            outs.append(mm(q_eff[c, h], s_b) + o_zero[c, h])
            states[h] = mm(p_mat[c, h], s_b) + b_mat[c, h]
        o_ref[c * c_len:(c + 1) * c_len, :] = jnp.concatenate(outs, axis=1)
    for h in range(n_heads):
        s_scr[h] = states[h]


def _gdn_core_call(prep, batch, seq_len, n_heads, c_len, ts=512):
    t, width = prep[0].shape
    n_seq_tiles = seq_len // ts
    spec = pl.BlockSpec((ts, width), lambda b, s: (b * n_seq_tiles + s, 0))
    return pl.pallas_call(
        functools.partial(_gdn_core_kernel, ts=ts, n_heads=n_heads, c_len=c_len),
        grid=(batch, n_seq_tiles),
        in_specs=[spec] * 7 + [pl.BlockSpec((ts, LANES), lambda b, s: (b * n_seq_tiles + s, 0)),
                               pl.BlockSpec((8, ts), lambda b, s: (0, b * n_seq_tiles + s))],
        out_specs=spec,
        out_shape=jax.ShapeDtypeStruct((t, width), F32),
        scratch_shapes=[pltpu.VMEM((n_heads, HEAD_DIM, HEAD_DIM), F32)],
        compiler_params=_cparams("arbitrary", "arbitrary"),
        name="gdn_core",
    )(*prep)


def _merge_kernel(x_ref, ysb_ref, og_ref, z_ref, g0_ref, g1_ref, mod_ref, onw_ref, n2_ref, bd_ref,
                  wsb_ref, wgdn_ref, wo_ref, wq_ref, x1_o, h2t_o, pq_o):
    og = og_ref[...]
    ygdn = og * lax.rsqrt(_group_sum_sq(og, bd_ref[...]) + EPS) * onw_ref[...] * _silu(z_ref[...].astype(F32))
    m = (_sigmoid(g0_ref[...].astype(F32)) * jnp.dot(ysb_ref[...].astype(BF16), wsb_ref[...],
                                                      preferred_element_type=F32)
         + _sigmoid(g1_ref[...].astype(F32)) * jnp.dot(ygdn.astype(BF16), wgdn_ref[...], preferred_element_type=F32))
    gate1 = mod_ref[0, 2:3, :]
    shift2 = mod_ref[0, 3:4, :]
    scale2 = mod_ref[0, 4:5, :]
    x1 = x_ref[...] + gate1 * jnp.dot(m.astype(BF16), wo_ref[...], preferred_element_type=F32)
    x1_o[...] = x1
    ms2 = jnp.mean(x1 * x1, axis=-1, keepdims=True)
    h2 = x1 * lax.rsqrt(ms2 + EPS) * n2_ref[...] * (1.0 + scale2) + shift2
    h2t_o[...] = h2.T.astype(BF16)
    pq_o[...] = jnp.dot(h2.astype(BF16), wq_ref[...], preferred_element_type=F32)


def _merge_call(x2d, ysb, ogdn, proj, mod, onw, n2w, wsb, wgdn, wo, wq, seq_len, col_g0, col_g1, col_z, tm=512):
    t, d = x2d.shape
    width = ysb.shape[1]
    nq = wq.shape[1]
    bd = _block_diag_ones(2 * LANES, HEAD_DIM, 1.0 / HEAD_DIM, BF16)
    tiles_per_seq = seq_len // tm
    full = lambda a: pl.BlockSpec(a.shape, lambda i: (0,) * a.ndim)
    once = lambda a: pl.BlockSpec(a.shape, lambda i: (0,) * a.ndim, pipeline_mode=pl.Buffered(1))
    return pl.pallas_call(
        _merge_kernel,
        grid=(t // tm,),
        in_specs=[pl.BlockSpec((tm, d), lambda i: (i, 0)),
                  pl.BlockSpec((tm, width), lambda i: (i, 0)),
                  pl.BlockSpec((tm, width), lambda i: (i, 0)),
                  pl.BlockSpec((tm, width), lambda i: (i, col_z)),
                  pl.BlockSpec((tm, d), lambda i: (i, col_g0)),
                  pl.BlockSpec((tm, d), lambda i: (i, col_g1)),
                  pl.BlockSpec((1, 6, d), lambda i: (i // tiles_per_seq, 0, 0)),
                  full(onw), full(n2w), full(bd), once(wsb), once(wgdn), once(wo), once(wq)],
        out_specs=[pl.BlockSpec((tm, d), lambda i: (i, 0)),
                   pl.BlockSpec((d, tm), lambda i: (0, i)),
                   pl.BlockSpec((tm, nq), lambda i: (i, 0))],
        out_shape=[jax.ShapeDtypeStruct((t, d), F32),
                   jax.ShapeDtypeStruct((d, t), BF16),
                   jax.ShapeDtypeStruct((t, nq), F32)],
        compiler_params=_cparams("arbitrary"),
        name="merge_proj",
    )(x2d, ysb, ogdn, proj, proj, proj, mod, onw, n2w, bd, wsb, wgdn, wo, wq)


def _extract_topk(s, k, break_ties):
    n = s.shape[0]
    if not break_ties:
        lowest_bits = -8388609
        vals = []
        for r in range(k):
            m = jnp.max(s, axis=0, keepdims=True)
            marker = lax.bitcast_convert_type(jnp.int32(lowest_bits - r), F32)
            s = jnp.where(s == m, marker, s)
            vals.append(m)
        took = jnp.int32(lowest_bits) - lax.bitcast_convert_type(s, jnp.int32)
        rank = jnp.where((took >= 0) & (took < k), took.astype(F32), NOT_RANKED)
        return vals, rank
    iota = lax.broadcasted_iota(jnp.int32, s.shape, 0).astype(F32)
    rank = jnp.full(s.shape, NOT_RANKED, F32)
    vals = []
    for r in range(k):
        m = jnp.max(s, axis=0, keepdims=True)
        hit = iota == jnp.min(jnp.where(s == m, iota, float(n)), axis=0, keepdims=True)
        rank = jnp.where(hit, float(r), rank)
        s = jnp.where(hit, -jnp.inf, s)
        vals.append(m)
    return vals, rank


def _candidate_tables(k):
    pairs = [(a, b) for a in range(k) for b in range(k) if (a + 1) * (b + 1) <= k]
    n_pad = -(-len(pairs) // 8) * 8
    sel_a = jnp.zeros((n_pad, k), F32).at[jnp.arange(len(pairs)), jnp.array([a for a, _ in pairs])].set(1.0)
    sel_b = jnp.zeros((n_pad, k), F32).at[jnp.arange(len(pairs)), jnp.array([b for _, b in pairs])].set(1.0)
    return sel_a, sel_b, len(pairs)


def _bf16_pair_words(x):
    bits = lax.bitcast_convert_type(x.astype(BF16).astype(F32), jnp.uint32)
    return bits | (bits >> 16)


def _route_kernel(pq_ref, keys_ref, sela_ref, selb_ref, rank2_o, e2_o, cnt1_o, w1_o, *, n_cand):
    k = PEER_TOPK
    hp, _, _, half = keys_ref.shape
    tt = pq_ref.shape[0]
    part_scores = [[lax.dot_general(keys_ref[hh, part], pq_ref[:, (2 * hh + part) * half:(2 * hh + part + 1) * half],
                                    NT_DIMS, preferred_element_type=F32, precision=HI) for hh in range(hp)]
                   for part in range(2)]
    sel_a = sela_ref[...]

    def route(break_ties, heads):
        w = len(heads) * tt
        iota_k = lax.broadcasted_iota(jnp.int32, (k, w), 0).astype(F32)
        s_all = jnp.concatenate([part_scores[part][hh] for part in range(2) for hh in heads], axis=1)
        scores = (s_all[:, :w], s_all[:, w:])
        vals, rank = _extract_topk(s_all, k, break_ties)
        top = jnp.concatenate(vals, axis=0)
        top1, top2 = top[:, :w], top[:, w:]
        v1, v2 = [v[:, :w] for v in vals], [v[:, w:] for v in vals]
        rank1, rank2 = rank[:, :w], rank[:, w:]
        cand = (jnp.dot(sel_a, top1, preferred_element_type=F32, precision=HI)
                + jnp.dot(selb_ref[...], top2, preferred_element_type=F32, precision=HI))
        cand_row = lax.broadcasted_iota(jnp.int32, cand.shape, 0)
        _, cand_rank = _extract_topk(jnp.where(cand_row < n_cand, cand, -jnp.inf), k, break_ties)
        chosen = (cand_rank < float(k)).astype(BF16)
        count = lax.dot_general(sel_a.astype(BF16), chosen, TN_DIMS, preferred_element_type=F32)
        e1 = jnp.exp(top1 - v1[0])
        e2 = jnp.exp(top2 - v2[0])
        z = jnp.zeros_like(v1[0])
        for a in range(k):
            z = z + e1[a:a + 1] * jnp.sum(jnp.where(iota_k < count[a:a + 1], e2, 0.0), axis=0, keepdims=True)
        inv_z = 1.0 / z
        cnt1 = jnp.zeros_like(rank1)
        for a in range(k):
            cnt1 = jnp.where(rank1 == float(a), count[a:a + 1], cnt1)
        e2_all = jnp.exp(scores[1] - v2[0]).astype(BF16)
        w1_all = jnp.where(rank1 < float(k), jnp.exp(scores[0] - v1[0]) * inv_z, 0.0)
        cnt1_words = _bf16_pair_words(cnt1)
        w1_words = _bf16_pair_words(w1_all)
        n_ranked = jnp.sum((rank < float(k)).astype(F32), axis=0, keepdims=True)
        n_chosen = jnp.sum((cand_rank < float(k)).astype(F32), axis=0, keepdims=True)
        off_by = jnp.maximum(jnp.maximum(jnp.abs(n_ranked[:, :w] - k), jnp.abs(n_ranked[:, w:] - k)),
                             jnp.abs(n_chosen - k))
        excess = []
        for pos, hh in enumerate(heads):
            lanes = slice(pos * tt, (pos + 1) * tt)
            rank2_o[hh] = rank2[:, lanes].astype(BF16).reshape(rank2_o.shape[1:])
            e2_o[hh] = e2_all[:, lanes].reshape(e2_o.shape[1:])
            cnt1_o[hh] = cnt1_words[:, lanes]
            w1_o[hh] = w1_words[:, lanes]
            excess.append(jnp.max(off_by[:, lanes]))
        return excess

    excess = route(False, list(range(hp)))
    for hh in range(hp):
        @pl.when(excess[hh] > 0.0)
        def _(hh=hh):
            route(True, [hh])


def _route_call(pq, sub_keys, tt=256, hp=4):
    t = pq.shape[0]
    n_heads, _, n_keys, half = sub_keys.shape
    sel_a, sel_b, n_cand = _candidate_tables(PEER_TOPK)
    out_spec = pl.BlockSpec((hp, n_keys, tt), lambda i, h: (h, 0, i))
    shape = lambda dt: jax.ShapeDtypeStruct((n_heads, n_keys, t), dt)
    tiled_spec = pl.BlockSpec((hp, n_keys // BF16_TILE_ROWS, BF16_TILE_ROWS, tt), lambda i, h: (h, 0, 0, i))
    tiled_shape = jax.ShapeDtypeStruct((n_heads, n_keys // BF16_TILE_ROWS, BF16_TILE_ROWS, t), BF16)
    return pl.pallas_call(
        functools.partial(_route_kernel, n_cand=n_cand),
        grid=(t // tt, n_heads // hp),
        in_specs=[pl.BlockSpec((tt, hp * 2 * half), lambda i, h: (i, h)),
                  pl.BlockSpec((hp, 2, n_keys, half), lambda i, h: (h, 0, 0, 0)),
                  pl.BlockSpec(sel_a.shape, lambda i, h: (0, 0)),
                  pl.BlockSpec(sel_b.shape, lambda i, h: (0, 0))],
        out_specs=[tiled_spec, tiled_spec, out_spec, out_spec],
        out_shape=[tiled_shape, tiled_shape, shape(jnp.uint32), shape(jnp.uint32)],
        compiler_params=_cparams("arbitrary", "arbitrary"),
        name="peer_route",
    )(pq, sub_keys, sel_a, sel_b)


def _peer_kernel(h2t_ref, u_ref, vt_ref, rank2_ref, e2_ref, cnt1_ref, w1_ref, x1_ref, mod_ref,
                 o_ref, act_even, act_odd, acc_scr, *, n_heads, n_keys, ec, group, n_chunks, n_items):
    s = pl.program_id(0)
    chunk = jnp.clip(s - 1, 0, n_items - 1) % n_chunks

    @pl.when(s == 0)
    def _():
        act_odd[...] = jnp.zeros_like(act_odd)

    @pl.when((chunk == 0) | (s == 0))
    def _():
        acc_scr[...] = jnp.zeros_like(acc_scr)

    tt = h2t_ref.shape[1]
    zero = jnp.zeros((), BF16)
    n_groups = ec // group
    subs_per_group = group // n_keys

    def coef_of(grp, act):
        coefs = []
        for s_loc in range(subs_per_group):
            sub = grp * subs_per_group + s_loc
            gate = None
            for h in range(n_heads):
                cnt = pltpu.bitcast(jnp.broadcast_to(cnt1_ref[h, sub:sub + 1, :], (8, tt)), BF16)
                w1 = pltpu.bitcast(jnp.broadcast_to(w1_ref[h, sub:sub + 1, :], (8, tt)), BF16)
                term = jnp.where(rank2_ref[h] < cnt[None], e2_ref[h], zero) * w1[None]
                gate = term if gate is None else gate + term
            a = act[s_loc * n_keys:(s_loc + 1) * n_keys, :].astype(BF16)
            gelu = (0.5 * a) * (1.0 + lax.erf(a * (2.0 ** -0.5)))
            coefs.append(gate.reshape(n_keys, tt) * gelu)
        return jnp.concatenate(coefs, axis=0)

    def stages(act_w, act_r):
        total = acc_scr[...]
        half_t = tt // 2
        new_parts = []
        for grp in range(n_groups):
            rows = slice(grp * group, (grp + 1) * group)
            coef = coef_of(grp, act_r[rows, :])
            if grp < 2:
                cols = slice(grp * half_t, (grp + 1) * half_t)
                new_parts.append(jnp.dot(u_ref[...], h2t_ref[:, cols], preferred_element_type=F32))
            total = total + jnp.dot(vt_ref[:, rows], coef, preferred_element_type=F32)
        acc_scr[...] = total
        act_w[...] = jnp.concatenate(new_parts, axis=1)

    @pl.when(s % 2 == 0)
    def _():
        stages(act_even, act_odd)

    @pl.when(s % 2 == 1)
    def _():
        stages(act_odd, act_even)

    @pl.when((chunk == n_chunks - 1) & (s >= 1))
    def _():
        gate2 = mod_ref[0, 5:6, :]
        o_ref[...] = x1_ref[...] + gate2 * acc_scr[...].T


def _peer_call(h2t, u_b, vt_b, rank2, e2, cnt1, w1, x1, mod, seq_len, tt=512, ec=2048, group=1024):
    d, t = h2t.shape
    n_exp = u_b.shape[0]
    n_heads, n_key_tiles, tile_rows, _ = rank2.shape
    n_keys = n_key_tiles * tile_rows
    tiles_per_seq = seq_len // tt
    n_chunks = n_exp // ec
    n_items = (t // tt) * n_chunks
    first = lambda s: jnp.minimum(s, n_items - 1)
    second = lambda s: jnp.clip(s - 1, 0, n_items - 1)
    route_spec = pl.BlockSpec((n_heads, n_key_tiles, tile_rows, tt), lambda s: (0, 0, 0, second(s) // n_chunks))
    row_spec = pl.BlockSpec((n_heads, ec // n_keys, tt), lambda s: (0, second(s) % n_chunks, second(s) // n_chunks))
    return pl.pallas_call(
        functools.partial(_peer_kernel, n_heads=n_heads, n_keys=n_keys, ec=ec, group=group,
                          n_chunks=n_chunks, n_items=n_items),
        grid=(n_items + 1,),
        in_specs=[pl.BlockSpec((d, tt), lambda s: (0, first(s) // n_chunks)),
                  pl.BlockSpec((ec, d), lambda s: (first(s) % n_chunks, 0)),
                  pl.BlockSpec((d, ec), lambda s: (0, second(s) % n_chunks)),
                  route_spec, route_spec, row_spec, row_spec,
                  pl.BlockSpec((tt, d), lambda s: (second(s) // n_chunks, 0)),
                  pl.BlockSpec((1, 6, d), lambda s: (second(s) // n_chunks // tiles_per_seq, 0, 0))],
        out_specs=pl.BlockSpec((tt, d), lambda s: (second(s) // n_chunks, 0)),
        out_shape=jax.ShapeDtypeStruct((t, d), F32),
        scratch_shapes=[pltpu.VMEM((ec, tt), F32), pltpu.VMEM((ec, tt), F32), pltpu.VMEM((d, tt), F32)],
        compiler_params=_cparams("arbitrary"),
        name="peer_experts",
    )(h2t, u_b, vt_b, rank2, e2, cnt1, w1, x1, mod)


def _pack_in_proj(w_in, sb_w, gdn_qk_w, gdn_v_w, n_gdn_heads, d_model, tn):
    o_sbq, o_sbk, o_sbv = 0, sb_w, 2 * sb_w
    o_gdn = 3 * sb_w
    conv_w = 2 * gdn_qk_w + gdn_v_w
    o_a = o_gdn + conv_w
    o_b = o_a + n_gdn_heads
    o_z = o_b + n_gdn_heads
    o_gate = o_z + gdn_v_w
    pad = (-w_in.shape[1]) % tn
    packed = jnp.concatenate([
        w_in[:, o_sbq:o_a],
        w_in[:, o_gate:o_gate + 2 * d_model],
        w_in[:, o_z:o_z + gdn_v_w],
        w_in[:, o_a:o_z],
        jnp.zeros((w_in.shape[0], pad), w_in.dtype)], axis=1).astype(BF16)
    return packed


def _block(x2d, c, w_ada, b_ada, norm1_w, w_in, sb_q_norm_w, sb_k_norm_w, gdn_conv_w, gdn_A_log,
           gdn_dt_bias, gdn_o_norm_w, w_proj_sb, w_proj_gdn, w_o, norm2_w, peer_w_q, peer_sub_keys,
           peer_u, peer_v, batch, seq_len):
    t, d = x2d.shape
    sb_w = w_proj_sb.shape[0]
    gdn_v_w = w_proj_gdn.shape[0]
    gdn_qk_w = (gdn_conv_w.shape[1] - gdn_v_w) // 2
    n_sb_heads = sb_w // HEAD_DIM
    n_gdn_heads = gdn_v_w // HEAD_DIM
    assert gdn_qk_w == gdn_v_w == sb_w and d % (4 * LANES) == 0

    c_pad = jnp.zeros((8, d), F32).at[:batch].set(c)
    mod = _ada_call(c_pad, w_ada, b_ada)[:batch].reshape(batch, 6, d)

    tn = sb_w
    w_packed = _pack_in_proj(w_in, sb_w, gdn_qk_w, gdn_v_w, n_gdn_heads, d, tn)
    heads_per_tile = tn // HEAD_DIM
    q_tiles = sb_w // tn
    qkw = jnp.concatenate([jnp.tile(sb_q_norm_w * (HEAD_DIM ** -0.5 * LOG2E), (q_tiles, heads_per_tile)),
                           jnp.tile(sb_k_norm_w, (q_tiles, heads_per_tile))], axis=0).reshape(2 * q_tiles, 1, tn)
    proj, ab_logits = _inproj_call(x2d, mod, norm1_w.reshape(1, d), w_packed, qkw, seq_len, tn=tn)

    ysb = _sb_call(proj, batch, seq_len, n_sb_heads // 2)

    col_gdn = 3 * sb_w // gdn_v_w
    col_gate = (3 * sb_w + 3 * gdn_v_w) // d
    col_z = (3 * sb_w + 3 * gdn_v_w + 2 * d) // gdn_v_w
    prep = _gdn_prep_call(proj, ab_logits, gdn_conv_w, gdn_A_log, gdn_dt_bias, seq_len, n_gdn_heads,
                          (col_gdn, col_gdn + 1, col_gdn + 2), GDN_BLOCK)
    ogdn = _gdn_core_call(prep, batch, seq_len, n_gdn_heads, GDN_BLOCK)

    x1, h2, pq = _merge_call(
        x2d, ysb, ogdn, proj, mod, jnp.tile(gdn_o_norm_w, n_gdn_heads).reshape(1, gdn_v_w), norm2_w.reshape(1, d),
        w_proj_sb.astype(BF16), w_proj_gdn.astype(BF16), w_o.astype(BF16), peer_w_q.astype(BF16),
        seq_len, col_gate, col_gate + 1, col_z)

    rank2, e2, cnt1, w1 = _route_call(pq, peer_sub_keys)
    return _peer_call(h2, peer_u.astype(BF16), peer_v.T.astype(BF16), rank2, e2, cnt1, w1, x1, mod, seq_len)


def kernel(x, c, w_ada, b_ada, norm1_w, w_in, sb_q_norm_w, sb_k_norm_w, gdn_conv_w, gdn_A_log, gdn_dt_bias,
           gdn_o_norm_w, w_proj_sb, w_proj_gdn, w_o, norm2_w, peer_w_q, peer_sub_keys, peer_u, peer_v):
    batch, seq_len, d = x.shape
    x2d = x.reshape(batch * seq_len, d)
    params = (w_ada, b_ada, norm1_w, w_in, sb_q_norm_w, sb_k_norm_w, gdn_conv_w, gdn_A_log, gdn_dt_bias,
              gdn_o_norm_w, w_proj_sb, w_proj_gdn, w_o, norm2_w, peer_w_q, peer_sub_keys, peer_u, peer_v)
    depth = w_ada.shape[0]
    for l in range(depth):
        layer = [p.reshape(p.shape[1:]) if depth == 1 else p[l] for p in params]
        x2d = _block(x2d, c, *layer, batch, seq_len)
    return x2d.reshape(batch, seq_len, d)
```
